```python
import jax, jax.numpy as jnp
from jax import lax
import numpy as np

D_MODEL = 1024
BATCH = 8
SEQ = 4096
DEPTH = 1

GRID_W = 64
CTX_LEN = 256
EPS = 1e-6
ATTN_HEADS = 8
ATTN_KV_HEADS = 2
ATTN_HEAD_DIM = 64
ATTN_WIDTH = ATTN_HEADS * ATTN_HEAD_DIM
KV_WIDTH = ATTN_KV_HEADS * ATTN_HEAD_DIM
Q_BLOCK = 128
ROPE_THETA = 10000.0
HGRN_HEADS = 4
HGRN_HEAD_DIM = 128
HGRN_WIDTH = HGRN_HEADS * HGRN_HEAD_DIM
HGRN_CHUNK = 64
MIX_WIDTH = ATTN_WIDTH + HGRN_WIDTH
IN_PROJ_WIDTH = ATTN_WIDTH + 2 * KV_WIDTH + 5 * HGRN_WIDTH
N_GROUPS = 4
EXPERTS_PER_GROUP = 8
N_EXPERTS = N_GROUPS * EXPERTS_PER_GROUP
TOP_K = 2
EXPERT_FF = 512
MOE_BLOCK = 128

kernel_name = "hymba_style_gqa_hgrn2_hmoe_dit_layer"


def rms_norm(x, g):
    xf = x.astype(jnp.float32)
    y = xf * lax.rsqrt(jnp.mean(xf * xf, axis=-1, keepdims=True) + EPS)
    return (y * g.astype(jnp.float32)).astype(x.dtype)


def head_rms_norm(x, g, n_heads):
    shp = x.shape
    xf = x.astype(jnp.float32).reshape(shp[:-1] + (n_heads, shp[-1] // n_heads))
    xf = xf * lax.rsqrt(jnp.mean(xf * xf, axis=-1, keepdims=True) + EPS)
    return (xf.reshape(shp) * g.astype(jnp.float32)).astype(x.dtype)


def modulate(x, g, shift, scale):
    return rms_norm(x, g) * (1 + scale) + shift


def adaln(cond, w_ada, b_ada):
    m = jax.nn.silu(cond) @ w_ada + b_ada
    return jnp.split(m, 6, axis=-1)


def split_proj(p):
    points = []
    acc = 0
    for w in (ATTN_WIDTH, KV_WIDTH, KV_WIDTH, HGRN_WIDTH, HGRN_WIDTH, HGRN_WIDTH, HGRN_WIDTH):
        acc += w
        points.append(acc)
    return jnp.split(p, points, axis=-1)


def axial_rope_tables(n_tokens):
    rows = n_tokens // GRID_W
    row = jnp.repeat(jnp.arange(rows), GRID_W).astype(jnp.float32)
    col = jnp.tile(jnp.arange(GRID_W), rows).astype(jnp.float32)
    half = ATTN_HEAD_DIM // 2
    freqs = ROPE_THETA ** (-jnp.arange(0, half, 2, dtype=jnp.float32) / half)
    ang = jnp.stack([row[:, None] * freqs, col[:, None] * freqs], axis=1)
    return jnp.cos(ang), jnp.sin(ang)


def apply_rope(x, cos, sin):
    B, S, H, dh = x.shape
    xr = x.astype(jnp.float32).reshape(B, S, H, 2, dh // 4, 2)
    x1, x2 = xr[..., 0], xr[..., 1]
    c = cos[None, :, None]
    s = sin[None, :, None]
    out = jnp.stack([x1 * c - x2 * s, x1 * s + x2 * c], axis=-1)
    return out.reshape(B, S, H, dh).astype(x.dtype)


def attention_group(q_l, k_l, v_l, q_c, k_c, v_c, q_norm_g, k_norm_g, with_ctx):
    B, S, _ = q_l.shape
    L = q_c.shape[1]
    G = ATTN_HEADS // ATTN_KV_HEADS

    def heads(t, h):
        return t.reshape(t.shape[0], t.shape[1], h, ATTN_HEAD_DIM)

    cos, sin = axial_rope_tables(S)
    ql = apply_rope(rms_norm(heads(q_l, ATTN_HEADS), q_norm_g), cos, sin)
    kl = apply_rope(rms_norm(heads(k_l, ATTN_KV_HEADS), k_norm_g), cos, sin)
    vl = heads(v_l, ATTN_KV_HEADS)
    kc = rms_norm(heads(k_c, ATTN_KV_HEADS), k_norm_g)
    vc = heads(v_c, ATTN_KV_HEADS)
    scale = ATTN_HEAD_DIM ** -0.5
    k_all = jnp.concatenate([kc, kl], axis=1)
    v_all = jnp.concatenate([vc, vl], axis=1)

    def attend(q, k, v):
        s = jnp.einsum('bqkgd,bskd->bkgqs', q, k).astype(jnp.float32) * scale
        p = jax.nn.softmax(s, axis=-1).astype(v.dtype)
        return jnp.einsum('bkgqs,bskd->bqkgd', p, v)

    nb = S // Q_BLOCK
    qb = jnp.moveaxis(ql.reshape(B, nb, Q_BLOCK, ATTN_KV_HEADS, G, ATTN_HEAD_DIM), 1, 0)
    ob = lax.map(lambda q: attend(q, k_all, v_all), qb)
    o_l = jnp.moveaxis(ob, 0, 1).reshape(B, S, ATTN_WIDTH)
    o_c = None
    if with_ctx:
        qc = rms_norm(heads(q_c, ATTN_HEADS), q_norm_g)
        qc = qc.reshape(B, L, ATTN_KV_HEADS, G, ATTN_HEAD_DIM)
        o_c = attend(qc, kc, vc).reshape(B, L, ATTN_WIDTH)
    return o_l, o_c


def gla_scan(q, k, v, logf, s0):
    B, L, H, dk = q.shape
    dv = v.shape[-1]
    n = L // HGRN_CHUNK

    def chunks(t):
        return jnp.moveaxis(t.astype(jnp.float32).reshape(B, n, HGRN_CHUNK, H, t.shape[-1]), 1, 0)

    causal = jnp.tril(jnp.ones((HGRN_CHUNK, HGRN_CHUNK), dtype=bool))[None, :, :, None, None]

    def step(S, inp):
        qc, kc, vc, gc = inp
        b = jnp.cumsum(gc, axis=1)
        inter = jnp.einsum('bthd,bhdv->bthv', qc * jnp.exp(b), S)
        decay = jnp.exp(jnp.where(causal, b[:, :, None] - b[:, None, :], -jnp.inf))
        A = jnp.einsum('bthd,bshd,btshd->bhts', qc, kc, decay)
        intra = jnp.einsum('bhts,bshv->bthv', A, vc)
        b_end = b[:, -1]
        S_new = jnp.exp(b_end)[..., None] * S + jnp.einsum(
            'bshd,bshv->bhdv', kc * jnp.exp(b_end[:, None] - b), vc)
        return S_new, inter + intra

    S, o = lax.scan(step, s0, (chunks(q), chunks(k), chunks(v), chunks(logf)))
    return S, jnp.moveaxis(o, 0, 1).reshape(B, L, H, dv)


def hgrn2_group(lat, ctx, lb):
    def heads(t):
        return t.reshape(t.shape[0], t.shape[1], HGRN_HEADS, HGRN_HEAD_DIM)

    def prep(q, ff, fb, i):
        q = jax.nn.silu(q.astype(jnp.float32)) * (HGRN_HEAD_DIM ** -0.5)

        def gate(fr, lbd):
            f = lbd + (1 - lbd) * jax.nn.sigmoid(fr.astype(jnp.float32))
            return heads(1 - f), heads(jnp.log(f))

        return heads(q), heads(i), gate(ff, lb[0]), gate(fb, lb[1])

    qc, ic, (kcf, gcf), (kcb, gcb) = prep(*ctx)
    ql, il, (klf, glf), (klb, glb) = prep(*lat)
    B = ql.shape[0]
    s0 = jnp.zeros((B, HGRN_HEADS, HGRN_HEAD_DIM, HGRN_HEAD_DIM), jnp.float32)

    def flip(t):
        return jnp.flip(t, axis=1)

    s_cf, o_cf = gla_scan(qc, kcf, ic, gcf, s0)
    s_cb, o_cb = gla_scan(flip(qc), flip(kcb), flip(ic), flip(gcb), s0)
    _, o_lf = gla_scan(ql, klf, il, glf, s_cf)
    _, o_lb = gla_scan(flip(ql), flip(klb), flip(il), flip(glb), s_cb)
    o_l = o_lf + flip(o_lb)
    o_c = o_cf + flip(o_cb)
    return (o_l.reshape(o_l.shape[0], o_l.shape[1], HGRN_WIDTH),
            o_c.reshape(o_c.shape[0], o_c.shape[1], HGRN_WIDTH))


def mixing(h, hc, w_in, q_norm_g, k_norm_g, attn_out_g, lb, hgrn_out_g, w_out, with_ctx):
    qa, ka, va, qr, ffr, fbr, ir, gr = split_proj(h @ w_in)
    qac, kac, vac, qrc, ffrc, fbrc, irc, grc = split_proj(hc @ w_in)
    oa, oac = attention_group(qa, ka, va, qac, kac, vac, q_norm_g, k_norm_g, with_ctx)
    orl, orc = hgrn2_group((qr, ffr, fbr, ir), (qrc, ffrc, fbrc, irc), lb)
    oa = head_rms_norm(oa, attn_out_g, ATTN_HEADS)
    orl = (head_rms_norm(orl, hgrn_out_g, HGRN_HEADS) * jax.nn.silu(gr)).astype(h.dtype)
    mix_l = jnp.concatenate([oa, orl], axis=-1) @ w_out
    mix_c = None
    if with_ctx:
        oac = head_rms_norm(oac, attn_out_g, ATTN_HEADS)
        orc = (head_rms_norm(orc, hgrn_out_g, HGRN_HEADS) * jax.nn.silu(grc)).astype(hc.dtype)
        mix_c = jnp.concatenate([oac, orc], axis=-1) @ w_out
    return mix_l, mix_c


def moe_ffn(h, w_router_grp, b_router_grp, w_router_exp, b_router_exp, w_exp_gate, w_exp_up, w_exp_down):
    N, D = h.shape
    pg = jax.nn.softmax((h @ w_router_grp).astype(jnp.float32) + b_router_grp, axis=-1)
    pg_top, g_sel = lax.top_k(pg, 1)
    le = ((h @ w_router_exp).astype(jnp.float32) + b_router_exp).reshape(N, N_GROUPS, EXPERTS_PER_GROUP)
    le = jnp.take_along_axis(le, g_sel[:, :, None], axis=1)[:, 0]
    pe = jax.nn.softmax(le, axis=-1)
    pe_top, e_loc = lax.top_k(pe, TOP_K)
    w = pg_top * pe_top / jnp.sum(pe_top, axis=-1, keepdims=True)
    e_id = g_sel * EXPERTS_PER_GROUP + e_loc

    A = N * TOP_K
    e_flat = e_id.reshape(-1)
    tok = jnp.repeat(jnp.arange(N, dtype=jnp.int32), TOP_K)
    w_flat = w.reshape(-1)
    order = jnp.argsort(e_flat)
    e_s, tok_s, w_s = e_flat[order], tok[order], w_flat[order]
    counts = jnp.bincount(e_flat, length=N_EXPERTS)
    padded = (counts + MOE_BLOCK - 1) // MOE_BLOCK * MOE_BLOCK
    start = jnp.cumsum(counts) - counts
    pend = jnp.cumsum(padded)
    pstart = pend - padded
    dest = pstart[e_s] + (jnp.arange(A, dtype=jnp.int32) - start[e_s])
    cap = -(-A // MOE_BLOCK) * MOE_BLOCK + N_EXPERTS * MOE_BLOCK
    n_blk = cap // MOE_BLOCK
    slot_tok = jnp.zeros((cap,), jnp.int32).at[dest].set(tok_s)
    slot_w = jnp.zeros((cap,), jnp.float32).at[dest].set(w_s)
    blk_start = jnp.arange(n_blk, dtype=jnp.int32) * MOE_BLOCK
    blk_e = jnp.minimum(jnp.sum(blk_start[:, None] >= pend[None, :], axis=1), N_EXPERTS - 1)
    xs = h[slot_tok].reshape(n_blk, MOE_BLOCK, D)

    def expert_block(args):
        xb, e = args
        a = jax.nn.silu(xb @ w_exp_gate[e]) * (xb @ w_exp_up[e])
        return a @ w_exp_down[e]

    ys = lax.map(expert_block, (xs, blk_e)).reshape(cap, D)
    out = jnp.zeros((N, D), jnp.float32).at[slot_tok].add(ys.astype(jnp.float32) * slot_w[:, None])
    return out.astype(h.dtype)


def trunk_layer(x, xc, c, c_ctx, w_ada, b_ada, norm1_g, norm2_g, w_in, q_norm_g, k_norm_g,
                attn_out_g, lb, hgrn_out_g, w_out, w_router_grp, b_router_grp, w_router_exp,
                b_router_exp, w_exp_gate, w_exp_up, w_exp_down, with_ctx):
    B, S, D = x.shape
    sh1, sc1, gt1, sh2, sc2, gt2 = adaln(c, w_ada, b_ada)
    csh1, csc1, cgt1, csh2, csc2, cgt2 = adaln(c_ctx, w_ada, b_ada)
    h = modulate(x, norm1_g, sh1[:, None], sc1[:, None])
    hc = modulate(xc, norm1_g, csh1, csc1)
    mix_l, mix_c = mixing(h, hc, w_in, q_norm_g, k_norm_g, attn_out_g, lb, hgrn_out_g, w_out, with_ctx)
    moe_args = (w_router_grp, b_router_grp, w_router_exp, b_router_exp, w_exp_gate, w_exp_up, w_exp_down)
    x = x + gt1[:, None] * mix_l
    h2 = modulate(x, norm2_g, sh2[:, None], sc2[:, None])
    x = x + gt2[:, None] * moe_ffn(h2.reshape(B * S, D), *moe_args).reshape(B, S, D)
    if with_ctx:
        xc = xc + cgt1 * mix_c
        h2c = modulate(xc, norm2_g, csh2, csc2)
        xc = xc + cgt2 * moe_ffn(h2c.reshape(-1, D), *moe_args).reshape(xc.shape)
    return x, xc


def setup_inputs(seed: int = 0) -> dict:
    key = jax.random.key(seed)
    ks = jax.random.split(key, 22)
    f32 = jnp.float32
    D = D_MODEL

    def nrm(k, shape, s):
        return jax.random.normal(k, shape, f32) * s

    return {
        "x": nrm(ks[0], (BATCH, SEQ, D), 1.0),
        "c": nrm(ks[1], (BATCH, D), 1.0),
        "ctx": nrm(ks[2], (BATCH, CTX_LEN, D), 1.0),
        "c_ctx": nrm(ks[3], (D,), 1.0),
        "w_ada": nrm(ks[4], (DEPTH, D, 6 * D), 0.5 * D ** -0.5),
        "b_ada": nrm(ks[5], (DEPTH, 6 * D), 0.01),
        "norm1_g": 1.0 + nrm(ks[6], (DEPTH, D), 0.1),
        "norm2_g": 1.0 + nrm(ks[7], (DEPTH, D), 0.1),
        "w_in": nrm(ks[8], (DEPTH, D, IN_PROJ_WIDTH), D ** -0.5),
        "q_norm_g": 1.0 + nrm(ks[9], (DEPTH, ATTN_HEAD_DIM), 0.1),
        "k_norm_g": 1.0 + nrm(ks[10], (DEPTH, ATTN_HEAD_DIM), 0.1),
        "attn_out_g": 1.0 + nrm(ks[11], (DEPTH, ATTN_WIDTH), 0.1),
        "hgrn_lb": nrm(ks[12], (2, DEPTH + 1, HGRN_WIDTH), 0.5),
        "hgrn_out_g": 1.0 + nrm(ks[13], (DEPTH, HGRN_WIDTH), 0.1),
        "w_out": nrm(ks[14], (DEPTH, MIX_WIDTH, D), MIX_WIDTH ** -0.5),
        "w_router_grp": nrm(ks[15], (DEPTH, D, N_GROUPS), D ** -0.5),
        "b_router_grp": nrm(ks[16], (DEPTH, N_GROUPS), 0.01),
        "w_router_exp": nrm(ks[17], (DEPTH, D, N_EXPERTS), D ** -0.5),
        "b_router_exp": nrm(ks[18], (DEPTH, N_EXPERTS), 0.01),
        "w_exp_gate": nrm(ks[19], (DEPTH, N_EXPERTS, D, EXPERT_FF), D ** -0.5),
        "w_exp_up": nrm(ks[20], (DEPTH, N_EXPERTS, D, EXPERT_FF), D ** -0.5),
        "w_exp_down": nrm(ks[21], (DEPTH, N_EXPERTS, EXPERT_FF, D), EXPERT_FF ** -0.5),
    }


def reference(x, c, ctx, c_ctx, w_ada, b_ada, norm1_g, norm2_g, w_in, q_norm_g, k_norm_g,
              attn_out_g, hgrn_lb, hgrn_out_g, w_out, w_router_grp, b_router_grp, w_router_exp,
              b_router_exp, w_exp_gate, w_exp_up, w_exp_down):
    lb_all = jnp.cumsum(jax.nn.softmax(hgrn_lb.astype(jnp.float32), axis=1), axis=1)
    xc = ctx
    for layer in range(DEPTH):
        x, xc = trunk_layer(
            x, xc, c, c_ctx, w_ada[layer], b_ada[layer], norm1_g[layer], norm2_g[layer],
            w_in[layer], q_norm_g[layer], k_norm_g[layer], attn_out_g[layer], lb_all[:, layer],
            hgrn_out_g[layer], w_out[layer], w_router_grp[layer], b_router_grp[layer],
            w_router_exp[layer], b_router_exp[layer], w_exp_gate[layer], w_exp_up[layer],
            w_exp_down[layer], with_ctx=layer < DEPTH - 1)
    return x
```

```python
import functools

import numpy as np
import jax
import jax.numpy as jnp
from jax import lax
from jax.experimental import pallas as pl
from jax.experimental.pallas import tpu as pltpu

F32 = jnp.float32
BF16 = jnp.bfloat16

GRID_W = 64
EPS = 1e-6
ATTN_HEADS = 8
ATTN_KV_HEADS = 2
ATTN_HEAD_DIM = 64
ATTN_GROUP = ATTN_HEADS // ATTN_KV_HEADS
ATTN_WIDTH = ATTN_HEADS * ATTN_HEAD_DIM
KV_WIDTH = ATTN_KV_HEADS * ATTN_HEAD_DIM
ROPE_THETA = 10000.0
HGRN_HEADS = 4
HGRN_HEAD_DIM = 128
HGRN_WIDTH = HGRN_HEADS * HGRN_HEAD_DIM
N_GROUPS = 4
EXPERTS_PER_GROUP = 8
N_EXPERTS = N_GROUPS * EXPERTS_PER_GROUP
TOP_K = 2
EXPERT_FF = 512

LANES = 128
SUBLANES = 8
MXU_DIM = 256
VMEM_LIMIT_BYTES = 48 * 1024 * 1024

TOKEN_TILE = 256
ATTN_Q_TILE = 128
ATTN_KV_TILE = 256
HGRN_CHUNK = 64
HGRN_DIAG = 8
MOE_BLOCK = 256
ROUTE_LANES = 8
NEG_BIG = -1e30

_QA0 = 0
_KA0 = _QA0 + ATTN_WIDTH
_VA0 = _KA0 + KV_WIDTH
_QR0 = _VA0 + KV_WIDTH
_FF0 = _QR0 + HGRN_WIDTH
_FB0 = _FF0 + HGRN_WIDTH
_IR0 = _FB0 + HGRN_WIDTH
_GR0 = _IR0 + HGRN_WIDTH
_QK_WIDTH = ATTN_WIDTH + KV_WIDTH


def _dot(a, b):
    return jnp.dot(a, b, preferred_element_type=F32)


def _dot_nt(a, b):
    return lax.dot_general(a, b, (((1,), (1,)), ((), ())), preferred_element_type=F32)


def _dot_tn(a, b):
    return lax.dot_general(a, b, (((0,), (0,)), ((), ())), preferred_element_type=F32)


def _split_bf16(x):
    hi = x.astype(BF16)
    lo = (x - hi.astype(F32)).astype(BF16)
    return hi, lo


def _sigmoid(x):
    return 1.0 / (1.0 + jnp.exp(-x))


def _silu(x):
    return x * _sigmoid(x)


def _params(*sem):
    return pltpu.CompilerParams(dimension_semantics=sem, vmem_limit_bytes=VMEM_LIMIT_BYTES)


def _head_sum_matrix():
    idx = np.arange(LANES) // ATTN_HEAD_DIM
    return jnp.asarray(idx[:, None] == idx[None, :], dtype=BF16)


def _head_rms_scale(x, ones_bd):
    ssq = _dot((x * x).astype(BF16), ones_bd)
    return lax.rsqrt(ssq * (1.0 / ATTN_HEAD_DIM) + EPS)


def _adaln_kernel(cond_ref, w_ref, b_ref, o_ref):
    s = _silu(cond_ref[...])
    s_hi, s_lo = _split_bf16(s)
    w_hi, w_lo = _split_bf16(w_ref[...])
    o_ref[...] = _dot(s_hi, w_hi) + _dot(s_lo, w_hi) + _dot(s_hi, w_lo) + b_ref[...]


def _adaln(cond, w_ada, b_ada):
    rows, d = cond.shape
    n = w_ada.shape[1]
    tn = n // 6
    return pl.pallas_call(
        _adaln_kernel,
        out_shape=jax.ShapeDtypeStruct((rows, n), F32),
        grid=(n // tn,),
        in_specs=[pl.BlockSpec((rows, d), lambda j: (0, 0)),
                  pl.BlockSpec((d, tn), lambda j: (0, j)),
                  pl.BlockSpec((1, tn), lambda j: (0, j))],
        out_specs=pl.BlockSpec((rows, tn), lambda j: (0, j)),
        compiler_params=_params("arbitrary"),
        name="adaln",
    )(cond, w_ada, b_ada.reshape(1, n))


def _rope_tables(n_ctx, n_lat):
    half = ATTN_HEAD_DIM // 2
    freqs = ROPE_THETA ** (-np.arange(0, half, 2, dtype=np.float64) / half)
    tok = np.arange(n_lat)
    pos = np.stack([tok // GRID_W, tok % GRID_W], axis=1).astype(np.float64)
    lane = np.arange(ATTN_HEAD_DIM)
    axis = lane // half
    fi = (lane % half) // 2
    ang = pos[:, axis] * freqs[fi][None, :]
    sign = np.where(lane % 2 == 1, 1.0, -1.0)
    cos = np.concatenate([np.ones((n_ctx, ATTN_HEAD_DIM)), np.cos(ang)], axis=0)
    sin = np.concatenate([np.zeros((n_ctx, ATTN_HEAD_DIM)), np.sin(ang) * sign], axis=0)
    reps = LANES // ATTN_HEAD_DIM
    return (jnp.asarray(np.tile(cos, (1, reps)), F32), jnp.asarray(np.tile(sin, (1, reps)), F32))


def _in_proj_kernel(n_ctx_tiles, ctx_ref, x_ref, shift_ref, scale_ref, g1_ref, w_ref, qkg_ref,
                    ones_ref, cos_ref, sin_ref,
                    qa_ref, ka_ref, va_ref, qr_ref, ff_ref, fb_ref, ir_ref, gr_ref):
    i = pl.program_id(1)
    xt = jnp.where(i < n_ctx_tiles, ctx_ref[0], x_ref[0])
    ms = jnp.mean(xt * xt, axis=-1, keepdims=True)
    h = xt * lax.rsqrt(ms + EPS) * g1_ref[...]
    h = h * (1.0 + scale_ref[0]) + shift_ref[0]
    p = _dot(h.astype(BF16), w_ref[...])

    ones_bd = ones_ref[...]
    cos = cos_ref[...]
    sin = sin_ref[...]
    even = lax.broadcasted_iota(jnp.int32, cos.shape, 1) % 2 == 0
    slabs = []
    for c0 in range(0, _QK_WIDTH, LANES):
        t = p[:, _QA0 + c0:_QA0 + c0 + LANES]
        t = t * _head_rms_scale(t, ones_bd) * qkg_ref[:, c0:c0 + LANES]
        partner = jnp.where(even, pltpu.roll(t, LANES - 1, 1), pltpu.roll(t, 1, 1))
        slabs.append((t * cos + partner * sin).astype(BF16))
    qa_ref[0] = jnp.concatenate(slabs[:ATTN_WIDTH // LANES], axis=1)
    ka_ref[0] = jnp.concatenate(slabs[ATTN_WIDTH // LANES:], axis=1)
    va_ref[0] = p[:, _VA0:_VA0 + KV_WIDTH].astype(BF16)

    qr_ref[0] = (_silu(p[:, _QR0:_QR0 + HGRN_WIDTH]) * (HGRN_HEAD_DIM ** -0.5)).astype(BF16)
    ff_ref[0] = p[:, _FF0:_FF0 + HGRN_WIDTH]
    fb_ref[0] = p[:, _FB0:_FB0 + HGRN_WIDTH]
    ir_ref[0] = p[:, _IR0:_IR0 + HGRN_WIDTH].astype(BF16)
    gr_ref[0] = _silu(p[:, _GR0:_GR0 + HGRN_WIDTH]).astype(BF16)


def _in_proj(ctx, x, shift, scale, g1, w_in, qkg, cos, sin):
    b, n_ctx, d = ctx.shape
    s = x.shape[1]
    tm = TOKEN_TILE
    nct = n_ctx // tm
    n_all = n_ctx + s
    nt = n_all // tm
    pw = w_in.shape[1]
    ones_bd = _head_sum_matrix()

    def tok_spec(w):
        return pl.BlockSpec((1, tm, w), lambda bi, i: (bi, i, 0))

    mod_spec = pl.BlockSpec((1, 1, d), lambda bi, i: (jnp.where(i < nct, b, bi), 0, 0))
    outs = [(ATTN_WIDTH, BF16), (KV_WIDTH, BF16), (KV_WIDTH, BF16), (HGRN_WIDTH, BF16),
            (HGRN_WIDTH, F32), (HGRN_WIDTH, F32), (HGRN_WIDTH, BF16), (HGRN_WIDTH, BF16)]
    return pl.pallas_call(
        functools.partial(_in_proj_kernel, nct),
        out_shape=[jax.ShapeDtypeStruct((b, n_all, w), dt) for w, dt in outs],
        grid=(b, nt),
        in_specs=[
            pl.BlockSpec((1, tm, d), lambda bi, i: (bi, jnp.minimum(i, nct - 1), 0)),
            pl.BlockSpec((1, tm, d), lambda bi, i: (bi, jnp.maximum(i - nct, 0), 0)),
            mod_spec, mod_spec,
            pl.BlockSpec((1, d), lambda bi, i: (0, 0)),
            pl.BlockSpec((d, pw), lambda bi, i: (0, 0)),
            pl.BlockSpec((1, _QK_WIDTH), lambda bi, i: (0, 0)),
            pl.BlockSpec((LANES, LANES), lambda bi, i: (0, 0)),
            pl.BlockSpec((tm, LANES), lambda bi, i: (i, 0)),
            pl.BlockSpec((tm, LANES), lambda bi, i: (i, 0)),
        ],
        out_specs=[tok_spec(w) for w, _ in outs],
        compiler_params=_params("arbitrary", "arbitrary"),
        name="in_proj",
    )(ctx, x, shift, scale, g1, w_in, qkg, ones_bd, cos, sin)


def _attention_kernel(n_kv_tiles, q_ref, k_ref, v_ref, o_ref):
    tq = q_ref.shape[1]
    q = q_ref[0]
    qs = jnp.concatenate(
        [q[:, h * ATTN_HEAD_DIM:(h + 1) * ATTN_HEAD_DIM] for h in range(ATTN_GROUP)], axis=0)
    rows = ATTN_GROUP * tq

    def body(j, carry):
        m, l, acc = carry
        start = pl.multiple_of(j * ATTN_KV_TILE, ATTN_KV_TILE)
        k = k_ref[0, 0, pl.ds(start, ATTN_KV_TILE), :]
        v = v_ref[0, 0, pl.ds(start, ATTN_KV_TILE), :]
        s = _dot_nt(qs, k)
        m_new = jnp.maximum(m, jnp.max(s, axis=-1, keepdims=True))
        p = jnp.exp(s - m_new)
        alpha = jnp.exp(m - m_new)
        l = alpha * l + jnp.sum(p, axis=-1, keepdims=True)
        acc = alpha * acc + _dot(p.astype(BF16), v)
        return m_new, l, acc

    init = (jnp.full((rows, 1), NEG_BIG, F32), jnp.zeros((rows, 1), F32),
            jnp.zeros((rows, ATTN_HEAD_DIM), F32))
    _, l, acc = lax.fori_loop(0, n_kv_tiles, body, init)
    o = acc / l
    o_ref[0] = jnp.concatenate(
        [o[h * tq:(h + 1) * tq] for h in range(ATTN_GROUP)], axis=1).astype(o_ref.dtype)


def _attention(qa, k_heads, v_heads, n_ctx):
    b, n_all, _ = qa.shape
    s = n_all - n_ctx
    tq = ATTN_Q_TILE
    q_off = n_ctx // tq
    gw = ATTN_GROUP * ATTN_HEAD_DIM
    kv_spec = pl.BlockSpec((1, 1, n_all, ATTN_HEAD_DIM), lambda bi, kv, i: (bi, kv, 0, 0))
    return pl.pallas_call(
        functools.partial(_attention_kernel, n_all // ATTN_KV_TILE),
        out_shape=jax.ShapeDtypeStruct((b, s, ATTN_WIDTH), BF16),
        grid=(b, ATTN_KV_HEADS, s // tq),
        in_specs=[pl.BlockSpec((1, tq, gw), lambda bi, kv, i: (bi, i + q_off, kv)),
                  kv_spec, kv_spec],
        out_specs=pl.BlockSpec((1, tq, gw), lambda bi, kv, i: (bi, i, kv)),
        compiler_params=_params("arbitrary", "arbitrary", "arbitrary"),
        name="attention",
    )(qa, k_heads, v_heads)


def _hgrn_masks(reverse):
    c = HGRN_CHUNK
    t = lax.broadcasted_iota(jnp.int32, (c, c), 0)
    u = lax.broadcasted_iota(jnp.int32, (c, c), 1)
    tri = (u >= t) if reverse else (u <= t)
    levels = []
    size = c // 2
    while size >= HGRN_DIAG:
        same_parent = (t // (2 * size)) == (u // (2 * size))
        t_late = (t // size) % 2 == 1
        u_late = (u // size) % 2 == 1
        pair = (u_late & ~t_late) if reverse else (t_late & ~u_late)
        levels.append((size, same_parent & pair))
        size //= 2
    return tri.astype(BF16), levels


def _hgrn_chunk(q, fr, v, lb, st, reverse, tri, levels):
    c = HGRN_CHUNK
    f = lb + (1.0 - lb) * _sigmoid(fr)
    k = 1.0 - f
    g = jnp.log(f)
    g_hi, g_lo = _split_bf16(g)
    bcum = _dot(tri, g_hi) + _dot(tri, g_lo)
    qf = q.astype(F32)

    end = 0 if reverse else c - 1
    b_end = bcum[end:end + 1, :]
    inter = _dot_nt((qf * jnp.exp(bcum)).astype(BF16), st.astype(BF16))
    ke = (k * jnp.exp(b_end - bcum)).astype(BF16)
    st_new = st * jnp.exp(b_end) + _dot_tn(v, ke)

    row = lax.broadcasted_iota(jnp.int32, (c, 1), 0)
    a = jnp.zeros((c, c), F32)
    for size, mask in levels:
        pieces = []
        for p0 in range(0, c, 2 * size):
            r = p0 + size if reverse else p0 + size - 1
            pieces.append(jnp.broadcast_to(bcum[r:r + 1, :], (2 * size, bcum.shape[1])))
        ref = jnp.concatenate(pieces, axis=0)
        late = (row // size) % 2 == 1
        q_side = ~late if reverse else late
        ql = qf * jnp.exp(jnp.where(q_side, bcum - ref, NEG_BIG))
        kl = k * jnp.exp(jnp.where(q_side, NEG_BIG, ref - bcum))
        a = a + jnp.where(mask, _dot_nt(ql.astype(BF16), kl.astype(BF16)), 0.0)

    sub = lax.broadcasted_iota(jnp.int32, (HGRN_DIAG, 1), 0)
    lane = lax.broadcasted_iota(jnp.int32, (HGRN_DIAG, c), 1)
    diag_rows = []
    for blk in range(c // HGRN_DIAG):
        r0 = blk * HGRN_DIAG
        qb = qf[r0:r0 + HGRN_DIAG]
        bb = bcum[r0:r0 + HGRN_DIAG]
        rows_acc = jnp.zeros((HGRN_DIAG, c), F32)
        for si in range(HGRN_DIAG):
            r = r0 + si
            valid = (sub <= si) if reverse else (sub >= si)
            e = jnp.where(valid, bb - bcum[r:r + 1, :], NEG_BIG)
            col = jnp.sum(qb * k[r:r + 1, :] * jnp.exp(e), axis=-1, keepdims=True)
            rows_acc = jnp.where(lane == r, col, rows_acc)
        diag_rows.append(rows_acc)
    a = a + jnp.concatenate(diag_rows, axis=0)

    o = inter + _dot(a.astype(BF16), v)
    return o, st_new


def _hgrn_kernel(qf_ref, if_ref, ff_ref, qb_ref, ib_ref, fb_ref, lb_ref, of_ref, ob_ref,
                 sf_ref, sb_ref):
    @pl.when(pl.program_id(1) == 0)
    def _():
        sf_ref[...] = jnp.zeros_like(sf_ref)
        sb_ref[...] = jnp.zeros_like(sb_ref)

    dirs = ((False, qf_ref, if_ref, ff_ref, of_ref, sf_ref),
            (True, qb_ref, ib_ref, fb_ref, ob_ref, sb_ref))
    for d, (reverse, q_ref, i_ref, f_ref, o_ref, s_ref) in enumerate(dirs):
        tri, levels = _hgrn_masks(reverse)
        outs = []
        for h in range(HGRN_HEADS):
            sl = slice(h * HGRN_HEAD_DIM, (h + 1) * HGRN_HEAD_DIM)
            o, st = _hgrn_chunk(q_ref[0, :, sl], f_ref[0, :, sl], i_ref[0, :, sl],
                                lb_ref[d:d + 1, sl], s_ref[h], reverse, tri, levels)
            s_ref[h] = st
            outs.append(o)
        o_ref[0] = jnp.concatenate(outs, axis=1)


def _hgrn(qr, ir, ff, fb, lb, n_ctx):
    b, n_all, w = qr.shape
    c = HGRN_CHUNK
    nc = n_all // c
    ncc = n_ctx // c

    def fwd(bi, j):
        return (bi, j, 0)

    def bwd(bi, j):
        return (bi, jnp.where(j < ncc, ncc - 1 - j, nc - 1 - (j - ncc)), 0)

    blk = (1, c, w)
    state = pltpu.VMEM((HGRN_HEADS, HGRN_HEAD_DIM, HGRN_HEAD_DIM), F32)
    return pl.pallas_call(
        _hgrn_kernel,
        out_shape=[jax.ShapeDtypeStruct((b, n_all, w), F32)] * 2,
        grid=(b, nc),
        in_specs=[pl.BlockSpec(blk, fwd), pl.BlockSpec(blk, fwd), pl.BlockSpec(blk, fwd),
                  pl.BlockSpec(blk, bwd), pl.BlockSpec(blk, bwd), pl.BlockSpec(blk, bwd),
                  pl.BlockSpec((2, w), lambda bi, j: (0, 0))],
        out_specs=[pl.BlockSpec(blk, fwd), pl.BlockSpec(blk, bwd)],
        scratch_shapes=[state, state],
        compiler_params=_params("arbitrary", "arbitrary"),
        name="hgrn",
    )(qr, ir, ff, qr, ir, fb, lb)


_ROUTE_GROUP_LANE0 = N_EXPERTS


def _lane_min_index(cond, lane):
    return jnp.min(jnp.where(cond, lane, LANES), axis=-1, keepdims=True)


def _out_proj_kernel(oa_ref, of_ref, ob_ref, gr_ref, x_ref, gate_ref, shift_ref, scale_ref,
                     ag_ref, hg_ref, g2_ref, w_ref, ones_ref, wr_hi_ref, wr_lo_ref, br_ref,
                     x1_ref, h2_ref, ri_ref, rw_ref, cnt_ref, carry_ref):
    first = (pl.program_id(0) == 0) & (pl.program_id(1) == 0)

    @pl.when(first)
    def _():
        carry_ref[...] = jnp.zeros_like(carry_ref)

    tm = x_ref.shape[1]
    ones_bd = ones_ref[...]
    slabs = []
    for c0 in range(0, ATTN_WIDTH, LANES):
        t = oa_ref[0, :, c0:c0 + LANES].astype(F32)
        slabs.append((t * _head_rms_scale(t, ones_bd) * ag_ref[:, c0:c0 + LANES]).astype(BF16))
    orr = of_ref[0] + ob_ref[0]
    parts = []
    for h in range(HGRN_HEADS):
        sl = slice(h * HGRN_HEAD_DIM, (h + 1) * HGRN_HEAD_DIM)
        oh = orr[:, sl]
        parts.append(oh * lax.rsqrt(jnp.mean(oh * oh, axis=-1, keepdims=True) + EPS))
    orr = jnp.concatenate(parts, axis=1) * hg_ref[...] * gr_ref[0].astype(F32)
    mix_in = jnp.concatenate(slabs + [orr.astype(BF16)], axis=1)
    x1 = x_ref[0] + gate_ref[0] * _dot(mix_in, w_ref[...])
    x1_ref[0] = x1
    h2 = x1 * lax.rsqrt(jnp.mean(x1 * x1, axis=-1, keepdims=True) + EPS) * g2_ref[...]
    h2 = h2 * (1.0 + scale_ref[0]) + shift_ref[0]
    h2_ref[0] = h2

    h_hi, h_lo = _split_bf16(h2)
    logits = (_dot(h_hi, wr_hi_ref[...]) + _dot(h_lo, wr_hi_ref[...])
              + _dot(h_hi, wr_lo_ref[...]) + br_ref[...])
    lane = lax.broadcasted_iota(jnp.int32, logits.shape, 1)
    is_grp = (lane >= _ROUTE_GROUP_LANE0) & (lane < _ROUTE_GROUP_LANE0 + N_GROUPS)
    lg = jnp.where(is_grp, logits, NEG_BIG)
    mg = jnp.max(lg, axis=-1, keepdims=True)
    g_sel = _lane_min_index(lg == mg, lane) - _ROUTE_GROUP_LANE0
    pg_top = 1.0 / jnp.sum(jnp.exp(lg - mg), axis=-1, keepdims=True)
    in_grp = (lane < N_EXPERTS) & ((lane // EXPERTS_PER_GROUP) == g_sel)
    le = jnp.where(in_grp, logits, NEG_BIG)
    m1 = jnp.max(le, axis=-1, keepdims=True)
    e1 = _lane_min_index(le == m1, lane)
    le2 = jnp.where(lane == e1, NEG_BIG, le)
    m2 = jnp.max(le2, axis=-1, keepdims=True)
    e2 = _lane_min_index(le2 == m2, lane)
    r2 = jnp.exp(m2 - m1)
    w1 = pg_top / (1.0 + r2)
    w2 = pg_top * r2 / (1.0 + r2)

    onehot = ((lane == e1) | (lane == e2)).astype(BF16)
    rt = lax.broadcasted_iota(jnp.int32, (tm, tm), 0)
    ru = lax.broadcasted_iota(jnp.int32, (tm, tm), 1)
    before = _dot((ru < rt).astype(BF16), onehot) + carry_ref[0:1, :]
    rank1 = jnp.sum(jnp.where(lane == e1, before, 0.0), axis=-1, keepdims=True)
    rank2 = jnp.sum(jnp.where(lane == e2, before, 0.0), axis=-1, keepdims=True)
    total = carry_ref[0:1, :] + jnp.sum(onehot.astype(F32), axis=0, keepdims=True)
    carry_ref[...] = jnp.broadcast_to(total, carry_ref.shape)
    cnt_ref[...] = jnp.broadcast_to(total, cnt_ref.shape).astype(jnp.int32)

    rl = lax.broadcasted_iota(jnp.int32, (tm, ROUTE_LANES), 1)
    ri = jnp.where(rl == 0, e1, jnp.where(rl == 1, e2, jnp.where(
        rl == 2, rank1.astype(jnp.int32), jnp.where(rl == 3, rank2.astype(jnp.int32), 0))))
    ri_ref[0] = ri
    rw_ref[0] = jnp.where(rl == 0, w1, jnp.where(rl == 1, w2, 0.0))


def _out_proj(oa, o_f, o_b, gr, x, gate, shift, scale, ag, hg, g2, w_out, wr_hi, wr_lo, br, n_ctx):
    b, s, d = x.shape
    tm = TOKEN_TILE
    off = n_ctx // tm
    ones_bd = _head_sum_matrix()

    def lat(w):
        return pl.BlockSpec((1, tm, w), lambda bi, i: (bi, i, 0))

    def allrows(w):
        return pl.BlockSpec((1, tm, w), lambda bi, i: (bi, i + off, 0))

    def mod():
        return pl.BlockSpec((1, 1, d), lambda bi, i: (bi, 0, 0))

    def const(shape):
        return pl.BlockSpec(shape, lambda bi, i: (0,) * len(shape))

    return pl.pallas_call(
        _out_proj_kernel,
        out_shape=[jax.ShapeDtypeStruct((b, s, d), F32), jax.ShapeDtypeStruct((b, s, d), F32),
                   jax.ShapeDtypeStruct((b, s, ROUTE_LANES), jnp.int32),
                   jax.ShapeDtypeStruct((b, s, ROUTE_LANES), F32),
                   jax.ShapeDtypeStruct((SUBLANES, LANES), jnp.int32)],
        grid=(b, s // tm),
        in_specs=[lat(ATTN_WIDTH), allrows(HGRN_WIDTH), allrows(HGRN_WIDTH), allrows(HGRN_WIDTH),
                  lat(d), mod(), mod(), mod(),
                  const((1, ATTN_WIDTH)), const((1, HGRN_WIDTH)), const((1, d)),
                  const((ATTN_WIDTH + HGRN_WIDTH, d)), const((LANES, LANES)),
                  const((d, LANES)), const((d, LANES)), const((1, LANES))],
        out_specs=[lat(d), lat(d), lat(ROUTE_LANES), lat(ROUTE_LANES),
                   const((SUBLANES, LANES))],
        scratch_shapes=[pltpu.VMEM((SUBLANES, LANES), F32)],
        compiler_params=_params("arbitrary", "arbitrary"),
        name="out_proj",
    )(oa, o_f, o_b, gr, x, gate, shift, scale, ag, hg, g2, w_out, ones_bd, wr_hi, wr_lo, br)


def _gather_rows(idx_ref, n_rows, src_hbm, dst, sem):
    def body(r, carry):
        pltpu.make_async_copy(src_hbm.at[pl.ds(idx_ref[0, 0, r], 1)], dst.at[pl.ds(r, 1)],
                              sem).start()
        return carry
    lax.fori_loop(0, n_rows, body, 0)


def _wait_rows(src_hbm, dst, sem):
    pltpu.make_async_copy(src_hbm.at[pl.ds(0, dst.shape[0])], dst, sem).wait()


def _experts_kernel(meta_ref, tok_cur_ref, tok_next_ref, h2_hbm, wg_ref, wu_ref, wd_ref, ys_ref,
                    xbuf, sem):
    j = pl.program_id(0)
    n_used = meta_ref[0]
    slot = j % 2

    @pl.when((j == 0) & (n_used > 0))
    def _():
        _gather_rows(tok_cur_ref, MOE_BLOCK, h2_hbm, xbuf.at[0], sem.at[0])

    @pl.when(j + 1 < n_used)
    def _():
        _gather_rows(tok_next_ref, MOE_BLOCK, h2_hbm, xbuf.at[1 - slot], sem.at[1 - slot])

    @pl.when(j < n_used)
    def _():
        _wait_rows(h2_hbm, xbuf.at[slot], sem.at[slot])
        xb = xbuf[slot].astype(BF16)
        a = _silu(_dot(xb, wg_ref[0])) * _dot(xb, wu_ref[0])
        ys_ref[...] = _dot(a.astype(BF16), wd_ref[0])

    @pl.when(j >= n_used)
    def _():
        ys_ref[...] = jnp.zeros_like(ys_ref)


def _experts(meta, slot_tok, h2_rows, wg, wu, wd):
    n_blk = slot_tok.shape[0]
    d = h2_rows.shape[1]
    ff = wg.shape[2]

    def wspec(shape):
        return pl.BlockSpec((1,) + shape, lambda j, meta: (meta[1 + j], 0, 0))

    tok_blk = (1, 1, MOE_BLOCK)
    return pl.pallas_call(
        _experts_kernel,
        out_shape=jax.ShapeDtypeStruct((n_blk * MOE_BLOCK, d), F32),
        grid_spec=pltpu.PrefetchScalarGridSpec(
            num_scalar_prefetch=1,
            grid=(n_blk,),
            in_specs=[
                pl.BlockSpec(tok_blk, lambda j, meta: (j, 0, 0), memory_space=pltpu.SMEM),
                pl.BlockSpec(tok_blk, lambda j, meta: (jnp.minimum(j + 1, n_blk - 1), 0, 0),
                             memory_space=pltpu.SMEM),
                pl.BlockSpec(memory_space=pl.ANY),
                wspec((d, ff)), wspec((d, ff)), wspec((ff, d)),
            ],
            out_specs=pl.BlockSpec((MOE_BLOCK, d), lambda j, meta: (j, 0)),
            scratch_shapes=[pltpu.VMEM((2, MOE_BLOCK, d), F32), pltpu.SemaphoreType.DMA((2,))],
        ),
        compiler_params=_params("arbitrary"),
        name="experts",
    )(meta, slot_tok, slot_tok, h2_rows, wg, wu, wd)


def _combine_kernel(dest_ref, ys_hbm, x1_ref, gate_ref, rw_ref, o_ref, buf, sem):
    tm = x1_ref.shape[1]
    for k in range(TOP_K):
        def body(r, carry, k=k):
            pltpu.make_async_copy(ys_hbm.at[pl.ds(dest_ref[0, 0, k * tm + r], 1)],
                                  buf.at[k, pl.ds(r, 1)], sem.at[k]).start()
            return carry
        lax.fori_loop(0, tm, body, 0)
    rw = rw_ref[0]
    for k in range(TOP_K):
        _wait_rows(ys_hbm, buf.at[k], sem.at[k])
    moe = rw[:, 0:1] * buf[0] + rw[:, 1:2] * buf[1]
    o_ref[0] = x1_ref[0] + gate_ref[0] * moe


def _combine(dest_tiles, ys, x1, gate, rw):
    b, s, d = x1.shape
    tm = TOKEN_TILE
    nt = s // tm
    return pl.pallas_call(
        _combine_kernel,
        out_shape=jax.ShapeDtypeStruct((b, s, d), F32),
        grid=(b, nt),
        in_specs=[
            pl.BlockSpec((1, 1, TOP_K * tm), lambda bi, i: (bi * nt + i, 0, 0),
                         memory_space=pltpu.SMEM),
            pl.BlockSpec(memory_space=pl.ANY),
            pl.BlockSpec((1, tm, d), lambda bi, i: (bi, i, 0)),
            pl.BlockSpec((1, 1, d), lambda bi, i: (bi, 0, 0)),
            pl.BlockSpec((1, tm, ROUTE_LANES), lambda bi, i: (bi, i, 0)),
        ],
        out_specs=pl.BlockSpec((1, tm, d), lambda bi, i: (bi, i, 0)),
        scratch_shapes=[pltpu.VMEM((TOP_K, tm, d), F32), pltpu.SemaphoreType.DMA((TOP_K,))],
        compiler_params=_params("arbitrary", "arbitrary"),
        name="combine",
    )(dest_tiles, ys, x1, gate, rw)


def _layer(x, ctx, c, c_ctx, w_ada, b_ada, norm1_g, norm2_g, w_in, q_norm_g, k_norm_g, attn_out_g,
           lb, hgrn_out_g, w_out, w_router_grp, b_router_grp, w_router_exp, b_router_exp,
           w_exp_gate, w_exp_up, w_exp_down):
    b, s, d = x.shape
    n_ctx = ctx.shape[1]
    assert n_ctx % TOKEN_TILE == 0 and s % TOKEN_TILE == 0 and s % GRID_W == 0
    assert n_ctx % HGRN_CHUNK == 0 and (n_ctx + s) % ATTN_KV_TILE == 0
    n_all = n_ctx + s

    cond = jnp.zeros((2 * SUBLANES, d), F32).at[:b].set(c).at[b].set(c_ctx)
    assert b + 1 <= cond.shape[0]
    mods = _adaln(cond, w_ada, b_ada)[:b + 1].reshape(b + 1, 1, 6, d)
    sh1, sc1, gt1, sh2, sc2, gt2 = (mods[:, :, m] for m in range(6))

    scale_q = ATTN_HEAD_DIM ** -0.5
    qkg = jnp.concatenate([jnp.tile(q_norm_g, ATTN_HEADS) * scale_q,
                           jnp.tile(k_norm_g, ATTN_KV_HEADS)]).reshape(1, _QK_WIDTH)
    cos, sin = _rope_tables(n_ctx, s)
    qa, ka, va, qr, ff, fb, ir, gr = _in_proj(
        ctx, x, sh1, sc1, norm1_g.reshape(1, d), w_in.astype(BF16), qkg, cos, sin)

    def kv_heads(t):
        return t.reshape(b, n_all, ATTN_KV_HEADS, ATTN_HEAD_DIM).transpose(0, 2, 1, 3)

    oa = _attention(qa, kv_heads(ka), kv_heads(va), n_ctx)
    o_f, o_b = _hgrn(qr, ir, ff, fb, lb, n_ctx)

    w_router = jnp.zeros((d, LANES), F32)
    w_router = w_router.at[:, :N_EXPERTS].set(w_router_exp)
    w_router = w_router.at[:, _ROUTE_GROUP_LANE0:_ROUTE_GROUP_LANE0 + N_GROUPS].set(w_router_grp)
    b_router = jnp.zeros((1, LANES), F32)
    b_router = b_router.at[0, :N_EXPERTS].set(b_router_exp)
    b_router = b_router.at[0, _ROUTE_GROUP_LANE0:_ROUTE_GROUP_LANE0 + N_GROUPS].set(b_router_grp)
    wr_hi, wr_lo = _split_bf16(w_router)
    x1, h2, ri, rw, counts = _out_proj(
        oa, o_f, o_b, gr, x, gt1[:b], sh2[:b], sc2[:b], attn_out_g.reshape(1, -1),
        hgrn_out_g.reshape(1, -1), norm2_g.reshape(1, d), w_out.astype(BF16), wr_hi, wr_lo,
        b_router, n_ctx)

    n_tok = b * s
    counts = counts[0, :N_EXPERTS]
    padded = (counts + MOE_BLOCK - 1) // MOE_BLOCK * MOE_BLOCK
    pend = jnp.cumsum(padded)
    pstart = pend - padded
    n_blk = n_tok * TOP_K // MOE_BLOCK + N_EXPERTS
    ri = ri.reshape(n_tok, ROUTE_LANES)
    e_id = ri[:, :TOP_K]
    dest = pstart[e_id] + ri[:, TOP_K:2 * TOP_K]
    tok = jnp.broadcast_to(jnp.arange(n_tok, dtype=jnp.int32)[:, None], dest.shape)
    slot_tok = jnp.zeros((n_blk * MOE_BLOCK,), jnp.int32).at[dest.reshape(-1)].set(tok.reshape(-1))
    blk_start = jnp.arange(n_blk, dtype=jnp.int32) * MOE_BLOCK
    blk_e = jnp.minimum(jnp.sum(blk_start[:, None] >= pend[None, :], axis=1), N_EXPERTS - 1)
    n_used = pend[-1] // MOE_BLOCK
    meta = jnp.concatenate([n_used[None], blk_e]).astype(jnp.int32)

    ys = _experts(meta, slot_tok.reshape(n_blk, 1, MOE_BLOCK), h2.reshape(n_tok, d),
                  w_exp_gate.astype(BF16), w_exp_up.astype(BF16), w_exp_down.astype(BF16))

    nt = n_tok // TOKEN_TILE
    dest_tiles = dest.reshape(nt, TOKEN_TILE, TOP_K).transpose(0, 2, 1).reshape(nt, 1, -1)
    return _combine(dest_tiles, ys, x1, gt2[:b], rw)


def kernel(x, c, ctx, c_ctx, w_ada, b_ada, norm1_g, norm2_g, w_in, q_norm_g, k_norm_g, attn_out_g,
           hgrn_lb, hgrn_out_g, w_out, w_router_grp, b_router_grp, w_router_exp, b_router_exp,
           w_exp_gate, w_exp_up, w_exp_down):
    depth = w_in.shape[0]
    assert depth == 1, "context stream update between layers is not implemented"
    lb_all = jnp.cumsum(jax.nn.softmax(hgrn_lb.astype(F32), axis=1), axis=1)
    layer = 0
    return _layer(x, ctx, c, c_ctx, w_ada[layer], b_ada[layer], norm1_g[layer], norm2_g[layer],
                  w_in[layer], q_norm_g[layer], k_norm_g[layer], attn_out_g[layer],
                  lb_all[:, layer], hgrn_out_g[layer], w_out[layer], w_router_grp[layer],
                  b_router_grp[layer], w_router_exp[layer], b_router_exp[layer],
                  w_exp_gate[layer], w_exp_up[layer], w_exp_down[layer])
```

```python
import functools

import numpy as np
import jax
import jax.numpy as jnp
from jax import lax
from jax.experimental import pallas as pl
from jax.experimental.pallas import tpu as pltpu

F32 = jnp.float32
BF16 = jnp.bfloat16

GRID_W = 64
EPS = 1e-6
ATTN_HEADS = 8
ATTN_KV_HEADS = 2
ATTN_HEAD_DIM = 64
ATTN_GROUP = ATTN_HEADS // ATTN_KV_HEADS
ATTN_WIDTH = ATTN_HEADS * ATTN_HEAD_DIM
KV_WIDTH = ATTN_KV_HEADS * ATTN_HEAD_DIM
ROPE_THETA = 10000.0
HGRN_HEADS = 4
HGRN_HEAD_DIM = 128
HGRN_WIDTH = HGRN_HEADS * HGRN_HEAD_DIM
N_GROUPS = 4
EXPERTS_PER_GROUP = 8
N_EXPERTS = N_GROUPS * EXPERTS_PER_GROUP
TOP_K = 2
EXPERT_FF = 512

LANES = 128
SUBLANES = 8
MXU_DIM = 256
VMEM_LIMIT_BYTES = 48 * 1024 * 1024

TOKEN_TILE = 256
ATTN_Q_TILE = 256
ATTN_KV_TILE = 256
HGRN_CHUNK = 64
HGRN_DIAG = 8
MOE_BLOCK = 256
GATHER_UNROLL = 8
ROUTE_LANES = 8
NEG_BIG = -1e30

_QA0 = 0
_KA0 = _QA0 + ATTN_WIDTH
_VA0 = _KA0 + KV_WIDTH
_QR0 = _VA0 + KV_WIDTH
_FF0 = _QR0 + HGRN_WIDTH
_FB0 = _FF0 + HGRN_WIDTH
_IR0 = _FB0 + HGRN_WIDTH
_GR0 = _IR0 + HGRN_WIDTH
_QK_WIDTH = ATTN_WIDTH + KV_WIDTH


def _dot(a, b):
    return jnp.dot(a, b, preferred_element_type=F32)


def _dot_nt(a, b):
    return lax.dot_general(a, b, (((1,), (1,)), ((), ())), preferred_element_type=F32)


def _dot_tn(a, b):
    return lax.dot_general(a, b, (((0,), (0,)), ((), ())), preferred_element_type=F32)


def _split_bf16(x):
    hi = x.astype(BF16)
    lo = (x - hi.astype(F32)).astype(BF16)
    return hi, lo


def _sigmoid(x):
    return 1.0 / (1.0 + jnp.exp(-x))


def _silu(x):
    return x * _sigmoid(x)


def _params(*sem):
    return pltpu.CompilerParams(dimension_semantics=sem, vmem_limit_bytes=VMEM_LIMIT_BYTES)


def _head_sum_matrix():
    idx = np.arange(LANES) // ATTN_HEAD_DIM
    return jnp.asarray(idx[:, None] == idx[None, :], dtype=BF16)


def _head_rms_scale(x, ones_bd):
    ssq = _dot((x * x).astype(BF16), ones_bd)
    return lax.rsqrt(ssq * (1.0 / ATTN_HEAD_DIM) + EPS)


def _adaln_kernel(cond_ref, w_ref, b_ref, o_ref):
    s = _silu(cond_ref[...])
    s_hi, s_lo = _split_bf16(s)
    w_hi, w_lo = _split_bf16(w_ref[...])
    o_ref[...] = _dot(s_hi, w_hi) + _dot(s_lo, w_hi) + _dot(s_hi, w_lo) + b_ref[...]


def _adaln(cond, w_ada, b_ada):
    rows, d = cond.shape
    n = w_ada.shape[1]
    tn = n // 6
    return pl.pallas_call(
        _adaln_kernel,
        out_shape=jax.ShapeDtypeStruct((rows, n), F32),
        grid=(n // tn,),
        in_specs=[pl.BlockSpec((rows, d), lambda j: (0, 0)),
                  pl.BlockSpec((d, tn), lambda j: (0, j)),
                  pl.BlockSpec((1, tn), lambda j: (0, j))],
        out_specs=pl.BlockSpec((rows, tn), lambda j: (0, j)),
        compiler_params=_params("arbitrary"),
        name="adaln",
    )(cond, w_ada, b_ada.reshape(1, n))


def _rope_tables(n_ctx, n_lat):
    half = ATTN_HEAD_DIM // 2
    freqs = ROPE_THETA ** (-np.arange(0, half, 2, dtype=np.float64) / half)
    tok = np.arange(n_lat)
    pos = np.stack([tok // GRID_W, tok % GRID_W], axis=1).astype(np.float64)
    lane = np.arange(ATTN_HEAD_DIM)
    axis = lane // half
    fi = (lane % half) // 2
    ang = pos[:, axis] * freqs[fi][None, :]
    sign = np.where(lane % 2 == 1, 1.0, -1.0)
    cos = np.concatenate([np.ones((n_ctx, ATTN_HEAD_DIM)), np.cos(ang)], axis=0)
    sin = np.concatenate([np.zeros((n_ctx, ATTN_HEAD_DIM)), np.sin(ang) * sign], axis=0)
    reps = LANES // ATTN_HEAD_DIM
    return (jnp.asarray(np.tile(cos, (1, reps)), F32), jnp.asarray(np.tile(sin, (1, reps)), F32))


def _in_proj_kernel(n_ctx_tiles, ctx_ref, x_ref, shift_ref, scale_ref, g1_ref, w_ref, qkg_ref,
                    ones_ref, cos_ref, sin_ref,
                    qa_ref, ka_ref, va_ref, qr_ref, ff_ref, fb_ref, ir_ref, gr_ref):
    i = pl.program_id(1)
    xt = jnp.where(i < n_ctx_tiles, ctx_ref[0], x_ref[0])
    ms = jnp.mean(xt * xt, axis=-1, keepdims=True)
    h = xt * lax.rsqrt(ms + EPS) * g1_ref[...]
    h = h * (1.0 + scale_ref[0]) + shift_ref[0]
    p = _dot(h.astype(BF16), w_ref[...])

    ones_bd = ones_ref[...]
    cos = cos_ref[...]
    sin = sin_ref[...]
    even = lax.broadcasted_iota(jnp.int32, cos.shape, 1) % 2 == 0
    slabs = []
    for c0 in range(0, _QK_WIDTH, LANES):
        t = p[:, _QA0 + c0:_QA0 + c0 + LANES]
        t = t * _head_rms_scale(t, ones_bd) * qkg_ref[:, c0:c0 + LANES]
        partner = jnp.where(even, pltpu.roll(t, LANES - 1, 1), pltpu.roll(t, 1, 1))
        slabs.append((t * cos + partner * sin).astype(BF16))
    qa_ref[0] = jnp.concatenate(slabs[:ATTN_WIDTH // LANES], axis=1)
    ka_ref[0] = jnp.concatenate(slabs[ATTN_WIDTH // LANES:], axis=1)
    va_ref[0] = p[:, _VA0:_VA0 + KV_WIDTH].astype(BF16)

    qr_ref[0] = (_silu(p[:, _QR0:_QR0 + HGRN_WIDTH]) * (HGRN_HEAD_DIM ** -0.5)).astype(BF16)
    ff_ref[0] = p[:, _FF0:_FF0 + HGRN_WIDTH]
    fb_ref[0] = p[:, _FB0:_FB0 + HGRN_WIDTH]
    ir_ref[0] = p[:, _IR0:_IR0 + HGRN_WIDTH].astype(BF16)
    gr_ref[0] = _silu(p[:, _GR0:_GR0 + HGRN_WIDTH]).astype(BF16)


def _in_proj(ctx, x, shift, scale, g1, w_in, qkg, cos, sin):
    b, n_ctx, d = ctx.shape
    s = x.shape[1]
    tm = TOKEN_TILE
    nct = n_ctx // tm
    n_all = n_ctx + s
    nt = n_all // tm
    pw = w_in.shape[1]
    ones_bd = _head_sum_matrix()

    def tok_spec(w):
        return pl.BlockSpec((1, tm, w), lambda bi, i: (bi, i, 0))

    mod_spec = pl.BlockSpec((1, 1, d), lambda bi, i: (jnp.where(i < nct, b, bi), 0, 0))
    outs = [(ATTN_WIDTH, BF16), (KV_WIDTH, BF16), (KV_WIDTH, BF16), (HGRN_WIDTH, BF16),
            (HGRN_WIDTH, F32), (HGRN_WIDTH, F32), (HGRN_WIDTH, BF16), (HGRN_WIDTH, BF16)]
    return pl.pallas_call(
        functools.partial(_in_proj_kernel, nct),
        out_shape=[jax.ShapeDtypeStruct((b, n_all, w), dt) for w, dt in outs],
        grid=(b, nt),
        in_specs=[
            pl.BlockSpec((1, tm, d), lambda bi, i: (bi, jnp.minimum(i, nct - 1), 0)),
            pl.BlockSpec((1, tm, d), lambda bi, i: (bi, jnp.maximum(i - nct, 0), 0)),
            mod_spec, mod_spec,
            pl.BlockSpec((1, d), lambda bi, i: (0, 0)),
            pl.BlockSpec((d, pw), lambda bi, i: (0, 0)),
            pl.BlockSpec((1, _QK_WIDTH), lambda bi, i: (0, 0)),
            pl.BlockSpec((LANES, LANES), lambda bi, i: (0, 0)),
            pl.BlockSpec((tm, LANES), lambda bi, i: (i, 0)),
            pl.BlockSpec((tm, LANES), lambda bi, i: (i, 0)),
        ],
        out_specs=[tok_spec(w) for w, _ in outs],
        compiler_params=_params("arbitrary", "arbitrary"),
        name="in_proj",
    )(ctx, x, shift, scale, g1, w_in, qkg, ones_bd, cos, sin)


def _attention_kernel(n_kv_tiles, q_ref, k_ref, vt_ref, o_ref, qs_ref, s_ref):
    tq = q_ref.shape[1]
    cols = ATTN_GROUP * tq
    for h in range(ATTN_GROUP):
        qs_ref[h * tq:(h + 1) * tq, :] = q_ref[0, :, h * ATTN_HEAD_DIM:(h + 1) * ATTN_HEAD_DIM]

    def tile(j):
        return pl.ds(pl.multiple_of(j * ATTN_KV_TILE, ATTN_KV_TILE), ATTN_KV_TILE)

    def score(j, slot):
        s_ref[slot] = _dot_nt(k_ref[0, 0, tile(j), :], qs_ref[...])

    def absorb(j, slot, carry):
        m, acc = carry
        s = s_ref[slot]
        m_new = jnp.maximum(m, jnp.max(s, axis=0, keepdims=True))
        p = jnp.exp2(s - m_new).astype(BF16)
        acc = jnp.exp2(m - m_new) * acc + _dot(vt_ref[0, 0, :, tile(j)], p)
        return m_new, acc

    def pair(i, carry):
        score(2 * i + 1, 1)
        carry = absorb(2 * i, 0, carry)
        score(2 * i + 2, 0)
        return absorb(2 * i + 1, 1, carry)

    init = (jnp.full((1, cols), NEG_BIG, F32), jnp.zeros((vt_ref.shape[2], cols), F32))
    n_pairs = (n_kv_tiles - 1) // 2
    score(0, 0)
    carry = lax.fori_loop(0, n_pairs, pair, init)
    if n_kv_tiles % 2 == 1:
        carry = absorb(n_kv_tiles - 1, 0, carry)
    else:
        score(n_kv_tiles - 1, 1)
        carry = absorb(n_kv_tiles - 2, 0, carry)
        carry = absorb(n_kv_tiles - 1, 1, carry)
    acc = carry[1]
    o = acc[:ATTN_HEAD_DIM] / acc[ATTN_HEAD_DIM:ATTN_HEAD_DIM + 1]
    o_ref[0] = jnp.concatenate(
        [o[:, h * tq:(h + 1) * tq].T for h in range(ATTN_GROUP)], axis=1).astype(o_ref.dtype)


def _attention(qa, k_heads, vt_heads, n_ctx):
    b, n_all, _ = qa.shape
    s = n_all - n_ctx
    tq = ATTN_Q_TILE
    q_off = n_ctx // tq
    gw = ATTN_GROUP * ATTN_HEAD_DIM
    vt_rows = vt_heads.shape[2]
    return pl.pallas_call(
        functools.partial(_attention_kernel, n_all // ATTN_KV_TILE),
        out_shape=jax.ShapeDtypeStruct((b, s, ATTN_WIDTH), BF16),
        grid=(b, ATTN_KV_HEADS, s // tq),
        in_specs=[pl.BlockSpec((1, tq, gw), lambda bi, kv, i: (bi, i + q_off, kv)),
                  pl.BlockSpec((1, 1, n_all, ATTN_HEAD_DIM), lambda bi, kv, i: (bi, kv, 0, 0)),
                  pl.BlockSpec((1, 1, vt_rows, n_all), lambda bi, kv, i: (bi, kv, 0, 0))],
        out_specs=pl.BlockSpec((1, tq, gw), lambda bi, kv, i: (bi, i, kv)),
        scratch_shapes=[pltpu.VMEM((ATTN_GROUP * tq, ATTN_HEAD_DIM), BF16),
                        pltpu.VMEM((2, ATTN_KV_TILE, ATTN_GROUP * tq), F32)],
        compiler_params=_params("arbitrary", "arbitrary", "arbitrary"),
        name="attention",
    )(qa, k_heads, vt_heads)


def _hgrn_masks(reverse):
    c = HGRN_CHUNK
    t = lax.broadcasted_iota(jnp.int32, (c, c), 0)
    u = lax.broadcasted_iota(jnp.int32, (c, c), 1)
    tri = (u >= t) if reverse else (u <= t)
    levels = []
    size = c // 2
    while size >= HGRN_DIAG:
        same_parent = (t // (2 * size)) == (u // (2 * size))
        t_late = (t // size) % 2 == 1
        u_late = (u // size) % 2 == 1
        pair = (u_late & ~t_late) if reverse else (t_late & ~u_late)
        levels.append((size, same_parent & pair))
        size //= 2
    return tri.astype(BF16), levels


def _hgrn_chunk(q, fr, v, lb, st, reverse, tri, levels):
    c = HGRN_CHUNK
    f = lb + (1.0 - lb) * _sigmoid(fr)
    k = 1.0 - f
    g = jnp.log(f)
    g_hi, g_lo = _split_bf16(g)
    bcum = _dot(tri, g_hi) + _dot(tri, g_lo)
    qf = q.astype(F32)

    end = 0 if reverse else c - 1
    b_end = bcum[end:end + 1, :]
    inter = _dot_nt((qf * jnp.exp(bcum)).astype(BF16), st.astype(BF16))
    ke = (k * jnp.exp(b_end - bcum)).astype(BF16)
    st_new = st * jnp.exp(b_end) + _dot_tn(v, ke)

    row = lax.broadcasted_iota(jnp.int32, (c, 1), 0)
    a = jnp.zeros((c, c), F32)
    for size, mask in levels:
        pieces = []
        for p0 in range(0, c, 2 * size):
            r = p0 + size if reverse else p0 + size - 1
            pieces.append(jnp.broadcast_to(bcum[r:r + 1, :], (2 * size, bcum.shape[1])))
        ref = jnp.concatenate(pieces, axis=0)
        late = (row // size) % 2 == 1
        q_side = ~late if reverse else late
        ql = qf * jnp.exp(jnp.where(q_side, bcum - ref, NEG_BIG))
        kl = k * jnp.exp(jnp.where(q_side, NEG_BIG, ref - bcum))
        a = a + jnp.where(mask, _dot_nt(ql.astype(BF16), kl.astype(BF16)), 0.0)

    sub = lax.broadcasted_iota(jnp.int32, (HGRN_DIAG, 1), 0)
    lane = lax.broadcasted_iota(jnp.int32, (HGRN_DIAG, c), 1)
    diag_rows = []
    for blk in range(c // HGRN_DIAG):
        r0 = blk * HGRN_DIAG
        qb = qf[r0:r0 + HGRN_DIAG]
        bb = bcum[r0:r0 + HGRN_DIAG]
        rows_acc = jnp.zeros((HGRN_DIAG, c), F32)
        for si in range(HGRN_DIAG):
            r = r0 + si
            valid = (sub <= si) if reverse else (sub >= si)
            e = jnp.where(valid, bb - bcum[r:r + 1, :], NEG_BIG)
            col = jnp.sum(qb * k[r:r + 1, :] * jnp.exp(e), axis=-1, keepdims=True)
            rows_acc = jnp.where(lane == r, col, rows_acc)
        diag_rows.append(rows_acc)
    a = a + jnp.concatenate(diag_rows, axis=0)

    o = inter + _dot(a.astype(BF16), v)
    return o, st_new


def _hgrn_kernel(qf_ref, if_ref, ff_ref, qb_ref, ib_ref, fb_ref, lb_ref, of_ref, ob_ref,
                 sf_ref, sb_ref):
    @pl.when(pl.program_id(1) == 0)
    def _():
        sf_ref[...] = jnp.zeros_like(sf_ref)
        sb_ref[...] = jnp.zeros_like(sb_ref)

    dirs = ((False, qf_ref, if_ref, ff_ref, of_ref, sf_ref),
            (True, qb_ref, ib_ref, fb_ref, ob_ref, sb_ref))
    for d, (reverse, q_ref, i_ref, f_ref, o_ref, s_ref) in enumerate(dirs):
        tri, levels = _hgrn_masks(reverse)
        outs = []
        for h in range(HGRN_HEADS):
            sl = slice(h * HGRN_HEAD_DIM, (h + 1) * HGRN_HEAD_DIM)
            o, st = _hgrn_chunk(q_ref[0, :, sl], f_ref[0, :, sl], i_ref[0, :, sl],
                                lb_ref[d:d + 1, sl], s_ref[h], reverse, tri, levels)
            s_ref[h] = st
            outs.append(o)
        o_ref[0] = jnp.concatenate(outs, axis=1)


def _hgrn(qr, ir, ff, fb, lb, n_ctx):
    b, n_all, w = qr.shape
    c = HGRN_CHUNK
    nc = n_all // c
    ncc = n_ctx // c

    def fwd(bi, j):
        return (bi, j, 0)

    def bwd(bi, j):
        return (bi, jnp.where(j < ncc, ncc - 1 - j, nc - 1 - (j - ncc)), 0)

    blk = (1, c, w)
    state = pltpu.VMEM((HGRN_HEADS, HGRN_HEAD_DIM, HGRN_HEAD_DIM), F32)
    return pl.pallas_call(
        _hgrn_kernel,
        out_shape=[jax.ShapeDtypeStruct((b, n_all, w), F32)] * 2,
        grid=(b, nc),
        in_specs=[pl.BlockSpec(blk, fwd), pl.BlockSpec(blk, fwd), pl.BlockSpec(blk, fwd),
                  pl.BlockSpec(blk, bwd), pl.BlockSpec(blk, bwd), pl.BlockSpec(blk, bwd),
                  pl.BlockSpec((2, w), lambda bi, j: (0, 0))],
        out_specs=[pl.BlockSpec(blk, fwd), pl.BlockSpec(blk, bwd)],
        scratch_shapes=[state, state],
        compiler_params=_params("arbitrary", "arbitrary"),
        name="hgrn",
    )(qr, ir, ff, qr, ir, fb, lb)


_ROUTE_GROUP_LANE0 = N_EXPERTS


def _lane_min_index(cond, lane):
    return jnp.min(jnp.where(cond, lane, LANES), axis=-1, keepdims=True)


def _out_proj_kernel(oa_ref, of_ref, ob_ref, gr_ref, x_ref, gate_ref, shift_ref, scale_ref,
                     ag_ref, hg_ref, g2_ref, w_ref, ones_ref, wr_hi_ref, wr_lo_ref, br_ref,
                     x1_ref, h2_ref, ri_ref, rw_ref, cnt_ref, carry_ref):
    first = (pl.program_id(0) == 0) & (pl.program_id(1) == 0)

    @pl.when(first)
    def _():
        carry_ref[...] = jnp.zeros_like(carry_ref)

    tm = x_ref.shape[1]
    ones_bd = ones_ref[...]
    slabs = []
    for c0 in range(0, ATTN_WIDTH, LANES):
        t = oa_ref[0, :, c0:c0 + LANES].astype(F32)
        slabs.append((t * _head_rms_scale(t, ones_bd) * ag_ref[:, c0:c0 + LANES]).astype(BF16))
    orr = of_ref[0] + ob_ref[0]
    parts = []
    for h in range(HGRN_HEADS):
        sl = slice(h * HGRN_HEAD_DIM, (h + 1) * HGRN_HEAD_DIM)
        oh = orr[:, sl]
        parts.append(oh * lax.rsqrt(jnp.mean(oh * oh, axis=-1, keepdims=True) + EPS))
    orr = jnp.concatenate(parts, axis=1) * hg_ref[...] * gr_ref[0].astype(F32)
    mix_in = jnp.concatenate(slabs + [orr.astype(BF16)], axis=1)
    x1 = x_ref[0] + gate_ref[0] * _dot(mix_in, w_ref[...])
    x1_ref[0] = x1
    h2 = x1 * lax.rsqrt(jnp.mean(x1 * x1, axis=-1, keepdims=True) + EPS) * g2_ref[...]
    h2 = h2 * (1.0 + scale_ref[0]) + shift_ref[0]
    h2_ref[0] = h2

    h_hi, h_lo = _split_bf16(h2)
    logits = (_dot(h_hi, wr_hi_ref[...]) + _dot(h_lo, wr_hi_ref[...])
              + _dot(h_hi, wr_lo_ref[...]) + br_ref[...])
    lane = lax.broadcasted_iota(jnp.int32, logits.shape, 1)
    is_grp = (lane >= _ROUTE_GROUP_LANE0) & (lane < _ROUTE_GROUP_LANE0 + N_GROUPS)
    lg = jnp.where(is_grp, logits, NEG_BIG)
    mg = jnp.max(lg, axis=-1, keepdims=True)
    g_sel = _lane_min_index(lg == mg, lane) - _ROUTE_GROUP_LANE0
    pg_top = 1.0 / jnp.sum(jnp.exp(lg - mg), axis=-1, keepdims=True)
    in_grp = (lane < N_EXPERTS) & ((lane // EXPERTS_PER_GROUP) == g_sel)
    le = jnp.where(in_grp, logits, NEG_BIG)
    m1 = jnp.max(le, axis=-1, keepdims=True)
    e1 = _lane_min_index(le == m1, lane)
    le2 = jnp.where(lane == e1, NEG_BIG, le)
    m2 = jnp.max(le2, axis=-1, keepdims=True)
    e2 = _lane_min_index(le2 == m2, lane)
    r2 = jnp.exp(m2 - m1)
    w1 = pg_top / (1.0 + r2)
    w2 = pg_top * r2 / (1.0 + r2)

    onehot = ((lane == e1) | (lane == e2)).astype(BF16)
    rt = lax.broadcasted_iota(jnp.int32, (tm, tm), 0)
    ru = lax.broadcasted_iota(jnp.int32, (tm, tm), 1)
    before = _dot((ru < rt).astype(BF16), onehot) + carry_ref[0:1, :]
    rank1 = jnp.sum(jnp.where(lane == e1, before, 0.0), axis=-1, keepdims=True)
    rank2 = jnp.sum(jnp.where(lane == e2, before, 0.0), axis=-1, keepdims=True)
    total = carry_ref[0:1, :] + jnp.sum(onehot.astype(F32), axis=0, keepdims=True)
    carry_ref[...] = jnp.broadcast_to(total, carry_ref.shape)
    cnt_ref[...] = jnp.broadcast_to(total, cnt_ref.shape).astype(jnp.int32)

    rl = lax.broadcasted_iota(jnp.int32, (tm, ROUTE_LANES), 1)
    ri = jnp.where(rl == 0, e1, jnp.where(rl == 1, e2, jnp.where(
        rl == 2, rank1.astype(jnp.int32), jnp.where(rl == 3, rank2.astype(jnp.int32), 0))))
    ri_ref[0] = ri
    rw_ref[0] = jnp.where(rl == 0, w1, jnp.where(rl == 1, w2, 0.0))


def _out_proj(oa, o_f, o_b, gr, x, gate, shift, scale, ag, hg, g2, w_out, wr_hi, wr_lo, br, n_ctx):
    b, s, d = x.shape
    tm = TOKEN_TILE
    off = n_ctx // tm
    ones_bd = _head_sum_matrix()

    def lat(w):
        return pl.BlockSpec((1, tm, w), lambda bi, i: (bi, i, 0))

    def allrows(w):
        return pl.BlockSpec((1, tm, w), lambda bi, i: (bi, i + off, 0))

    def mod():
        return pl.BlockSpec((1, 1, d), lambda bi, i: (bi, 0, 0))

    def const(shape):
        return pl.BlockSpec(shape, lambda bi, i: (0,) * len(shape))

    return pl.pallas_call(
        _out_proj_kernel,
        out_shape=[jax.ShapeDtypeStruct((b, s, d), F32), jax.ShapeDtypeStruct((b, s, d), F32),
                   jax.ShapeDtypeStruct((b, s, ROUTE_LANES), jnp.int32),
                   jax.ShapeDtypeStruct((b, s, ROUTE_LANES), F32),
                   jax.ShapeDtypeStruct((SUBLANES, LANES), jnp.int32)],
        grid=(b, s // tm),
        in_specs=[lat(ATTN_WIDTH), allrows(HGRN_WIDTH), allrows(HGRN_WIDTH), allrows(HGRN_WIDTH),
                  lat(d), mod(), mod(), mod(),
                  const((1, ATTN_WIDTH)), const((1, HGRN_WIDTH)), const((1, d)),
                  const((ATTN_WIDTH + HGRN_WIDTH, d)), const((LANES, LANES)),
                  const((d, LANES)), const((d, LANES)), const((1, LANES))],
        out_specs=[lat(d), lat(d), lat(ROUTE_LANES), lat(ROUTE_LANES),
                   const((SUBLANES, LANES))],
        scratch_shapes=[pltpu.VMEM((SUBLANES, LANES), F32)],
        compiler_params=_params("arbitrary", "arbitrary"),
        name="out_proj",
    )(oa, o_f, o_b, gr, x, gate, shift, scale, ag, hg, g2, w_out, ones_bd, wr_hi, wr_lo, br)


def _gather_rows(idx_ref, idx0, n_rows, src_hbm, dst, sem):
    def body(r, carry):
        pltpu.make_async_copy(src_hbm.at[pl.ds(idx_ref[0, 0, idx0 + r], 1)],
                              dst.at[pl.ds(r, 1)], sem).start()
        return carry
    lax.fori_loop(0, n_rows, body, 0, unroll=GATHER_UNROLL)


def _wait_rows(src_hbm, dst, sem):
    pltpu.make_async_copy(src_hbm.at[pl.ds(0, dst.shape[0])], dst, sem).wait()


def _experts_kernel(meta_ref, tok_cur_ref, tok_next_ref, h2_hbm, wg_ref, wu_ref, wd_ref, ys_ref,
                    xbuf, sem):
    j = pl.program_id(0)
    n_used = meta_ref[0]
    slot = j % 2

    @pl.when((j == 0) & (n_used > 0))
    def _():
        _gather_rows(tok_cur_ref, 0, MOE_BLOCK, h2_hbm, xbuf.at[0], sem.at[0])

    @pl.when(j + 1 < n_used)
    def _():
        _gather_rows(tok_next_ref, 0, MOE_BLOCK, h2_hbm, xbuf.at[1 - slot], sem.at[1 - slot])

    @pl.when(j < n_used)
    def _():
        _wait_rows(h2_hbm, xbuf.at[slot], sem.at[slot])
        xb = xbuf[slot].astype(BF16)
        a = _silu(_dot(xb, wg_ref[0])) * _dot(xb, wu_ref[0])
        ys_ref[...] = _dot(a.astype(BF16), wd_ref[0])

    @pl.when(j >= n_used)
    def _():
        ys_ref[...] = jnp.zeros_like(ys_ref)


def _experts(meta, slot_tok, h2_rows, wg, wu, wd):
    n_blk = slot_tok.shape[0]
    d = h2_rows.shape[1]
    ff = wg.shape[2]

    def wspec(shape):
        return pl.BlockSpec((1,) + shape, lambda j, meta: (meta[1 + j], 0, 0))

    tok_blk = (1, 1, MOE_BLOCK)
    return pl.pallas_call(
        _experts_kernel,
        out_shape=jax.ShapeDtypeStruct((n_blk * MOE_BLOCK, d), F32),
        grid_spec=pltpu.PrefetchScalarGridSpec(
            num_scalar_prefetch=1,
            grid=(n_blk,),
            in_specs=[
                pl.BlockSpec(tok_blk, lambda j, meta: (j, 0, 0), memory_space=pltpu.SMEM),
                pl.BlockSpec(tok_blk, lambda j, meta: (jnp.minimum(j + 1, n_blk - 1), 0, 0),
                             memory_space=pltpu.SMEM),
                pl.BlockSpec(memory_space=pl.ANY),
                wspec((d, ff)), wspec((d, ff)), wspec((ff, d)),
            ],
            out_specs=pl.BlockSpec((MOE_BLOCK, d), lambda j, meta: (j, 0)),
            scratch_shapes=[pltpu.VMEM((2, MOE_BLOCK, d), F32), pltpu.SemaphoreType.DMA((2,))],
        ),
        compiler_params=_params("arbitrary"),
        name="experts",
    )(meta, slot_tok, slot_tok, h2_rows, wg, wu, wd)


def _combine_kernel(dest_cur_ref, dest_next_ref, ys_hbm, x1_ref, gate_ref, rw_ref, o_ref,
                    buf, sem):
    i = pl.program_id(0)
    slot = i % 2
    tm = x1_ref.shape[0]

    def start(idx_ref, sl):
        for k in range(TOP_K):
            _gather_rows(idx_ref, k * tm, tm, ys_hbm, buf.at[sl, k], sem.at[sl, k])

    @pl.when(i == 0)
    def _():
        start(dest_cur_ref, 0)

    @pl.when(i + 1 < pl.num_programs(0))
    def _():
        start(dest_next_ref, 1 - slot)

    rw = rw_ref[...]
    moe = None
    for k in range(TOP_K):
        _wait_rows(ys_hbm, buf.at[slot, k], sem.at[slot, k])
        term = rw[:, k:k + 1] * buf[slot, k]
        moe = term if moe is None else moe + term
    o_ref[...] = x1_ref[...] + gate_ref[0] * moe


def _combine(dest_tiles, ys, x1, gate, rw):
    n_tok, d = x1.shape
    tm = TOKEN_TILE
    nt = n_tok // tm
    per_batch = nt // gate.shape[0]
    idx_blk = (1, 1, TOP_K * tm)
    return pl.pallas_call(
        _combine_kernel,
        out_shape=jax.ShapeDtypeStruct((n_tok, d), F32),
        grid=(nt,),
        in_specs=[
            pl.BlockSpec(idx_blk, lambda i: (i, 0, 0), memory_space=pltpu.SMEM),
            pl.BlockSpec(idx_blk, lambda i: (jnp.minimum(i + 1, nt - 1), 0, 0),
                         memory_space=pltpu.SMEM),
            pl.BlockSpec(memory_space=pl.ANY),
            pl.BlockSpec((tm, d), lambda i: (i, 0)),
            pl.BlockSpec((1, 1, d), lambda i: (i // per_batch, 0, 0)),
            pl.BlockSpec((tm, ROUTE_LANES), lambda i: (i, 0)),
        ],
        out_specs=pl.BlockSpec((tm, d), lambda i: (i, 0)),
        scratch_shapes=[pltpu.VMEM((2, TOP_K, tm, d), F32),
                        pltpu.SemaphoreType.DMA((2, TOP_K))],
        compiler_params=_params("arbitrary"),
        name="combine",
    )(dest_tiles, dest_tiles, ys, x1, gate, rw)


def _layer(x, ctx, c, c_ctx, w_ada, b_ada, norm1_g, norm2_g, w_in, q_norm_g, k_norm_g, attn_out_g,
           lb, hgrn_out_g, w_out, w_router_grp, b_router_grp, w_router_exp, b_router_exp,
           w_exp_gate, w_exp_up, w_exp_down):
    b, s, d = x.shape
    n_ctx = ctx.shape[1]
    assert n_ctx % TOKEN_TILE == 0 and s % TOKEN_TILE == 0 and s % GRID_W == 0
    assert n_ctx % HGRN_CHUNK == 0 and (n_ctx + s) % ATTN_KV_TILE == 0
    n_all = n_ctx + s

    cond = jnp.zeros((2 * SUBLANES, d), F32).at[:b].set(c).at[b].set(c_ctx)
    assert b + 1 <= cond.shape[0]
    mods = _adaln(cond, w_ada, b_ada)[:b + 1].reshape(b + 1, 1, 6, d)
    sh1, sc1, gt1, sh2, sc2, gt2 = (mods[:, :, m] for m in range(6))

    scale_q = ATTN_HEAD_DIM ** -0.5 * np.log2(np.e)
    qkg = jnp.concatenate([jnp.tile(q_norm_g, ATTN_HEADS) * scale_q,
                           jnp.tile(k_norm_g, ATTN_KV_HEADS)]).reshape(1, _QK_WIDTH)
    cos, sin = _rope_tables(n_ctx, s)
    qa, ka, va, qr, ff, fb, ir, gr = _in_proj(
        ctx, x, sh1, sc1, norm1_g.reshape(1, d), w_in.astype(BF16), qkg, cos, sin)

    def kv_heads(t):
        return t.reshape(b, n_all, ATTN_KV_HEADS, ATTN_HEAD_DIM).transpose(0, 2, 1, 3)

    vt = kv_heads(va).transpose(0, 1, 3, 2)
    vt = jnp.concatenate([vt, jnp.ones((b, ATTN_KV_HEADS, 2 * SUBLANES, n_all), BF16)], axis=2)
    oa = _attention(qa, kv_heads(ka), vt, n_ctx)
    o_f, o_b = _hgrn(qr, ir, ff, fb, lb, n_ctx)

    w_router = jnp.zeros((d, LANES), F32)
    w_router = w_router.at[:, :N_EXPERTS].set(w_router_exp)
    w_router = w_router.at[:, _ROUTE_GROUP_LANE0:_ROUTE_GROUP_LANE0 + N_GROUPS].set(w_router_grp)
    b_router = jnp.zeros((1, LANES), F32)
    b_router = b_router.at[0, :N_EXPERTS].set(b_router_exp)
    b_router = b_router.at[0, _ROUTE_GROUP_LANE0:_ROUTE_GROUP_LANE0 + N_GROUPS].set(b_router_grp)
    wr_hi, wr_lo = _split_bf16(w_router)
    x1, h2, ri, rw, counts = _out_proj(
        oa, o_f, o_b, gr, x, gt1[:b], sh2[:b], sc2[:b], attn_out_g.reshape(1, -1),
        hgrn_out_g.reshape(1, -1), norm2_g.reshape(1, d), w_out.astype(BF16), wr_hi, wr_lo,
        b_router, n_ctx)

    n_tok = b * s
    counts = counts[0, :N_EXPERTS]
    padded = (counts + MOE_BLOCK - 1) // MOE_BLOCK * MOE_BLOCK
    pend = jnp.cumsum(padded)
    pstart = pend - padded
    n_blk = n_tok * TOP_K // MOE_BLOCK + N_EXPERTS
    ri = ri.reshape(n_tok, ROUTE_LANES)
    e_id = ri[:, :TOP_K]
    dest = pstart[e_id] + ri[:, TOP_K:2 * TOP_K]
    tok = jnp.broadcast_to(jnp.arange(n_tok, dtype=jnp.int32)[:, None], dest.shape)
    slot_tok = jnp.zeros((n_blk * MOE_BLOCK,), jnp.int32).at[dest.reshape(-1)].set(tok.reshape(-1))
    blk_start = jnp.arange(n_blk, dtype=jnp.int32) * MOE_BLOCK
    blk_e = jnp.minimum(jnp.sum(blk_start[:, None] >= pend[None, :], axis=1), N_EXPERTS - 1)
    n_used = pend[-1] // MOE_BLOCK
    meta = jnp.concatenate([n_used[None], blk_e]).astype(jnp.int32)

    ys = _experts(meta, slot_tok.reshape(n_blk, 1, MOE_BLOCK), h2.reshape(n_tok, d),
                  w_exp_gate.astype(BF16), w_exp_up.astype(BF16), w_exp_down.astype(BF16))

    nt = n_tok // TOKEN_TILE
    dest_tiles = dest.reshape(nt, TOKEN_TILE, TOP_K).transpose(0, 2, 1).reshape(nt, 1, -1)
    out = _combine(dest_tiles, ys, x1.reshape(n_tok, d), gt2[:b], rw.reshape(n_tok, ROUTE_LANES))
    return out.reshape(b, s, d)


def kernel(x, c, ctx, c_ctx, w_ada, b_ada, norm1_g, norm2_g, w_in, q_norm_g, k_norm_g, attn_out_g,
           hgrn_lb, hgrn_out_g, w_out, w_router_grp, b_router_grp, w_router_exp, b_router_exp,
           w_exp_gate, w_exp_up, w_exp_down):
    depth = w_in.shape[0]
    assert depth == 1, "context stream update between layers is not implemented"
    lb_all = jnp.cumsum(jax.nn.softmax(hgrn_lb.astype(F32), axis=1), axis=1)
    layer = 0
    return _layer(x, ctx, c, c_ctx, w_ada[layer], b_ada[layer], norm1_g[layer], norm2_g[layer],
                  w_in[layer], q_norm_g[layer], k_norm_g[layer], attn_out_g[layer],
                  lb_all[:, layer], hgrn_out_g[layer], w_out[layer], w_router_grp[layer],
                  b_router_grp[layer], w_router_exp[layer], b_router_exp[layer],
                  w_exp_gate[layer], w_exp_up[layer], w_exp_down[layer])
```

```python
import functools

import numpy as np
import jax
import jax.numpy as jnp
from jax import lax
from jax.experimental import pallas as pl
from jax.experimental.pallas import tpu as pltpu

F32 = jnp.float32
BF16 = jnp.bfloat16

GRID_W = 64
EPS = 1e-6
ATTN_HEADS = 8
ATTN_KV_HEADS = 2
ATTN_HEAD_DIM = 64
ATTN_GROUP = ATTN_HEADS // ATTN_KV_HEADS
ATTN_WIDTH = ATTN_HEADS * ATTN_HEAD_DIM
KV_WIDTH = ATTN_KV_HEADS * ATTN_HEAD_DIM
ROPE_THETA = 10000.0
HGRN_HEADS = 4
HGRN_HEAD_DIM = 128
HGRN_WIDTH = HGRN_HEADS * HGRN_HEAD_DIM
N_GROUPS = 4
EXPERTS_PER_GROUP = 8
N_EXPERTS = N_GROUPS * EXPERTS_PER_GROUP
TOP_K = 2
EXPERT_FF = 512

LANES = 128
SUBLANES = 8
MXU_DIM = 256
VMEM_LIMIT_BYTES = 48 * 1024 * 1024

TOKEN_TILE = 256
ATTN_Q_TILE = 512
ATTN_KV_TILE = 256
ATTN_KV_UNROLL = 4
ATTN_BOUND_SLACK = 1.02
ATTN_BOUND_MAX = 60.0
HGRN_CHUNK = 64
HGRN_DIAG = 8
MOE_BLOCK = 256
GATHER_UNROLL = 8
ROUTE_LANES = 8
NEG_BIG = -1e30

_QA0 = 0
_KA0 = _QA0 + ATTN_WIDTH
_VA0 = _KA0 + KV_WIDTH
_QR0 = _VA0 + KV_WIDTH
_FF0 = _QR0 + HGRN_WIDTH
_FB0 = _FF0 + HGRN_WIDTH
_IR0 = _FB0 + HGRN_WIDTH
_GR0 = _IR0 + HGRN_WIDTH
_QK_WIDTH = ATTN_WIDTH + KV_WIDTH


def _dot(a, b):
    return jnp.dot(a, b, preferred_element_type=F32)


def _dot_nt(a, b):
    return lax.dot_general(a, b, (((1,), (1,)), ((), ())), preferred_element_type=F32)


def _dot_tn(a, b):
    return lax.dot_general(a, b, (((0,), (0,)), ((), ())), preferred_element_type=F32)


def _split_bf16(x):
    hi = x.astype(BF16)
    lo = (x - hi.astype(F32)).astype(BF16)
    return hi, lo


def _sigmoid(x):
    return 1.0 / (1.0 + jnp.exp(-x))


def _silu(x):
    return x * _sigmoid(x)


def _params(*sem):
    return pltpu.CompilerParams(dimension_semantics=sem, vmem_limit_bytes=VMEM_LIMIT_BYTES)


def _head_sum_matrix():
    idx = np.arange(LANES) // ATTN_HEAD_DIM
    return jnp.asarray(idx[:, None] == idx[None, :], dtype=BF16)


def _head_rms_scale(x, ones_bd):
    ssq = _dot((x * x).astype(BF16), ones_bd)
    return lax.rsqrt(ssq * (1.0 / ATTN_HEAD_DIM) + EPS)


def _adaln_kernel(cond_ref, w_ref, b_ref, o_ref):
    s = _silu(cond_ref[...])
    s_hi, s_lo = _split_bf16(s)
    w_hi, w_lo = _split_bf16(w_ref[...])
    o_ref[...] = _dot(s_hi, w_hi) + _dot(s_lo, w_hi) + _dot(s_hi, w_lo) + b_ref[...]


def _adaln(cond, w_ada, b_ada):
    rows, d = cond.shape
    n = w_ada.shape[1]
    tn = n // 6
    return pl.pallas_call(
        _adaln_kernel,
        out_shape=jax.ShapeDtypeStruct((rows, n), F32),
        grid=(n // tn,),
        in_specs=[pl.BlockSpec((rows, d), lambda j: (0, 0)),
                  pl.BlockSpec((d, tn), lambda j: (0, j)),
                  pl.BlockSpec((1, tn), lambda j: (0, j))],
        out_specs=pl.BlockSpec((rows, tn), lambda j: (0, j)),
        compiler_params=_params("arbitrary"),
        name="adaln",
    )(cond, w_ada, b_ada.reshape(1, n))


def _rope_tables(n_ctx, n_lat):
    half = ATTN_HEAD_DIM // 2
    freqs = ROPE_THETA ** (-np.arange(0, half, 2, dtype=np.float64) / half)
    tok = np.arange(n_lat)
    pos = np.stack([tok // GRID_W, tok % GRID_W], axis=1).astype(np.float64)
    lane = np.arange(ATTN_HEAD_DIM)
    axis = lane // half
    fi = (lane % half) // 2
    ang = pos[:, axis] * freqs[fi][None, :]
    sign = np.where(lane % 2 == 1, 1.0, -1.0)
    cos = np.concatenate([np.ones((n_ctx, ATTN_HEAD_DIM)), np.cos(ang)], axis=0)
    sin = np.concatenate([np.zeros((n_ctx, ATTN_HEAD_DIM)), np.sin(ang) * sign], axis=0)
    reps = LANES // ATTN_HEAD_DIM
    return (jnp.asarray(np.tile(cos, (1, reps)), F32), jnp.asarray(np.tile(sin, (1, reps)), F32))


def _in_proj_kernel(n_ctx_tiles, ctx_ref, x_ref, shift_ref, scale_ref, g1_ref, w_ref, qkg_ref,
                    ones_ref, cos_ref, sin_ref,
                    qa_ref, ka_ref, va_ref, qr_ref, ff_ref, fb_ref, ir_ref, gr_ref):
    i = pl.program_id(1)
    xt = jnp.where(i < n_ctx_tiles, ctx_ref[0], x_ref[0])
    ms = jnp.mean(xt * xt, axis=-1, keepdims=True)
    h = xt * lax.rsqrt(ms + EPS) * g1_ref[...]
    h = h * (1.0 + scale_ref[0]) + shift_ref[0]
    p = _dot(h.astype(BF16), w_ref[...])

    ones_bd = ones_ref[...]
    cos = cos_ref[...]
    sin = sin_ref[...]
    even = lax.broadcasted_iota(jnp.int32, cos.shape, 1) % 2 == 0
    slabs = []
    for c0 in range(0, _QK_WIDTH, LANES):
        t = p[:, _QA0 + c0:_QA0 + c0 + LANES]
        t = t * _head_rms_scale(t, ones_bd) * qkg_ref[:, c0:c0 + LANES]
        partner = jnp.where(even, pltpu.roll(t, LANES - 1, 1), pltpu.roll(t, 1, 1))
        slabs.append((t * cos + partner * sin).astype(BF16))
    @pl.when(i >= n_ctx_tiles)
    def _():
        qa_ref[0] = jnp.concatenate(slabs[:ATTN_WIDTH // LANES], axis=1)

    ka_ref[0] = jnp.concatenate(slabs[ATTN_WIDTH // LANES:], axis=1)
    va_ref[0] = p[:, _VA0:_VA0 + KV_WIDTH].astype(BF16)

    qr_ref[0] = (_silu(p[:, _QR0:_QR0 + HGRN_WIDTH]) * (HGRN_HEAD_DIM ** -0.5)).astype(BF16)
    ff_ref[0] = p[:, _FF0:_FF0 + HGRN_WIDTH]
    fb_ref[0] = p[:, _FB0:_FB0 + HGRN_WIDTH]
    ir_ref[0] = p[:, _IR0:_IR0 + HGRN_WIDTH].astype(BF16)
    gr_ref[0] = _silu(p[:, _GR0:_GR0 + HGRN_WIDTH]).astype(BF16)


def _in_proj(ctx, x, shift, scale, g1, w_in, qkg, cos, sin):
    b, n_ctx, d = ctx.shape
    s = x.shape[1]
    tm = TOKEN_TILE
    nct = n_ctx // tm
    n_all = n_ctx + s
    nt = n_all // tm
    pw = w_in.shape[1]
    ones_bd = _head_sum_matrix()

    def tok_spec(w):
        return pl.BlockSpec((1, tm, w), lambda bi, i: (bi, i, 0))

    mod_spec = pl.BlockSpec((1, 1, d), lambda bi, i: (jnp.where(i < nct, b, bi), 0, 0))
    outs = [(ATTN_WIDTH, BF16), (KV_WIDTH, BF16), (KV_WIDTH, BF16), (HGRN_WIDTH, BF16),
            (HGRN_WIDTH, F32), (HGRN_WIDTH, F32), (HGRN_WIDTH, BF16), (HGRN_WIDTH, BF16)]
    lat_spec = pl.BlockSpec((1, tm, ATTN_WIDTH), lambda bi, i: (bi, jnp.maximum(i - nct, 0), 0))
    return pl.pallas_call(
        functools.partial(_in_proj_kernel, nct),
        out_shape=[jax.ShapeDtypeStruct((b, s if j == 0 else n_all, w), dt)
                   for j, (w, dt) in enumerate(outs)],
        grid=(b, nt),
        in_specs=[
            pl.BlockSpec((1, tm, d), lambda bi, i: (bi, jnp.minimum(i, nct - 1), 0)),
            pl.BlockSpec((1, tm, d), lambda bi, i: (bi, jnp.maximum(i - nct, 0), 0)),
            mod_spec, mod_spec,
            pl.BlockSpec((1, d), lambda bi, i: (0, 0)),
            pl.BlockSpec((d, pw), lambda bi, i: (0, 0)),
            pl.BlockSpec((1, _QK_WIDTH), lambda bi, i: (0, 0)),
            pl.BlockSpec((LANES, LANES), lambda bi, i: (0, 0)),
            pl.BlockSpec((tm, LANES), lambda bi, i: (i, 0)),
            pl.BlockSpec((tm, LANES), lambda bi, i: (i, 0)),
        ],
        out_specs=[lat_spec] + [tok_spec(w) for w, _ in outs[1:]],
        compiler_params=_params("arbitrary", "arbitrary"),
        name="in_proj",
    )(ctx, x, shift, scale, g1, w_in, qkg, ones_bd, cos, sin)


def _attention_kernel(n_kv_tiles, q_ref, k_ref, vt_ref, o_ref, qs_ref, s_ref, ksq_ref):
    tq = q_ref.shape[1]
    cols = ATTN_GROUP * tq
    for h in range(ATTN_GROUP):
        qs_ref[h * tq:(h + 1) * tq, :] = q_ref[0, :, h * ATTN_HEAD_DIM:(h + 1) * ATTN_HEAD_DIM]

    @pl.when(pl.program_id(2) == 0)
    def _():
        k = k_ref[0, 0].astype(F32)
        ksq = jnp.max(jnp.sum(k * k, axis=1, keepdims=True), axis=0, keepdims=True)
        ksq_ref[...] = jnp.broadcast_to(ksq, ksq_ref.shape)

    def tile(j):
        return pl.ds(pl.multiple_of(j * ATTN_KV_TILE, ATTN_KV_TILE), ATTN_KV_TILE)

    def scores(j):
        return _dot_nt(k_ref[0, 0, tile(j), :], qs_ref[...])

    def weighted(j, p):
        return _dot(vt_ref[0, 0, :, tile(j)], p)

    def finish(acc):
        o = acc[:ATTN_HEAD_DIM] / acc[ATTN_HEAD_DIM:ATTN_HEAD_DIM + 1]
        o_ref[0] = jnp.concatenate(
            [o[:, h * tq:(h + 1) * tq].T for h in range(ATTN_GROUP)], axis=1).astype(o_ref.dtype)

    qf = qs_ref[...].astype(F32)
    qsq = _dot_nt(jnp.ones((SUBLANES, ATTN_HEAD_DIM), BF16), (qf * qf).astype(BF16))[0:1]
    bound = jnp.sqrt(qsq * ksq_ref[0:1, 0:1]) * ATTN_BOUND_SLACK
    acc0 = jnp.zeros((vt_ref.shape[2], cols), F32)
    safe = jnp.max(bound) <= ATTN_BOUND_MAX

    @pl.when(safe)
    def _():
        def absorb(j, acc):
            return acc + weighted(j, jnp.exp2(scores(j) - bound).astype(BF16))

        def group(i, acc):
            for u in range(ATTN_KV_UNROLL):
                acc = absorb(ATTN_KV_UNROLL * i + u, acc)
            return acc

        n_groups = n_kv_tiles // ATTN_KV_UNROLL
        acc = lax.fori_loop(0, n_groups, group, acc0)
        for j in range(n_groups * ATTN_KV_UNROLL, n_kv_tiles):
            acc = absorb(j, acc)
        finish(acc)

    @pl.when(jnp.logical_not(safe))
    def _():
        def score(j, slot):
            s_ref[slot] = scores(j)

        def absorb(j, slot, carry):
            m, acc = carry
            s = s_ref[slot]
            m_new = jnp.maximum(m, jnp.max(s, axis=0, keepdims=True))
            p = jnp.exp2(s - m_new).astype(BF16)
            return m_new, jnp.exp2(m - m_new) * acc + weighted(j, p)

        def pair(i, carry):
            score(2 * i + 1, 1)
            carry = absorb(2 * i, 0, carry)
            score(2 * i + 2, 0)
            return absorb(2 * i + 1, 1, carry)

        n_pairs = (n_kv_tiles - 1) // 2
        score(0, 0)
        carry = lax.fori_loop(0, n_pairs, pair, (jnp.full((1, cols), NEG_BIG, F32), acc0))
        if n_kv_tiles % 2 == 1:
            carry = absorb(n_kv_tiles - 1, 0, carry)
        else:
            score(n_kv_tiles - 1, 1)
            carry = absorb(n_kv_tiles - 2, 0, carry)
            carry = absorb(n_kv_tiles - 1, 1, carry)
        finish(carry[1])


def _attention(qa, k_heads, vt_heads):
    b, s, _ = qa.shape
    n_all = k_heads.shape[2]
    tq = ATTN_Q_TILE
    gw = ATTN_GROUP * ATTN_HEAD_DIM
    vt_rows = vt_heads.shape[2]
    return pl.pallas_call(
        functools.partial(_attention_kernel, n_all // ATTN_KV_TILE),
        out_shape=jax.ShapeDtypeStruct((b, s, ATTN_WIDTH), BF16),
        grid=(b, ATTN_KV_HEADS, s // tq),
        in_specs=[pl.BlockSpec((1, tq, gw), lambda bi, kv, i: (bi, i, kv)),
                  pl.BlockSpec((1, 1, n_all, ATTN_HEAD_DIM), lambda bi, kv, i: (bi, kv, 0, 0)),
                  pl.BlockSpec((1, 1, vt_rows, n_all), lambda bi, kv, i: (bi, kv, 0, 0))],
        out_specs=pl.BlockSpec((1, tq, gw), lambda bi, kv, i: (bi, i, kv)),
        scratch_shapes=[pltpu.VMEM((ATTN_GROUP * tq, ATTN_HEAD_DIM), BF16),
                        pltpu.VMEM((2, ATTN_KV_TILE, ATTN_GROUP * tq), F32),
                        pltpu.VMEM((SUBLANES, LANES), F32)],
        compiler_params=_params("arbitrary", "arbitrary", "arbitrary"),
        name="attention",
    )(qa, k_heads, vt_heads)


def _hgrn_masks(reverse):
    c = HGRN_CHUNK
    t = lax.broadcasted_iota(jnp.int32, (c, c), 0)
    u = lax.broadcasted_iota(jnp.int32, (c, c), 1)
    tri = (u >= t) if reverse else (u <= t)
    levels = []
    size = c // 2
    while size >= HGRN_DIAG:
        same_parent = (t // (2 * size)) == (u // (2 * size))
        levels.append((size, same_parent if 2 * size < c else None))
        size //= 2
    return tri.astype(BF16), levels


def _hgrn_chunk(q, fr, v, lb, st, reverse, tri, levels):
    c = HGRN_CHUNK
    f = lb + (1.0 - lb) * _sigmoid(fr)
    k = 1.0 - f
    g_hi, g_lo = _split_bf16(jnp.log2(f))
    bcum = _dot(tri, g_hi) + _dot(tri, g_lo)
    qf = q.astype(F32)
    yield None

    end = 0 if reverse else c - 1
    b_end = bcum[end:end + 1, :]
    inter = _dot_nt((qf * jnp.exp2(bcum)).astype(BF16), st.astype(BF16))
    ke = (k * jnp.exp2(b_end - bcum)).astype(BF16)
    st_new = st * jnp.exp2(b_end) + _dot_tn(v, ke)

    a = None
    for size, mask in levels:
        q_rows, k_rows = [], []
        zeros = jnp.zeros((size, qf.shape[1]), BF16)
        for p0 in range(0, c, 2 * size):
            early = slice(p0, p0 + size)
            late = slice(p0 + size, p0 + 2 * size)
            r = p0 + size if reverse else p0 + size - 1
            ref = bcum[r:r + 1, :]
            q_sl, k_sl = (early, late) if reverse else (late, early)
            q_blk = (qf[q_sl] * jnp.exp2(bcum[q_sl] - ref)).astype(BF16)
            k_blk = (k[k_sl] * jnp.exp2(ref - bcum[k_sl])).astype(BF16)
            q_rows += [q_blk, zeros] if reverse else [zeros, q_blk]
            k_rows += [zeros, k_blk] if reverse else [k_blk, zeros]
        a_l = _dot_nt(jnp.concatenate(q_rows, axis=0), jnp.concatenate(k_rows, axis=0))
        if mask is not None:
            a_l = jnp.where(mask, a_l, 0.0)
        a = a_l if a is None else a + a_l
    yield None

    sub = lax.broadcasted_iota(jnp.int32, (HGRN_DIAG, 1), 0)
    lane = lax.broadcasted_iota(jnp.int32, (HGRN_DIAG, c), 1)
    diag_rows = []
    for blk in range(c // HGRN_DIAG):
        r0 = blk * HGRN_DIAG
        qb = qf[r0:r0 + HGRN_DIAG]
        bb = bcum[r0:r0 + HGRN_DIAG]
        rows_acc = jnp.zeros((HGRN_DIAG, c), F32)
        for si in range(HGRN_DIAG):
            r = r0 + si
            valid = (sub <= si) if reverse else (sub >= si)
            e = jnp.where(valid, bb - bcum[r:r + 1, :], NEG_BIG)
            col = jnp.sum(qb * k[r:r + 1, :] * jnp.exp2(e), axis=-1, keepdims=True)
            rows_acc = jnp.where(lane == r, col, rows_acc)
        diag_rows.append(rows_acc)
    a = a + jnp.concatenate(diag_rows, axis=0)
    yield None

    yield inter + _dot(a.astype(BF16), v), st_new


_HGRN_STAGES = 4


def _hgrn_kernel(qf_ref, if_ref, ff_ref, qb_ref, ib_ref, fb_ref, lb_ref, of_ref, ob_ref,
                 sf_ref, sb_ref):
    @pl.when(pl.program_id(1) == 0)
    def _():
        sf_ref[...] = jnp.zeros_like(sf_ref)
        sb_ref[...] = jnp.zeros_like(sb_ref)

    dirs = ((False, qf_ref, if_ref, ff_ref, of_ref, sf_ref),
            (True, qb_ref, ib_ref, fb_ref, ob_ref, sb_ref))
    masks = [_hgrn_masks(reverse) for reverse, *_ in dirs]
    chunks = []
    for h in range(HGRN_HEADS):
        sl = slice(h * HGRN_HEAD_DIM, (h + 1) * HGRN_HEAD_DIM)
        for d, (reverse, q_ref, i_ref, f_ref, o_ref, s_ref) in enumerate(dirs):
            gen = _hgrn_chunk(q_ref[0, :, sl], f_ref[0, :, sl], i_ref[0, :, sl],
                              lb_ref[d:d + 1, sl], s_ref[h], reverse, *masks[d])
            chunks.append((gen, o_ref, s_ref, h, sl))

    for gen, *_ in chunks:
        next(gen)
    for step in range(len(chunks) + _HGRN_STAGES - 2):
        for stage in range(1, _HGRN_STAGES):
            idx = step - (stage - 1)
            if 0 <= idx < len(chunks):
                gen, o_ref, s_ref, h, sl = chunks[idx]
                result = next(gen)
                if stage == _HGRN_STAGES - 1:
                    o, st = result
                    o_ref[0, :, sl] = o
                    s_ref[h] = st


def _hgrn(qr, ir, ff, fb, lb, n_ctx):
    b, n_all, w = qr.shape
    c = HGRN_CHUNK
    nc = n_all // c
    ncc = n_ctx // c

    def fwd(bi, j):
        return (bi, j, 0)

    def bwd(bi, j):
        return (bi, jnp.where(j < ncc, ncc - 1 - j, nc - 1 - (j - ncc)), 0)

    blk = (1, c, w)
    state = pltpu.VMEM((HGRN_HEADS, HGRN_HEAD_DIM, HGRN_HEAD_DIM), F32)
    return pl.pallas_call(
        _hgrn_kernel,
        out_shape=[jax.ShapeDtypeStruct((b, n_all, w), F32)] * 2,
        grid=(b, nc),
        in_specs=[pl.BlockSpec(blk, fwd), pl.BlockSpec(blk, fwd), pl.BlockSpec(blk, fwd),
                  pl.BlockSpec(blk, bwd), pl.BlockSpec(blk, bwd), pl.BlockSpec(blk, bwd),
                  pl.BlockSpec((2, w), lambda bi, j: (0, 0))],
        out_specs=[pl.BlockSpec(blk, fwd), pl.BlockSpec(blk, bwd)],
        scratch_shapes=[state, state],
        compiler_params=_params("arbitrary", "arbitrary"),
        name="hgrn",
    )(qr, ir, ff, qr, ir, fb, lb)


_ROUTE_GROUP_LANE0 = N_EXPERTS


def _lane_min_index(cond, lane):
    return jnp.min(jnp.where(cond, lane, LANES), axis=-1, keepdims=True)


def _out_proj_kernel(oa_ref, of_ref, ob_ref, gr_ref, x_ref, gate_ref, shift_ref, scale_ref,
                     ag_ref, hg_ref, g2_ref, w_ref, ones_ref, wr_hi_ref, wr_lo_ref, br_ref,
                     x1_ref, h2_ref, ri_ref, rw_ref, cnt_ref, carry_ref):
    first = (pl.program_id(0) == 0) & (pl.program_id(1) == 0)

    @pl.when(first)
    def _():
        carry_ref[...] = jnp.zeros_like(carry_ref)

    tm = x_ref.shape[1]
    ones_bd = ones_ref[...]
    slabs = []
    for c0 in range(0, ATTN_WIDTH, LANES):
        t = oa_ref[0, :, c0:c0 + LANES].astype(F32)
        slabs.append((t * _head_rms_scale(t, ones_bd) * ag_ref[:, c0:c0 + LANES]).astype(BF16))
    orr = of_ref[0] + ob_ref[0]
    parts = []
    for h in range(HGRN_HEADS):
        sl = slice(h * HGRN_HEAD_DIM, (h + 1) * HGRN_HEAD_DIM)
        oh = orr[:, sl]
        parts.append(oh * lax.rsqrt(jnp.mean(oh * oh, axis=-1, keepdims=True) + EPS))
    orr = jnp.concatenate(parts, axis=1) * hg_ref[...] * gr_ref[0].astype(F32)
    mix_in = jnp.concatenate(slabs + [orr.astype(BF16)], axis=1)
    x1 = x_ref[0] + gate_ref[0] * _dot(mix_in, w_ref[...])
    x1_ref[0] = x1
    h2 = x1 * lax.rsqrt(jnp.mean(x1 * x1, axis=-1, keepdims=True) + EPS) * g2_ref[...]
    h2 = h2 * (1.0 + scale_ref[0]) + shift_ref[0]
    h2_ref[0] = h2

    h_hi, h_lo = _split_bf16(h2)
    logits = (_dot(h_hi, wr_hi_ref[...]) + _dot(h_lo, wr_hi_ref[...])
              + _dot(h_hi, wr_lo_ref[...]) + br_ref[...])
    lane = lax.broadcasted_iota(jnp.int32, logits.shape, 1)
    is_grp = (lane >= _ROUTE_GROUP_LANE0) & (lane < _ROUTE_GROUP_LANE0 + N_GROUPS)
    lg = jnp.where(is_grp, logits, NEG_BIG)
    mg = jnp.max(lg, axis=-1, keepdims=True)
    g_sel = _lane_min_index(lg == mg, lane) - _ROUTE_GROUP_LANE0
    pg_top = 1.0 / jnp.sum(jnp.exp(lg - mg), axis=-1, keepdims=True)
    in_grp = (lane < N_EXPERTS) & ((lane // EXPERTS_PER_GROUP) == g_sel)
    le = jnp.where(in_grp, logits, NEG_BIG)
    m1 = jnp.max(le, axis=-1, keepdims=True)
    e1 = _lane_min_index(le == m1, lane)
    le2 = jnp.where(lane == e1, NEG_BIG, le)
    m2 = jnp.max(le2, axis=-1, keepdims=True)
    e2 = _lane_min_index(le2 == m2, lane)
    r2 = jnp.exp(m2 - m1)
    w1 = pg_top / (1.0 + r2)
    w2 = pg_top * r2 / (1.0 + r2)

    onehot = ((lane == e1) | (lane == e2)).astype(BF16)
    rt = lax.broadcasted_iota(jnp.int32, (tm, tm), 0)
    ru = lax.broadcasted_iota(jnp.int32, (tm, tm), 1)
    before = _dot((ru < rt).astype(BF16), onehot) + carry_ref[0:1, :]
    rank1 = jnp.sum(jnp.where(lane == e1, before, 0.0), axis=-1, keepdims=True)
    rank2 = jnp.sum(jnp.where(lane == e2, before, 0.0), axis=-1, keepdims=True)
    total = carry_ref[0:1, :] + jnp.sum(onehot.astype(F32), axis=0, keepdims=True)
    carry_ref[...] = jnp.broadcast_to(total, carry_ref.shape)
    cnt_ref[...] = jnp.broadcast_to(total, cnt_ref.shape).astype(jnp.int32)

    rl = lax.broadcasted_iota(jnp.int32, (tm, ROUTE_LANES), 1)
    ri = jnp.where(rl == 0, e1, jnp.where(rl == 1, e2, jnp.where(
        rl == 2, rank1.astype(jnp.int32), jnp.where(rl == 3, rank2.astype(jnp.int32), 0))))
    ri_ref[0] = ri
    rw_ref[0] = jnp.where(rl == 0, w1, jnp.where(rl == 1, w2, 0.0))


def _out_proj(oa, o_f, o_b, gr, x, gate, shift, scale, ag, hg, g2, w_out, wr_hi, wr_lo, br, n_ctx):
    b, s, d = x.shape
    tm = TOKEN_TILE
    off = n_ctx // tm
    ones_bd = _head_sum_matrix()

    def lat(w):
        return pl.BlockSpec((1, tm, w), lambda bi, i: (bi, i, 0))

    def allrows(w):
        return pl.BlockSpec((1, tm, w), lambda bi, i: (bi, i + off, 0))

    def mod():
        return pl.BlockSpec((1, 1, d), lambda bi, i: (bi, 0, 0))

    def const(shape):
        return pl.BlockSpec(shape, lambda bi, i: (0,) * len(shape))

    return pl.pallas_call(
        _out_proj_kernel,
        out_shape=[jax.ShapeDtypeStruct((b, s, d), F32), jax.ShapeDtypeStruct((b, s, d), F32),
                   jax.ShapeDtypeStruct((b, s, ROUTE_LANES), jnp.int32),
                   jax.ShapeDtypeStruct((b, s, ROUTE_LANES), F32),
                   jax.ShapeDtypeStruct((SUBLANES, LANES), jnp.int32)],
        grid=(b, s // tm),
        in_specs=[lat(ATTN_WIDTH), allrows(HGRN_WIDTH), allrows(HGRN_WIDTH), allrows(HGRN_WIDTH),
                  lat(d), mod(), mod(), mod(),
                  const((1, ATTN_WIDTH)), const((1, HGRN_WIDTH)), const((1, d)),
                  const((ATTN_WIDTH + HGRN_WIDTH, d)), const((LANES, LANES)),
                  const((d, LANES)), const((d, LANES)), const((1, LANES))],
        out_specs=[lat(d), lat(d), lat(ROUTE_LANES), lat(ROUTE_LANES),
                   const((SUBLANES, LANES))],
        scratch_shapes=[pltpu.VMEM((SUBLANES, LANES), F32)],
        compiler_params=_params("arbitrary", "arbitrary"),
        name="out_proj",
    )(oa, o_f, o_b, gr, x, gate, shift, scale, ag, hg, g2, w_out, ones_bd, wr_hi, wr_lo, br)


def _gather_rows(idx_ref, idx0, n_rows, src_hbm, dst, sem):
    def body(r, carry):
        pltpu.make_async_copy(src_hbm.at[pl.ds(idx_ref[0, 0, idx0 + r], 1)],
                              dst.at[pl.ds(r, 1)], sem).start()
        return carry
    lax.fori_loop(0, n_rows, body, 0, unroll=GATHER_UNROLL)


def _wait_rows(src_hbm, dst, sem):
    pltpu.make_async_copy(src_hbm.at[pl.ds(0, dst.shape[0])], dst, sem).wait()


def _experts_kernel(meta_ref, tok_cur_ref, tok_next_ref, h2_hbm, wg_ref, wu_ref, wd_ref, ys_ref,
                    xbuf, sem):
    j = pl.program_id(0)
    n_used = meta_ref[0]
    slot = j % 2

    @pl.when((j == 0) & (n_used > 0))
    def _():
        _gather_rows(tok_cur_ref, 0, MOE_BLOCK, h2_hbm, xbuf.at[0], sem.at[0])

    @pl.when(j + 1 < n_used)
    def _():
        _gather_rows(tok_next_ref, 0, MOE_BLOCK, h2_hbm, xbuf.at[1 - slot], sem.at[1 - slot])

    @pl.when(j < n_used)
    def _():
        _wait_rows(h2_hbm, xbuf.at[slot], sem.at[slot])
        xb = xbuf[slot].astype(BF16)
        a = _silu(_dot(xb, wg_ref[0])) * _dot(xb, wu_ref[0])
        ys_ref[...] = _dot(a.astype(BF16), wd_ref[0])

    @pl.when(j >= n_used)
    def _():
        ys_ref[...] = jnp.zeros_like(ys_ref)


def _experts(meta, slot_tok, h2_rows, wg, wu, wd):
    n_blk = slot_tok.shape[0]
    d = h2_rows.shape[1]
    ff = wg.shape[2]

    def wspec(shape):
        return pl.BlockSpec((1,) + shape, lambda j, meta: (meta[1 + j], 0, 0))

    tok_blk = (1, 1, MOE_BLOCK)
    return pl.pallas_call(
        _experts_kernel,
        out_shape=jax.ShapeDtypeStruct((n_blk * MOE_BLOCK, d), F32),
        grid_spec=pltpu.PrefetchScalarGridSpec(
            num_scalar_prefetch=1,
            grid=(n_blk,),
            in_specs=[
                pl.BlockSpec(tok_blk, lambda j, meta: (j, 0, 0), memory_space=pltpu.SMEM),
                pl.BlockSpec(tok_blk, lambda j, meta: (jnp.minimum(j + 1, n_blk - 1), 0, 0),
                             memory_space=pltpu.SMEM),
                pl.BlockSpec(memory_space=pl.ANY),
                wspec((d, ff)), wspec((d, ff)), wspec((ff, d)),
            ],
            out_specs=pl.BlockSpec((MOE_BLOCK, d), lambda j, meta: (j, 0)),
            scratch_shapes=[pltpu.VMEM((2, MOE_BLOCK, d), F32), pltpu.SemaphoreType.DMA((2,))],
        ),
        compiler_params=_params("arbitrary"),
        name="experts",
    )(meta, slot_tok, slot_tok, h2_rows, wg, wu, wd)


def _combine_kernel(dest_cur_ref, dest_next_ref, ys_hbm, x1_ref, gate_ref, rw_ref, o_ref,
                    buf, sem):
    i = pl.program_id(0)
    slot = i % 2
    tm = x1_ref.shape[0]

    def start(idx_ref, sl):
        for k in range(TOP_K):
            _gather_rows(idx_ref, k * tm, tm, ys_hbm, buf.at[sl, k], sem.at[sl, k])

    @pl.when(i == 0)
    def _():
        start(dest_cur_ref, 0)

    @pl.when(i + 1 < pl.num_programs(0))
    def _():
        start(dest_next_ref, 1 - slot)

    rw = rw_ref[...]
    moe = None
    for k in range(TOP_K):
        _wait_rows(ys_hbm, buf.at[slot, k], sem.at[slot, k])
        term = rw[:, k:k + 1] * buf[slot, k]
        moe = term if moe is None else moe + term
    o_ref[...] = x1_ref[...] + gate_ref[0] * moe


def _combine(dest_tiles, ys, x1, gate, rw):
    n_tok, d = x1.shape
    tm = TOKEN_TILE
    nt = n_tok // tm
    per_batch = nt // gate.shape[0]
    idx_blk = (1, 1, TOP_K * tm)
    return pl.pallas_call(
        _combine_kernel,
        out_shape=jax.ShapeDtypeStruct((n_tok, d), F32),
        grid=(nt,),
        in_specs=[
            pl.BlockSpec(idx_blk, lambda i: (i, 0, 0), memory_space=pltpu.SMEM),
            pl.BlockSpec(idx_blk, lambda i: (jnp.minimum(i + 1, nt - 1), 0, 0),
                         memory_space=pltpu.SMEM),
            pl.BlockSpec(memory_space=pl.ANY),
            pl.BlockSpec((tm, d), lambda i: (i, 0)),
            pl.BlockSpec((1, 1, d), lambda i: (i // per_batch, 0, 0)),
            pl.BlockSpec((tm, ROUTE_LANES), lambda i: (i, 0)),
        ],
        out_specs=pl.BlockSpec((tm, d), lambda i: (i, 0)),
        scratch_shapes=[pltpu.VMEM((2, TOP_K, tm, d), F32),
                        pltpu.SemaphoreType.DMA((2, TOP_K))],
        compiler_params=_params("arbitrary"),
        name="combine",
    )(dest_tiles, dest_tiles, ys, x1, gate, rw)


def _layer(x, ctx, c, c_ctx, w_ada, b_ada, norm1_g, norm2_g, w_in, q_norm_g, k_norm_g, attn_out_g,
           lb, hgrn_out_g, w_out, w_router_grp, b_router_grp, w_router_exp, b_router_exp,
           w_exp_gate, w_exp_up, w_exp_down):
    b, s, d = x.shape
    n_ctx = ctx.shape[1]
    assert n_ctx % TOKEN_TILE == 0 and s % TOKEN_TILE == 0 and s % GRID_W == 0
    assert n_ctx % HGRN_CHUNK == 0 and (n_ctx + s) % ATTN_KV_TILE == 0
    n_all = n_ctx + s

    cond = jnp.zeros((2 * SUBLANES, d), F32).at[:b].set(c).at[b].set(c_ctx)
    assert b + 1 <= cond.shape[0]
    mods = _adaln(cond, w_ada, b_ada)[:b + 1].reshape(b + 1, 1, 6, d)
    sh1, sc1, gt1, sh2, sc2, gt2 = (mods[:, :, m] for m in range(6))

    scale_q = ATTN_HEAD_DIM ** -0.5 * np.log2(np.e)
    qkg = jnp.concatenate([jnp.tile(q_norm_g, ATTN_HEADS) * scale_q,
                           jnp.tile(k_norm_g, ATTN_KV_HEADS)]).reshape(1, _QK_WIDTH)
    cos, sin = _rope_tables(n_ctx, s)
    qa, ka, va, qr, ff, fb, ir, gr = _in_proj(
        ctx, x, sh1, sc1, norm1_g.reshape(1, d), w_in.astype(BF16), qkg, cos, sin)

    def kv_heads(t):
        return t.reshape(b, n_all, ATTN_KV_HEADS, ATTN_HEAD_DIM).transpose(0, 2, 1, 3)

    vt = kv_heads(va).transpose(0, 1, 3, 2)
    vt = jnp.concatenate([vt, jnp.ones((b, ATTN_KV_HEADS, 2 * SUBLANES, n_all), BF16)], axis=2)
    oa = _attention(qa, kv_heads(ka), vt)
    o_f, o_b = _hgrn(qr, ir, ff, fb, lb, n_ctx)

    w_router = jnp.zeros((d, LANES), F32)
    w_router = w_router.at[:, :N_EXPERTS].set(w_router_exp)
    w_router = w_router.at[:, _ROUTE_GROUP_LANE0:_ROUTE_GROUP_LANE0 + N_GROUPS].set(w_router_grp)
    b_router = jnp.zeros((1, LANES), F32)
    b_router = b_router.at[0, :N_EXPERTS].set(b_router_exp)
    b_router = b_router.at[0, _ROUTE_GROUP_LANE0:_ROUTE_GROUP_LANE0 + N_GROUPS].set(b_router_grp)
    wr_hi, wr_lo = _split_bf16(w_router)
    x1, h2, ri, rw, counts = _out_proj(
        oa, o_f, o_b, gr, x, gt1[:b], sh2[:b], sc2[:b], attn_out_g.reshape(1, -1),
        hgrn_out_g.reshape(1, -1), norm2_g.reshape(1, d), w_out.astype(BF16), wr_hi, wr_lo,
        b_router, n_ctx)

    n_tok = b * s
    counts = counts[0, :N_EXPERTS]
    padded = (counts + MOE_BLOCK - 1) // MOE_BLOCK * MOE_BLOCK
    pend = jnp.cumsum(padded)
    pstart = pend - padded
    n_blk = n_tok * TOP_K // MOE_BLOCK + N_EXPERTS
    ri = ri.reshape(n_tok, ROUTE_LANES)
    e_id = ri[:, :TOP_K]
    dest = pstart[e_id] + ri[:, TOP_K:2 * TOP_K]
    tok = jnp.broadcast_to(jnp.arange(n_tok, dtype=jnp.int32)[:, None], dest.shape)
    slot_tok = jnp.zeros((n_blk * MOE_BLOCK,), jnp.int32).at[dest.reshape(-1)].set(tok.reshape(-1))
    blk_start = jnp.arange(n_blk, dtype=jnp.int32) * MOE_BLOCK
    blk_e = jnp.minimum(jnp.sum(blk_start[:, None] >= pend[None, :], axis=1), N_EXPERTS - 1)
    n_used = pend[-1] // MOE_BLOCK
    meta = jnp.concatenate([n_used[None], blk_e]).astype(jnp.int32)

    ys = _experts(meta, slot_tok.reshape(n_blk, 1, MOE_BLOCK), h2.reshape(n_tok, d),
                  w_exp_gate.astype(BF16), w_exp_up.astype(BF16), w_exp_down.astype(BF16))

    nt = n_tok // TOKEN_TILE
    dest_tiles = dest.reshape(nt, TOKEN_TILE, TOP_K).transpose(0, 2, 1).reshape(nt, 1, -1)
    out = _combine(dest_tiles, ys, x1.reshape(n_tok, d), gt2[:b], rw.reshape(n_tok, ROUTE_LANES))
    return out.reshape(b, s, d)


def kernel(x, c, ctx, c_ctx, w_ada, b_ada, norm1_g, norm2_g, w_in, q_norm_g, k_norm_g, attn_out_g,
           hgrn_lb, hgrn_out_g, w_out, w_router_grp, b_router_grp, w_router_exp, b_router_exp,
           w_exp_gate, w_exp_up, w_exp_down):
    depth = w_in.shape[0]
    assert depth == 1, "context stream update between layers is not implemented"
    lb_all = jnp.cumsum(jax.nn.softmax(hgrn_lb.astype(F32), axis=1), axis=1)
    layer = 0
    return _layer(x, ctx, c, c_ctx, w_ada[layer], b_ada[layer], norm1_g[layer], norm2_g[layer],
                  w_in[layer], q_norm_g[layer], k_norm_g[layer], attn_out_g[layer],
                  lb_all[:, layer], hgrn_out_g[layer], w_out[layer], w_router_grp[layer],
                  b_router_grp[layer], w_router_exp[layer], b_router_exp[layer],
                  w_exp_gate[layer], w_exp_up[layer], w_exp_down[layer])
```

```python
import functools

import numpy as np
import jax
import jax.numpy as jnp
from jax import lax
from jax.experimental import pallas as pl
from jax.experimental.pallas import tpu as pltpu

F32 = jnp.float32
BF16 = jnp.bfloat16

GRID_W = 64
EPS = 1e-6
ATTN_HEADS = 8
ATTN_KV_HEADS = 2
ATTN_HEAD_DIM = 64
ATTN_GROUP = ATTN_HEADS // ATTN_KV_HEADS
ATTN_WIDTH = ATTN_HEADS * ATTN_HEAD_DIM
KV_WIDTH = ATTN_KV_HEADS * ATTN_HEAD_DIM
ROPE_THETA = 10000.0
HGRN_HEADS = 4
HGRN_HEAD_DIM = 128
HGRN_WIDTH = HGRN_HEADS * HGRN_HEAD_DIM
N_GROUPS = 4
EXPERTS_PER_GROUP = 8
N_EXPERTS = N_GROUPS * EXPERTS_PER_GROUP
TOP_K = 2
EXPERT_FF = 512

LANES = 128
SUBLANES = 8
MXU_DIM = 256
VMEM_LIMIT_BYTES = 48 * 1024 * 1024

TOKEN_TILE = 256
ATTN_Q_TILE = 512
ATTN_KV_TILE = 256
ATTN_KV_UNROLL = 4
ATTN_BOUND_SLACK = 1.02
ATTN_BOUND_MAX = 60.0
HGRN_CHUNK = 64
HGRN_DIAG = 8
MOE_BLOCK = 256
GATHER_UNROLL = 8
ROUTE_LANES = 8
NEG_BIG = -1e30

_QA0 = 0
_KA0 = _QA0 + ATTN_WIDTH
_VA0 = _KA0 + KV_WIDTH
_QR0 = _VA0 + KV_WIDTH
_FF0 = _QR0 + HGRN_WIDTH
_FB0 = _FF0 + HGRN_WIDTH
_IR0 = _FB0 + HGRN_WIDTH
_GR0 = _IR0 + HGRN_WIDTH
_QK_WIDTH = ATTN_WIDTH + KV_WIDTH


def _dot(a, b):
    return jnp.dot(a, b, preferred_element_type=F32)


def _dot_nt(a, b):
    return lax.dot_general(a, b, (((1,), (1,)), ((), ())), preferred_element_type=F32)


def _dot_tn(a, b):
    return lax.dot_general(a, b, (((0,), (0,)), ((), ())), preferred_element_type=F32)


def _split_bf16(x):
    hi = x.astype(BF16)
    lo = (x - hi.astype(F32)).astype(BF16)
    return hi, lo


def _sigmoid(x):
    return 1.0 / (1.0 + jnp.exp(-x))


def _silu(x):
    return x * _sigmoid(x)


def _params(*sem):
    return pltpu.CompilerParams(dimension_semantics=sem, vmem_limit_bytes=VMEM_LIMIT_BYTES)


def _head_sum_matrix():
    idx = np.arange(LANES) // ATTN_HEAD_DIM
    return jnp.asarray(idx[:, None] == idx[None, :], dtype=BF16)


def _head_rms_scale(x, ones_bd):
    ssq = _dot((x * x).astype(BF16), ones_bd)
    return lax.rsqrt(ssq * (1.0 / ATTN_HEAD_DIM) + EPS)


def _adaln_kernel(cond_ref, w_ref, b_ref, o_ref):
    s = _silu(cond_ref[...])
    s_hi, s_lo = _split_bf16(s)
    w_hi, w_lo = _split_bf16(w_ref[...])
    o_ref[...] = _dot(s_hi, w_hi) + _dot(s_lo, w_hi) + _dot(s_hi, w_lo) + b_ref[...]


def _adaln(cond, w_ada, b_ada):
    rows, d = cond.shape
    n = w_ada.shape[1]
    tn = n // 6
    return pl.pallas_call(
        _adaln_kernel,
        out_shape=jax.ShapeDtypeStruct((rows, n), F32),
        grid=(n // tn,),
        in_specs=[pl.BlockSpec((rows, d), lambda j: (0, 0)),
                  pl.BlockSpec((d, tn), lambda j: (0, j)),
                  pl.BlockSpec((1, tn), lambda j: (0, j))],
        out_specs=pl.BlockSpec((rows, tn), lambda j: (0, j)),
        compiler_params=_params("arbitrary"),
        name="adaln",
    )(cond, w_ada, b_ada.reshape(1, n))


def _rope_tables(n_ctx, n_lat):
    half = ATTN_HEAD_DIM // 2
    freqs = ROPE_THETA ** (-np.arange(0, half, 2, dtype=np.float64) / half)
    tok = np.arange(n_lat)
    pos = np.stack([tok // GRID_W, tok % GRID_W], axis=1).astype(np.float64)
    lane = np.arange(ATTN_HEAD_DIM)
    axis = lane // half
    fi = (lane % half) // 2
    ang = pos[:, axis] * freqs[fi][None, :]
    sign = np.where(lane % 2 == 1, 1.0, -1.0)
    cos = np.concatenate([np.ones((n_ctx, ATTN_HEAD_DIM)), np.cos(ang)], axis=0)
    sin = np.concatenate([np.zeros((n_ctx, ATTN_HEAD_DIM)), np.sin(ang) * sign], axis=0)
    reps = LANES // ATTN_HEAD_DIM
    return (jnp.asarray(np.tile(cos, (1, reps)), F32), jnp.asarray(np.tile(sin, (1, reps)), F32))


def _in_proj_kernel(n_ctx_tiles, ctx_ref, x_ref, shift_ref, scale_ref, g1_ref, w_ref, qkg_ref,
                    ones_ref, cos_ref, sin_ref,
                    qa_ref, ka_ref, va_ref, qr_ref, ff_ref, fb_ref, ir_ref, gr_ref):
    i = pl.program_id(1)
    xt = jnp.where(i < n_ctx_tiles, ctx_ref[0], x_ref[0])
    ms = jnp.mean(xt * xt, axis=-1, keepdims=True)
    h = xt * lax.rsqrt(ms + EPS) * g1_ref[...]
    h = h * (1.0 + scale_ref[0]) + shift_ref[0]
    p = _dot(h.astype(BF16), w_ref[...])

    ones_bd = ones_ref[...]
    cos = cos_ref[...]
    sin = sin_ref[...]
    even = lax.broadcasted_iota(jnp.int32, cos.shape, 1) % 2 == 0
    slabs = []
    for c0 in range(0, _QK_WIDTH, LANES):
        t = p[:, _QA0 + c0:_QA0 + c0 + LANES]
        t = t * _head_rms_scale(t, ones_bd) * qkg_ref[:, c0:c0 + LANES]
        partner = jnp.where(even, pltpu.roll(t, LANES - 1, 1), pltpu.roll(t, 1, 1))
        slabs.append((t * cos + partner * sin).astype(BF16))
    qa_ref[0] = jnp.concatenate(slabs[:ATTN_WIDTH // LANES], axis=1)
    ka_ref[0] = jnp.concatenate(slabs[ATTN_WIDTH // LANES:], axis=1)
    va_ref[0] = p[:, _VA0:_VA0 + KV_WIDTH].astype(BF16)

    qr_ref[0] = (_silu(p[:, _QR0:_QR0 + HGRN_WIDTH]) * (HGRN_HEAD_DIM ** -0.5)).astype(BF16)
    ff_ref[0] = p[:, _FF0:_FF0 + HGRN_WIDTH]
    fb_ref[0] = p[:, _FB0:_FB0 + HGRN_WIDTH]
    ir_ref[0] = p[:, _IR0:_IR0 + HGRN_WIDTH].astype(BF16)
    gr_ref[0] = _silu(p[:, _GR0:_GR0 + HGRN_WIDTH]).astype(BF16)


def _in_proj(ctx, x, shift, scale, g1, w_in, qkg, cos, sin):
    b, n_ctx, d = ctx.shape
    s = x.shape[1]
    tm = TOKEN_TILE
    nct = n_ctx // tm
    n_all = n_ctx + s
    nt = n_all // tm
    pw = w_in.shape[1]
    ones_bd = _head_sum_matrix()

    def tok_spec(w):
        return pl.BlockSpec((1, tm, w), lambda bi, i: (bi, i, 0))

    mod_spec = pl.BlockSpec((1, 1, d), lambda bi, i: (jnp.where(i < nct, b, bi), 0, 0))
    outs = [(ATTN_WIDTH, BF16), (KV_WIDTH, BF16), (KV_WIDTH, BF16), (HGRN_WIDTH, BF16),
            (HGRN_WIDTH, F32), (HGRN_WIDTH, F32), (HGRN_WIDTH, BF16), (HGRN_WIDTH, BF16)]
    lat_spec = pl.BlockSpec((1, tm, ATTN_WIDTH), lambda bi, i: (bi, jnp.maximum(i - nct, 0), 0))
    return pl.pallas_call(
        functools.partial(_in_proj_kernel, nct),
        out_shape=[jax.ShapeDtypeStruct((b, s if j == 0 else n_all, w), dt)
                   for j, (w, dt) in enumerate(outs)],
        grid=(b, nt),
        in_specs=[
            pl.BlockSpec((1, tm, d), lambda bi, i: (bi, jnp.minimum(i, nct - 1), 0)),
            pl.BlockSpec((1, tm, d), lambda bi, i: (bi, jnp.maximum(i - nct, 0), 0)),
            mod_spec, mod_spec,
            pl.BlockSpec((1, d), lambda bi, i: (0, 0)),
            pl.BlockSpec((d, pw), lambda bi, i: (0, 0)),
            pl.BlockSpec((1, _QK_WIDTH), lambda bi, i: (0, 0)),
            pl.BlockSpec((LANES, LANES), lambda bi, i: (0, 0)),
            pl.BlockSpec((tm, LANES), lambda bi, i: (i, 0)),
            pl.BlockSpec((tm, LANES), lambda bi, i: (i, 0)),
        ],
        out_specs=[lat_spec] + [tok_spec(w) for w, _ in outs[1:]],
        compiler_params=_params("arbitrary", "arbitrary"),
        name="in_proj",
    )(ctx, x, shift, scale, g1, w_in, qkg, ones_bd, cos, sin)


def _attention_kernel(n_kv_tiles, q_ref, k_ref, vt_ref, o_ref, qs_ref, s_ref, ksq_ref):
    tq = q_ref.shape[1]
    cols = ATTN_GROUP * tq
    for h in range(ATTN_GROUP):
        qs_ref[h * tq:(h + 1) * tq, :] = q_ref[0, :, h * ATTN_HEAD_DIM:(h + 1) * ATTN_HEAD_DIM]

    @pl.when(pl.program_id(2) == 0)
    def _():
        k = k_ref[0, 0].astype(F32)
        ksq = jnp.max(jnp.sum(k * k, axis=1, keepdims=True), axis=0, keepdims=True)
        ksq_ref[...] = jnp.broadcast_to(ksq, ksq_ref.shape)

    def tile(j):
        return pl.ds(pl.multiple_of(j * ATTN_KV_TILE, ATTN_KV_TILE), ATTN_KV_TILE)

    def scores(j):
        return _dot_nt(k_ref[0, 0, tile(j), :], qs_ref[...])

    def weighted(j, p):
        return _dot(vt_ref[0, 0, :, tile(j)], p)

    def finish(acc):
        o = acc[:ATTN_HEAD_DIM] / acc[ATTN_HEAD_DIM:ATTN_HEAD_DIM + 1]
        o_ref[0] = jnp.concatenate(
            [o[:, h * tq:(h + 1) * tq].T for h in range(ATTN_GROUP)], axis=1).astype(o_ref.dtype)

    qf = qs_ref[...].astype(F32)
    qsq = _dot_nt(jnp.ones((SUBLANES, ATTN_HEAD_DIM), BF16), (qf * qf).astype(BF16))[0:1]
    bound = jnp.sqrt(qsq * ksq_ref[0:1, 0:1]) * ATTN_BOUND_SLACK
    acc0 = jnp.zeros((vt_ref.shape[2], cols), F32)
    safe = jnp.max(bound) <= ATTN_BOUND_MAX

    @pl.when(safe)
    def _():
        def absorb(j, acc):
            return acc + weighted(j, jnp.exp2(scores(j) - bound).astype(BF16))

        def group(i, acc):
            for u in range(ATTN_KV_UNROLL):
                acc = absorb(ATTN_KV_UNROLL * i + u, acc)
            return acc

        n_groups = n_kv_tiles // ATTN_KV_UNROLL
        acc = lax.fori_loop(0, n_groups, group, acc0)
        for j in range(n_groups * ATTN_KV_UNROLL, n_kv_tiles):
            acc = absorb(j, acc)
        finish(acc)

    @pl.when(jnp.logical_not(safe))
    def _():
        def score(j, slot):
            s_ref[slot] = scores(j)

        def absorb(j, slot, carry):
            m, acc = carry
            s = s_ref[slot]
            m_new = jnp.maximum(m, jnp.max(s, axis=0, keepdims=True))
            p = jnp.exp2(s - m_new).astype(BF16)
            return m_new, jnp.exp2(m - m_new) * acc + weighted(j, p)

        def pair(i, carry):
            score(2 * i + 1, 1)
            carry = absorb(2 * i, 0, carry)
            score(2 * i + 2, 0)
            return absorb(2 * i + 1, 1, carry)

        n_pairs = (n_kv_tiles - 1) // 2
        score(0, 0)
        carry = lax.fori_loop(0, n_pairs, pair, (jnp.full((1, cols), NEG_BIG, F32), acc0))
        if n_kv_tiles % 2 == 1:
            carry = absorb(n_kv_tiles - 1, 0, carry)
        else:
            score(n_kv_tiles - 1, 1)
            carry = absorb(n_kv_tiles - 2, 0, carry)
            carry = absorb(n_kv_tiles - 1, 1, carry)
        finish(carry[1])


def _attention(qa, k_heads, vt_heads):
    b, s, _ = qa.shape
    n_all = k_heads.shape[2]
    tq = ATTN_Q_TILE
    gw = ATTN_GROUP * ATTN_HEAD_DIM
    vt_rows = vt_heads.shape[2]
    return pl.pallas_call(
        functools.partial(_attention_kernel, n_all // ATTN_KV_TILE),
        out_shape=jax.ShapeDtypeStruct((b, s, ATTN_WIDTH), BF16),
        grid=(b, ATTN_KV_HEADS, s // tq),
        in_specs=[pl.BlockSpec((1, tq, gw), lambda bi, kv, i: (bi, i, kv)),
                  pl.BlockSpec((1, 1, n_all, ATTN_HEAD_DIM), lambda bi, kv, i: (bi, kv, 0, 0)),
                  pl.BlockSpec((1, 1, vt_rows, n_all), lambda bi, kv, i: (bi, kv, 0, 0))],
        out_specs=pl.BlockSpec((1, tq, gw), lambda bi, kv, i: (bi, i, kv)),
        scratch_shapes=[pltpu.VMEM((ATTN_GROUP * tq, ATTN_HEAD_DIM), BF16),
                        pltpu.VMEM((2, ATTN_KV_TILE, ATTN_GROUP * tq), F32),
                        pltpu.VMEM((SUBLANES, LANES), F32)],
        compiler_params=_params("arbitrary", "arbitrary", "arbitrary"),
        name="attention",
    )(qa, k_heads, vt_heads)


def _hgrn_masks(reverse):
    c = HGRN_CHUNK
    t = lax.broadcasted_iota(jnp.int32, (c, c), 0)
    u = lax.broadcasted_iota(jnp.int32, (c, c), 1)
    tri = (u >= t) if reverse else (u <= t)
    levels = []
    size = c // 2
    while size >= HGRN_DIAG:
        same_parent = (t // (2 * size)) == (u // (2 * size))
        levels.append((size, same_parent if 2 * size < c else None))
        size //= 2
    return tri.astype(BF16), levels


def _hgrn_chunk(q, fr, v, lb, st, reverse, tri, levels):
    c = HGRN_CHUNK
    f = lb + (1.0 - lb) * _sigmoid(fr)
    k = 1.0 - f
    g_hi, g_lo = _split_bf16(jnp.log2(f))
    bcum = _dot(tri, g_hi) + _dot(tri, g_lo)
    qf = q.astype(F32)
    yield None

    end = 0 if reverse else c - 1
    b_end = bcum[end:end + 1, :]
    inter = _dot_nt((qf * jnp.exp2(bcum)).astype(BF16), st.astype(BF16))
    ke = (k * jnp.exp2(b_end - bcum)).astype(BF16)
    st_new = st * jnp.exp2(b_end) + _dot_tn(v, ke)

    a = None
    for size, mask in levels:
        q_rows, k_rows = [], []
        zeros = jnp.zeros((size, qf.shape[1]), BF16)
        for p0 in range(0, c, 2 * size):
            early = slice(p0, p0 + size)
            late = slice(p0 + size, p0 + 2 * size)
            r = p0 + size if reverse else p0 + size - 1
            ref = bcum[r:r + 1, :]
            q_sl, k_sl = (early, late) if reverse else (late, early)
            q_blk = (qf[q_sl] * jnp.exp2(bcum[q_sl] - ref)).astype(BF16)
            k_blk = (k[k_sl] * jnp.exp2(ref - bcum[k_sl])).astype(BF16)
            q_rows += [q_blk, zeros] if reverse else [zeros, q_blk]
            k_rows += [zeros, k_blk] if reverse else [k_blk, zeros]
        a_l = _dot_nt(jnp.concatenate(q_rows, axis=0), jnp.concatenate(k_rows, axis=0))
        if mask is not None:
            a_l = jnp.where(mask, a_l, 0.0)
        a = a_l if a is None else a + a_l
    yield None

    sub = lax.broadcasted_iota(jnp.int32, (HGRN_DIAG, 1), 0)
    lane = lax.broadcasted_iota(jnp.int32, (HGRN_DIAG, c), 1)
    diag_rows = []
    for blk in range(c // HGRN_DIAG):
        r0 = blk * HGRN_DIAG
        qb = qf[r0:r0 + HGRN_DIAG]
        bb = bcum[r0:r0 + HGRN_DIAG]
        rows_acc = jnp.zeros((HGRN_DIAG, c), F32)
        for si in range(HGRN_DIAG):
            r = r0 + si
            valid = (sub <= si) if reverse else (sub >= si)
            e = jnp.where(valid, bb - bcum[r:r + 1, :], NEG_BIG)
            col = jnp.sum(qb * k[r:r + 1, :] * jnp.exp2(e), axis=-1, keepdims=True)
            rows_acc = jnp.where(lane == r, col, rows_acc)
        diag_rows.append(rows_acc)
    a = a + jnp.concatenate(diag_rows, axis=0)
    yield None

    yield inter + _dot(a.astype(BF16), v), st_new


_HGRN_STAGES = 4


def _hgrn_kernel(qf_ref, if_ref, ff_ref, qb_ref, ib_ref, fb_ref, lb_ref, of_ref, ob_ref,
                 sf_ref, sb_ref):
    @pl.when(pl.program_id(1) == 0)
    def _():
        sf_ref[...] = jnp.zeros_like(sf_ref)
        sb_ref[...] = jnp.zeros_like(sb_ref)

    dirs = ((False, qf_ref, if_ref, ff_ref, of_ref, sf_ref),
            (True, qb_ref, ib_ref, fb_ref, ob_ref, sb_ref))
    masks = [_hgrn_masks(reverse) for reverse, *_ in dirs]
    chunks = []
    for h in range(HGRN_HEADS):
        sl = slice(h * HGRN_HEAD_DIM, (h + 1) * HGRN_HEAD_DIM)
        for d, (reverse, q_ref, i_ref, f_ref, o_ref, s_ref) in enumerate(dirs):
            gen = _hgrn_chunk(q_ref[0, :, sl], f_ref[0, :, sl], i_ref[0, :, sl],
                              lb_ref[d:d + 1, sl], s_ref[h], reverse, *masks[d])
            chunks.append((gen, o_ref, s_ref, h, sl))

    for gen, *_ in chunks:
        next(gen)
    for step in range(len(chunks) + _HGRN_STAGES - 2):
        for stage in range(1, _HGRN_STAGES):
            idx = step - (stage - 1)
            if 0 <= idx < len(chunks):
                gen, o_ref, s_ref, h, sl = chunks[idx]
                result = next(gen)
                if stage == _HGRN_STAGES - 1:
                    o, st = result
                    o_ref[0, :, sl] = o
                    s_ref[h] = st


def _hgrn(qr, ir, ff, fb, lb, n_ctx):
    b, n_all, w = qr.shape
    c = HGRN_CHUNK
    nc = n_all // c
    ncc = n_ctx // c

    def fwd(bi, j):
        return (bi, j, 0)

    def bwd(bi, j):
        return (bi, jnp.where(j < ncc, ncc - 1 - j, nc - 1 - (j - ncc)), 0)

    blk = (1, c, w)
    state = pltpu.VMEM((HGRN_HEADS, HGRN_HEAD_DIM, HGRN_HEAD_DIM), F32)
    return pl.pallas_call(
        _hgrn_kernel,
        out_shape=[jax.ShapeDtypeStruct((b, n_all, w), F32)] * 2,
        grid=(b, nc),
        in_specs=[pl.BlockSpec(blk, fwd), pl.BlockSpec(blk, fwd), pl.BlockSpec(blk, fwd),
                  pl.BlockSpec(blk, bwd), pl.BlockSpec(blk, bwd), pl.BlockSpec(blk, bwd),
                  pl.BlockSpec((2, w), lambda bi, j: (0, 0))],
        out_specs=[pl.BlockSpec(blk, fwd), pl.BlockSpec(blk, bwd)],
        scratch_shapes=[state, state],
        compiler_params=_params("arbitrary", "arbitrary"),
        name="hgrn",
    )(qr, ir, ff, qr, ir, fb, lb)


_ROUTE_GROUP_LANE0 = N_EXPERTS


def _lane_min_index(cond, lane):
    return jnp.min(jnp.where(cond, lane, LANES), axis=-1, keepdims=True)


def _out_proj_kernel(oa_ref, of_ref, ob_ref, gr_ref, x_ref, gate_ref, shift_ref, scale_ref,
                     ag_ref, hg_ref, g2_ref, w_ref, ones_ref, wr_hi_ref, wr_lo_ref, br_ref,
                     x1_ref, h2_hbm, ri_ref, rw_ref, cnt_ref, carry_ref, h2_stage, h2_sem):
    step = pl.program_id(0) * pl.num_programs(1) + pl.program_id(1)
    last = pl.num_programs(0) * pl.num_programs(1) - 1
    slot = step % 2
    tm = x_ref.shape[1]

    def h2_copies(sl, row0):
        return _slab_copies(h2_stage.at[sl], h2_hbm, row0, h2_sem.at[sl], True)

    @pl.when(step == 0)
    def _():
        carry_ref[...] = jnp.zeros_like(carry_ref)

    @pl.when(step >= 2)
    def _():
        for cp in h2_copies(slot, 0):
            cp.wait()

    ones_bd = ones_ref[...]
    slabs = []
    for c0 in range(0, ATTN_WIDTH, LANES):
        t = oa_ref[0, :, c0:c0 + LANES].astype(F32)
        slabs.append((t * _head_rms_scale(t, ones_bd) * ag_ref[:, c0:c0 + LANES]).astype(BF16))
    orr = of_ref[0] + ob_ref[0]
    parts = []
    for h in range(HGRN_HEADS):
        sl = slice(h * HGRN_HEAD_DIM, (h + 1) * HGRN_HEAD_DIM)
        oh = orr[:, sl]
        parts.append(oh * lax.rsqrt(jnp.mean(oh * oh, axis=-1, keepdims=True) + EPS))
    orr = jnp.concatenate(parts, axis=1) * hg_ref[...] * gr_ref[0].astype(F32)
    mix_in = jnp.concatenate(slabs + [orr.astype(BF16)], axis=1)
    x1 = x_ref[0] + gate_ref[0] * _dot(mix_in, w_ref[...])
    x1_ref[0] = x1
    h2 = x1 * lax.rsqrt(jnp.mean(x1 * x1, axis=-1, keepdims=True) + EPS) * g2_ref[...]
    h2 = h2 * (1.0 + scale_ref[0]) + shift_ref[0]
    _stage_store(h2_stage.at[slot], h2)
    for cp in h2_copies(slot, step * tm):
        cp.start()

    h_hi, h_lo = _split_bf16(h2)
    logits = (_dot(h_hi, wr_hi_ref[...]) + _dot(h_lo, wr_hi_ref[...])
              + _dot(h_hi, wr_lo_ref[...]) + br_ref[...])
    lane = lax.broadcasted_iota(jnp.int32, logits.shape, 1)
    is_grp = (lane >= _ROUTE_GROUP_LANE0) & (lane < _ROUTE_GROUP_LANE0 + N_GROUPS)
    lg = jnp.where(is_grp, logits, NEG_BIG)
    mg = jnp.max(lg, axis=-1, keepdims=True)
    g_sel = _lane_min_index(lg == mg, lane) - _ROUTE_GROUP_LANE0
    pg_top = 1.0 / jnp.sum(jnp.exp(lg - mg), axis=-1, keepdims=True)
    in_grp = (lane < N_EXPERTS) & ((lane // EXPERTS_PER_GROUP) == g_sel)
    le = jnp.where(in_grp, logits, NEG_BIG)
    m1 = jnp.max(le, axis=-1, keepdims=True)
    e1 = _lane_min_index(le == m1, lane)
    le2 = jnp.where(lane == e1, NEG_BIG, le)
    m2 = jnp.max(le2, axis=-1, keepdims=True)
    e2 = _lane_min_index(le2 == m2, lane)
    r2 = jnp.exp(m2 - m1)
    w1 = pg_top / (1.0 + r2)
    w2 = pg_top * r2 / (1.0 + r2)

    onehot = ((lane == e1) | (lane == e2)).astype(BF16)
    rt = lax.broadcasted_iota(jnp.int32, (tm, tm), 0)
    ru = lax.broadcasted_iota(jnp.int32, (tm, tm), 1)
    before = _dot((ru < rt).astype(BF16), onehot) + carry_ref[0:1, :]
    rank1 = jnp.sum(jnp.where(lane == e1, before, 0.0), axis=-1, keepdims=True)
    rank2 = jnp.sum(jnp.where(lane == e2, before, 0.0), axis=-1, keepdims=True)
    total = carry_ref[0:1, :] + jnp.sum(onehot.astype(F32), axis=0, keepdims=True)
    carry_ref[...] = jnp.broadcast_to(total, carry_ref.shape)
    cnt_ref[...] = jnp.broadcast_to(total, cnt_ref.shape).astype(jnp.int32)

    rl = lax.broadcasted_iota(jnp.int32, (tm, ROUTE_LANES), 1)
    ri = jnp.where(rl == 0, e1, jnp.where(rl == 1, e2, jnp.where(
        rl == 2, rank1.astype(jnp.int32), jnp.where(rl == 3, rank2.astype(jnp.int32), 0))))
    ri_ref[0] = ri
    rw_ref[0] = jnp.where(rl == 0, w1, jnp.where(rl == 1, w2, 0.0))

    @pl.when(step == last)
    def _():
        for cp in h2_copies(slot, 0):
            cp.wait()

    @pl.when((step == last) & (step >= 1))
    def _():
        for cp in h2_copies(1 - slot, 0):
            cp.wait()


def _out_proj(oa, o_f, o_b, gr, x, gate, shift, scale, ag, hg, g2, w_out, wr_hi, wr_lo, br, n_ctx):
    b, s, d = x.shape
    tm = TOKEN_TILE
    off = n_ctx // tm
    ones_bd = _head_sum_matrix()

    def lat(w):
        return pl.BlockSpec((1, tm, w), lambda bi, i: (bi, i, 0))

    def allrows(w):
        return pl.BlockSpec((1, tm, w), lambda bi, i: (bi, i + off, 0))

    def mod():
        return pl.BlockSpec((1, 1, d), lambda bi, i: (bi, 0, 0))

    def const(shape):
        return pl.BlockSpec(shape, lambda bi, i: (0,) * len(shape))

    return pl.pallas_call(
        _out_proj_kernel,
        out_shape=[jax.ShapeDtypeStruct((b, s, d), F32),
                   jax.ShapeDtypeStruct((b * s, d // LANES, LANES), F32),
                   jax.ShapeDtypeStruct((b, s, ROUTE_LANES), jnp.int32),
                   jax.ShapeDtypeStruct((b, s, ROUTE_LANES), F32),
                   jax.ShapeDtypeStruct((SUBLANES, LANES), jnp.int32)],
        grid=(b, s // tm),
        in_specs=[lat(ATTN_WIDTH), allrows(HGRN_WIDTH), allrows(HGRN_WIDTH), allrows(HGRN_WIDTH),
                  lat(d), mod(), mod(), mod(),
                  const((1, ATTN_WIDTH)), const((1, HGRN_WIDTH)), const((1, d)),
                  const((ATTN_WIDTH + HGRN_WIDTH, d)), const((LANES, LANES)),
                  const((d, LANES)), const((d, LANES)), const((1, LANES))],
        out_specs=[lat(d), pl.BlockSpec(memory_space=pl.ANY), lat(ROUTE_LANES),
                   lat(ROUTE_LANES), const((SUBLANES, LANES))],
        scratch_shapes=[pltpu.VMEM((SUBLANES, LANES), F32),
                        pltpu.VMEM((2, d // LANES, tm, LANES), F32),
                        pltpu.SemaphoreType.DMA((2,))],
        compiler_params=_params("arbitrary", "arbitrary"),
        name="out_proj",
    )(oa, o_f, o_b, gr, x, gate, shift, scale, ag, hg, g2, w_out, ones_bd, wr_hi, wr_lo, br)


def _slab_copies(stage, hbm, row0, sem, to_hbm):
    copies = []
    for c in range(stage.shape[0]):
        view = hbm.at[pl.ds(row0, stage.shape[1]), c, :]
        src, dst = (stage.at[c], view) if to_hbm else (view, stage.at[c])
        copies.append(pltpu.make_async_copy(src, dst, sem))
    return copies


def _stage_store(stage, x):
    for c in range(stage.shape[0]):
        stage[c] = x[:, c * LANES:(c + 1) * LANES]


def _stage_load(stage):
    return jnp.concatenate([stage[c] for c in range(stage.shape[0])], axis=1)


def _dispatch_kernel(dest_ref, h2_hbm, xs_init_hbm, xs_hbm, sem):
    del xs_init_hbm
    i = pl.program_id(0)
    last = pl.num_programs(0) - 1
    slot = i % 2
    tm = TOKEN_TILE

    def wait(sl):
        n = TOP_K * tm
        pltpu.make_async_copy(h2_hbm.at[pl.ds(0, n)], xs_hbm.at[pl.ds(0, n)], sem.at[sl]).wait()

    @pl.when(i >= 2)
    def _():
        wait(slot)

    def body(g, carry):
        r0 = g * GATHER_UNROLL
        rows = [[dest_ref[0, 0, k * tm + r0 + u] for k in range(TOP_K)]
                for u in range(GATHER_UNROLL)]
        for u in range(GATHER_UNROLL):
            for k in range(TOP_K):
                pltpu.make_async_copy(h2_hbm.at[i * tm + r0 + u], xs_hbm.at[rows[u][k]],
                                      sem.at[slot]).start()
        return carry
    lax.fori_loop(0, tm // GATHER_UNROLL, body, 0)

    @pl.when(i == last)
    def _():
        wait(slot)

    @pl.when((i == last) & (i >= 1))
    def _():
        wait(1 - slot)


def _dispatch(dest_tiles, h2_rows, n_slots):
    nt = dest_tiles.shape[0]
    xs_init = jnp.zeros((n_slots,) + h2_rows.shape[1:], F32)
    return pl.pallas_call(
        _dispatch_kernel,
        out_shape=jax.ShapeDtypeStruct(xs_init.shape, F32),
        grid=(nt,),
        in_specs=[pl.BlockSpec((1, 1, dest_tiles.shape[2]), lambda i: (i, 0, 0),
                               memory_space=pltpu.SMEM),
                  pl.BlockSpec(memory_space=pl.ANY), pl.BlockSpec(memory_space=pl.ANY)],
        out_specs=pl.BlockSpec(memory_space=pl.ANY),
        scratch_shapes=[pltpu.SemaphoreType.DMA((2,))],
        input_output_aliases={2: 0},
        compiler_params=_params("arbitrary"),
        name="dispatch",
    )(dest_tiles, h2_rows, xs_init)


def _experts_kernel(meta_ref, xs_hbm, wg_ref, wu_ref, wd_ref, ys_hbm,
                    xstage, ystage, wg_bf, wu_bf, wd_bf, sem_in, sem_out):
    j = pl.program_id(0)
    last = pl.num_programs(0) - 1
    n_used = meta_ref[0]
    slot = j % 2

    def fetch(blk, sl):
        for cp in _slab_copies(xstage.at[sl], xs_hbm, blk * MOE_BLOCK, sem_in.at[sl], False):
            cp.start()

    def wait_out(sl):
        for cp in _slab_copies(ystage.at[sl], ys_hbm, 0, sem_out.at[sl], True):
            cp.wait()

    @pl.when((j == 0) & (n_used > 0))
    def _():
        fetch(0, 0)

    @pl.when(j + 1 < n_used)
    def _():
        fetch(j + 1, 1 - slot)

    new_expert = (j == 0) | (meta_ref[1 + j] != meta_ref[jnp.maximum(j, 1)])

    @pl.when((j < n_used) & new_expert)
    def _():
        wg_bf[...] = wg_ref[0].astype(BF16)
        wu_bf[...] = wu_ref[0].astype(BF16)
        wd_bf[...] = wd_ref[0].astype(BF16)

    @pl.when(j >= 2)
    def _():
        wait_out(slot)

    @pl.when(j < n_used)
    def _():
        for cp in _slab_copies(xstage.at[slot], xs_hbm, 0, sem_in.at[slot], False):
            cp.wait()
        xb = _stage_load(xstage.at[slot]).astype(BF16)
        a = _silu(_dot(xb, wg_bf[...])) * _dot(xb, wu_bf[...])
        _stage_store(ystage.at[slot], _dot(a.astype(BF16), wd_bf[...]))

    @pl.when(j >= n_used)
    def _():
        ystage[slot] = jnp.zeros(ystage.shape[1:], F32)

    for cp in _slab_copies(ystage.at[slot], ys_hbm, j * MOE_BLOCK, sem_out.at[slot], True):
        cp.start()

    @pl.when(j == last)
    def _():
        wait_out(slot)

    @pl.when((j == last) & (j >= 1))
    def _():
        wait_out(1 - slot)


def _experts(meta, xs, wg, wu, wd):
    n_slots, n_slabs, _ = xs.shape
    n_blk = n_slots // MOE_BLOCK
    d = wg.shape[1]
    ff = wg.shape[2]

    def wspec(shape):
        return pl.BlockSpec((1,) + shape, lambda j, meta: (meta[1 + j], 0, 0))

    stage = pltpu.VMEM((2, n_slabs, MOE_BLOCK, LANES), F32)
    return pl.pallas_call(
        _experts_kernel,
        out_shape=jax.ShapeDtypeStruct(xs.shape, F32),
        grid_spec=pltpu.PrefetchScalarGridSpec(
            num_scalar_prefetch=1,
            grid=(n_blk,),
            in_specs=[pl.BlockSpec(memory_space=pl.ANY),
                      wspec((d, ff)), wspec((d, ff)), wspec((ff, d))],
            out_specs=pl.BlockSpec(memory_space=pl.ANY),
            scratch_shapes=[stage, stage, pltpu.VMEM((d, ff), BF16), pltpu.VMEM((d, ff), BF16),
                            pltpu.VMEM((ff, d), BF16),
                            pltpu.SemaphoreType.DMA((2,)), pltpu.SemaphoreType.DMA((2,))],
        ),
        compiler_params=_params("arbitrary"),
        name="experts",
    )(meta, xs, wg, wu, wd)


def _combine_kernel(dest_cur_ref, dest_next_ref, ys_hbm, x1_ref, gate_ref, rw_ref, o_ref,
                    buf, sem):
    i = pl.program_id(0)
    slot = i % 2
    tm = x1_ref.shape[0]
    groups = tm // SUBLANES

    def start(idx_ref, sl):
        for k in range(TOP_K):
            def body(g, carry, k=k):
                rows = [idx_ref[0, 0, k * tm + g * SUBLANES + u] for u in range(SUBLANES)]
                for u in range(SUBLANES):
                    pltpu.make_async_copy(ys_hbm.at[rows[u]], buf.at[sl, k, g, :, u, :],
                                          sem.at[sl, k]).start()
                return carry
            lax.fori_loop(0, groups, body, 0)

    def wait(sl, k):
        for u in range(SUBLANES):
            pltpu.make_async_copy(ys_hbm.at[pl.ds(0, groups)], buf.at[sl, k, :, :, u, :],
                                  sem.at[sl, k]).wait()

    @pl.when(i == 0)
    def _():
        start(dest_cur_ref, 0)

    @pl.when(i + 1 < pl.num_programs(0))
    def _():
        start(dest_next_ref, 1 - slot)

    rw = rw_ref[...]
    moe = None
    for k in range(TOP_K):
        wait(slot, k)
        rows = jnp.concatenate([buf[slot, k, :, c].reshape(tm, LANES)
                                for c in range(buf.shape[3])], axis=1)
        term = rw[:, k:k + 1] * rows
        moe = term if moe is None else moe + term
    o_ref[...] = x1_ref[...] + gate_ref[0] * moe


def _combine(dest_tiles, ys, x1, gate, rw):
    n_tok, d = x1.shape
    tm = TOKEN_TILE
    nt = n_tok // tm
    per_batch = nt // gate.shape[0]
    idx_blk = (1, 1, TOP_K * tm)
    return pl.pallas_call(
        _combine_kernel,
        out_shape=jax.ShapeDtypeStruct((n_tok, d), F32),
        grid=(nt,),
        in_specs=[
            pl.BlockSpec(idx_blk, lambda i: (i, 0, 0), memory_space=pltpu.SMEM),
            pl.BlockSpec(idx_blk, lambda i: (jnp.minimum(i + 1, nt - 1), 0, 0),
                         memory_space=pltpu.SMEM),
            pl.BlockSpec(memory_space=pl.ANY),
            pl.BlockSpec((tm, d), lambda i: (i, 0)),
            pl.BlockSpec((1, 1, d), lambda i: (i // per_batch, 0, 0)),
            pl.BlockSpec((tm, ROUTE_LANES), lambda i: (i, 0)),
        ],
        out_specs=pl.BlockSpec((tm, d), lambda i: (i, 0)),
        scratch_shapes=[pltpu.VMEM((2, TOP_K, tm // SUBLANES, d // LANES, SUBLANES, LANES), F32),
                        pltpu.SemaphoreType.DMA((2, TOP_K))],
        compiler_params=_params("arbitrary"),
        name="combine",
    )(dest_tiles, dest_tiles, ys, x1, gate, rw)


def _layer(x, ctx, c, c_ctx, w_ada, b_ada, norm1_g, norm2_g, w_in, q_norm_g, k_norm_g, attn_out_g,
           lb, hgrn_out_g, w_out, w_router_grp, b_router_grp, w_router_exp, b_router_exp,
           w_exp_gate, w_exp_up, w_exp_down):
    b, s, d = x.shape
    n_ctx = ctx.shape[1]
    assert n_ctx % TOKEN_TILE == 0 and s % TOKEN_TILE == 0 and s % GRID_W == 0
    assert n_ctx % HGRN_CHUNK == 0 and (n_ctx + s) % ATTN_KV_TILE == 0
    n_all = n_ctx + s

    cond = jnp.zeros((2 * SUBLANES, d), F32).at[:b].set(c).at[b].set(c_ctx)
    assert b + 1 <= cond.shape[0]
    mods = _adaln(cond, w_ada, b_ada)[:b + 1].reshape(b + 1, 1, 6, d)
    sh1, sc1, gt1, sh2, sc2, gt2 = (mods[:, :, m] for m in range(6))

    scale_q = ATTN_HEAD_DIM ** -0.5 * np.log2(np.e)
    qkg = jnp.concatenate([jnp.tile(q_norm_g, ATTN_HEADS) * scale_q,
                           jnp.tile(k_norm_g, ATTN_KV_HEADS)]).reshape(1, _QK_WIDTH)
    cos, sin = _rope_tables(n_ctx, s)
    qa, ka, va, qr, ff, fb, ir, gr = _in_proj(
        ctx, x, sh1, sc1, norm1_g.reshape(1, d), w_in.astype(BF16), qkg, cos, sin)

    def kv_heads(t):
        return t.reshape(b, n_all, ATTN_KV_HEADS, ATTN_HEAD_DIM).transpose(0, 2, 1, 3)

    vt = kv_heads(va).transpose(0, 1, 3, 2)
    vt = jnp.concatenate([vt, jnp.ones((b, ATTN_KV_HEADS, 2 * SUBLANES, n_all), BF16)], axis=2)
    oa = _attention(qa, kv_heads(ka), vt)
    o_f, o_b = _hgrn(qr, ir, ff, fb, lb, n_ctx)

    w_router = jnp.zeros((d, LANES), F32)
    w_router = w_router.at[:, :N_EXPERTS].set(w_router_exp)
    w_router = w_router.at[:, _ROUTE_GROUP_LANE0:_ROUTE_GROUP_LANE0 + N_GROUPS].set(w_router_grp)
    b_router = jnp.zeros((1, LANES), F32)
    b_router = b_router.at[0, :N_EXPERTS].set(b_router_exp)
    b_router = b_router.at[0, _ROUTE_GROUP_LANE0:_ROUTE_GROUP_LANE0 + N_GROUPS].set(b_router_grp)
    wr_hi, wr_lo = _split_bf16(w_router)
    x1, h2, ri, rw, counts = _out_proj(
        oa, o_f, o_b, gr, x, gt1[:b], sh2[:b], sc2[:b], attn_out_g.reshape(1, -1),
        hgrn_out_g.reshape(1, -1), norm2_g.reshape(1, d), w_out.astype(BF16), wr_hi, wr_lo,
        b_router, n_ctx)

    n_tok = b * s
    counts = counts[0, :N_EXPERTS]
    padded = (counts + MOE_BLOCK - 1) // MOE_BLOCK * MOE_BLOCK
    pend = jnp.cumsum(padded)
    pstart = pend - padded
    n_blk = n_tok * TOP_K // MOE_BLOCK + N_EXPERTS
    ri = ri.reshape(n_tok, ROUTE_LANES)
    e_id = ri[:, :TOP_K]
    dest = pstart[e_id] + ri[:, TOP_K:2 * TOP_K]
    blk_start = jnp.arange(n_blk, dtype=jnp.int32) * MOE_BLOCK
    blk_e = jnp.minimum(jnp.sum(blk_start[:, None] >= pend[None, :], axis=1), N_EXPERTS - 1)
    n_used = pend[-1] // MOE_BLOCK
    meta = jnp.concatenate([n_used[None], blk_e]).astype(jnp.int32)
    nt = n_tok // TOKEN_TILE
    dest_tiles = dest.reshape(nt, TOKEN_TILE, TOP_K).transpose(0, 2, 1).reshape(nt, 1, -1)

    xs = _dispatch(dest_tiles, h2, n_blk * MOE_BLOCK)
    ys = _experts(meta, xs, w_exp_gate, w_exp_up, w_exp_down)
    out = _combine(dest_tiles, ys, x1.reshape(n_tok, d), gt2[:b], rw.reshape(n_tok, ROUTE_LANES))
    return out.reshape(b, s, d)


def kernel(x, c, ctx, c_ctx, w_ada, b_ada, norm1_g, norm2_g, w_in, q_norm_g, k_norm_g, attn_out_g,
           hgrn_lb, hgrn_out_g, w_out, w_router_grp, b_router_grp, w_router_exp, b_router_exp,
           w_exp_gate, w_exp_up, w_exp_down):
    depth = w_in.shape[0]
    assert depth == 1, "context stream update between layers is not implemented"
    lb_all = jnp.cumsum(jax.nn.softmax(hgrn_lb.astype(F32), axis=1), axis=1)
    layer = 0
    return _layer(x, ctx, c, c_ctx, w_ada[layer], b_ada[layer], norm1_g[layer], norm2_g[layer],
                  w_in[layer], q_norm_g[layer], k_norm_g[layer], attn_out_g[layer],
                  lb_all[:, layer], hgrn_out_g[layer], w_out[layer], w_router_grp[layer],
                  b_router_grp[layer], w_router_exp[layer], b_router_exp[layer],
                  w_exp_gate[layer], w_exp_up[layer], w_exp_down[layer])
```

```python
import functools

import numpy as np
import jax
import jax.numpy as jnp
from jax import lax
from jax.experimental import pallas as pl
from jax.experimental.pallas import tpu as pltpu

F32 = jnp.float32
BF16 = jnp.bfloat16

GRID_W = 64
EPS = 1e-6
ATTN_HEADS = 8
ATTN_KV_HEADS = 2
ATTN_HEAD_DIM = 64
ATTN_GROUP = ATTN_HEADS // ATTN_KV_HEADS
ATTN_WIDTH = ATTN_HEADS * ATTN_HEAD_DIM
KV_WIDTH = ATTN_KV_HEADS * ATTN_HEAD_DIM
ROPE_THETA = 10000.0
HGRN_HEADS = 4
HGRN_HEAD_DIM = 128
HGRN_WIDTH = HGRN_HEADS * HGRN_HEAD_DIM
N_GROUPS = 4
EXPERTS_PER_GROUP = 8
N_EXPERTS = N_GROUPS * EXPERTS_PER_GROUP
TOP_K = 2
EXPERT_FF = 512

LANES = 128
SUBLANES = 8
MXU_DIM = 256
VMEM_LIMIT_BYTES = 48 * 1024 * 1024

TOKEN_TILE = 256
ATTN_Q_TILE = 512
ATTN_KV_TILE = 256
ATTN_KV_UNROLL = 4
ATTN_BOUND_SLACK = 1.02
ATTN_BOUND_MAX = 60.0
HGRN_CHUNK = 64
HGRN_DIAG = 8
MOE_BLOCK = 256
GATHER_UNROLL = 8
ROUTE_LANES = 8
NEG_BIG = -1e30

_QA0 = 0
_KA0 = _QA0 + ATTN_WIDTH
_VA0 = _KA0 + KV_WIDTH
_QR0 = _VA0 + KV_WIDTH
_FF0 = _QR0 + HGRN_WIDTH
_FB0 = _FF0 + HGRN_WIDTH
_IR0 = _FB0 + HGRN_WIDTH
_GR0 = _IR0 + HGRN_WIDTH
_QK_WIDTH = ATTN_WIDTH + KV_WIDTH


def _dot(a, b):
    return jnp.dot(a, b, preferred_element_type=F32)


def _dot_nt(a, b):
    return lax.dot_general(a, b, (((1,), (1,)), ((), ())), preferred_element_type=F32)


def _dot_tn(a, b):
    return lax.dot_general(a, b, (((0,), (0,)), ((), ())), preferred_element_type=F32)


def _split_bf16(x):
    hi = x.astype(BF16)
    lo = (x - hi.astype(F32)).astype(BF16)
    return hi, lo


def _sigmoid(x):
    return 1.0 / (1.0 + jnp.exp(-x))


def _silu(x):
    return x * _sigmoid(x)


def _params(*sem):
    return pltpu.CompilerParams(dimension_semantics=sem, vmem_limit_bytes=VMEM_LIMIT_BYTES)


def _head_sum_matrix():
    idx = np.arange(LANES) // ATTN_HEAD_DIM
    return jnp.asarray(idx[:, None] == idx[None, :], dtype=BF16)


def _head_rms_scale(x, ones_bd):
    ssq = _dot((x * x).astype(BF16), ones_bd)
    return lax.rsqrt(ssq * (1.0 / ATTN_HEAD_DIM) + EPS)


def _adaln_kernel(cond_ref, w_ref, b_ref, o_ref):
    s = _silu(cond_ref[...])
    s_hi, s_lo = _split_bf16(s)
    w_hi, w_lo = _split_bf16(w_ref[...])
    o_ref[...] = _dot(s_hi, w_hi) + _dot(s_lo, w_hi) + _dot(s_hi, w_lo) + b_ref[...]


def _adaln(cond, w_ada, b_ada):
    rows, d = cond.shape
    n = w_ada.shape[1]
    tn = n // 6
    return pl.pallas_call(
        _adaln_kernel,
        out_shape=jax.ShapeDtypeStruct((rows, n), F32),
        grid=(n // tn,),
        in_specs=[pl.BlockSpec((rows, d), lambda j: (0, 0)),
                  pl.BlockSpec((d, tn), lambda j: (0, j)),
                  pl.BlockSpec((1, tn), lambda j: (0, j))],
        out_specs=pl.BlockSpec((rows, tn), lambda j: (0, j)),
        compiler_params=_params("arbitrary"),
        name="adaln",
    )(cond, w_ada, b_ada.reshape(1, n))


def _rope_tables(n_ctx, n_lat):
    half = ATTN_HEAD_DIM // 2
    freqs = ROPE_THETA ** (-np.arange(0, half, 2, dtype=np.float64) / half)
    tok = np.arange(n_lat)
    pos = np.stack([tok // GRID_W, tok % GRID_W], axis=1).astype(np.float64)
    lane = np.arange(ATTN_HEAD_DIM)
    axis = lane // half
    fi = (lane % half) // 2
    ang = pos[:, axis] * freqs[fi][None, :]
    sign = np.where(lane % 2 == 1, 1.0, -1.0)
    cos = np.concatenate([np.ones((n_ctx, ATTN_HEAD_DIM)), np.cos(ang)], axis=0)
    sin = np.concatenate([np.zeros((n_ctx, ATTN_HEAD_DIM)), np.sin(ang) * sign], axis=0)
    reps = LANES // ATTN_HEAD_DIM
    return (jnp.asarray(np.tile(cos, (1, reps)), F32), jnp.asarray(np.tile(sin, (1, reps)), F32))


def _in_proj_kernel(n_ctx_tiles, ctx_ref, x_ref, shift_ref, scale_ref, g1_ref, w_ref, qkg_ref,
                    ones_ref, cos_ref, sin_ref,
                    qa_ref, ka_ref, va_ref, qr_ref, ff_ref, fb_ref, ir_ref, gr_ref):
    i = pl.program_id(1)
    xt = jnp.where(i < n_ctx_tiles, ctx_ref[0], x_ref[0])
    ms = jnp.mean(xt * xt, axis=-1, keepdims=True)
    h = xt * lax.rsqrt(ms + EPS) * g1_ref[...]
    h = h * (1.0 + scale_ref[0]) + shift_ref[0]
    p = _dot(h.astype(BF16), w_ref[...])

    ones_bd = ones_ref[...]
    cos = cos_ref[...]
    sin = sin_ref[...]
    even = lax.broadcasted_iota(jnp.int32, cos.shape, 1) % 2 == 0
    slabs = []
    for c0 in range(0, _QK_WIDTH, LANES):
        t = p[:, _QA0 + c0:_QA0 + c0 + LANES]
        t = t * _head_rms_scale(t, ones_bd) * qkg_ref[:, c0:c0 + LANES]
        partner = jnp.where(even, pltpu.roll(t, LANES - 1, 1), pltpu.roll(t, 1, 1))
        slabs.append((t * cos + partner * sin).astype(BF16))
    qa_ref[0] = jnp.concatenate(slabs[:ATTN_WIDTH // LANES], axis=1)
    ka_ref[0] = jnp.concatenate(slabs[ATTN_WIDTH // LANES:], axis=1)
    va_ref[0] = p[:, _VA0:_VA0 + KV_WIDTH].astype(BF16)

    qr_ref[0] = (_silu(p[:, _QR0:_QR0 + HGRN_WIDTH]) * (HGRN_HEAD_DIM ** -0.5)).astype(BF16)
    ff_ref[0] = p[:, _FF0:_FF0 + HGRN_WIDTH]
    fb_ref[0] = p[:, _FB0:_FB0 + HGRN_WIDTH]
    ir_ref[0] = p[:, _IR0:_IR0 + HGRN_WIDTH].astype(BF16)
    gr_ref[0] = _silu(p[:, _GR0:_GR0 + HGRN_WIDTH]).astype(BF16)


def _in_proj(ctx, x, shift, scale, g1, w_in, qkg, cos, sin):
    b, n_ctx, d = ctx.shape
    s = x.shape[1]
    tm = TOKEN_TILE
    nct = n_ctx // tm
    n_all = n_ctx + s
    nt = n_all // tm
    pw = w_in.shape[1]
    ones_bd = _head_sum_matrix()

    def tok_spec(w):
        return pl.BlockSpec((1, tm, w), lambda bi, i: (bi, i, 0))

    mod_spec = pl.BlockSpec((1, 1, d), lambda bi, i: (jnp.where(i < nct, b, bi), 0, 0))
    outs = [(ATTN_WIDTH, BF16), (KV_WIDTH, BF16), (KV_WIDTH, BF16), (HGRN_WIDTH, BF16),
            (HGRN_WIDTH, F32), (HGRN_WIDTH, F32), (HGRN_WIDTH, BF16), (HGRN_WIDTH, BF16)]
    lat_spec = pl.BlockSpec((1, tm, ATTN_WIDTH), lambda bi, i: (bi, jnp.maximum(i - nct, 0), 0))
    return pl.pallas_call(
        functools.partial(_in_proj_kernel, nct),
        out_shape=[jax.ShapeDtypeStruct((b, s if j == 0 else n_all, w), dt)
                   for j, (w, dt) in enumerate(outs)],
        grid=(b, nt),
        in_specs=[
            pl.BlockSpec((1, tm, d), lambda bi, i: (bi, jnp.minimum(i, nct - 1), 0)),
            pl.BlockSpec((1, tm, d), lambda bi, i: (bi, jnp.maximum(i - nct, 0), 0)),
            mod_spec, mod_spec,
            pl.BlockSpec((1, d), lambda bi, i: (0, 0)),
            pl.BlockSpec((d, pw), lambda bi, i: (0, 0)),
            pl.BlockSpec((1, _QK_WIDTH), lambda bi, i: (0, 0)),
            pl.BlockSpec((LANES, LANES), lambda bi, i: (0, 0)),
            pl.BlockSpec((tm, LANES), lambda bi, i: (i, 0)),
            pl.BlockSpec((tm, LANES), lambda bi, i: (i, 0)),
        ],
        out_specs=[lat_spec] + [tok_spec(w) for w, _ in outs[1:]],
        compiler_params=_params("arbitrary", "arbitrary"),
        name="in_proj",
    )(ctx, x, shift, scale, g1, w_in, qkg, ones_bd, cos, sin)


def _attention_kernel(n_kv_tiles, q_ref, k_ref, vt_ref, o_ref, qs_ref, s_ref, ksq_ref):
    tq = q_ref.shape[1]
    cols = ATTN_GROUP * tq
    for h in range(ATTN_GROUP):
        qs_ref[h * tq:(h + 1) * tq, :] = q_ref[0, :, h * ATTN_HEAD_DIM:(h + 1) * ATTN_HEAD_DIM]

    @pl.when(pl.program_id(2) == 0)
    def _():
        k = k_ref[0, 0].astype(F32)
        ksq = jnp.max(jnp.sum(k * k, axis=1, keepdims=True), axis=0, keepdims=True)
        ksq_ref[...] = jnp.broadcast_to(ksq, ksq_ref.shape)

    def tile(j):
        return pl.ds(pl.multiple_of(j * ATTN_KV_TILE, ATTN_KV_TILE), ATTN_KV_TILE)

    def scores(j):
        return _dot_nt(k_ref[0, 0, tile(j), :], qs_ref[...])

    def weighted(j, p):
        return _dot(vt_ref[0, 0, :, tile(j)], p)

    def finish(acc):
        o = acc[:ATTN_HEAD_DIM] / acc[ATTN_HEAD_DIM:ATTN_HEAD_DIM + 1]
        o_ref[0] = jnp.concatenate(
            [o[:, h * tq:(h + 1) * tq].T for h in range(ATTN_GROUP)], axis=1).astype(o_ref.dtype)

    qf = qs_ref[...].astype(F32)
    qsq = _dot_nt(jnp.ones((SUBLANES, ATTN_HEAD_DIM), BF16), (qf * qf).astype(BF16))[0:1]
    bound = jnp.sqrt(qsq * ksq_ref[0:1, 0:1]) * ATTN_BOUND_SLACK
    acc0 = jnp.zeros((vt_ref.shape[2], cols), F32)
    safe = jnp.max(bound) <= ATTN_BOUND_MAX

    @pl.when(safe)
    def _():
        def absorb(j, acc):
            return acc + weighted(j, jnp.exp2(scores(j) - bound).astype(BF16))

        def group(i, acc):
            for u in range(ATTN_KV_UNROLL):
                acc = absorb(ATTN_KV_UNROLL * i + u, acc)
            return acc

        n_groups = n_kv_tiles // ATTN_KV_UNROLL
        acc = lax.fori_loop(0, n_groups, group, acc0)
        for j in range(n_groups * ATTN_KV_UNROLL, n_kv_tiles):
            acc = absorb(j, acc)
        finish(acc)

    @pl.when(jnp.logical_not(safe))
    def _():
        def score(j, slot):
            s_ref[slot] = scores(j)

        def absorb(j, slot, carry):
            m, acc = carry
            s = s_ref[slot]
            m_new = jnp.maximum(m, jnp.max(s, axis=0, keepdims=True))
            p = jnp.exp2(s - m_new).astype(BF16)
            return m_new, jnp.exp2(m - m_new) * acc + weighted(j, p)

        def pair(i, carry):
            score(2 * i + 1, 1)
            carry = absorb(2 * i, 0, carry)
            score(2 * i + 2, 0)
            return absorb(2 * i + 1, 1, carry)

        n_pairs = (n_kv_tiles - 1) // 2
        score(0, 0)
        carry = lax.fori_loop(0, n_pairs, pair, (jnp.full((1, cols), NEG_BIG, F32), acc0))
        if n_kv_tiles % 2 == 1:
            carry = absorb(n_kv_tiles - 1, 0, carry)
        else:
            score(n_kv_tiles - 1, 1)
            carry = absorb(n_kv_tiles - 2, 0, carry)
            carry = absorb(n_kv_tiles - 1, 1, carry)
        finish(carry[1])


def _attention(qa, k_heads, vt_heads):
    b, s, _ = qa.shape
    n_all = k_heads.shape[2]
    tq = ATTN_Q_TILE
    gw = ATTN_GROUP * ATTN_HEAD_DIM
    vt_rows = vt_heads.shape[2]
    return pl.pallas_call(
        functools.partial(_attention_kernel, n_all // ATTN_KV_TILE),
        out_shape=jax.ShapeDtypeStruct((b, s, ATTN_WIDTH), BF16),
        grid=(b, ATTN_KV_HEADS, s // tq),
        in_specs=[pl.BlockSpec((1, tq, gw), lambda bi, kv, i: (bi, i, kv)),
                  pl.BlockSpec((1, 1, n_all, ATTN_HEAD_DIM), lambda bi, kv, i: (bi, kv, 0, 0)),
                  pl.BlockSpec((1, 1, vt_rows, n_all), lambda bi, kv, i: (bi, kv, 0, 0))],
        out_specs=pl.BlockSpec((1, tq, gw), lambda bi, kv, i: (bi, i, kv)),
        scratch_shapes=[pltpu.VMEM((ATTN_GROUP * tq, ATTN_HEAD_DIM), BF16),
                        pltpu.VMEM((2, ATTN_KV_TILE, ATTN_GROUP * tq), F32),
                        pltpu.VMEM((SUBLANES, LANES), F32)],
        compiler_params=_params("arbitrary", "arbitrary", "arbitrary"),
        name="attention",
    )(qa, k_heads, vt_heads)


def _hgrn_masks(reverse):
    c = HGRN_CHUNK
    t = lax.broadcasted_iota(jnp.int32, (c, c), 0)
    u = lax.broadcasted_iota(jnp.int32, (c, c), 1)
    tri = (u >= t) if reverse else (u <= t)
    levels = []
    size = c // 2
    while size >= HGRN_DIAG:
        same_parent = (t // (2 * size)) == (u // (2 * size))
        levels.append((size, same_parent if 2 * size < c else None))
        size //= 2
    return tri.astype(BF16), levels


def _hgrn_chunk(q, fr, v, lb, st, reverse, tri, levels):
    c = HGRN_CHUNK
    f = lb + (1.0 - lb) * _sigmoid(fr)
    k = 1.0 - f
    g_hi, g_lo = _split_bf16(jnp.log2(f))
    bcum = _dot(tri, g_hi) + _dot(tri, g_lo)
    qf = q.astype(F32)
    yield None

    end = 0 if reverse else c - 1
    b_end = bcum[end:end + 1, :]
    inter = _dot_nt((qf * jnp.exp2(bcum)).astype(BF16), st.astype(BF16))
    ke = (k * jnp.exp2(b_end - bcum)).astype(BF16)
    st_new = st * jnp.exp2(b_end) + _dot_tn(v, ke)

    a = None
    for size, mask in levels:
        q_rows, k_rows = [], []
        zeros = jnp.zeros((size, qf.shape[1]), BF16)
        for p0 in range(0, c, 2 * size):
            early = slice(p0, p0 + size)
            late = slice(p0 + size, p0 + 2 * size)
            r = p0 + size if reverse else p0 + size - 1
            ref = bcum[r:r + 1, :]
            q_sl, k_sl = (early, late) if reverse else (late, early)
            q_blk = (qf[q_sl] * jnp.exp2(bcum[q_sl] - ref)).astype(BF16)
            k_blk = (k[k_sl] * jnp.exp2(ref - bcum[k_sl])).astype(BF16)
            q_rows += [q_blk, zeros] if reverse else [zeros, q_blk]
            k_rows += [zeros, k_blk] if reverse else [k_blk, zeros]
        a_l = _dot_nt(jnp.concatenate(q_rows, axis=0), jnp.concatenate(k_rows, axis=0))
        if mask is not None:
            a_l = jnp.where(mask, a_l, 0.0)
        a = a_l if a is None else a + a_l
    yield None

    sub = lax.broadcasted_iota(jnp.int32, (HGRN_DIAG, 1), 0)
    lane = lax.broadcasted_iota(jnp.int32, (HGRN_DIAG, c), 1)
    diag_rows = []
    for blk in range(c // HGRN_DIAG):
        r0 = blk * HGRN_DIAG
        qb = qf[r0:r0 + HGRN_DIAG]
        bb = bcum[r0:r0 + HGRN_DIAG]
        rows_acc = jnp.zeros((HGRN_DIAG, c), F32)
        for si in range(HGRN_DIAG):
            r = r0 + si
            valid = (sub <= si) if reverse else (sub >= si)
            e = jnp.where(valid, bb - bcum[r:r + 1, :], NEG_BIG)
            col = jnp.sum(qb * k[r:r + 1, :] * jnp.exp2(e), axis=-1, keepdims=True)
            rows_acc = jnp.where(lane == r, col, rows_acc)
        diag_rows.append(rows_acc)
    a = a + jnp.concatenate(diag_rows, axis=0)
    yield None

    yield inter + _dot(a.astype(BF16), v), st_new


_HGRN_STAGES = 4


def _hgrn_kernel(qf_ref, if_ref, ff_ref, qb_ref, ib_ref, fb_ref, lb_ref, of_ref, ob_ref,
                 sf_ref, sb_ref):
    @pl.when(pl.program_id(1) == 0)
    def _():
        sf_ref[...] = jnp.zeros_like(sf_ref)
        sb_ref[...] = jnp.zeros_like(sb_ref)

    dirs = ((False, qf_ref, if_ref, ff_ref, of_ref, sf_ref),
            (True, qb_ref, ib_ref, fb_ref, ob_ref, sb_ref))
    masks = [_hgrn_masks(reverse) for reverse, *_ in dirs]
    chunks = []
    for h in range(HGRN_HEADS):
        sl = slice(h * HGRN_HEAD_DIM, (h + 1) * HGRN_HEAD_DIM)
        for d, (reverse, q_ref, i_ref, f_ref, o_ref, s_ref) in enumerate(dirs):
            gen = _hgrn_chunk(q_ref[0, :, sl], f_ref[0, :, sl], i_ref[0, :, sl],
                              lb_ref[d:d + 1, sl], s_ref[h], reverse, *masks[d])
            chunks.append((gen, o_ref, s_ref, h, sl))

    for gen, *_ in chunks:
        next(gen)
    for step in range(len(chunks) + _HGRN_STAGES - 2):
        for stage in range(1, _HGRN_STAGES):
            idx = step - (stage - 1)
            if 0 <= idx < len(chunks):
                gen, o_ref, s_ref, h, sl = chunks[idx]
                result = next(gen)
                if stage == _HGRN_STAGES - 1:
                    o, st = result
                    o_ref[0, :, sl] = o
                    s_ref[h] = st


def _hgrn(qr, ir, ff, fb, lb, n_ctx):
    b, n_all, w = qr.shape
    c = HGRN_CHUNK
    nc = n_all // c
    ncc = n_ctx // c

    def fwd(bi, j):
        return (bi, j, 0)

    def bwd(bi, j):
        return (bi, jnp.where(j < ncc, ncc - 1 - j, nc - 1 - (j - ncc)), 0)

    blk = (1, c, w)
    state = pltpu.VMEM((HGRN_HEADS, HGRN_HEAD_DIM, HGRN_HEAD_DIM), F32)
    return pl.pallas_call(
        _hgrn_kernel,
        out_shape=[jax.ShapeDtypeStruct((b, n_all, w), F32)] * 2,
        grid=(b, nc),
        in_specs=[pl.BlockSpec(blk, fwd), pl.BlockSpec(blk, fwd), pl.BlockSpec(blk, fwd),
                  pl.BlockSpec(blk, bwd), pl.BlockSpec(blk, bwd), pl.BlockSpec(blk, bwd),
                  pl.BlockSpec((2, w), lambda bi, j: (0, 0))],
        out_specs=[pl.BlockSpec(blk, fwd), pl.BlockSpec(blk, bwd)],
        scratch_shapes=[state, state],
        compiler_params=_params("arbitrary", "arbitrary"),
        name="hgrn",
    )(qr, ir, ff, qr, ir, fb, lb)


_ROUTE_GROUP_LANE0 = N_EXPERTS


def _lane_min_index(cond, lane):
    return jnp.min(jnp.where(cond, lane, LANES), axis=-1, keepdims=True)


def _out_proj_kernel(oa_ref, of_ref, ob_ref, gr_ref, x_ref, gate_ref, shift_ref, scale_ref,
                     ag_ref, hg_ref, g2_ref, w_ref, ones_ref, wr_hi_ref, wr_lo_ref, br_ref,
                     x1_ref, h2_ref, ri_ref, rw_ref, cnt_ref, carry_ref):
    first = (pl.program_id(0) == 0) & (pl.program_id(1) == 0)

    @pl.when(first)
    def _():
        carry_ref[...] = jnp.zeros_like(carry_ref)

    tm = x_ref.shape[1]

    ones_bd = ones_ref[...]
    slabs = []
    for c0 in range(0, ATTN_WIDTH, LANES):
        t = oa_ref[0, :, c0:c0 + LANES].astype(F32)
        slabs.append((t * _head_rms_scale(t, ones_bd) * ag_ref[:, c0:c0 + LANES]).astype(BF16))
    orr = of_ref[0] + ob_ref[0]
    parts = []
    for h in range(HGRN_HEADS):
        sl = slice(h * HGRN_HEAD_DIM, (h + 1) * HGRN_HEAD_DIM)
        oh = orr[:, sl]
        parts.append(oh * lax.rsqrt(jnp.mean(oh * oh, axis=-1, keepdims=True) + EPS))
    orr = jnp.concatenate(parts, axis=1) * hg_ref[...] * gr_ref[0].astype(F32)
    mix_in = jnp.concatenate(slabs + [orr.astype(BF16)], axis=1)
    x1 = x_ref[0] + gate_ref[0] * _dot(mix_in, w_ref[...])
    x1_ref[0] = x1
    h2 = x1 * lax.rsqrt(jnp.mean(x1 * x1, axis=-1, keepdims=True) + EPS) * g2_ref[...]
    h2 = h2 * (1.0 + scale_ref[0]) + shift_ref[0]
    for c in range(h2_ref.shape[1]):
        h2_ref[:, c] = h2[:, c * LANES:(c + 1) * LANES].reshape(tm // SUBLANES, SUBLANES, LANES)

    h_hi, h_lo = _split_bf16(h2)
    logits = (_dot(h_hi, wr_hi_ref[...]) + _dot(h_lo, wr_hi_ref[...])
              + _dot(h_hi, wr_lo_ref[...]) + br_ref[...])
    lane = lax.broadcasted_iota(jnp.int32, logits.shape, 1)
    is_grp = (lane >= _ROUTE_GROUP_LANE0) & (lane < _ROUTE_GROUP_LANE0 + N_GROUPS)
    lg = jnp.where(is_grp, logits, NEG_BIG)
    mg = jnp.max(lg, axis=-1, keepdims=True)
    g_sel = _lane_min_index(lg == mg, lane) - _ROUTE_GROUP_LANE0
    pg_top = 1.0 / jnp.sum(jnp.exp(lg - mg), axis=-1, keepdims=True)
    in_grp = (lane < N_EXPERTS) & ((lane // EXPERTS_PER_GROUP) == g_sel)
    le = jnp.where(in_grp, logits, NEG_BIG)
    m1 = jnp.max(le, axis=-1, keepdims=True)
    e1 = _lane_min_index(le == m1, lane)
    le2 = jnp.where(lane == e1, NEG_BIG, le)
    m2 = jnp.max(le2, axis=-1, keepdims=True)
    e2 = _lane_min_index(le2 == m2, lane)
    r2 = jnp.exp(m2 - m1)
    w1 = pg_top / (1.0 + r2)
    w2 = pg_top * r2 / (1.0 + r2)

    onehot = ((lane == e1) | (lane == e2)).astype(BF16)
    rt = lax.broadcasted_iota(jnp.int32, (tm, tm), 0)
    ru = lax.broadcasted_iota(jnp.int32, (tm, tm), 1)
    before = _dot((ru < rt).astype(BF16), onehot) + carry_ref[0:1, :]
    rank1 = jnp.sum(jnp.where(lane == e1, before, 0.0), axis=-1, keepdims=True)
    rank2 = jnp.sum(jnp.where(lane == e2, before, 0.0), axis=-1, keepdims=True)
    total = carry_ref[0:1, :] + jnp.sum(onehot.astype(F32), axis=0, keepdims=True)
    carry_ref[...] = jnp.broadcast_to(total, carry_ref.shape)
    cnt_ref[...] = jnp.broadcast_to(total, cnt_ref.shape).astype(jnp.int32)

    rl = lax.broadcasted_iota(jnp.int32, (tm, ROUTE_LANES), 1)
    ri = jnp.where(rl == 0, e1, jnp.where(rl == 1, e2, jnp.where(
        rl == 2, rank1.astype(jnp.int32), jnp.where(rl == 3, rank2.astype(jnp.int32), 0))))
    ri_ref[0] = ri
    rw_ref[0] = jnp.where(rl == 0, w1, jnp.where(rl == 1, w2, 0.0))


def _out_proj(oa, o_f, o_b, gr, x, gate, shift, scale, ag, hg, g2, w_out, wr_hi, wr_lo, br, n_ctx):
    b, s, d = x.shape
    tm = TOKEN_TILE
    off = n_ctx // tm
    ones_bd = _head_sum_matrix()

    def lat(w):
        return pl.BlockSpec((1, tm, w), lambda bi, i: (bi, i, 0))

    def allrows(w):
        return pl.BlockSpec((1, tm, w), lambda bi, i: (bi, i + off, 0))

    def mod():
        return pl.BlockSpec((1, 1, d), lambda bi, i: (bi, 0, 0))

    def const(shape):
        return pl.BlockSpec(shape, lambda bi, i: (0,) * len(shape))

    return pl.pallas_call(
        _out_proj_kernel,
        out_shape=[jax.ShapeDtypeStruct((b, s, d), F32),
                   jax.ShapeDtypeStruct((b * s // SUBLANES, d // LANES, SUBLANES, LANES), F32),
                   jax.ShapeDtypeStruct((b, s, ROUTE_LANES), jnp.int32),
                   jax.ShapeDtypeStruct((b, s, ROUTE_LANES), F32),
                   jax.ShapeDtypeStruct((SUBLANES, LANES), jnp.int32)],
        grid=(b, s // tm),
        in_specs=[lat(ATTN_WIDTH), allrows(HGRN_WIDTH), allrows(HGRN_WIDTH), allrows(HGRN_WIDTH),
                  lat(d), mod(), mod(), mod(),
                  const((1, ATTN_WIDTH)), const((1, HGRN_WIDTH)), const((1, d)),
                  const((ATTN_WIDTH + HGRN_WIDTH, d)), const((LANES, LANES)),
                  const((d, LANES)), const((d, LANES)), const((1, LANES))],
        out_specs=[lat(d),
                   pl.BlockSpec((tm // SUBLANES, d // LANES, SUBLANES, LANES),
                                lambda bi, i: (bi * (s // tm) + i, 0, 0, 0)),
                   lat(ROUTE_LANES), lat(ROUTE_LANES), const((SUBLANES, LANES))],
        scratch_shapes=[pltpu.VMEM((SUBLANES, LANES), F32)],
        compiler_params=_params("arbitrary", "arbitrary"),
        name="out_proj",
    )(oa, o_f, o_b, gr, x, gate, shift, scale, ag, hg, g2, w_out, ones_bd, wr_hi, wr_lo, br)


def _slab_copies(stage, hbm, row0, sem, to_hbm):
    copies = []
    for c in range(stage.shape[0]):
        view = hbm.at[pl.ds(row0, stage.shape[1]), c, :]
        src, dst = (stage.at[c], view) if to_hbm else (view, stage.at[c])
        copies.append(pltpu.make_async_copy(src, dst, sem))
    return copies


def _stage_store(stage, x):
    for c in range(stage.shape[0]):
        stage[c] = x[:, c * LANES:(c + 1) * LANES]


def _stage_load(stage):
    return jnp.concatenate([stage[c] for c in range(stage.shape[0])], axis=1)


_DISPATCH_SLOTS = 3


def _dispatch_kernel(dest_ref, h2_hbm, xs_init_hbm, xs_hbm, buf, sem_in, sem_out):
    del xs_init_hbm
    i = pl.program_id(0)
    n = pl.num_programs(0)
    groups = buf.shape[1]
    tm = groups * SUBLANES
    slot = i % _DISPATCH_SLOTS

    def fetch(t, sl):
        return pltpu.make_async_copy(h2_hbm.at[pl.ds(t * groups, groups)], buf.at[sl],
                                     sem_in.at[sl])

    def wait_rows(sl):
        for _ in range(TOP_K):
            pltpu.make_async_copy(h2_hbm.at[pl.ds(0, groups)], buf.at[sl], sem_out.at[sl]).wait()

    @pl.when(i == 0)
    def _():
        fetch(0, 0).start()

    @pl.when(i >= 2)
    def _():
        wait_rows((i + 1) % _DISPATCH_SLOTS)

    @pl.when(i + 1 < n)
    def _():
        fetch(i + 1, (i + 1) % _DISPATCH_SLOTS).start()

    fetch(i, slot).wait()

    def body(g, carry):
        rows = [[dest_ref[0, 0, k * tm + g * SUBLANES + u] for k in range(TOP_K)]
                for u in range(SUBLANES)]
        for u in range(SUBLANES):
            for k in range(TOP_K):
                pltpu.make_async_copy(buf.at[slot, g, :, u, :], xs_hbm.at[rows[u][k]],
                                      sem_out.at[slot]).start()
        return carry
    lax.fori_loop(0, groups, body, 0)

    @pl.when(i == n - 1)
    def _():
        wait_rows(slot)

    @pl.when((i == n - 1) & (i >= 1))
    def _():
        wait_rows((i - 1) % _DISPATCH_SLOTS)


def _dispatch(dest_tiles, h2_tiles, n_slots):
    nt = dest_tiles.shape[0]
    n_groups, n_slabs, _, _ = h2_tiles.shape
    groups = n_groups // nt
    xs_init = jnp.zeros((n_slots, n_slabs, LANES), F32)
    return pl.pallas_call(
        _dispatch_kernel,
        out_shape=jax.ShapeDtypeStruct(xs_init.shape, F32),
        grid=(nt,),
        in_specs=[pl.BlockSpec((1, 1, dest_tiles.shape[2]), lambda i: (i, 0, 0),
                               memory_space=pltpu.SMEM),
                  pl.BlockSpec(memory_space=pl.ANY), pl.BlockSpec(memory_space=pl.ANY)],
        out_specs=pl.BlockSpec(memory_space=pl.ANY),
        scratch_shapes=[pltpu.VMEM((_DISPATCH_SLOTS, groups, n_slabs, SUBLANES, LANES), F32),
                        pltpu.SemaphoreType.DMA((_DISPATCH_SLOTS,)),
                        pltpu.SemaphoreType.DMA((_DISPATCH_SLOTS,))],
        input_output_aliases={2: 0},
        compiler_params=_params("arbitrary"),
        name="dispatch",
    )(dest_tiles, h2_tiles, xs_init)


def _experts_kernel(meta_ref, xs_hbm, wg_ref, wu_ref, wd_ref, ys_hbm,
                    xstage, ystage, wg_bf, wu_bf, wd_bf, sem_in, sem_out):
    j = pl.program_id(0)
    last = pl.num_programs(0) - 1
    n_used = meta_ref[0]
    slot = j % 2

    def fetch(blk, sl):
        for cp in _slab_copies(xstage.at[sl], xs_hbm, blk * MOE_BLOCK, sem_in.at[sl], False):
            cp.start()

    def wait_out(sl):
        for cp in _slab_copies(ystage.at[sl], ys_hbm, 0, sem_out.at[sl], True):
            cp.wait()

    @pl.when((j == 0) & (n_used > 0))
    def _():
        fetch(0, 0)

    @pl.when(j + 1 < n_used)
    def _():
        fetch(j + 1, 1 - slot)

    new_expert = (j == 0) | (meta_ref[1 + j] != meta_ref[jnp.maximum(j, 1)])

    @pl.when((j < n_used) & new_expert)
    def _():
        wg_bf[...] = wg_ref[0].astype(BF16)
        wu_bf[...] = wu_ref[0].astype(BF16)
        wd_bf[...] = wd_ref[0].astype(BF16)

    @pl.when(j >= 2)
    def _():
        wait_out(slot)

    @pl.when(j < n_used)
    def _():
        for cp in _slab_copies(xstage.at[slot], xs_hbm, 0, sem_in.at[slot], False):
            cp.wait()
        xb = _stage_load(xstage.at[slot]).astype(BF16)
        a = _silu(_dot(xb, wg_bf[...])) * _dot(xb, wu_bf[...])
        _stage_store(ystage.at[slot], _dot(a.astype(BF16), wd_bf[...]))

    @pl.when(j >= n_used)
    def _():
        ystage[slot] = jnp.zeros(ystage.shape[1:], F32)

    for cp in _slab_copies(ystage.at[slot], ys_hbm, j * MOE_BLOCK, sem_out.at[slot], True):
        cp.start()

    @pl.when(j == last)
    def _():
        wait_out(slot)

    @pl.when((j == last) & (j >= 1))
    def _():
        wait_out(1 - slot)


def _experts(meta, xs, wg, wu, wd):
    n_slots, n_slabs, _ = xs.shape
    n_blk = n_slots // MOE_BLOCK
    d = wg.shape[1]
    ff = wg.shape[2]

    def wspec(shape):
        return pl.BlockSpec((1,) + shape, lambda j, meta: (meta[1 + j], 0, 0))

    stage = pltpu.VMEM((2, n_slabs, MOE_BLOCK, LANES), F32)
    return pl.pallas_call(
        _experts_kernel,
        out_shape=jax.ShapeDtypeStruct(xs.shape, F32),
        grid_spec=pltpu.PrefetchScalarGridSpec(
            num_scalar_prefetch=1,
            grid=(n_blk,),
            in_specs=[pl.BlockSpec(memory_space=pl.ANY),
                      wspec((d, ff)), wspec((d, ff)), wspec((ff, d))],
            out_specs=pl.BlockSpec(memory_space=pl.ANY),
            scratch_shapes=[stage, stage, pltpu.VMEM((d, ff), BF16), pltpu.VMEM((d, ff), BF16),
                            pltpu.VMEM((ff, d), BF16),
                            pltpu.SemaphoreType.DMA((2,)), pltpu.SemaphoreType.DMA((2,))],
        ),
        compiler_params=_params("arbitrary"),
        name="experts",
    )(meta, xs, wg, wu, wd)


def _combine_kernel(dest_cur_ref, dest_next_ref, ys_hbm, x1_ref, gate_ref, rw_ref, o_ref,
                    buf, sem):
    i = pl.program_id(0)
    slot = i % 2
    tm = x1_ref.shape[0]
    groups = tm // SUBLANES

    def start(idx_ref, sl):
        for k in range(TOP_K):
            def body(g, carry, k=k):
                rows = [idx_ref[0, 0, k * tm + g * SUBLANES + u] for u in range(SUBLANES)]
                for u in range(SUBLANES):
                    pltpu.make_async_copy(ys_hbm.at[rows[u]], buf.at[sl, k, g, :, u, :],
                                          sem.at[sl, k]).start()
                return carry
            lax.fori_loop(0, groups, body, 0)

    def wait(sl, k):
        for u in range(SUBLANES):
            pltpu.make_async_copy(ys_hbm.at[pl.ds(0, groups)], buf.at[sl, k, :, :, u, :],
                                  sem.at[sl, k]).wait()

    @pl.when(i == 0)
    def _():
        start(dest_cur_ref, 0)

    @pl.when(i + 1 < pl.num_programs(0))
    def _():
        start(dest_next_ref, 1 - slot)

    rw = rw_ref[...]
    moe = None
    for k in range(TOP_K):
        wait(slot, k)
        rows = jnp.concatenate([buf[slot, k, :, c].reshape(tm, LANES)
                                for c in range(buf.shape[3])], axis=1)
        term = rw[:, k:k + 1] * rows
        moe = term if moe is None else moe + term
    o_ref[...] = x1_ref[...] + gate_ref[0] * moe


def _combine(dest_tiles, ys, x1, gate, rw):
    n_tok, d = x1.shape
    tm = TOKEN_TILE
    nt = n_tok // tm
    per_batch = nt // gate.shape[0]
    idx_blk = (1, 1, TOP_K * tm)
    return pl.pallas_call(
        _combine_kernel,
        out_shape=jax.ShapeDtypeStruct((n_tok, d), F32),
        grid=(nt,),
        in_specs=[
            pl.BlockSpec(idx_blk, lambda i: (i, 0, 0), memory_space=pltpu.SMEM),
            pl.BlockSpec(idx_blk, lambda i: (jnp.minimum(i + 1, nt - 1), 0, 0),
                         memory_space=pltpu.SMEM),
            pl.BlockSpec(memory_space=pl.ANY),
            pl.BlockSpec((tm, d), lambda i: (i, 0)),
            pl.BlockSpec((1, 1, d), lambda i: (i // per_batch, 0, 0)),
            pl.BlockSpec((tm, ROUTE_LANES), lambda i: (i, 0)),
        ],
        out_specs=pl.BlockSpec((tm, d), lambda i: (i, 0)),
        scratch_shapes=[pltpu.VMEM((2, TOP_K, tm // SUBLANES, d // LANES, SUBLANES, LANES), F32),
                        pltpu.SemaphoreType.DMA((2, TOP_K))],
        compiler_params=_params("arbitrary"),
        name="combine",
    )(dest_tiles, dest_tiles, ys, x1, gate, rw)


def _layer(x, ctx, c, c_ctx, w_ada, b_ada, norm1_g, norm2_g, w_in, q_norm_g, k_norm_g, attn_out_g,
           lb, hgrn_out_g, w_out, w_router_grp, b_router_grp, w_router_exp, b_router_exp,
           w_exp_gate, w_exp_up, w_exp_down):
    b, s, d = x.shape
    n_ctx = ctx.shape[1]
    assert n_ctx % TOKEN_TILE == 0 and s % TOKEN_TILE == 0 and s % GRID_W == 0
    assert n_ctx % HGRN_CHUNK == 0 and (n_ctx + s) % ATTN_KV_TILE == 0
    n_all = n_ctx + s

    cond = jnp.zeros((2 * SUBLANES, d), F32).at[:b].set(c).at[b].set(c_ctx)
    assert b + 1 <= cond.shape[0]
    mods = _adaln(cond, w_ada, b_ada)[:b + 1].reshape(b + 1, 1, 6, d)
    sh1, sc1, gt1, sh2, sc2, gt2 = (mods[:, :, m] for m in range(6))

    scale_q = ATTN_HEAD_DIM ** -0.5 * np.log2(np.e)
    qkg = jnp.concatenate([jnp.tile(q_norm_g, ATTN_HEADS) * scale_q,
                           jnp.tile(k_norm_g, ATTN_KV_HEADS)]).reshape(1, _QK_WIDTH)
    cos, sin = _rope_tables(n_ctx, s)
    qa, ka, va, qr, ff, fb, ir, gr = _in_proj(
        ctx, x, sh1, sc1, norm1_g.reshape(1, d), w_in.astype(BF16), qkg, cos, sin)

    def kv_heads(t):
        return t.reshape(b, n_all, ATTN_KV_HEADS, ATTN_HEAD_DIM).transpose(0, 2, 1, 3)

    vt = kv_heads(va).transpose(0, 1, 3, 2)
    vt = jnp.concatenate([vt, jnp.ones((b, ATTN_KV_HEADS, 2 * SUBLANES, n_all), BF16)], axis=2)
    oa = _attention(qa, kv_heads(ka), vt)
    o_f, o_b = _hgrn(qr, ir, ff, fb, lb, n_ctx)

    w_router = jnp.zeros((d, LANES), F32)
    w_router = w_router.at[:, :N_EXPERTS].set(w_router_exp)
    w_router = w_router.at[:, _ROUTE_GROUP_LANE0:_ROUTE_GROUP_LANE0 + N_GROUPS].set(w_router_grp)
    b_router = jnp.zeros((1, LANES), F32)
    b_router = b_router.at[0, :N_EXPERTS].set(b_router_exp)
    b_router = b_router.at[0, _ROUTE_GROUP_LANE0:_ROUTE_GROUP_LANE0 + N_GROUPS].set(b_router_grp)
    wr_hi, wr_lo = _split_bf16(w_router)
    x1, h2, ri, rw, counts = _out_proj(
        oa, o_f, o_b, gr, x, gt1[:b], sh2[:b], sc2[:b], attn_out_g.reshape(1, -1),
        hgrn_out_g.reshape(1, -1), norm2_g.reshape(1, d), w_out.astype(BF16), wr_hi, wr_lo,
        b_router, n_ctx)

    n_tok = b * s
    counts = counts[0, :N_EXPERTS]
    padded = (counts + MOE_BLOCK - 1) // MOE_BLOCK * MOE_BLOCK
    pend = jnp.cumsum(padded)
    pstart = pend - padded
    n_blk = n_tok * TOP_K // MOE_BLOCK + N_EXPERTS
    ri = ri.reshape(n_tok, ROUTE_LANES)
    e_id = ri[:, :TOP_K]
    dest = pstart[e_id] + ri[:, TOP_K:2 * TOP_K]
    blk_start = jnp.arange(n_blk, dtype=jnp.int32) * MOE_BLOCK
    blk_e = jnp.minimum(jnp.sum(blk_start[:, None] >= pend[None, :], axis=1), N_EXPERTS - 1)
    n_used = pend[-1] // MOE_BLOCK
    meta = jnp.concatenate([n_used[None], blk_e]).astype(jnp.int32)
    nt = n_tok // TOKEN_TILE
    dest_tiles = dest.reshape(nt, TOKEN_TILE, TOP_K).transpose(0, 2, 1).reshape(nt, 1, -1)

    xs = _dispatch(dest_tiles, h2, n_blk * MOE_BLOCK)
    ys = _experts(meta, xs, w_exp_gate, w_exp_up, w_exp_down)
    out = _combine(dest_tiles, ys, x1.reshape(n_tok, d), gt2[:b], rw.reshape(n_tok, ROUTE_LANES))
    return out.reshape(b, s, d)


def kernel(x, c, ctx, c_ctx, w_ada, b_ada, norm1_g, norm2_g, w_in, q_norm_g, k_norm_g, attn_out_g,
           hgrn_lb, hgrn_out_g, w_out, w_router_grp, b_router_grp, w_router_exp, b_router_exp,
           w_exp_gate, w_exp_up, w_exp_down):
    depth = w_in.shape[0]
    assert depth == 1, "context stream update between layers is not implemented"
    lb_all = jnp.cumsum(jax.nn.softmax(hgrn_lb.astype(F32), axis=1), axis=1)
    layer = 0
    return _layer(x, ctx, c, c_ctx, w_ada[layer], b_ada[layer], norm1_g[layer], norm2_g[layer],
                  w_in[layer], q_norm_g[layer], k_norm_g[layer], attn_out_g[layer],
                  lb_all[:, layer], hgrn_out_g[layer], w_out[layer], w_router_grp[layer],
                  b_router_grp[layer], w_router_exp[layer], b_router_exp[layer],
                  w_exp_gate[layer], w_exp_up[layer], w_exp_down[layer])
```

```python
import functools

import numpy as np
import jax
import jax.numpy as jnp
from jax import lax
from jax.experimental import pallas as pl
from jax.experimental.pallas import tpu as pltpu

F32 = jnp.float32
BF16 = jnp.bfloat16

GRID_W = 64
EPS = 1e-6
ATTN_HEADS = 8
ATTN_KV_HEADS = 2
ATTN_HEAD_DIM = 64
ATTN_GROUP = ATTN_HEADS // ATTN_KV_HEADS
ATTN_WIDTH = ATTN_HEADS * ATTN_HEAD_DIM
KV_WIDTH = ATTN_KV_HEADS * ATTN_HEAD_DIM
ROPE_THETA = 10000.0
HGRN_HEADS = 4
HGRN_HEAD_DIM = 128
HGRN_WIDTH = HGRN_HEADS * HGRN_HEAD_DIM
N_GROUPS = 4
EXPERTS_PER_GROUP = 8
N_EXPERTS = N_GROUPS * EXPERTS_PER_GROUP
TOP_K = 2
EXPERT_FF = 512

LANES = 128
SUBLANES = 8
MXU_DIM = 256
VMEM_LIMIT_BYTES = 48 * 1024 * 1024

TOKEN_TILE = 256
ATTN_Q_TILE = 1024
ATTN_KV_TILE = 256
ATTN_KV_UNROLL = 4
ATTN_BOUND_SLACK = 1.02
ATTN_BOUND_MAX = 60.0
HGRN_CHUNK = 64
HGRN_STEP_CHUNKS = 4
MOE_BLOCK = 256
GATHER_UNROLL = 8
ROUTE_LANES = 8
NEG_BIG = -1e30

_QA0 = 0
_KA0 = _QA0 + ATTN_WIDTH
_VA0 = _KA0 + KV_WIDTH
_QR0 = _VA0 + KV_WIDTH
_FF0 = _QR0 + HGRN_WIDTH
_FB0 = _FF0 + HGRN_WIDTH
_IR0 = _FB0 + HGRN_WIDTH
_GR0 = _IR0 + HGRN_WIDTH
_QK_WIDTH = ATTN_WIDTH + KV_WIDTH


def _dot(a, b):
    return jnp.dot(a, b, preferred_element_type=F32)


def _dot_nt(a, b):
    return lax.dot_general(a, b, (((1,), (1,)), ((), ())), preferred_element_type=F32)


def _dot_tn(a, b):
    return lax.dot_general(a, b, (((0,), (0,)), ((), ())), preferred_element_type=F32)


def _split_bf16(x):
    hi = x.astype(BF16)
    lo = (x - hi.astype(F32)).astype(BF16)
    return hi, lo


def _sigmoid(x):
    return 1.0 / (1.0 + jnp.exp(-x))


def _silu(x):
    return x * _sigmoid(x)


def _params(*sem):
    return pltpu.CompilerParams(dimension_semantics=sem, vmem_limit_bytes=VMEM_LIMIT_BYTES)


def _head_sum_matrix():
    idx = np.arange(LANES) // ATTN_HEAD_DIM
    return jnp.asarray(idx[:, None] == idx[None, :], dtype=BF16)


def _head_rms_scale(x, ones_bd):
    ssq = _dot((x * x).astype(BF16), ones_bd)
    return lax.rsqrt(ssq * (1.0 / ATTN_HEAD_DIM) + EPS)


def _adaln_kernel(cond_ref, w_ref, b_ref, o_ref):
    s = _silu(cond_ref[...])
    s_hi, s_lo = _split_bf16(s)
    w_hi, w_lo = _split_bf16(w_ref[...])
    o_ref[...] = _dot(s_hi, w_hi) + _dot(s_lo, w_hi) + _dot(s_hi, w_lo) + b_ref[...]


def _adaln(cond, w_ada, b_ada):
    rows, d = cond.shape
    n = w_ada.shape[1]
    tn = n // 6
    return pl.pallas_call(
        _adaln_kernel,
        out_shape=jax.ShapeDtypeStruct((rows, n), F32),
        grid=(n // tn,),
        in_specs=[pl.BlockSpec((rows, d), lambda j: (0, 0)),
                  pl.BlockSpec((d, tn), lambda j: (0, j)),
                  pl.BlockSpec((1, tn), lambda j: (0, j))],
        out_specs=pl.BlockSpec((rows, tn), lambda j: (0, j)),
        compiler_params=_params("arbitrary"),
        name="adaln",
    )(cond, w_ada, b_ada.reshape(1, n))


def _rope_tables(n_ctx, n_lat):
    half = ATTN_HEAD_DIM // 2
    freqs = ROPE_THETA ** (-np.arange(0, half, 2, dtype=np.float64) / half)
    tok = np.arange(n_lat)
    pos = np.stack([tok // GRID_W, tok % GRID_W], axis=1).astype(np.float64)
    lane = np.arange(ATTN_HEAD_DIM)
    axis = lane // half
    fi = (lane % half) // 2
    ang = pos[:, axis] * freqs[fi][None, :]
    sign = np.where(lane % 2 == 1, 1.0, -1.0)
    cos = np.concatenate([np.ones((n_ctx, ATTN_HEAD_DIM)), np.cos(ang)], axis=0)
    sin = np.concatenate([np.zeros((n_ctx, ATTN_HEAD_DIM)), np.sin(ang) * sign], axis=0)
    reps = LANES // ATTN_HEAD_DIM
    return (jnp.asarray(np.tile(cos, (1, reps)), F32), jnp.asarray(np.tile(sin, (1, reps)), F32))


def _in_proj_kernel(n_ctx_tiles, ctx_ref, x_ref, shift_ref, scale_ref, g1_ref, w_ref, qkg_ref,
                    ones_ref, cos_ref, sin_ref,
                    qa_ref, ka_ref, va_ref, qr_ref, ff_ref, fb_ref, ir_ref, gr_ref):
    i = pl.program_id(1)
    xt = jnp.where(i < n_ctx_tiles, ctx_ref[0], x_ref[0])
    ms = jnp.mean(xt * xt, axis=-1, keepdims=True)
    h = xt * lax.rsqrt(ms + EPS) * g1_ref[...]
    h = h * (1.0 + scale_ref[0]) + shift_ref[0]
    p = _dot(h.astype(BF16), w_ref[...])

    ones_bd = ones_ref[...]
    cos = cos_ref[...]
    sin = sin_ref[...]
    even = lax.broadcasted_iota(jnp.int32, cos.shape, 1) % 2 == 0
    slabs = []
    for c0 in range(0, _QK_WIDTH, LANES):
        t = p[:, _QA0 + c0:_QA0 + c0 + LANES]
        t = t * _head_rms_scale(t, ones_bd) * qkg_ref[:, c0:c0 + LANES]
        partner = jnp.where(even, pltpu.roll(t, LANES - 1, 1), pltpu.roll(t, 1, 1))
        slabs.append((t * cos + partner * sin).astype(BF16))
    qa_ref[0] = jnp.concatenate(slabs[:ATTN_WIDTH // LANES], axis=1)
    ka_ref[0] = jnp.concatenate(slabs[ATTN_WIDTH // LANES:], axis=1)
    va_ref[0] = p[:, _VA0:_VA0 + KV_WIDTH].astype(BF16)

    qr_ref[0] = (_silu(p[:, _QR0:_QR0 + HGRN_WIDTH]) * (HGRN_HEAD_DIM ** -0.5)).astype(BF16)
    ff_ref[0] = p[:, _FF0:_FF0 + HGRN_WIDTH]
    fb_ref[0] = p[:, _FB0:_FB0 + HGRN_WIDTH]
    ir_ref[0] = p[:, _IR0:_IR0 + HGRN_WIDTH].astype(BF16)
    gr_ref[0] = _silu(p[:, _GR0:_GR0 + HGRN_WIDTH]).astype(BF16)


def _in_proj(ctx, x, shift, scale, g1, w_in, qkg, cos, sin):
    b, n_ctx, d = ctx.shape
    s = x.shape[1]
    tm = TOKEN_TILE
    nct = n_ctx // tm
    n_all = n_ctx + s
    nt = n_all // tm
    pw = w_in.shape[1]
    ones_bd = _head_sum_matrix()

    def tok_spec(w):
        return pl.BlockSpec((1, tm, w), lambda bi, i: (bi, i, 0))

    mod_spec = pl.BlockSpec((1, 1, d), lambda bi, i: (jnp.where(i < nct, b, bi), 0, 0))
    outs = [(ATTN_WIDTH, BF16), (KV_WIDTH, BF16), (KV_WIDTH, BF16), (HGRN_WIDTH, BF16),
            (HGRN_WIDTH, F32), (HGRN_WIDTH, F32), (HGRN_WIDTH, BF16), (HGRN_WIDTH, BF16)]
    lat_spec = pl.BlockSpec((1, tm, ATTN_WIDTH), lambda bi, i: (bi, jnp.maximum(i - nct, 0), 0))
    return pl.pallas_call(
        functools.partial(_in_proj_kernel, nct),
        out_shape=[jax.ShapeDtypeStruct((b, s if j == 0 else n_all, w), dt)
                   for j, (w, dt) in enumerate(outs)],
        grid=(b, nt),
        in_specs=[
            pl.BlockSpec((1, tm, d), lambda bi, i: (bi, jnp.minimum(i, nct - 1), 0)),
            pl.BlockSpec((1, tm, d), lambda bi, i: (bi, jnp.maximum(i - nct, 0), 0)),
            mod_spec, mod_spec,
            pl.BlockSpec((1, d), lambda bi, i: (0, 0)),
            pl.BlockSpec((d, pw), lambda bi, i: (0, 0)),
            pl.BlockSpec((1, _QK_WIDTH), lambda bi, i: (0, 0)),
            pl.BlockSpec((LANES, LANES), lambda bi, i: (0, 0)),
            pl.BlockSpec((tm, LANES), lambda bi, i: (i, 0)),
            pl.BlockSpec((tm, LANES), lambda bi, i: (i, 0)),
        ],
        out_specs=[lat_spec] + [tok_spec(w) for w, _ in outs[1:]],
        compiler_params=_params("arbitrary", "arbitrary"),
        name="in_proj",
    )(ctx, x, shift, scale, g1, w_in, qkg, ones_bd, cos, sin)


def _attention_kernel(n_kv_tiles, q_ref, k_ref, vt_ref, o_ref, qs_ref, s_ref, ksq_ref):
    tq = q_ref.shape[1]
    cols = ATTN_GROUP * tq
    for h in range(ATTN_GROUP):
        qs_ref[h * tq:(h + 1) * tq, :] = q_ref[0, :, h * ATTN_HEAD_DIM:(h + 1) * ATTN_HEAD_DIM]

    @pl.when(pl.program_id(2) == 0)
    def _():
        k = k_ref[0, 0].astype(F32)
        ksq = jnp.max(jnp.sum(k * k, axis=1, keepdims=True), axis=0, keepdims=True)
        ksq_ref[...] = jnp.broadcast_to(ksq, ksq_ref.shape)

    def tile(j):
        return pl.ds(pl.multiple_of(j * ATTN_KV_TILE, ATTN_KV_TILE), ATTN_KV_TILE)

    def scores(j):
        return _dot_nt(k_ref[0, 0, tile(j), :], qs_ref[...])

    def weighted(j, p):
        return _dot(vt_ref[0, 0, :, tile(j)], p)

    def finish(acc):
        o = acc[:ATTN_HEAD_DIM] / acc[ATTN_HEAD_DIM:ATTN_HEAD_DIM + 1]
        o_ref[0] = jnp.concatenate(
            [o[:, h * tq:(h + 1) * tq].T for h in range(ATTN_GROUP)], axis=1).astype(o_ref.dtype)

    qf = qs_ref[...].astype(F32)
    qsq = _dot_nt(jnp.ones((SUBLANES, ATTN_HEAD_DIM), BF16), (qf * qf).astype(BF16))[0:1]
    bound = jnp.sqrt(qsq * ksq_ref[0:1, 0:1]) * ATTN_BOUND_SLACK
    acc0 = jnp.zeros((vt_ref.shape[2], cols), F32)
    safe = jnp.max(bound) <= ATTN_BOUND_MAX

    @pl.when(safe)
    def _():
        def absorb(j, acc):
            return acc + weighted(j, jnp.exp2(scores(j) - bound).astype(BF16))

        def group(i, acc):
            for u in range(ATTN_KV_UNROLL):
                acc = absorb(ATTN_KV_UNROLL * i + u, acc)
            return acc

        n_groups = n_kv_tiles // ATTN_KV_UNROLL
        acc = lax.fori_loop(0, n_groups, group, acc0)
        for j in range(n_groups * ATTN_KV_UNROLL, n_kv_tiles):
            acc = absorb(j, acc)
        finish(acc)

    @pl.when(jnp.logical_not(safe))
    def _():
        def score(j, slot):
            s_ref[slot] = scores(j)

        def absorb(j, slot, carry):
            m, acc = carry
            s = s_ref[slot]
            m_new = jnp.maximum(m, jnp.max(s, axis=0, keepdims=True))
            p = jnp.exp2(s - m_new).astype(BF16)
            return m_new, jnp.exp2(m - m_new) * acc + weighted(j, p)

        def pair(i, carry):
            score(2 * i + 1, 1)
            carry = absorb(2 * i, 0, carry)
            score(2 * i + 2, 0)
            return absorb(2 * i + 1, 1, carry)

        n_pairs = (n_kv_tiles - 1) // 2
        score(0, 0)
        carry = lax.fori_loop(0, n_pairs, pair, (jnp.full((1, cols), NEG_BIG, F32), acc0))
        if n_kv_tiles % 2 == 1:
            carry = absorb(n_kv_tiles - 1, 0, carry)
        else:
            score(n_kv_tiles - 1, 1)
            carry = absorb(n_kv_tiles - 2, 0, carry)
            carry = absorb(n_kv_tiles - 1, 1, carry)
        finish(carry[1])


def _attention(qa, k_heads, vt_heads):
    b, s, _ = qa.shape
    n_all = k_heads.shape[2]
    tq = ATTN_Q_TILE
    gw = ATTN_GROUP * ATTN_HEAD_DIM
    vt_rows = vt_heads.shape[2]
    return pl.pallas_call(
        functools.partial(_attention_kernel, n_all // ATTN_KV_TILE),
        out_shape=jax.ShapeDtypeStruct((b, s, ATTN_WIDTH), BF16),
        grid=(b, ATTN_KV_HEADS, s // tq),
        in_specs=[pl.BlockSpec((1, tq, gw), lambda bi, kv, i: (bi, i, kv)),
                  pl.BlockSpec((1, 1, n_all, ATTN_HEAD_DIM), lambda bi, kv, i: (bi, kv, 0, 0)),
                  pl.BlockSpec((1, 1, vt_rows, n_all), lambda bi, kv, i: (bi, kv, 0, 0))],
        out_specs=pl.BlockSpec((1, tq, gw), lambda bi, kv, i: (bi, i, kv)),
        scratch_shapes=[pltpu.VMEM((ATTN_GROUP * tq, ATTN_HEAD_DIM), BF16),
                        pltpu.VMEM((2, ATTN_KV_TILE, ATTN_GROUP * tq), F32),
                        pltpu.VMEM((SUBLANES, LANES), F32)],
        compiler_params=_params("arbitrary", "arbitrary", "arbitrary"),
        name="attention",
    )(qa, k_heads, vt_heads)


def _hgrn_masks(reverse):
    c = HGRN_CHUNK
    t = lax.broadcasted_iota(jnp.int32, (c, c), 0)
    u = lax.broadcasted_iota(jnp.int32, (c, c), 1)
    tri = (u >= t) if reverse else (u <= t)
    levels = []
    size = c // 2
    while size >= 1:
        same_parent = (t // (2 * size)) == (u // (2 * size))
        levels.append((size, same_parent if 2 * size < c else None))
        size //= 2
    levels.append((0, t == u))
    return tri.astype(BF16), levels


def _hgrn_chunk(q, fr, v, lb, state, reverse, tri, levels):
    c = HGRN_CHUNK
    coarse_levels = [lv for lv in levels if lv[0] >= SUBLANES]
    fine_levels = [lv for lv in levels if lv[0] < SUBLANES]
    f = lb + (1.0 - lb) * _sigmoid(fr)
    k = 1.0 - f
    g_hi, g_lo = _split_bf16(jnp.log2(f))
    bcum = _dot(tri, g_hi) + _dot(tri, g_lo)
    qf = q.astype(F32)
    yield None

    st = state()
    end = 0 if reverse else c - 1
    b_end = bcum[end:end + 1, :]
    inter = _dot_nt((qf * jnp.exp2(bcum)).astype(BF16), st.astype(BF16))
    ke = (k * jnp.exp2(b_end - bcum)).astype(BF16)
    st_add = _dot_tn(v, ke)

    products = []
    for size, mask in coarse_levels:
        q_rows, k_rows = [], []
        zeros = jnp.zeros((size, qf.shape[1]), BF16)
        for p0 in range(0, c, 2 * size):
            early = slice(p0, p0 + size)
            late = slice(p0 + size, p0 + 2 * size)
            r = p0 + size if reverse else p0 + size - 1
            ref = bcum[r:r + 1, :]
            q_sl, k_sl = (early, late) if reverse else (late, early)
            q_blk = (qf[q_sl] * jnp.exp2(bcum[q_sl] - ref)).astype(BF16)
            k_blk = (k[k_sl] * jnp.exp2(ref - bcum[k_sl])).astype(BF16)
            q_rows += [q_blk, zeros] if reverse else [zeros, q_blk]
            k_rows += [zeros, k_blk] if reverse else [k_blk, zeros]
        products.append((mask, _dot_nt(jnp.concatenate(q_rows, axis=0),
                                       jnp.concatenate(k_rows, axis=0))))
    yield None

    row = lax.broadcasted_iota(jnp.int32, (c, 1), 0)
    sub = lax.broadcasted_iota(jnp.int32, (SUBLANES, 1), 0)
    fine_products = []
    for size, mask in fine_levels:
        if size == 0:
            q_l, k_l = q, k.astype(BF16)
        else:
            late = (row // size) % 2 == 1
            q_side = jnp.logical_not(late) if reverse else late
            if size == 1:
                fac_q, fac_k = f, None
            else:
                groups = []
                for r0 in range(0, c, SUBLANES):
                    ref = None
                    for p0 in range(0, SUBLANES, 2 * size):
                        r = r0 + p0 + (size if reverse else size - 1)
                        piece = jnp.broadcast_to(bcum[r:r + 1, :], (SUBLANES, bcum.shape[1]))
                        ref = piece if ref is None else jnp.where(sub >= p0, piece, ref)
                    groups.append(ref)
                ref = jnp.concatenate(groups, axis=0)
                fac_q = fac_k = jnp.exp2(jnp.where(q_side, bcum - ref, ref - bcum))
            q_l = jnp.where(q_side, qf * fac_q, 0.0).astype(BF16)
            k_l = jnp.where(q_side, 0.0, k if fac_k is None else k * fac_k).astype(BF16)
        fine_products.append((mask, _dot_nt(q_l, k_l)))
    a = None
    for mask, a_l in products:
        if mask is not None:
            a_l = jnp.where(mask, a_l, 0.0)
        a = a_l if a is None else a + a_l
    yield None

    for mask, a_l in fine_products:
        a = a + jnp.where(mask, a_l, 0.0)
    intra = _dot(a.astype(BF16), v)
    st_new = st * jnp.exp2(b_end) + st_add
    yield None

    yield inter + intra, st_new


_HGRN_STAGES = 5


def _hgrn_kernel(qf_ref, if_ref, ff_ref, qb_ref, ib_ref, fb_ref, lb_ref, of_ref, ob_ref,
                 sf_ref, sb_ref):
    @pl.when(pl.program_id(1) == 0)
    def _():
        sf_ref[...] = jnp.zeros_like(sf_ref)
        sb_ref[...] = jnp.zeros_like(sb_ref)

    dirs = ((False, qf_ref, if_ref, ff_ref, of_ref, sf_ref),
            (True, qb_ref, ib_ref, fb_ref, ob_ref, sb_ref))
    masks = [_hgrn_masks(reverse) for reverse, *_ in dirs]
    n_sub = qf_ref.shape[1] // HGRN_CHUNK
    latest = {}
    chunks = []
    for t in range(n_sub):
        for h in range(HGRN_HEADS):
            sl = slice(h * HGRN_HEAD_DIM, (h + 1) * HGRN_HEAD_DIM)
            for d, (reverse, q_ref, i_ref, f_ref, o_ref, s_ref) in enumerate(dirs):
                r0 = (n_sub - 1 - t if reverse else t) * HGRN_CHUNK
                rows = slice(r0, r0 + HGRN_CHUNK)

                def state(h=h, d=d, s_ref=s_ref):
                    return latest[h, d] if (h, d) in latest else s_ref[h]

                gen = _hgrn_chunk(q_ref[0, rows, sl], f_ref[0, rows, sl], i_ref[0, rows, sl],
                                  lb_ref[d:d + 1, sl], state, reverse, *masks[d])
                chunks.append((gen, o_ref, s_ref, h, d, rows, sl, t == n_sub - 1))

    assert len(dirs) * HGRN_HEADS >= _HGRN_STAGES
    for step in range(len(chunks) + _HGRN_STAGES - 1):
        for stage in range(_HGRN_STAGES):
            idx = step - stage
            if 0 <= idx < len(chunks):
                gen, o_ref, s_ref, h, d, rows, sl, is_last = chunks[idx]
                result = next(gen)
                if stage == _HGRN_STAGES - 1:
                    o, st = result
                    o_ref[0, rows, sl] = o
                    latest[h, d] = st
                    if is_last:
                        s_ref[h] = st


def _hgrn(qr, ir, ff, fb, lb, n_ctx):
    b, n_all, w = qr.shape
    c = HGRN_CHUNK * HGRN_STEP_CHUNKS
    assert n_ctx % c == 0 and n_all % c == 0
    nc = n_all // c
    ncc = n_ctx // c

    def fwd(bi, j):
        return (bi, j, 0)

    def bwd(bi, j):
        return (bi, jnp.where(j < ncc, ncc - 1 - j, nc - 1 - (j - ncc)), 0)

    blk = (1, c, w)
    state = pltpu.VMEM((HGRN_HEADS, HGRN_HEAD_DIM, HGRN_HEAD_DIM), F32)
    return pl.pallas_call(
        _hgrn_kernel,
        out_shape=[jax.ShapeDtypeStruct((b, n_all, w), F32)] * 2,
        grid=(b, nc),
        in_specs=[pl.BlockSpec(blk, fwd), pl.BlockSpec(blk, fwd), pl.BlockSpec(blk, fwd),
                  pl.BlockSpec(blk, bwd), pl.BlockSpec(blk, bwd), pl.BlockSpec(blk, bwd),
                  pl.BlockSpec((2, w), lambda bi, j: (0, 0))],
        out_specs=[pl.BlockSpec(blk, fwd), pl.BlockSpec(blk, bwd)],
        scratch_shapes=[state, state],
        compiler_params=_params("arbitrary", "arbitrary"),
        name="hgrn",
    )(qr, ir, ff, qr, ir, fb, lb)


_ROUTE_GROUP_LANE0 = N_EXPERTS


def _lane_min_index(cond, lane):
    return jnp.min(jnp.where(cond, lane, LANES), axis=-1, keepdims=True)


def _out_proj_kernel(oa_ref, of_ref, ob_ref, gr_ref, x_ref, gate_ref, shift_ref, scale_ref,
                     ag_ref, hg_ref, g2_ref, w_ref, ones_ref, wr_hi_ref, wr_lo_ref, br_ref,
                     x1_ref, h2_ref, ri_ref, rw_ref, cnt_ref, carry_ref):
    first = (pl.program_id(0) == 0) & (pl.program_id(1) == 0)

    @pl.when(first)
    def _():
        carry_ref[...] = jnp.zeros_like(carry_ref)

    tm = x_ref.shape[1]

    ones_bd = ones_ref[...]
    slabs = []
    for c0 in range(0, ATTN_WIDTH, LANES):
        t = oa_ref[0, :, c0:c0 + LANES].astype(F32)
        slabs.append((t * _head_rms_scale(t, ones_bd) * ag_ref[:, c0:c0 + LANES]).astype(BF16))
    orr = of_ref[0] + ob_ref[0]
    parts = []
    for h in range(HGRN_HEADS):
        sl = slice(h * HGRN_HEAD_DIM, (h + 1) * HGRN_HEAD_DIM)
        oh = orr[:, sl]
        parts.append(oh * lax.rsqrt(jnp.mean(oh * oh, axis=-1, keepdims=True) + EPS))
    orr = jnp.concatenate(parts, axis=1) * hg_ref[...] * gr_ref[0].astype(F32)
    mix_in = jnp.concatenate(slabs + [orr.astype(BF16)], axis=1)
    x1 = x_ref[0] + gate_ref[0] * _dot(mix_in, w_ref[...])
    x1_ref[0] = x1
    h2 = x1 * lax.rsqrt(jnp.mean(x1 * x1, axis=-1, keepdims=True) + EPS) * g2_ref[...]
    h2 = h2 * (1.0 + scale_ref[0]) + shift_ref[0]
    for c in range(h2_ref.shape[1]):
        h2_ref[:, c] = h2[:, c * LANES:(c + 1) * LANES].reshape(tm // SUBLANES, SUBLANES, LANES)

    h_hi, h_lo = _split_bf16(h2)
    logits = (_dot(h_hi, wr_hi_ref[...]) + _dot(h_lo, wr_hi_ref[...])
              + _dot(h_hi, wr_lo_ref[...]) + br_ref[...])
    lane = lax.broadcasted_iota(jnp.int32, logits.shape, 1)
    is_grp = (lane >= _ROUTE_GROUP_LANE0) & (lane < _ROUTE_GROUP_LANE0 + N_GROUPS)
    lg = jnp.where(is_grp, logits, NEG_BIG)
    mg = jnp.max(lg, axis=-1, keepdims=True)
    g_sel = _lane_min_index(lg == mg, lane) - _ROUTE_GROUP_LANE0
    pg_top = 1.0 / jnp.sum(jnp.exp(lg - mg), axis=-1, keepdims=True)
    in_grp = (lane < N_EXPERTS) & ((lane // EXPERTS_PER_GROUP) == g_sel)
    le = jnp.where(in_grp, logits, NEG_BIG)
    m1 = jnp.max(le, axis=-1, keepdims=True)
    e1 = _lane_min_index(le == m1, lane)
    le2 = jnp.where(lane == e1, NEG_BIG, le)
    m2 = jnp.max(le2, axis=-1, keepdims=True)
    e2 = _lane_min_index(le2 == m2, lane)
    r2 = jnp.exp(m2 - m1)
    w1 = pg_top / (1.0 + r2)
    w2 = pg_top * r2 / (1.0 + r2)

    onehot = ((lane == e1) | (lane == e2)).astype(BF16)
    rt = lax.broadcasted_iota(jnp.int32, (tm, tm), 0)
    ru = lax.broadcasted_iota(jnp.int32, (tm, tm), 1)
    before = _dot((ru < rt).astype(BF16), onehot) + carry_ref[0:1, :]
    rank1 = jnp.sum(jnp.where(lane == e1, before, 0.0), axis=-1, keepdims=True)
    rank2 = jnp.sum(jnp.where(lane == e2, before, 0.0), axis=-1, keepdims=True)
    total = carry_ref[0:1, :] + jnp.sum(onehot.astype(F32), axis=0, keepdims=True)
    carry_ref[...] = jnp.broadcast_to(total, carry_ref.shape)
    cnt_ref[...] = jnp.broadcast_to(total, cnt_ref.shape).astype(jnp.int32)

    rl = lax.broadcasted_iota(jnp.int32, (tm, ROUTE_LANES), 1)
    ri = jnp.where(rl == 0, e1, jnp.where(rl == 1, e2, jnp.where(
        rl == 2, rank1.astype(jnp.int32), jnp.where(rl == 3, rank2.astype(jnp.int32), 0))))
    ri_ref[0] = ri
    rw_ref[0] = jnp.where(rl == 0, w1, jnp.where(rl == 1, w2, 0.0))


def _out_proj(oa, o_f, o_b, gr, x, gate, shift, scale, ag, hg, g2, w_out, wr_hi, wr_lo, br, n_ctx):
    b, s, d = x.shape
    tm = TOKEN_TILE
    off = n_ctx // tm
    ones_bd = _head_sum_matrix()

    def lat(w):
        return pl.BlockSpec((1, tm, w), lambda bi, i: (bi, i, 0))

    def allrows(w):
        return pl.BlockSpec((1, tm, w), lambda bi, i: (bi, i + off, 0))

    def mod():
        return pl.BlockSpec((1, 1, d), lambda bi, i: (bi, 0, 0))

    def const(shape):
        return pl.BlockSpec(shape, lambda bi, i: (0,) * len(shape))

    return pl.pallas_call(
        _out_proj_kernel,
        out_shape=[jax.ShapeDtypeStruct((b, s, d), F32),
                   jax.ShapeDtypeStruct((b * s // SUBLANES, d // LANES, SUBLANES, LANES), F32),
                   jax.ShapeDtypeStruct((b, s, ROUTE_LANES), jnp.int32),
                   jax.ShapeDtypeStruct((b, s, ROUTE_LANES), F32),
                   jax.ShapeDtypeStruct((SUBLANES, LANES), jnp.int32)],
        grid=(b, s // tm),
        in_specs=[lat(ATTN_WIDTH), allrows(HGRN_WIDTH), allrows(HGRN_WIDTH), allrows(HGRN_WIDTH),
                  lat(d), mod(), mod(), mod(),
                  const((1, ATTN_WIDTH)), const((1, HGRN_WIDTH)), const((1, d)),
                  const((ATTN_WIDTH + HGRN_WIDTH, d)), const((LANES, LANES)),
                  const((d, LANES)), const((d, LANES)), const((1, LANES))],
        out_specs=[lat(d),
                   pl.BlockSpec((tm // SUBLANES, d // LANES, SUBLANES, LANES),
                                lambda bi, i: (bi * (s // tm) + i, 0, 0, 0)),
                   lat(ROUTE_LANES), lat(ROUTE_LANES), const((SUBLANES, LANES))],
        scratch_shapes=[pltpu.VMEM((SUBLANES, LANES), F32)],
        compiler_params=_params("arbitrary", "arbitrary"),
        name="out_proj",
    )(oa, o_f, o_b, gr, x, gate, shift, scale, ag, hg, g2, w_out, ones_bd, wr_hi, wr_lo, br)


def _slab_copies(stage, hbm, row0, sem, to_hbm):
    copies = []
    for c in range(stage.shape[0]):
        view = hbm.at[pl.ds(row0, stage.shape[1]), c, :]
        src, dst = (stage.at[c], view) if to_hbm else (view, stage.at[c])
        copies.append(pltpu.make_async_copy(src, dst, sem))
    return copies


def _stage_store(stage, x):
    for c in range(stage.shape[0]):
        stage[c] = x[:, c * LANES:(c + 1) * LANES]


def _stage_load(stage):
    return jnp.concatenate([stage[c] for c in range(stage.shape[0])], axis=1)


_DISPATCH_SLOTS = 3


def _dispatch_kernel(dest_ref, h2_hbm, xs_init_hbm, xs_hbm, buf, sem_in, sem_out):
    del xs_init_hbm
    i = pl.program_id(0)
    n = pl.num_programs(0)
    groups = buf.shape[1]
    tm = groups * SUBLANES
    slot = i % _DISPATCH_SLOTS

    def fetch(t, sl):
        return pltpu.make_async_copy(h2_hbm.at[pl.ds(t * groups, groups)], buf.at[sl],
                                     sem_in.at[sl])

    def wait_rows(sl):
        for _ in range(TOP_K):
            pltpu.make_async_copy(h2_hbm.at[pl.ds(0, groups)], buf.at[sl], sem_out.at[sl]).wait()

    @pl.when(i == 0)
    def _():
        fetch(0, 0).start()

    @pl.when(i >= 2)
    def _():
        wait_rows((i + 1) % _DISPATCH_SLOTS)

    @pl.when(i + 1 < n)
    def _():
        fetch(i + 1, (i + 1) % _DISPATCH_SLOTS).start()

    fetch(i, slot).wait()

    def body(g, carry):
        rows = [[dest_ref[0, 0, k * tm + g * SUBLANES + u] for k in range(TOP_K)]
                for u in range(SUBLANES)]
        for u in range(SUBLANES):
            for k in range(TOP_K):
                pltpu.make_async_copy(buf.at[slot, g, :, u, :], xs_hbm.at[rows[u][k]],
                                      sem_out.at[slot]).start()
        return carry
    lax.fori_loop(0, groups, body, 0)

    @pl.when(i == n - 1)
    def _():
        wait_rows(slot)

    @pl.when((i == n - 1) & (i >= 1))
    def _():
        wait_rows((i - 1) % _DISPATCH_SLOTS)


def _dispatch(dest_tiles, h2_tiles, n_slots):
    nt = dest_tiles.shape[0]
    n_groups, n_slabs, _, _ = h2_tiles.shape
    groups = n_groups // nt
    xs_init = jnp.zeros((n_slots, n_slabs, LANES), F32)
    return pl.pallas_call(
        _dispatch_kernel,
        out_shape=jax.ShapeDtypeStruct(xs_init.shape, F32),
        grid=(nt,),
        in_specs=[pl.BlockSpec((1, 1, dest_tiles.shape[2]), lambda i: (i, 0, 0),
                               memory_space=pltpu.SMEM),
                  pl.BlockSpec(memory_space=pl.ANY), pl.BlockSpec(memory_space=pl.ANY)],
        out_specs=pl.BlockSpec(memory_space=pl.ANY),
        scratch_shapes=[pltpu.VMEM((_DISPATCH_SLOTS, groups, n_slabs, SUBLANES, LANES), F32),
                        pltpu.SemaphoreType.DMA((_DISPATCH_SLOTS,)),
                        pltpu.SemaphoreType.DMA((_DISPATCH_SLOTS,))],
        input_output_aliases={2: 0},
        compiler_params=_params("arbitrary"),
        name="dispatch",
    )(dest_tiles, h2_tiles, xs_init)


def _experts_kernel(meta_ref, xs_hbm, wg_ref, wu_ref, wd_ref, ys_hbm,
                    xstage, ystage, wg_bf, wu_bf, wd_bf, sem_in, sem_out):
    j = pl.program_id(0)
    last = pl.num_programs(0) - 1
    n_used = meta_ref[0]
    slot = j % 2

    def fetch(blk, sl):
        for cp in _slab_copies(xstage.at[sl], xs_hbm, blk * MOE_BLOCK, sem_in.at[sl], False):
            cp.start()

    def wait_out(sl):
        for cp in _slab_copies(ystage.at[sl], ys_hbm, 0, sem_out.at[sl], True):
            cp.wait()

    @pl.when((j == 0) & (n_used > 0))
    def _():
        fetch(0, 0)

    @pl.when(j + 1 < n_used)
    def _():
        fetch(j + 1, 1 - slot)

    new_expert = (j == 0) | (meta_ref[1 + j] != meta_ref[jnp.maximum(j, 1)])

    @pl.when((j < n_used) & new_expert)
    def _():
        wg_bf[...] = wg_ref[0].astype(BF16)
        wu_bf[...] = wu_ref[0].astype(BF16)
        wd_bf[...] = wd_ref[0].astype(BF16)

    @pl.when(j >= 2)
    def _():
        wait_out(slot)

    @pl.when(j < n_used)
    def _():
        for cp in _slab_copies(xstage.at[slot], xs_hbm, 0, sem_in.at[slot], False):
            cp.wait()
        xb = _stage_load(xstage.at[slot]).astype(BF16)
        a = _silu(_dot(xb, wg_bf[...])) * _dot(xb, wu_bf[...])
        _stage_store(ystage.at[slot], _dot(a.astype(BF16), wd_bf[...]))

    @pl.when(j >= n_used)
    def _():
        ystage[slot] = jnp.zeros(ystage.shape[1:], F32)

    for cp in _slab_copies(ystage.at[slot], ys_hbm, j * MOE_BLOCK, sem_out.at[slot], True):
        cp.start()

    @pl.when(j == last)
    def _():
        wait_out(slot)

    @pl.when((j == last) & (j >= 1))
    def _():
        wait_out(1 - slot)


def _experts(meta, xs, wg, wu, wd):
    n_slots, n_slabs, _ = xs.shape
    n_blk = n_slots // MOE_BLOCK
    d = wg.shape[1]
    ff = wg.shape[2]

    def wspec(shape):
        return pl.BlockSpec((1,) + shape, lambda j, meta: (meta[1 + j], 0, 0))

    stage = pltpu.VMEM((2, n_slabs, MOE_BLOCK, LANES), F32)
    return pl.pallas_call(
        _experts_kernel,
        out_shape=jax.ShapeDtypeStruct(xs.shape, F32),
        grid_spec=pltpu.PrefetchScalarGridSpec(
            num_scalar_prefetch=1,
            grid=(n_blk,),
            in_specs=[pl.BlockSpec(memory_space=pl.ANY),
                      wspec((d, ff)), wspec((d, ff)), wspec((ff, d))],
            out_specs=pl.BlockSpec(memory_space=pl.ANY),
            scratch_shapes=[stage, stage, pltpu.VMEM((d, ff), BF16), pltpu.VMEM((d, ff), BF16),
                            pltpu.VMEM((ff, d), BF16),
                            pltpu.SemaphoreType.DMA((2,)), pltpu.SemaphoreType.DMA((2,))],
        ),
        compiler_params=_params("arbitrary"),
        name="experts",
    )(meta, xs, wg, wu, wd)


def _combine_kernel(dest_cur_ref, dest_next_ref, ys_hbm, x1_ref, gate_ref, rw_ref, o_ref,
                    buf, sem):
    i = pl.program_id(0)
    slot = i % 2
    tm = x1_ref.shape[0]
    groups = tm // SUBLANES

    def start(idx_ref, sl):
        for k in range(TOP_K):
            def body(g, carry, k=k):
                rows = [idx_ref[0, 0, k * tm + g * SUBLANES + u] for u in range(SUBLANES)]
                for u in range(SUBLANES):
                    pltpu.make_async_copy(ys_hbm.at[rows[u]], buf.at[sl, k, g, :, u, :],
                                          sem.at[sl, k]).start()
                return carry
            lax.fori_loop(0, groups, body, 0)

    def wait(sl, k):
        for u in range(SUBLANES):
            pltpu.make_async_copy(ys_hbm.at[pl.ds(0, groups)], buf.at[sl, k, :, :, u, :],
                                  sem.at[sl, k]).wait()

    @pl.when(i == 0)
    def _():
        start(dest_cur_ref, 0)

    @pl.when(i + 1 < pl.num_programs(0))
    def _():
        start(dest_next_ref, 1 - slot)

    rw = rw_ref[...]
    moe = None
    for k in range(TOP_K):
        wait(slot, k)
        rows = jnp.concatenate([buf[slot, k, :, c].reshape(tm, LANES)
                                for c in range(buf.shape[3])], axis=1)
        term = rw[:, k:k + 1] * rows
        moe = term if moe is None else moe + term
    o_ref[...] = x1_ref[...] + gate_ref[0] * moe


def _combine(dest_tiles, ys, x1, gate, rw):
    n_tok, d = x1.shape
    tm = TOKEN_TILE
    nt = n_tok // tm
    per_batch = nt // gate.shape[0]
    idx_blk = (1, 1, TOP_K * tm)
    return pl.pallas_call(
        _combine_kernel,
        out_shape=jax.ShapeDtypeStruct((n_tok, d), F32),
        grid=(nt,),
        in_specs=[
            pl.BlockSpec(idx_blk, lambda i: (i, 0, 0), memory_space=pltpu.SMEM),
            pl.BlockSpec(idx_blk, lambda i: (jnp.minimum(i + 1, nt - 1), 0, 0),
                         memory_space=pltpu.SMEM),
            pl.BlockSpec(memory_space=pl.ANY),
            pl.BlockSpec((tm, d), lambda i: (i, 0)),
            pl.BlockSpec((1, 1, d), lambda i: (i // per_batch, 0, 0)),
            pl.BlockSpec((tm, ROUTE_LANES), lambda i: (i, 0)),
        ],
        out_specs=pl.BlockSpec((tm, d), lambda i: (i, 0)),
        scratch_shapes=[pltpu.VMEM((2, TOP_K, tm // SUBLANES, d // LANES, SUBLANES, LANES), F32),
                        pltpu.SemaphoreType.DMA((2, TOP_K))],
        compiler_params=_params("arbitrary"),
        name="combine",
    )(dest_tiles, dest_tiles, ys, x1, gate, rw)


def _layer(x, ctx, c, c_ctx, w_ada, b_ada, norm1_g, norm2_g, w_in, q_norm_g, k_norm_g, attn_out_g,
           lb, hgrn_out_g, w_out, w_router_grp, b_router_grp, w_router_exp, b_router_exp,
           w_exp_gate, w_exp_up, w_exp_down):
    b, s, d = x.shape
    n_ctx = ctx.shape[1]
    assert n_ctx % TOKEN_TILE == 0 and s % TOKEN_TILE == 0 and s % GRID_W == 0
    assert n_ctx % HGRN_CHUNK == 0 and (n_ctx + s) % ATTN_KV_TILE == 0
    n_all = n_ctx + s

    cond = jnp.zeros((2 * SUBLANES, d), F32).at[:b].set(c).at[b].set(c_ctx)
    assert b + 1 <= cond.shape[0]
    mods = _adaln(cond, w_ada, b_ada)[:b + 1].reshape(b + 1, 1, 6, d)
    sh1, sc1, gt1, sh2, sc2, gt2 = (mods[:, :, m] for m in range(6))

    scale_q = ATTN_HEAD_DIM ** -0.5 * np.log2(np.e)
    qkg = jnp.concatenate([jnp.tile(q_norm_g, ATTN_HEADS) * scale_q,
                           jnp.tile(k_norm_g, ATTN_KV_HEADS)]).reshape(1, _QK_WIDTH)
    cos, sin = _rope_tables(n_ctx, s)
    qa, ka, va, qr, ff, fb, ir, gr = _in_proj(
        ctx, x, sh1, sc1, norm1_g.reshape(1, d), w_in.astype(BF16), qkg, cos, sin)

    def kv_heads(t):
        return t.reshape(b, n_all, ATTN_KV_HEADS, ATTN_HEAD_DIM).transpose(0, 2, 1, 3)

    vt = kv_heads(va).transpose(0, 1, 3, 2)
    vt = jnp.concatenate([vt, jnp.ones((b, ATTN_KV_HEADS, 2 * SUBLANES, n_all), BF16)], axis=2)
    oa = _attention(qa, kv_heads(ka), vt)
    o_f, o_b = _hgrn(qr, ir, ff, fb, lb, n_ctx)

    w_router = jnp.zeros((d, LANES), F32)
    w_router = w_router.at[:, :N_EXPERTS].set(w_router_exp)
    w_router = w_router.at[:, _ROUTE_GROUP_LANE0:_ROUTE_GROUP_LANE0 + N_GROUPS].set(w_router_grp)
    b_router = jnp.zeros((1, LANES), F32)
    b_router = b_router.at[0, :N_EXPERTS].set(b_router_exp)
    b_router = b_router.at[0, _ROUTE_GROUP_LANE0:_ROUTE_GROUP_LANE0 + N_GROUPS].set(b_router_grp)
    wr_hi, wr_lo = _split_bf16(w_router)
    x1, h2, ri, rw, counts = _out_proj(
        oa, o_f, o_b, gr, x, gt1[:b], sh2[:b], sc2[:b], attn_out_g.reshape(1, -1),
        hgrn_out_g.reshape(1, -1), norm2_g.reshape(1, d), w_out.astype(BF16), wr_hi, wr_lo,
        b_router, n_ctx)

    n_tok = b * s
    counts = counts[0, :N_EXPERTS]
    padded = (counts + MOE_BLOCK - 1) // MOE_BLOCK * MOE_BLOCK
    pend = jnp.cumsum(padded)
    pstart = pend - padded
    n_blk = n_tok * TOP_K // MOE_BLOCK + N_EXPERTS
    ri = ri.reshape(n_tok, ROUTE_LANES)
    e_id = ri[:, :TOP_K]
    dest = pstart[e_id] + ri[:, TOP_K:2 * TOP_K]
    blk_start = jnp.arange(n_blk, dtype=jnp.int32) * MOE_BLOCK
    blk_e = jnp.minimum(jnp.sum(blk_start[:, None] >= pend[None, :], axis=1), N_EXPERTS - 1)
    n_used = pend[-1] // MOE_BLOCK
    meta = jnp.concatenate([n_used[None], blk_e]).astype(jnp.int32)
    nt = n_tok // TOKEN_TILE
    dest_tiles = dest.reshape(nt, TOKEN_TILE, TOP_K).transpose(0, 2, 1).reshape(nt, 1, -1)

    xs = _dispatch(dest_tiles, h2, n_blk * MOE_BLOCK)
    ys = _experts(meta, xs, w_exp_gate, w_exp_up, w_exp_down)
    out = _combine(dest_tiles, ys, x1.reshape(n_tok, d), gt2[:b], rw.reshape(n_tok, ROUTE_LANES))
    return out.reshape(b, s, d)


def kernel(x, c, ctx, c_ctx, w_ada, b_ada, norm1_g, norm2_g, w_in, q_norm_g, k_norm_g, attn_out_g,
           hgrn_lb, hgrn_out_g, w_out, w_router_grp, b_router_grp, w_router_exp, b_router_exp,
           w_exp_gate, w_exp_up, w_exp_down):
    depth = w_in.shape[0]
    assert depth == 1, "context stream update between layers is not implemented"
    lb_all = jnp.cumsum(jax.nn.softmax(hgrn_lb.astype(F32), axis=1), axis=1)
    layer = 0
    return _layer(x, ctx, c, c_ctx, w_ada[layer], b_ada[layer], norm1_g[layer], norm2_g[layer],
                  w_in[layer], q_norm_g[layer], k_norm_g[layer], attn_out_g[layer],
                  lb_all[:, layer], hgrn_out_g[layer], w_out[layer], w_router_grp[layer],
                  b_router_grp[layer], w_router_exp[layer], b_router_exp[layer],
                  w_exp_gate[layer], w_exp_up[layer], w_exp_down[layer])
```

```python
import functools

import numpy as np
import jax
import jax.numpy as jnp
from jax import lax
from jax.experimental import pallas as pl
from jax.experimental.pallas import tpu as pltpu

F32 = jnp.float32
BF16 = jnp.bfloat16

GRID_W = 64
EPS = 1e-6
ATTN_HEADS = 8
ATTN_KV_HEADS = 2
ATTN_HEAD_DIM = 64
ATTN_GROUP = ATTN_HEADS // ATTN_KV_HEADS
ATTN_WIDTH = ATTN_HEADS * ATTN_HEAD_DIM
KV_WIDTH = ATTN_KV_HEADS * ATTN_HEAD_DIM
ROPE_THETA = 10000.0
HGRN_HEADS = 4
HGRN_HEAD_DIM = 128
HGRN_WIDTH = HGRN_HEADS * HGRN_HEAD_DIM
N_GROUPS = 4
EXPERTS_PER_GROUP = 8
N_EXPERTS = N_GROUPS * EXPERTS_PER_GROUP
TOP_K = 2
EXPERT_FF = 512

LANES = 128
SUBLANES = 8
MXU_DIM = 256
VMEM_LIMIT_BYTES = 48 * 1024 * 1024

TOKEN_TILE = 256
ATTN_Q_TILE = 1024
ATTN_KV_TILE = 256
ATTN_KV_UNROLL = 17
ATTN_BOUND_SLACK = 1.02
ATTN_BOUND_MAX = 60.0
HGRN_CHUNK = 64
HGRN_STEP_CHUNKS = 4
MOE_BLOCK = 256
GATHER_UNROLL = 8
ROUTE_LANES = 8
NEG_BIG = -1e30

_QA0 = 0
_KA0 = _QA0 + ATTN_WIDTH
_VA0 = _KA0 + KV_WIDTH
_QR0 = _VA0 + KV_WIDTH
_FF0 = _QR0 + HGRN_WIDTH
_FB0 = _FF0 + HGRN_WIDTH
_IR0 = _FB0 + HGRN_WIDTH
_GR0 = _IR0 + HGRN_WIDTH
_QK_WIDTH = ATTN_WIDTH + KV_WIDTH


def _dot(a, b):
    return jnp.dot(a, b, preferred_element_type=F32)


def _dot_nt(a, b):
    return lax.dot_general(a, b, (((1,), (1,)), ((), ())), preferred_element_type=F32)


def _dot_tn(a, b):
    return lax.dot_general(a, b, (((0,), (0,)), ((), ())), preferred_element_type=F32)


def _split_bf16(x):
    hi = x.astype(BF16)
    lo = (x - hi.astype(F32)).astype(BF16)
    return hi, lo


def _sigmoid(x):
    return 1.0 / (1.0 + jnp.exp(-x))


def _silu(x):
    return x * _sigmoid(x)


def _params(*sem):
    return pltpu.CompilerParams(dimension_semantics=sem, vmem_limit_bytes=VMEM_LIMIT_BYTES)


def _head_sum_matrix():
    idx = np.arange(LANES) // ATTN_HEAD_DIM
    return jnp.asarray(idx[:, None] == idx[None, :], dtype=BF16)


def _head_rms_scale(x, ones_bd):
    ssq = _dot((x * x).astype(BF16), ones_bd)
    return lax.rsqrt(ssq * (1.0 / ATTN_HEAD_DIM) + EPS)


def _adaln_kernel(cond_ref, w_ref, b_ref, o_ref):
    s = _silu(cond_ref[...])
    s_hi, s_lo = _split_bf16(s)
    w_hi, w_lo = _split_bf16(w_ref[...])
    o_ref[...] = _dot(s_hi, w_hi) + _dot(s_lo, w_hi) + _dot(s_hi, w_lo) + b_ref[...]


def _adaln(cond, w_ada, b_ada):
    rows, d = cond.shape
    n = w_ada.shape[1]
    tn = n // 6
    return pl.pallas_call(
        _adaln_kernel,
        out_shape=jax.ShapeDtypeStruct((rows, n), F32),
        grid=(n // tn,),
        in_specs=[pl.BlockSpec((rows, d), lambda j: (0, 0)),
                  pl.BlockSpec((d, tn), lambda j: (0, j)),
                  pl.BlockSpec((1, tn), lambda j: (0, j))],
        out_specs=pl.BlockSpec((rows, tn), lambda j: (0, j)),
        compiler_params=_params("arbitrary"),
        name="adaln",
    )(cond, w_ada, b_ada.reshape(1, n))


def _rope_tables(n_ctx, n_lat):
    half = ATTN_HEAD_DIM // 2
    freqs = ROPE_THETA ** (-np.arange(0, half, 2, dtype=np.float64) / half)
    tok = np.arange(n_lat)
    pos = np.stack([tok // GRID_W, tok % GRID_W], axis=1).astype(np.float64)
    lane = np.arange(ATTN_HEAD_DIM)
    axis = lane // half
    fi = (lane % half) // 2
    ang = pos[:, axis] * freqs[fi][None, :]
    sign = np.where(lane % 2 == 1, 1.0, -1.0)
    cos = np.concatenate([np.ones((n_ctx, ATTN_HEAD_DIM)), np.cos(ang)], axis=0)
    sin = np.concatenate([np.zeros((n_ctx, ATTN_HEAD_DIM)), np.sin(ang) * sign], axis=0)
    reps = LANES // ATTN_HEAD_DIM
    return (jnp.asarray(np.tile(cos, (1, reps)), F32), jnp.asarray(np.tile(sin, (1, reps)), F32))


def _in_proj_kernel(n_ctx_tiles, ctx_ref, x_ref, shift_ref, scale_ref, g1_ref, w_ref, qkg_ref,
                    ones_ref, cos_ref, sin_ref,
                    qa_ref, ka_ref, vt_ref, qr_ref, ff_ref, fb_ref, ir_ref, gr_ref):
    i = pl.program_id(1)
    xt = jnp.where(i < n_ctx_tiles, ctx_ref[0], x_ref[0])
    ms = jnp.mean(xt * xt, axis=-1, keepdims=True)
    h = xt * lax.rsqrt(ms + EPS) * g1_ref[...]
    h = h * (1.0 + scale_ref[0]) + shift_ref[0]
    p = _dot(h.astype(BF16), w_ref[...])

    ones_bd = ones_ref[...]
    cos = cos_ref[...]
    sin = sin_ref[...]
    even = lax.broadcasted_iota(jnp.int32, cos.shape, 1) % 2 == 0
    slabs = []
    for c0 in range(0, _QK_WIDTH, LANES):
        t = p[:, _QA0 + c0:_QA0 + c0 + LANES]
        t = t * _head_rms_scale(t, ones_bd) * qkg_ref[:, c0:c0 + LANES]
        partner = jnp.where(even, pltpu.roll(t, LANES - 1, 1), pltpu.roll(t, 1, 1))
        slabs.append((t * cos + partner * sin).astype(BF16))
    qa_ref[0] = jnp.concatenate(slabs[:ATTN_WIDTH // LANES], axis=1)
    k_all = jnp.concatenate(slabs[ATTN_WIDTH // LANES:], axis=1)
    vt_all = p[:, _VA0:_VA0 + KV_WIDTH].T
    ones = jnp.ones((vt_ref.shape[2] - ATTN_HEAD_DIM, vt_all.shape[1]), F32)
    for hd in range(ATTN_KV_HEADS):
        cols = slice(hd * ATTN_HEAD_DIM, (hd + 1) * ATTN_HEAD_DIM)
        ka_ref[0, hd] = k_all[:, cols]
        vt_ref[0, hd] = jnp.concatenate([vt_all[cols], ones], axis=0).astype(BF16)

    qr_ref[0] = (_silu(p[:, _QR0:_QR0 + HGRN_WIDTH]) * (HGRN_HEAD_DIM ** -0.5)).astype(BF16)
    ff_ref[0] = p[:, _FF0:_FF0 + HGRN_WIDTH]
    fb_ref[0] = p[:, _FB0:_FB0 + HGRN_WIDTH]
    ir_ref[0] = p[:, _IR0:_IR0 + HGRN_WIDTH].astype(BF16)
    gr_ref[0] = _silu(p[:, _GR0:_GR0 + HGRN_WIDTH]).astype(BF16)


def _in_proj(ctx, x, shift, scale, g1, w_in, qkg, cos, sin):
    b, n_ctx, d = ctx.shape
    s = x.shape[1]
    tm = TOKEN_TILE
    nct = n_ctx // tm
    n_all = n_ctx + s
    nt = n_all // tm
    pw = w_in.shape[1]
    ones_bd = _head_sum_matrix()

    def tok_spec(w):
        return pl.BlockSpec((1, tm, w), lambda bi, i: (bi, i, 0))

    mod_spec = pl.BlockSpec((1, 1, d), lambda bi, i: (jnp.where(i < nct, b, bi), 0, 0))
    outs = [(HGRN_WIDTH, BF16), (HGRN_WIDTH, F32), (HGRN_WIDTH, F32), (HGRN_WIDTH, BF16),
            (HGRN_WIDTH, BF16)]
    vt_rows = ATTN_HEAD_DIM + 2 * SUBLANES
    lat_spec = pl.BlockSpec((1, tm, ATTN_WIDTH), lambda bi, i: (bi, jnp.maximum(i - nct, 0), 0))
    k_spec = pl.BlockSpec((1, ATTN_KV_HEADS, tm, ATTN_HEAD_DIM), lambda bi, i: (bi, 0, i, 0))
    vt_spec = pl.BlockSpec((1, ATTN_KV_HEADS, vt_rows, tm), lambda bi, i: (bi, 0, 0, i))
    return pl.pallas_call(
        functools.partial(_in_proj_kernel, nct),
        out_shape=[jax.ShapeDtypeStruct((b, s, ATTN_WIDTH), BF16),
                   jax.ShapeDtypeStruct((b, ATTN_KV_HEADS, n_all, ATTN_HEAD_DIM), BF16),
                   jax.ShapeDtypeStruct((b, ATTN_KV_HEADS, vt_rows, n_all), BF16)]
        + [jax.ShapeDtypeStruct((b, n_all, w), dt) for w, dt in outs],
        grid=(b, nt),
        in_specs=[
            pl.BlockSpec((1, tm, d), lambda bi, i: (bi, jnp.minimum(i, nct - 1), 0)),
            pl.BlockSpec((1, tm, d), lambda bi, i: (bi, jnp.maximum(i - nct, 0), 0)),
            mod_spec, mod_spec,
            pl.BlockSpec((1, d), lambda bi, i: (0, 0)),
            pl.BlockSpec((d, pw), lambda bi, i: (0, 0)),
            pl.BlockSpec((1, _QK_WIDTH), lambda bi, i: (0, 0)),
            pl.BlockSpec((LANES, LANES), lambda bi, i: (0, 0)),
            pl.BlockSpec((tm, LANES), lambda bi, i: (i, 0)),
            pl.BlockSpec((tm, LANES), lambda bi, i: (i, 0)),
        ],
        out_specs=[lat_spec, k_spec, vt_spec] + [tok_spec(w) for w, _ in outs],
        compiler_params=_params("arbitrary", "arbitrary"),
        name="in_proj",
    )(ctx, x, shift, scale, g1, w_in, qkg, ones_bd, cos, sin)


def _attention_kernel(n_kv_tiles, q_ref, k_ref, vt_ref, o_ref, qs_ref, s_ref, ksq_ref):
    tq = q_ref.shape[1]
    cols = ATTN_GROUP * tq
    for h in range(ATTN_GROUP):
        qs_ref[h * tq:(h + 1) * tq, :] = q_ref[0, :, h * ATTN_HEAD_DIM:(h + 1) * ATTN_HEAD_DIM]

    @pl.when(pl.program_id(2) == 0)
    def _():
        k = k_ref[0, 0].astype(F32)
        ksq = jnp.max(jnp.sum(k * k, axis=1, keepdims=True), axis=0, keepdims=True)
        ksq_ref[...] = jnp.broadcast_to(ksq, ksq_ref.shape)

    def tile(j):
        return pl.ds(pl.multiple_of(j * ATTN_KV_TILE, ATTN_KV_TILE), ATTN_KV_TILE)

    def scores(j):
        return _dot_nt(k_ref[0, 0, tile(j), :], qs_ref[...])

    def weighted(j, p):
        return _dot(vt_ref[0, 0, :, tile(j)], p)

    def finish(acc):
        o = acc[:ATTN_HEAD_DIM] / acc[ATTN_HEAD_DIM:ATTN_HEAD_DIM + 1]
        o_ref[0] = jnp.concatenate(
            [o[:, h * tq:(h + 1) * tq].T for h in range(ATTN_GROUP)], axis=1).astype(o_ref.dtype)

    qf = qs_ref[...].astype(F32)
    qsq = _dot_nt(jnp.ones((SUBLANES, ATTN_HEAD_DIM), BF16), (qf * qf).astype(BF16))[0:1]
    bound = jnp.sqrt(qsq * ksq_ref[0:1, 0:1]) * ATTN_BOUND_SLACK
    acc0 = jnp.zeros((vt_ref.shape[2], cols), F32)
    safe = jnp.max(bound) <= ATTN_BOUND_MAX

    @pl.when(safe)
    def _():
        def absorb(j, acc):
            return acc + weighted(j, jnp.exp2(scores(j) - bound).astype(BF16))

        def group(i, acc):
            for u in range(ATTN_KV_UNROLL):
                acc = absorb(ATTN_KV_UNROLL * i + u, acc)
            return acc

        n_groups = n_kv_tiles // ATTN_KV_UNROLL
        acc = lax.fori_loop(0, n_groups, group, acc0)
        for j in range(n_groups * ATTN_KV_UNROLL, n_kv_tiles):
            acc = absorb(j, acc)
        finish(acc)

    @pl.when(jnp.logical_not(safe))
    def _():
        def score(j, slot):
            s_ref[slot] = scores(j)

        def absorb(j, slot, carry):
            m, acc = carry
            s = s_ref[slot]
            m_new = jnp.maximum(m, jnp.max(s, axis=0, keepdims=True))
            p = jnp.exp2(s - m_new).astype(BF16)
            return m_new, jnp.exp2(m - m_new) * acc + weighted(j, p)

        def pair(i, carry):
            score(2 * i + 1, 1)
            carry = absorb(2 * i, 0, carry)
            score(2 * i + 2, 0)
            return absorb(2 * i + 1, 1, carry)

        n_pairs = (n_kv_tiles - 1) // 2
        score(0, 0)
        carry = lax.fori_loop(0, n_pairs, pair, (jnp.full((1, cols), NEG_BIG, F32), acc0))
        if n_kv_tiles % 2 == 1:
            carry = absorb(n_kv_tiles - 1, 0, carry)
        else:
            score(n_kv_tiles - 1, 1)
            carry = absorb(n_kv_tiles - 2, 0, carry)
            carry = absorb(n_kv_tiles - 1, 1, carry)
        finish(carry[1])


def _attention(qa, k_heads, vt_heads):
    b, s, _ = qa.shape
    n_all = k_heads.shape[2]
    tq = ATTN_Q_TILE
    gw = ATTN_GROUP * ATTN_HEAD_DIM
    vt_rows = vt_heads.shape[2]
    return pl.pallas_call(
        functools.partial(_attention_kernel, n_all // ATTN_KV_TILE),
        out_shape=jax.ShapeDtypeStruct((b, s, ATTN_WIDTH), BF16),
        grid=(b, ATTN_KV_HEADS, s // tq),
        in_specs=[pl.BlockSpec((1, tq, gw), lambda bi, kv, i: (bi, i, kv)),
                  pl.BlockSpec((1, 1, n_all, ATTN_HEAD_DIM), lambda bi, kv, i: (bi, kv, 0, 0)),
                  pl.BlockSpec((1, 1, vt_rows, n_all), lambda bi, kv, i: (bi, kv, 0, 0))],
        out_specs=pl.BlockSpec((1, tq, gw), lambda bi, kv, i: (bi, i, kv)),
        scratch_shapes=[pltpu.VMEM((ATTN_GROUP * tq, ATTN_HEAD_DIM), BF16),
                        pltpu.VMEM((2, ATTN_KV_TILE, ATTN_GROUP * tq), F32),
                        pltpu.VMEM((SUBLANES, LANES), F32)],
        compiler_params=_params("arbitrary", "arbitrary", "arbitrary"),
        name="attention",
    )(qa, k_heads, vt_heads)


def _hgrn_masks(reverse):
    c = HGRN_CHUNK
    t = lax.broadcasted_iota(jnp.int32, (c, c), 0)
    u = lax.broadcasted_iota(jnp.int32, (c, c), 1)
    tri = (u >= t) if reverse else (u <= t)
    levels = []
    size = c // 2
    while size >= 1:
        same_parent = (t // (2 * size)) == (u // (2 * size))
        levels.append((size, same_parent if 2 * size < c else None))
        size //= 2
    levels.append((0, t == u))
    return tri.astype(BF16), levels


def _hgrn_chunk(q, fr, v, lb, state, reverse, tri, levels):
    c = HGRN_CHUNK
    coarse_levels = [lv for lv in levels if lv[0] >= SUBLANES]
    fine_levels = [lv for lv in levels if lv[0] < SUBLANES]
    f = lb + (1.0 - lb) * _sigmoid(fr)
    k = 1.0 - f
    g_hi, g_lo = _split_bf16(jnp.log2(f))
    bcum = _dot(tri, g_hi) + _dot(tri, g_lo)
    qf = q.astype(F32)
    yield None

    st = state()
    end = 0 if reverse else c - 1
    b_end = bcum[end:end + 1, :]
    inter = _dot_nt((qf * jnp.exp2(bcum)).astype(BF16), st.astype(BF16))
    ke = (k * jnp.exp2(b_end - bcum)).astype(BF16)
    st_add = _dot_tn(v, ke)

    products = []
    for size, mask in coarse_levels:
        q_rows, k_rows = [], []
        zeros = jnp.zeros((size, qf.shape[1]), BF16)
        for p0 in range(0, c, 2 * size):
            early = slice(p0, p0 + size)
            late = slice(p0 + size, p0 + 2 * size)
            r = p0 + size if reverse else p0 + size - 1
            ref = bcum[r:r + 1, :]
            q_sl, k_sl = (early, late) if reverse else (late, early)
            q_blk = (qf[q_sl] * jnp.exp2(bcum[q_sl] - ref)).astype(BF16)
            k_blk = (k[k_sl] * jnp.exp2(ref - bcum[k_sl])).astype(BF16)
            q_rows += [q_blk, zeros] if reverse else [zeros, q_blk]
            k_rows += [zeros, k_blk] if reverse else [k_blk, zeros]
        products.append((mask, _dot_nt(jnp.concatenate(q_rows, axis=0),
                                       jnp.concatenate(k_rows, axis=0))))
    yield None

    row = lax.broadcasted_iota(jnp.int32, (c, 1), 0)
    sub = lax.broadcasted_iota(jnp.int32, (SUBLANES, 1), 0)
    fine_products = []
    for size, mask in fine_levels:
        if size == 0:
            q_l, k_l = q, k.astype(BF16)
        else:
            late = (row // size) % 2 == 1
            q_side = jnp.logical_not(late) if reverse else late
            if size == 1:
                fac_q, fac_k = f, None
            else:
                groups = []
                for r0 in range(0, c, SUBLANES):
                    ref = None
                    for p0 in range(0, SUBLANES, 2 * size):
                        r = r0 + p0 + (size if reverse else size - 1)
                        piece = jnp.broadcast_to(bcum[r:r + 1, :], (SUBLANES, bcum.shape[1]))
                        ref = piece if ref is None else jnp.where(sub >= p0, piece, ref)
                    groups.append(ref)
                ref = jnp.concatenate(groups, axis=0)
                fac_q = fac_k = jnp.exp2(jnp.where(q_side, bcum - ref, ref - bcum))
            q_l = jnp.where(q_side, qf * fac_q, 0.0).astype(BF16)
            k_l = jnp.where(q_side, 0.0, k if fac_k is None else k * fac_k).astype(BF16)
        fine_products.append((mask, _dot_nt(q_l, k_l)))
    a = None
    for mask, a_l in products:
        if mask is not None:
            a_l = jnp.where(mask, a_l, 0.0)
        a = a_l if a is None else a + a_l
    yield None

    for mask, a_l in fine_products:
        a = a + jnp.where(mask, a_l, 0.0)
    intra = _dot(a.astype(BF16), v)
    st_new = st * jnp.exp2(b_end) + st_add
    yield None

    yield inter + intra, st_new


_HGRN_STAGES = 5


def _hgrn_kernel(qf_ref, if_ref, ff_ref, qb_ref, ib_ref, fb_ref, lb_ref, of_ref, ob_ref,
                 sf_ref, sb_ref):
    @pl.when(pl.program_id(1) == 0)
    def _():
        sf_ref[...] = jnp.zeros_like(sf_ref)
        sb_ref[...] = jnp.zeros_like(sb_ref)

    dirs = ((False, qf_ref, if_ref, ff_ref, of_ref, sf_ref),
            (True, qb_ref, ib_ref, fb_ref, ob_ref, sb_ref))
    masks = [_hgrn_masks(reverse) for reverse, *_ in dirs]
    n_sub = qf_ref.shape[1] // HGRN_CHUNK
    latest = {}
    chunks = []
    for t in range(n_sub):
        for h in range(HGRN_HEADS):
            sl = slice(h * HGRN_HEAD_DIM, (h + 1) * HGRN_HEAD_DIM)
            for d, (reverse, q_ref, i_ref, f_ref, o_ref, s_ref) in enumerate(dirs):
                r0 = (n_sub - 1 - t if reverse else t) * HGRN_CHUNK
                rows = slice(r0, r0 + HGRN_CHUNK)

                def state(h=h, d=d, s_ref=s_ref):
                    return latest[h, d] if (h, d) in latest else s_ref[h]

                gen = _hgrn_chunk(q_ref[0, rows, sl], f_ref[0, rows, sl], i_ref[0, rows, sl],
                                  lb_ref[d:d + 1, sl], state, reverse, *masks[d])
                chunks.append((gen, o_ref, s_ref, h, d, rows, sl, t == n_sub - 1))

    assert len(dirs) * HGRN_HEADS >= _HGRN_STAGES
    for step in range(len(chunks) + _HGRN_STAGES - 1):
        for stage in range(_HGRN_STAGES):
            idx = step - stage
            if 0 <= idx < len(chunks):
                gen, o_ref, s_ref, h, d, rows, sl, is_last = chunks[idx]
                result = next(gen)
                if stage == _HGRN_STAGES - 1:
                    o, st = result
                    o_ref[0, rows, sl] = o
                    latest[h, d] = st
                    if is_last:
                        s_ref[h] = st


def _hgrn(qr, ir, ff, fb, lb, n_ctx):
    b, n_all, w = qr.shape
    c = HGRN_CHUNK * HGRN_STEP_CHUNKS
    assert n_ctx % c == 0 and n_all % c == 0
    nc = n_all // c
    ncc = n_ctx // c

    def fwd(bi, j):
        return (bi, j, 0)

    def bwd(bi, j):
        return (bi, jnp.where(j < ncc, ncc - 1 - j, nc - 1 - (j - ncc)), 0)

    blk = (1, c, w)
    state = pltpu.VMEM((HGRN_HEADS, HGRN_HEAD_DIM, HGRN_HEAD_DIM), F32)
    return pl.pallas_call(
        _hgrn_kernel,
        out_shape=[jax.ShapeDtypeStruct((b, n_all, w), F32)] * 2,
        grid=(b, nc),
        in_specs=[pl.BlockSpec(blk, fwd), pl.BlockSpec(blk, fwd), pl.BlockSpec(blk, fwd),
                  pl.BlockSpec(blk, bwd), pl.BlockSpec(blk, bwd), pl.BlockSpec(blk, bwd),
                  pl.BlockSpec((2, w), lambda bi, j: (0, 0))],
        out_specs=[pl.BlockSpec(blk, fwd), pl.BlockSpec(blk, bwd)],
        scratch_shapes=[state, state],
        compiler_params=_params("arbitrary", "arbitrary"),
        name="hgrn",
    )(qr, ir, ff, qr, ir, fb, lb)


_ROUTE_GROUP_LANE0 = N_EXPERTS


def _lane_min_index(cond, lane):
    return jnp.min(jnp.where(cond, lane, LANES), axis=-1, keepdims=True)


def _out_proj_kernel(oa_ref, of_ref, ob_ref, gr_ref, x_ref, gate_ref, shift_ref, scale_ref,
                     ag_ref, hg_ref, g2_ref, w_ref, ones_ref, wr_hi_ref, wr_lo_ref, br_ref,
                     x1_ref, h2_ref, ri_ref, rw_ref, cnt_ref, carry_ref):
    first = (pl.program_id(0) == 0) & (pl.program_id(1) == 0)

    @pl.when(first)
    def _():
        carry_ref[...] = jnp.zeros_like(carry_ref)

    tm = x_ref.shape[1]

    ones_bd = ones_ref[...]
    slabs = []
    for c0 in range(0, ATTN_WIDTH, LANES):
        t = oa_ref[0, :, c0:c0 + LANES].astype(F32)
        slabs.append((t * _head_rms_scale(t, ones_bd) * ag_ref[:, c0:c0 + LANES]).astype(BF16))
    orr = of_ref[0] + ob_ref[0]
    parts = []
    for h in range(HGRN_HEADS):
        sl = slice(h * HGRN_HEAD_DIM, (h + 1) * HGRN_HEAD_DIM)
        oh = orr[:, sl]
        parts.append(oh * lax.rsqrt(jnp.mean(oh * oh, axis=-1, keepdims=True) + EPS))
    orr = jnp.concatenate(parts, axis=1) * hg_ref[...] * gr_ref[0].astype(F32)
    mix_in = jnp.concatenate(slabs + [orr.astype(BF16)], axis=1)
    x1 = x_ref[0] + gate_ref[0] * _dot(mix_in, w_ref[...])
    x1_ref[0] = x1
    h2 = x1 * lax.rsqrt(jnp.mean(x1 * x1, axis=-1, keepdims=True) + EPS) * g2_ref[...]
    h2 = h2 * (1.0 + scale_ref[0]) + shift_ref[0]
    for c in range(h2_ref.shape[1]):
        h2_ref[:, c] = h2[:, c * LANES:(c + 1) * LANES].reshape(tm // SUBLANES, SUBLANES, LANES)

    h_hi, h_lo = _split_bf16(h2)
    logits = (_dot(h_hi, wr_hi_ref[...]) + _dot(h_lo, wr_hi_ref[...])
              + _dot(h_hi, wr_lo_ref[...]) + br_ref[...])
    lane = lax.broadcasted_iota(jnp.int32, logits.shape, 1)
    is_grp = (lane >= _ROUTE_GROUP_LANE0) & (lane < _ROUTE_GROUP_LANE0 + N_GROUPS)
    lg = jnp.where(is_grp, logits, NEG_BIG)
    mg = jnp.max(lg, axis=-1, keepdims=True)
    g_sel = _lane_min_index(lg == mg, lane) - _ROUTE_GROUP_LANE0
    pg_top = 1.0 / jnp.sum(jnp.exp(lg - mg), axis=-1, keepdims=True)
    in_grp = (lane < N_EXPERTS) & ((lane // EXPERTS_PER_GROUP) == g_sel)
    le = jnp.where(in_grp, logits, NEG_BIG)
    m1 = jnp.max(le, axis=-1, keepdims=True)
    e1 = _lane_min_index(le == m1, lane)
    le2 = jnp.where(lane == e1, NEG_BIG, le)
    m2 = jnp.max(le2, axis=-1, keepdims=True)
    e2 = _lane_min_index(le2 == m2, lane)
    r2 = jnp.exp(m2 - m1)
    w1 = pg_top / (1.0 + r2)
    w2 = pg_top * r2 / (1.0 + r2)

    onehot = ((lane == e1) | (lane == e2)).astype(BF16)
    rt = lax.broadcasted_iota(jnp.int32, (tm, tm), 0)
    ru = lax.broadcasted_iota(jnp.int32, (tm, tm), 1)
    before = _dot((ru < rt).astype(BF16), onehot) + carry_ref[0:1, :]
    rank1 = jnp.sum(jnp.where(lane == e1, before, 0.0), axis=-1, keepdims=True)
    rank2 = jnp.sum(jnp.where(lane == e2, before, 0.0), axis=-1, keepdims=True)
    total = carry_ref[0:1, :] + jnp.sum(onehot.astype(F32), axis=0, keepdims=True)
    carry_ref[...] = jnp.broadcast_to(total, carry_ref.shape)
    cnt_ref[...] = jnp.broadcast_to(total, cnt_ref.shape).astype(jnp.int32)

    rl = lax.broadcasted_iota(jnp.int32, (tm, ROUTE_LANES), 1)
    ri = jnp.where(rl == 0, e1, jnp.where(rl == 1, e2, jnp.where(
        rl == 2, rank1.astype(jnp.int32), jnp.where(rl == 3, rank2.astype(jnp.int32), 0))))
    ri_ref[0] = ri
    rw_ref[0] = jnp.where(rl == 0, w1, jnp.where(rl == 1, w2, 0.0))


def _out_proj(oa, o_f, o_b, gr, x, gate, shift, scale, ag, hg, g2, w_out, wr_hi, wr_lo, br, n_ctx):
    b, s, d = x.shape
    tm = TOKEN_TILE
    off = n_ctx // tm
    ones_bd = _head_sum_matrix()

    def lat(w):
        return pl.BlockSpec((1, tm, w), lambda bi, i: (bi, i, 0))

    def allrows(w):
        return pl.BlockSpec((1, tm, w), lambda bi, i: (bi, i + off, 0))

    def mod():
        return pl.BlockSpec((1, 1, d), lambda bi, i: (bi, 0, 0))

    def const(shape):
        return pl.BlockSpec(shape, lambda bi, i: (0,) * len(shape))

    return pl.pallas_call(
        _out_proj_kernel,
        out_shape=[jax.ShapeDtypeStruct((b, s, d), F32),
                   jax.ShapeDtypeStruct((b * s // SUBLANES, d // LANES, SUBLANES, LANES), F32),
                   jax.ShapeDtypeStruct((b, s, ROUTE_LANES), jnp.int32),
                   jax.ShapeDtypeStruct((b, s, ROUTE_LANES), F32),
                   jax.ShapeDtypeStruct((SUBLANES, LANES), jnp.int32)],
        grid=(b, s // tm),
        in_specs=[lat(ATTN_WIDTH), allrows(HGRN_WIDTH), allrows(HGRN_WIDTH), allrows(HGRN_WIDTH),
                  lat(d), mod(), mod(), mod(),
                  const((1, ATTN_WIDTH)), const((1, HGRN_WIDTH)), const((1, d)),
                  const((ATTN_WIDTH + HGRN_WIDTH, d)), const((LANES, LANES)),
                  const((d, LANES)), const((d, LANES)), const((1, LANES))],
        out_specs=[lat(d),
                   pl.BlockSpec((tm // SUBLANES, d // LANES, SUBLANES, LANES),
                                lambda bi, i: (bi * (s // tm) + i, 0, 0, 0)),
                   lat(ROUTE_LANES), lat(ROUTE_LANES), const((SUBLANES, LANES))],
        scratch_shapes=[pltpu.VMEM((SUBLANES, LANES), F32)],
        compiler_params=_params("arbitrary", "arbitrary"),
        name="out_proj",
    )(oa, o_f, o_b, gr, x, gate, shift, scale, ag, hg, g2, w_out, ones_bd, wr_hi, wr_lo, br)


def _slab_copies(stage, hbm, row0, sem, to_hbm):
    copies = []
    for c in range(stage.shape[0]):
        view = hbm.at[pl.ds(row0, stage.shape[1]), c, :]
        src, dst = (stage.at[c], view) if to_hbm else (view, stage.at[c])
        copies.append(pltpu.make_async_copy(src, dst, sem))
    return copies


def _stage_store(stage, x):
    for c in range(stage.shape[0]):
        stage[c] = x[:, c * LANES:(c + 1) * LANES]


def _stage_load(stage):
    return jnp.concatenate([stage[c] for c in range(stage.shape[0])], axis=1)


_DISPATCH_SLOTS = 3


def _dispatch_kernel(dest_ref, h2_hbm, xs_init_hbm, xs_hbm, buf, sem_in, sem_out):
    del xs_init_hbm
    i = pl.program_id(0)
    n = pl.num_programs(0)
    groups = buf.shape[1]
    tm = groups * SUBLANES
    slot = i % _DISPATCH_SLOTS

    def fetch(t, sl):
        return pltpu.make_async_copy(h2_hbm.at[pl.ds(t * groups, groups)], buf.at[sl],
                                     sem_in.at[sl])

    def wait_rows(sl):
        for _ in range(TOP_K):
            pltpu.make_async_copy(h2_hbm.at[pl.ds(0, groups)], buf.at[sl], sem_out.at[sl]).wait()

    @pl.when(i == 0)
    def _():
        fetch(0, 0).start()

    @pl.when(i >= 2)
    def _():
        wait_rows((i + 1) % _DISPATCH_SLOTS)

    @pl.when(i + 1 < n)
    def _():
        fetch(i + 1, (i + 1) % _DISPATCH_SLOTS).start()

    fetch(i, slot).wait()

    def body(g, carry):
        rows = [[dest_ref[0, 0, k * tm + g * SUBLANES + u] for k in range(TOP_K)]
                for u in range(SUBLANES)]
        for u in range(SUBLANES):
            for k in range(TOP_K):
                pltpu.make_async_copy(buf.at[slot, g, :, u, :], xs_hbm.at[rows[u][k]],
                                      sem_out.at[slot]).start()
        return carry
    lax.fori_loop(0, groups, body, 0)

    @pl.when(i == n - 1)
    def _():
        wait_rows(slot)

    @pl.when((i == n - 1) & (i >= 1))
    def _():
        wait_rows((i - 1) % _DISPATCH_SLOTS)


def _dispatch(dest_tiles, h2_tiles, n_slots):
    nt = dest_tiles.shape[0]
    n_groups, n_slabs, _, _ = h2_tiles.shape
    groups = n_groups // nt
    xs_init = jnp.zeros((n_slots, n_slabs, LANES), F32)
    return pl.pallas_call(
        _dispatch_kernel,
        out_shape=jax.ShapeDtypeStruct(xs_init.shape, F32),
        grid=(nt,),
        in_specs=[pl.BlockSpec((1, 1, dest_tiles.shape[2]), lambda i: (i, 0, 0),
                               memory_space=pltpu.SMEM),
                  pl.BlockSpec(memory_space=pl.ANY), pl.BlockSpec(memory_space=pl.ANY)],
        out_specs=pl.BlockSpec(memory_space=pl.ANY),
        scratch_shapes=[pltpu.VMEM((_DISPATCH_SLOTS, groups, n_slabs, SUBLANES, LANES), F32),
                        pltpu.SemaphoreType.DMA((_DISPATCH_SLOTS,)),
                        pltpu.SemaphoreType.DMA((_DISPATCH_SLOTS,))],
        input_output_aliases={2: 0},
        compiler_params=_params("arbitrary"),
        name="dispatch",
    )(dest_tiles, h2_tiles, xs_init)


_EXPERT_IN_SLOTS = 3


def _experts_kernel(meta_ref, xs_hbm, wg_ref, wu_ref, wd_ref, ys_hbm,
                    xstage, ystage, wg_bf, wu_bf, wd_bf, sem_in, sem_out):
    j = pl.program_id(0)
    last = pl.num_programs(0) - 1
    n_used = meta_ref[0]
    slot = j % 2

    def fetch(blk, sl):
        for cp in _slab_copies(xstage.at[sl], xs_hbm, blk * MOE_BLOCK, sem_in.at[sl], False):
            cp.start()

    def wait_out(sl):
        for cp in _slab_copies(ystage.at[sl], ys_hbm, 0, sem_out.at[sl], True):
            cp.wait()

    in_slot = j % _EXPERT_IN_SLOTS

    @pl.when((j == 0) & (n_used > 0))
    def _():
        fetch(0, 0)

    @pl.when((j == 0) & (n_used > 1))
    def _():
        fetch(1, 1)

    @pl.when(j + 2 < n_used)
    def _():
        fetch(j + 2, (j + 2) % _EXPERT_IN_SLOTS)

    new_expert = (j == 0) | (meta_ref[1 + j] != meta_ref[jnp.maximum(j, 1)])

    @pl.when((j < n_used) & new_expert)
    def _():
        wg_bf[...] = wg_ref[0].astype(BF16)
        wu_bf[...] = wu_ref[0].astype(BF16)
        wd_bf[...] = wd_ref[0].astype(BF16)

    @pl.when(j >= 2)
    def _():
        wait_out(slot)

    @pl.when(j < n_used)
    def _():
        for cp in _slab_copies(xstage.at[in_slot], xs_hbm, 0, sem_in.at[in_slot], False):
            cp.wait()
        xb = _stage_load(xstage.at[in_slot]).astype(BF16)
        a = _silu(_dot(xb, wg_bf[...])) * _dot(xb, wu_bf[...])
        _stage_store(ystage.at[slot], _dot(a.astype(BF16), wd_bf[...]))

    @pl.when(j >= n_used)
    def _():
        ystage[slot] = jnp.zeros(ystage.shape[1:], F32)

    for cp in _slab_copies(ystage.at[slot], ys_hbm, j * MOE_BLOCK, sem_out.at[slot], True):
        cp.start()

    @pl.when(j == last)
    def _():
        wait_out(slot)

    @pl.when((j == last) & (j >= 1))
    def _():
        wait_out(1 - slot)


def _experts(meta, xs, wg, wu, wd):
    n_slots, n_slabs, _ = xs.shape
    n_blk = n_slots // MOE_BLOCK
    d = wg.shape[1]
    ff = wg.shape[2]

    def wspec(shape):
        return pl.BlockSpec((1,) + shape, lambda j, meta: (meta[1 + j], 0, 0))

    in_stage = pltpu.VMEM((_EXPERT_IN_SLOTS, n_slabs, MOE_BLOCK, LANES), F32)
    stage = pltpu.VMEM((2, n_slabs, MOE_BLOCK, LANES), F32)
    return pl.pallas_call(
        _experts_kernel,
        out_shape=jax.ShapeDtypeStruct(xs.shape, F32),
        grid_spec=pltpu.PrefetchScalarGridSpec(
            num_scalar_prefetch=1,
            grid=(n_blk,),
            in_specs=[pl.BlockSpec(memory_space=pl.ANY),
                      wspec((d, ff)), wspec((d, ff)), wspec((ff, d))],
            out_specs=pl.BlockSpec(memory_space=pl.ANY),
            scratch_shapes=[in_stage, stage, pltpu.VMEM((d, ff), BF16),
                            pltpu.VMEM((d, ff), BF16), pltpu.VMEM((ff, d), BF16),
                            pltpu.SemaphoreType.DMA((_EXPERT_IN_SLOTS,)),
                            pltpu.SemaphoreType.DMA((2,))],
        ),
        compiler_params=_params("arbitrary"),
        name="experts",
    )(meta, xs, wg, wu, wd)


def _combine_kernel(dest_cur_ref, dest_next_ref, ys_hbm, x1_ref, gate_ref, rw_ref, o_ref,
                    buf, sem):
    i = pl.program_id(0)
    slot = i % 2
    tm = x1_ref.shape[0]
    groups = tm // SUBLANES

    def start(idx_ref, sl):
        for k in range(TOP_K):
            def body(g, carry, k=k):
                rows = [idx_ref[0, 0, k * tm + g * SUBLANES + u] for u in range(SUBLANES)]
                for u in range(SUBLANES):
                    pltpu.make_async_copy(ys_hbm.at[rows[u]], buf.at[sl, k, g, :, u, :],
                                          sem.at[sl, k]).start()
                return carry
            lax.fori_loop(0, groups, body, 0)

    def wait(sl, k):
        for u in range(SUBLANES):
            pltpu.make_async_copy(ys_hbm.at[pl.ds(0, groups)], buf.at[sl, k, :, :, u, :],
                                  sem.at[sl, k]).wait()

    @pl.when(i == 0)
    def _():
        start(dest_cur_ref, 0)

    @pl.when(i + 1 < pl.num_programs(0))
    def _():
        start(dest_next_ref, 1 - slot)

    rw = rw_ref[...]
    moe = None
    for k in range(TOP_K):
        wait(slot, k)
        rows = jnp.concatenate([buf[slot, k, :, c].reshape(tm, LANES)
                                for c in range(buf.shape[3])], axis=1)
        term = rw[:, k:k + 1] * rows
        moe = term if moe is None else moe + term
    o_ref[...] = x1_ref[...] + gate_ref[0] * moe


def _combine(dest_tiles, ys, x1, gate, rw):
    n_tok, d = x1.shape
    tm = TOKEN_TILE
    nt = n_tok // tm
    per_batch = nt // gate.shape[0]
    idx_blk = (1, 1, TOP_K * tm)
    return pl.pallas_call(
        _combine_kernel,
        out_shape=jax.ShapeDtypeStruct((n_tok, d), F32),
        grid=(nt,),
        in_specs=[
            pl.BlockSpec(idx_blk, lambda i: (i, 0, 0), memory_space=pltpu.SMEM),
            pl.BlockSpec(idx_blk, lambda i: (jnp.minimum(i + 1, nt - 1), 0, 0),
                         memory_space=pltpu.SMEM),
            pl.BlockSpec(memory_space=pl.ANY),
            pl.BlockSpec((tm, d), lambda i: (i, 0)),
            pl.BlockSpec((1, 1, d), lambda i: (i // per_batch, 0, 0)),
            pl.BlockSpec((tm, ROUTE_LANES), lambda i: (i, 0)),
        ],
        out_specs=pl.BlockSpec((tm, d), lambda i: (i, 0)),
        scratch_shapes=[pltpu.VMEM((2, TOP_K, tm // SUBLANES, d // LANES, SUBLANES, LANES), F32),
                        pltpu.SemaphoreType.DMA((2, TOP_K))],
        compiler_params=_params("arbitrary"),
        name="combine",
    )(dest_tiles, dest_tiles, ys, x1, gate, rw)


def _layer(x, ctx, c, c_ctx, w_ada, b_ada, norm1_g, norm2_g, w_in, q_norm_g, k_norm_g, attn_out_g,
           lb, hgrn_out_g, w_out, w_router_grp, b_router_grp, w_router_exp, b_router_exp,
           w_exp_gate, w_exp_up, w_exp_down):
    b, s, d = x.shape
    n_ctx = ctx.shape[1]
    assert n_ctx % TOKEN_TILE == 0 and s % TOKEN_TILE == 0 and s % GRID_W == 0
    assert n_ctx % HGRN_CHUNK == 0 and (n_ctx + s) % ATTN_KV_TILE == 0
    n_all = n_ctx + s

    cond = jnp.zeros((2 * SUBLANES, d), F32).at[:b].set(c).at[b].set(c_ctx)
    assert b + 1 <= cond.shape[0]
    mods = _adaln(cond, w_ada, b_ada)[:b + 1].reshape(b + 1, 1, 6, d)
    sh1, sc1, gt1, sh2, sc2, gt2 = (mods[:, :, m] for m in range(6))

    scale_q = ATTN_HEAD_DIM ** -0.5 * np.log2(np.e)
    qkg = jnp.concatenate([jnp.tile(q_norm_g, ATTN_HEADS) * scale_q,
                           jnp.tile(k_norm_g, ATTN_KV_HEADS)]).reshape(1, _QK_WIDTH)
    cos, sin = _rope_tables(n_ctx, s)
    qa, ka, vt, qr, ff, fb, ir, gr = _in_proj(
        ctx, x, sh1, sc1, norm1_g.reshape(1, d), w_in.astype(BF16), qkg, cos, sin)
    oa = _attention(qa, ka, vt)
    o_f, o_b = _hgrn(qr, ir, ff, fb, lb, n_ctx)

    w_router = jnp.zeros((d, LANES), F32)
    w_router = w_router.at[:, :N_EXPERTS].set(w_router_exp)
    w_router = w_router.at[:, _ROUTE_GROUP_LANE0:_ROUTE_GROUP_LANE0 + N_GROUPS].set(w_router_grp)
    b_router = jnp.zeros((1, LANES), F32)
    b_router = b_router.at[0, :N_EXPERTS].set(b_router_exp)
    b_router = b_router.at[0, _ROUTE_GROUP_LANE0:_ROUTE_GROUP_LANE0 + N_GROUPS].set(b_router_grp)
    wr_hi, wr_lo = _split_bf16(w_router)
    x1, h2, ri, rw, counts = _out_proj(
        oa, o_f, o_b, gr, x, gt1[:b], sh2[:b], sc2[:b], attn_out_g.reshape(1, -1),
        hgrn_out_g.reshape(1, -1), norm2_g.reshape(1, d), w_out.astype(BF16), wr_hi, wr_lo,
        b_router, n_ctx)

    n_tok = b * s
    counts = counts[0, :N_EXPERTS]
    padded = (counts + MOE_BLOCK - 1) // MOE_BLOCK * MOE_BLOCK
    pend = jnp.cumsum(padded)
    pstart = pend - padded
    n_blk = n_tok * TOP_K // MOE_BLOCK + N_EXPERTS
    ri = ri.reshape(n_tok, ROUTE_LANES)
    e_id = ri[:, :TOP_K]
    dest = pstart[e_id] + ri[:, TOP_K:2 * TOP_K]
    blk_start = jnp.arange(n_blk, dtype=jnp.int32) * MOE_BLOCK
    blk_e = jnp.minimum(jnp.sum(blk_start[:, None] >= pend[None, :], axis=1), N_EXPERTS - 1)
    n_used = pend[-1] // MOE_BLOCK
    meta = jnp.concatenate([n_used[None], blk_e]).astype(jnp.int32)
    nt = n_tok // TOKEN_TILE
    dest_tiles = dest.reshape(nt, TOKEN_TILE, TOP_K).transpose(0, 2, 1).reshape(nt, 1, -1)

    xs = _dispatch(dest_tiles, h2, n_blk * MOE_BLOCK)
    ys = _experts(meta, xs, w_exp_gate, w_exp_up, w_exp_down)
    out = _combine(dest_tiles, ys, x1.reshape(n_tok, d), gt2[:b], rw.reshape(n_tok, ROUTE_LANES))
    return out.reshape(b, s, d)


def kernel(x, c, ctx, c_ctx, w_ada, b_ada, norm1_g, norm2_g, w_in, q_norm_g, k_norm_g, attn_out_g,
           hgrn_lb, hgrn_out_g, w_out, w_router_grp, b_router_grp, w_router_exp, b_router_exp,
           w_exp_gate, w_exp_up, w_exp_down):
    depth = w_in.shape[0]
    assert depth == 1, "context stream update between layers is not implemented"
    lb_all = jnp.cumsum(jax.nn.softmax(hgrn_lb.astype(F32), axis=1), axis=1)
    layer = 0
    return _layer(x, ctx, c, c_ctx, w_ada[layer], b_ada[layer], norm1_g[layer], norm2_g[layer],
                  w_in[layer], q_norm_g[layer], k_norm_g[layer], attn_out_g[layer],
                  lb_all[:, layer], hgrn_out_g[layer], w_out[layer], w_router_grp[layer],
                  b_router_grp[layer], w_router_exp[layer], b_router_exp[layer],
                  w_exp_gate[layer], w_exp_up[layer], w_exp_down[layer])
```

```python
import functools

import numpy as np
import jax
import jax.numpy as jnp
from jax import lax
from jax.experimental import pallas as pl
from jax.experimental.pallas import tpu as pltpu

F32 = jnp.float32
BF16 = jnp.bfloat16

GRID_W = 64
EPS = 1e-6
ATTN_HEADS = 8
ATTN_KV_HEADS = 2
ATTN_HEAD_DIM = 64
ATTN_GROUP = ATTN_HEADS // ATTN_KV_HEADS
ATTN_WIDTH = ATTN_HEADS * ATTN_HEAD_DIM
KV_WIDTH = ATTN_KV_HEADS * ATTN_HEAD_DIM
ROPE_THETA = 10000.0
HGRN_HEADS = 4
HGRN_HEAD_DIM = 128
HGRN_WIDTH = HGRN_HEADS * HGRN_HEAD_DIM
N_GROUPS = 4
EXPERTS_PER_GROUP = 8
N_EXPERTS = N_GROUPS * EXPERTS_PER_GROUP
TOP_K = 2
EXPERT_FF = 512

LANES = 128
SUBLANES = 8
MXU_DIM = 256
VMEM_LIMIT_BYTES = 48 * 1024 * 1024

TOKEN_TILE = 256
ATTN_Q_TILE = 1024
ATTN_KV_TILE = 256
ATTN_KV_UNROLL = 17
ATTN_BOUND_SLACK = 1.02
ATTN_BOUND_MAX = 60.0
HGRN_CHUNK = 64
HGRN_STEP_CHUNKS = 4
MOE_BLOCK = 256
GATHER_UNROLL = 8
ROUTE_LANES = 8
NEG_BIG = -1e30

_QA0 = 0
_KA0 = _QA0 + ATTN_WIDTH
_VA0 = _KA0 + KV_WIDTH
_QR0 = _VA0 + KV_WIDTH
_FF0 = _QR0 + HGRN_WIDTH
_FB0 = _FF0 + HGRN_WIDTH
_IR0 = _FB0 + HGRN_WIDTH
_GR0 = _IR0 + HGRN_WIDTH
_QK_WIDTH = ATTN_WIDTH + KV_WIDTH


def _dot(a, b):
    return jnp.dot(a, b, preferred_element_type=F32)


def _dot_nt(a, b):
    return lax.dot_general(a, b, (((1,), (1,)), ((), ())), preferred_element_type=F32)


def _dot_tn(a, b):
    return lax.dot_general(a, b, (((0,), (0,)), ((), ())), preferred_element_type=F32)


def _split_bf16(x):
    hi = x.astype(BF16)
    lo = (x - hi.astype(F32)).astype(BF16)
    return hi, lo


def _sigmoid(x):
    return 1.0 / (1.0 + jnp.exp(-x))


def _silu(x):
    return x * _sigmoid(x)


def _params(*sem):
    return pltpu.CompilerParams(dimension_semantics=sem, vmem_limit_bytes=VMEM_LIMIT_BYTES)


def _head_sum_matrix():
    idx = np.arange(LANES) // ATTN_HEAD_DIM
    return jnp.asarray(idx[:, None] == idx[None, :], dtype=BF16)


def _head_rms_scale(x, ones_bd):
    ssq = _dot((x * x).astype(BF16), ones_bd)
    return lax.rsqrt(ssq * (1.0 / ATTN_HEAD_DIM) + EPS)


def _adaln_kernel(cond_ref, w_ref, b_ref, o_ref):
    s = _silu(cond_ref[...])
    s_hi, s_lo = _split_bf16(s)
    w_hi, w_lo = _split_bf16(w_ref[...])
    o_ref[...] = _dot(s_hi, w_hi) + _dot(s_lo, w_hi) + _dot(s_hi, w_lo) + b_ref[...]


def _adaln(cond, w_ada, b_ada):
    rows, d = cond.shape
    n = w_ada.shape[1]
    tn = n // 6
    return pl.pallas_call(
        _adaln_kernel,
        out_shape=jax.ShapeDtypeStruct((rows, n), F32),
        grid=(n // tn,),
        in_specs=[pl.BlockSpec((rows, d), lambda j: (0, 0)),
                  pl.BlockSpec((d, tn), lambda j: (0, j)),
                  pl.BlockSpec((1, tn), lambda j: (0, j))],
        out_specs=pl.BlockSpec((rows, tn), lambda j: (0, j)),
        compiler_params=_params("arbitrary"),
        name="adaln",
    )(cond, w_ada, b_ada.reshape(1, n))


def _rope_tables(n_ctx, n_lat):
    half = ATTN_HEAD_DIM // 2
    freqs = ROPE_THETA ** (-np.arange(0, half, 2, dtype=np.float64) / half)
    tok = np.arange(n_lat)
    pos = np.stack([tok // GRID_W, tok % GRID_W], axis=1).astype(np.float64)
    lane = np.arange(ATTN_HEAD_DIM)
    axis = lane // half
    fi = (lane % half) // 2
    ang = pos[:, axis] * freqs[fi][None, :]
    sign = np.where(lane % 2 == 1, 1.0, -1.0)
    cos = np.concatenate([np.ones((n_ctx, ATTN_HEAD_DIM)), np.cos(ang)], axis=0)
    sin = np.concatenate([np.zeros((n_ctx, ATTN_HEAD_DIM)), np.sin(ang) * sign], axis=0)
    reps = LANES // ATTN_HEAD_DIM
    return (jnp.asarray(np.tile(cos, (1, reps)), F32), jnp.asarray(np.tile(sin, (1, reps)), F32))


def _in_proj_kernel(n_ctx_tiles, ctx_ref, x_ref, shift_ref, scale_ref, g1_ref, w_ref, qkg_ref,
                    ones_ref, cos_ref, sin_ref,
                    qa_ref, ka_ref, vt_ref, qr_ref, ff_ref, fb_ref, ir_ref, gr_ref):
    i = pl.program_id(1)
    xt = jnp.where(i < n_ctx_tiles, ctx_ref[0], x_ref[0])
    ms = jnp.mean(xt * xt, axis=-1, keepdims=True)
    h = xt * lax.rsqrt(ms + EPS) * g1_ref[...]
    h = h * (1.0 + scale_ref[0]) + shift_ref[0]
    p = _dot(h.astype(BF16), w_ref[...])

    ones_bd = ones_ref[...]
    cos = cos_ref[...]
    sin = sin_ref[...]
    even = lax.broadcasted_iota(jnp.int32, cos.shape, 1) % 2 == 0
    slabs = []
    for c0 in range(0, _QK_WIDTH, LANES):
        t = p[:, _QA0 + c0:_QA0 + c0 + LANES]
        t = t * _head_rms_scale(t, ones_bd) * qkg_ref[:, c0:c0 + LANES]
        partner = jnp.where(even, pltpu.roll(t, LANES - 1, 1), pltpu.roll(t, 1, 1))
        slabs.append((t * cos + partner * sin).astype(BF16))
    qa_ref[0] = jnp.concatenate(slabs[:ATTN_WIDTH // LANES], axis=1)
    k_all = jnp.concatenate(slabs[ATTN_WIDTH // LANES:], axis=1)
    vt_all = p[:, _VA0:_VA0 + KV_WIDTH].T
    ones = jnp.ones((vt_ref.shape[2] - ATTN_HEAD_DIM, vt_all.shape[1]), F32)
    for hd in range(ATTN_KV_HEADS):
        cols = slice(hd * ATTN_HEAD_DIM, (hd + 1) * ATTN_HEAD_DIM)
        ka_ref[0, hd] = k_all[:, cols]
        vt_ref[0, hd] = jnp.concatenate([vt_all[cols], ones], axis=0).astype(BF16)

    qr_ref[0] = (_silu(p[:, _QR0:_QR0 + HGRN_WIDTH]) * (HGRN_HEAD_DIM ** -0.5)).astype(BF16)
    ff_ref[0] = p[:, _FF0:_FF0 + HGRN_WIDTH]
    fb_ref[0] = p[:, _FB0:_FB0 + HGRN_WIDTH]
    ir_ref[0] = p[:, _IR0:_IR0 + HGRN_WIDTH].astype(BF16)
    gr_ref[0] = _silu(p[:, _GR0:_GR0 + HGRN_WIDTH]).astype(BF16)


def _in_proj(ctx, x, shift, scale, g1, w_in, qkg, cos, sin):
    b, n_ctx, d = ctx.shape
    s = x.shape[1]
    tm = TOKEN_TILE
    nct = n_ctx // tm
    n_all = n_ctx + s
    nt = n_all // tm
    pw = w_in.shape[1]
    ones_bd = _head_sum_matrix()

    def tok_spec(w):
        return pl.BlockSpec((1, tm, w), lambda bi, i: (bi, i, 0))

    mod_spec = pl.BlockSpec((1, 1, d), lambda bi, i: (jnp.where(i < nct, b, bi), 0, 0))
    outs = [(HGRN_WIDTH, BF16), (HGRN_WIDTH, F32), (HGRN_WIDTH, F32), (HGRN_WIDTH, BF16),
            (HGRN_WIDTH, BF16)]
    vt_rows = ATTN_HEAD_DIM + 2 * SUBLANES
    lat_spec = pl.BlockSpec((1, tm, ATTN_WIDTH), lambda bi, i: (bi, jnp.maximum(i - nct, 0), 0))
    k_spec = pl.BlockSpec((1, ATTN_KV_HEADS, tm, ATTN_HEAD_DIM), lambda bi, i: (bi, 0, i, 0))
    vt_spec = pl.BlockSpec((1, ATTN_KV_HEADS, vt_rows, tm), lambda bi, i: (bi, 0, 0, i))
    return pl.pallas_call(
        functools.partial(_in_proj_kernel, nct),
        out_shape=[jax.ShapeDtypeStruct((b, s, ATTN_WIDTH), BF16),
                   jax.ShapeDtypeStruct((b, ATTN_KV_HEADS, n_all, ATTN_HEAD_DIM), BF16),
                   jax.ShapeDtypeStruct((b, ATTN_KV_HEADS, vt_rows, n_all), BF16)]
        + [jax.ShapeDtypeStruct((b, n_all, w), dt) for w, dt in outs],
        grid=(b, nt),
        in_specs=[
            pl.BlockSpec((1, tm, d), lambda bi, i: (bi, jnp.minimum(i, nct - 1), 0)),
            pl.BlockSpec((1, tm, d), lambda bi, i: (bi, jnp.maximum(i - nct, 0), 0)),
            mod_spec, mod_spec,
            pl.BlockSpec((1, d), lambda bi, i: (0, 0)),
            pl.BlockSpec((d, pw), lambda bi, i: (0, 0)),
            pl.BlockSpec((1, _QK_WIDTH), lambda bi, i: (0, 0)),
            pl.BlockSpec((LANES, LANES), lambda bi, i: (0, 0)),
            pl.BlockSpec((tm, LANES), lambda bi, i: (i, 0)),
            pl.BlockSpec((tm, LANES), lambda bi, i: (i, 0)),
        ],
        out_specs=[lat_spec, k_spec, vt_spec] + [tok_spec(w) for w, _ in outs],
        compiler_params=_params("arbitrary", "arbitrary"),
        name="in_proj",
    )(ctx, x, shift, scale, g1, w_in, qkg, ones_bd, cos, sin)


def _attention_kernel(n_kv_tiles, q_ref, k_ref, vt_ref, o_ref, qs_ref, s_ref, ksq_ref):
    tq = q_ref.shape[1]
    cols = ATTN_GROUP * tq
    for h in range(ATTN_GROUP):
        qs_ref[h * tq:(h + 1) * tq, :] = q_ref[0, :, h * ATTN_HEAD_DIM:(h + 1) * ATTN_HEAD_DIM]

    @pl.when(pl.program_id(2) == 0)
    def _():
        k = k_ref[0, 0].astype(F32)
        ksq = jnp.max(jnp.sum(k * k, axis=1, keepdims=True), axis=0, keepdims=True)
        ksq_ref[...] = jnp.broadcast_to(ksq, ksq_ref.shape)

    def tile(j):
        return pl.ds(pl.multiple_of(j * ATTN_KV_TILE, ATTN_KV_TILE), ATTN_KV_TILE)

    def scores(j):
        return _dot_nt(k_ref[0, 0, tile(j), :], qs_ref[...])

    def weighted(j, p):
        return _dot(vt_ref[0, 0, :, tile(j)], p)

    def finish(acc):
        o = acc[:ATTN_HEAD_DIM] / acc[ATTN_HEAD_DIM:ATTN_HEAD_DIM + 1]
        o_ref[0] = jnp.concatenate(
            [o[:, h * tq:(h + 1) * tq].T for h in range(ATTN_GROUP)], axis=1).astype(o_ref.dtype)

    qf = qs_ref[...].astype(F32)
    qsq = _dot_nt(jnp.ones((SUBLANES, ATTN_HEAD_DIM), BF16), (qf * qf).astype(BF16))[0:1]
    bound = jnp.sqrt(qsq * ksq_ref[0:1, 0:1]) * ATTN_BOUND_SLACK
    acc0 = jnp.zeros((vt_ref.shape[2], cols), F32)
    safe = jnp.max(bound) <= ATTN_BOUND_MAX

    @pl.when(safe)
    def _():
        def absorb(j, acc):
            return acc + weighted(j, jnp.exp2(scores(j) - bound).astype(BF16))

        def group(i, acc):
            for u in range(ATTN_KV_UNROLL):
                acc = absorb(ATTN_KV_UNROLL * i + u, acc)
            return acc

        n_groups = n_kv_tiles // ATTN_KV_UNROLL
        acc = lax.fori_loop(0, n_groups, group, acc0)
        for j in range(n_groups * ATTN_KV_UNROLL, n_kv_tiles):
            acc = absorb(j, acc)
        finish(acc)

    @pl.when(jnp.logical_not(safe))
    def _():
        def score(j, slot):
            s_ref[slot] = scores(j)

        def absorb(j, slot, carry):
            m, acc = carry
            s = s_ref[slot]
            m_new = jnp.maximum(m, jnp.max(s, axis=0, keepdims=True))
            p = jnp.exp2(s - m_new).astype(BF16)
            return m_new, jnp.exp2(m - m_new) * acc + weighted(j, p)

        def pair(i, carry):
            score(2 * i + 1, 1)
            carry = absorb(2 * i, 0, carry)
            score(2 * i + 2, 0)
            return absorb(2 * i + 1, 1, carry)

        n_pairs = (n_kv_tiles - 1) // 2
        score(0, 0)
        carry = lax.fori_loop(0, n_pairs, pair, (jnp.full((1, cols), NEG_BIG, F32), acc0))
        if n_kv_tiles % 2 == 1:
            carry = absorb(n_kv_tiles - 1, 0, carry)
        else:
            score(n_kv_tiles - 1, 1)
            carry = absorb(n_kv_tiles - 2, 0, carry)
            carry = absorb(n_kv_tiles - 1, 1, carry)
        finish(carry[1])


def _attention(qa, k_heads, vt_heads):
    b, s, _ = qa.shape
    n_all = k_heads.shape[2]
    tq = ATTN_Q_TILE
    gw = ATTN_GROUP * ATTN_HEAD_DIM
    vt_rows = vt_heads.shape[2]
    return pl.pallas_call(
        functools.partial(_attention_kernel, n_all // ATTN_KV_TILE),
        out_shape=jax.ShapeDtypeStruct((b, s, ATTN_WIDTH), BF16),
        grid=(b, ATTN_KV_HEADS, s // tq),
        in_specs=[pl.BlockSpec((1, tq, gw), lambda bi, kv, i: (bi, i, kv)),
                  pl.BlockSpec((1, 1, n_all, ATTN_HEAD_DIM), lambda bi, kv, i: (bi, kv, 0, 0)),
                  pl.BlockSpec((1, 1, vt_rows, n_all), lambda bi, kv, i: (bi, kv, 0, 0))],
        out_specs=pl.BlockSpec((1, tq, gw), lambda bi, kv, i: (bi, i, kv)),
        scratch_shapes=[pltpu.VMEM((ATTN_GROUP * tq, ATTN_HEAD_DIM), BF16),
                        pltpu.VMEM((2, ATTN_KV_TILE, ATTN_GROUP * tq), F32),
                        pltpu.VMEM((SUBLANES, LANES), F32)],
        compiler_params=_params("arbitrary", "arbitrary", "arbitrary"),
        name="attention",
    )(qa, k_heads, vt_heads)


def _hgrn_masks(reverse):
    c = HGRN_CHUNK
    t = lax.broadcasted_iota(jnp.int32, (c, c), 0)
    u = lax.broadcasted_iota(jnp.int32, (c, c), 1)
    tri = (u >= t) if reverse else (u <= t)
    levels = []
    size = c // 2
    while size >= 1:
        same_parent = (t // (2 * size)) == (u // (2 * size))
        levels.append((size, same_parent if 2 * size < c else None))
        size //= 2
    levels.append((0, t == u))
    return tri.astype(BF16), levels


def _hgrn_chunk(q, fr, v, lb, state, reverse, tri, levels):
    c = HGRN_CHUNK
    coarse_levels = [lv for lv in levels if lv[0] >= SUBLANES]
    fine_levels = [lv for lv in levels if lv[0] < SUBLANES]
    f = lb + (1.0 - lb) * _sigmoid(fr)
    k = 1.0 - f
    g_hi, g_lo = _split_bf16(jnp.log2(f))
    bcum = _dot(tri, g_hi) + _dot(tri, g_lo)
    qf = q.astype(F32)
    yield None

    st = state()
    end = 0 if reverse else c - 1
    b_end = bcum[end:end + 1, :]
    inter = _dot_nt((qf * jnp.exp2(bcum)).astype(BF16), st.astype(BF16))
    ke = (k * jnp.exp2(b_end - bcum)).astype(BF16)
    st_add = _dot_tn(v, ke)

    products = []
    for size, mask in coarse_levels:
        q_rows, k_rows = [], []
        zeros = jnp.zeros((size, qf.shape[1]), BF16)
        for p0 in range(0, c, 2 * size):
            early = slice(p0, p0 + size)
            late = slice(p0 + size, p0 + 2 * size)
            r = p0 + size if reverse else p0 + size - 1
            ref = bcum[r:r + 1, :]
            q_sl, k_sl = (early, late) if reverse else (late, early)
            q_blk = (qf[q_sl] * jnp.exp2(bcum[q_sl] - ref)).astype(BF16)
            k_blk = (k[k_sl] * jnp.exp2(ref - bcum[k_sl])).astype(BF16)
            q_rows += [q_blk, zeros] if reverse else [zeros, q_blk]
            k_rows += [zeros, k_blk] if reverse else [k_blk, zeros]
        products.append((mask, _dot_nt(jnp.concatenate(q_rows, axis=0),
                                       jnp.concatenate(k_rows, axis=0))))
    yield None

    row = lax.broadcasted_iota(jnp.int32, (c, 1), 0)
    sub = lax.broadcasted_iota(jnp.int32, (SUBLANES, 1), 0)
    fine_products = []
    for size, mask in fine_levels:
        if size == 0:
            q_l, k_l = q, k.astype(BF16)
        else:
            late = (row // size) % 2 == 1
            q_side = jnp.logical_not(late) if reverse else late
            if size == 1:
                fac_q, fac_k = f, None
            else:
                groups = []
                for r0 in range(0, c, SUBLANES):
                    ref = None
                    for p0 in range(0, SUBLANES, 2 * size):
                        r = r0 + p0 + (size if reverse else size - 1)
                        piece = jnp.broadcast_to(bcum[r:r + 1, :], (SUBLANES, bcum.shape[1]))
                        ref = piece if ref is None else jnp.where(sub >= p0, piece, ref)
                    groups.append(ref)
                ref = jnp.concatenate(groups, axis=0)
                fac_q = fac_k = jnp.exp2(jnp.where(q_side, bcum - ref, ref - bcum))
            q_l = jnp.where(q_side, qf * fac_q, 0.0).astype(BF16)
            k_l = jnp.where(q_side, 0.0, k if fac_k is None else k * fac_k).astype(BF16)
        fine_products.append((mask, _dot_nt(q_l, k_l)))
    a = None
    for mask, a_l in products:
        if mask is not None:
            a_l = jnp.where(mask, a_l, 0.0)
        a = a_l if a is None else a + a_l
    yield None

    for mask, a_l in fine_products:
        a = a + jnp.where(mask, a_l, 0.0)
    intra = _dot(a.astype(BF16), v)
    st_new = st * jnp.exp2(b_end) + st_add
    yield None

    yield inter + intra, st_new


_HGRN_STAGES = 5


def _hgrn_kernel(qf_ref, if_ref, ff_ref, qb_ref, ib_ref, fb_ref, lb_ref, of_ref, ob_ref,
                 sf_ref, sb_ref):
    @pl.when(pl.program_id(1) == 0)
    def _():
        sf_ref[...] = jnp.zeros_like(sf_ref)
        sb_ref[...] = jnp.zeros_like(sb_ref)

    dirs = ((False, qf_ref, if_ref, ff_ref, of_ref, sf_ref),
            (True, qb_ref, ib_ref, fb_ref, ob_ref, sb_ref))
    masks = [_hgrn_masks(reverse) for reverse, *_ in dirs]
    n_sub = qf_ref.shape[1] // HGRN_CHUNK
    latest = {}
    chunks = []
    for t in range(n_sub):
        for h in range(HGRN_HEADS):
            sl = slice(h * HGRN_HEAD_DIM, (h + 1) * HGRN_HEAD_DIM)
            for d, (reverse, q_ref, i_ref, f_ref, o_ref, s_ref) in enumerate(dirs):
                r0 = (n_sub - 1 - t if reverse else t) * HGRN_CHUNK
                rows = slice(r0, r0 + HGRN_CHUNK)

                def state(h=h, d=d, s_ref=s_ref):
                    return latest[h, d] if (h, d) in latest else s_ref[h]

                gen = _hgrn_chunk(q_ref[0, rows, sl], f_ref[0, rows, sl], i_ref[0, rows, sl],
                                  lb_ref[d:d + 1, sl], state, reverse, *masks[d])
                chunks.append((gen, o_ref, s_ref, h, d, rows, sl, t == n_sub - 1))

    assert len(dirs) * HGRN_HEADS >= _HGRN_STAGES
    for step in range(len(chunks) + _HGRN_STAGES - 1):
        for stage in range(_HGRN_STAGES):
            idx = step - stage
            if 0 <= idx < len(chunks):
                gen, o_ref, s_ref, h, d, rows, sl, is_last = chunks[idx]
                result = next(gen)
                if stage == _HGRN_STAGES - 1:
                    o, st = result
                    o_ref[0, rows, sl] = o
                    latest[h, d] = st
                    if is_last:
                        s_ref[h] = st


def _hgrn(qr, ir, ff, fb, lb, n_ctx):
    b, n_all, w = qr.shape
    c = HGRN_CHUNK * HGRN_STEP_CHUNKS
    assert n_ctx % c == 0 and n_all % c == 0
    nc = n_all // c
    ncc = n_ctx // c

    def fwd(bi, j):
        return (bi, j, 0)

    def bwd(bi, j):
        return (bi, jnp.where(j < ncc, ncc - 1 - j, nc - 1 - (j - ncc)), 0)

    blk = (1, c, w)
    state = pltpu.VMEM((HGRN_HEADS, HGRN_HEAD_DIM, HGRN_HEAD_DIM), F32)
    return pl.pallas_call(
        _hgrn_kernel,
        out_shape=[jax.ShapeDtypeStruct((b, n_all, w), F32)] * 2,
        grid=(b, nc),
        in_specs=[pl.BlockSpec(blk, fwd), pl.BlockSpec(blk, fwd), pl.BlockSpec(blk, fwd),
                  pl.BlockSpec(blk, bwd), pl.BlockSpec(blk, bwd), pl.BlockSpec(blk, bwd),
                  pl.BlockSpec((2, w), lambda bi, j: (0, 0))],
        out_specs=[pl.BlockSpec(blk, fwd), pl.BlockSpec(blk, bwd)],
        scratch_shapes=[state, state],
        compiler_params=_params("arbitrary", "arbitrary"),
        name="hgrn",
    )(qr, ir, ff, qr, ir, fb, lb)


_ROUTE_GROUP_LANE0 = N_EXPERTS


def _lane_min_index(cond, lane):
    return jnp.min(jnp.where(cond, lane, LANES), axis=-1, keepdims=True)


def _out_proj_kernel(oa_ref, of_ref, ob_ref, gr_ref, x_ref, gate_ref, shift_ref, scale_ref,
                     ag_ref, hg_ref, g2_ref, w_ref, ones_ref, wr_hi_ref, wr_lo_ref, br_ref,
                     x1_ref, h2_ref, ri_ref, rw_ref, cnt_ref, carry_ref):
    first = (pl.program_id(0) == 0) & (pl.program_id(1) == 0)

    @pl.when(first)
    def _():
        carry_ref[...] = jnp.zeros_like(carry_ref)

    tm = x_ref.shape[1]

    ones_bd = ones_ref[...]
    slabs = []
    for c0 in range(0, ATTN_WIDTH, LANES):
        t = oa_ref[0, :, c0:c0 + LANES].astype(F32)
        slabs.append((t * _head_rms_scale(t, ones_bd) * ag_ref[:, c0:c0 + LANES]).astype(BF16))
    orr = of_ref[0] + ob_ref[0]
    parts = []
    for h in range(HGRN_HEADS):
        sl = slice(h * HGRN_HEAD_DIM, (h + 1) * HGRN_HEAD_DIM)
        oh = orr[:, sl]
        parts.append(oh * lax.rsqrt(jnp.mean(oh * oh, axis=-1, keepdims=True) + EPS))
    orr = jnp.concatenate(parts, axis=1) * hg_ref[...] * gr_ref[0].astype(F32)
    mix_in = jnp.concatenate(slabs + [orr.astype(BF16)], axis=1)
    x1 = x_ref[0] + gate_ref[0] * _dot(mix_in, w_ref[...])
    x1_ref[0] = x1
    h2 = x1 * lax.rsqrt(jnp.mean(x1 * x1, axis=-1, keepdims=True) + EPS) * g2_ref[...]
    h2 = h2 * (1.0 + scale_ref[0]) + shift_ref[0]
    for c in range(h2_ref.shape[1]):
        h2_ref[:, c] = h2[:, c * LANES:(c + 1) * LANES].reshape(tm // SUBLANES, SUBLANES, LANES)

    h_hi, h_lo = _split_bf16(h2)
    logits = (_dot(h_hi, wr_hi_ref[...]) + _dot(h_lo, wr_hi_ref[...])
              + _dot(h_hi, wr_lo_ref[...]) + br_ref[...])
    lane = lax.broadcasted_iota(jnp.int32, logits.shape, 1)
    is_grp = (lane >= _ROUTE_GROUP_LANE0) & (lane < _ROUTE_GROUP_LANE0 + N_GROUPS)
    lg = jnp.where(is_grp, logits, NEG_BIG)
    mg = jnp.max(lg, axis=-1, keepdims=True)
    g_sel = _lane_min_index(lg == mg, lane) - _ROUTE_GROUP_LANE0
    pg_top = 1.0 / jnp.sum(jnp.exp(lg - mg), axis=-1, keepdims=True)
    in_grp = (lane < N_EXPERTS) & ((lane // EXPERTS_PER_GROUP) == g_sel)
    le = jnp.where(in_grp, logits, NEG_BIG)
    m1 = jnp.max(le, axis=-1, keepdims=True)
    e1 = _lane_min_index(le == m1, lane)
    le2 = jnp.where(lane == e1, NEG_BIG, le)
    m2 = jnp.max(le2, axis=-1, keepdims=True)
    e2 = _lane_min_index(le2 == m2, lane)
    r2 = jnp.exp(m2 - m1)
    w1 = pg_top / (1.0 + r2)
    w2 = pg_top * r2 / (1.0 + r2)

    onehot = ((lane == e1) | (lane == e2)).astype(BF16)
    rt = lax.broadcasted_iota(jnp.int32, (tm, tm), 0)
    ru = lax.broadcasted_iota(jnp.int32, (tm, tm), 1)
    before = _dot((ru < rt).astype(BF16), onehot) + carry_ref[0:1, :]
    rank1 = jnp.sum(jnp.where(lane == e1, before, 0.0), axis=-1, keepdims=True)
    rank2 = jnp.sum(jnp.where(lane == e2, before, 0.0), axis=-1, keepdims=True)
    total = carry_ref[0:1, :] + jnp.sum(onehot.astype(F32), axis=0, keepdims=True)
    carry_ref[...] = jnp.broadcast_to(total, carry_ref.shape)
    cnt_ref[...] = jnp.broadcast_to(total, cnt_ref.shape).astype(jnp.int32)

    rec = jnp.where(lane == 0, e1.astype(F32), jnp.where(lane == 1, e2.astype(F32), jnp.where(
        lane == 2, rank1, jnp.where(lane == 3, rank2, 0.0))))
    ri_ref[0] = rec.T[:ROUTE_LANES].astype(jnp.int32)
    rl = lax.broadcasted_iota(jnp.int32, (tm, ROUTE_LANES), 1)
    rw_ref[0] = jnp.where(rl == 0, w1, jnp.where(rl == 1, w2, 0.0))


def _out_proj(oa, o_f, o_b, gr, x, gate, shift, scale, ag, hg, g2, w_out, wr_hi, wr_lo, br, n_ctx):
    b, s, d = x.shape
    tm = TOKEN_TILE
    off = n_ctx // tm
    ones_bd = _head_sum_matrix()

    def lat(w):
        return pl.BlockSpec((1, tm, w), lambda bi, i: (bi, i, 0))

    def allrows(w):
        return pl.BlockSpec((1, tm, w), lambda bi, i: (bi, i + off, 0))

    def mod():
        return pl.BlockSpec((1, 1, d), lambda bi, i: (bi, 0, 0))

    def const(shape):
        return pl.BlockSpec(shape, lambda bi, i: (0,) * len(shape))

    return pl.pallas_call(
        _out_proj_kernel,
        out_shape=[jax.ShapeDtypeStruct((b, s, d), F32),
                   jax.ShapeDtypeStruct((b * s // SUBLANES, d // LANES, SUBLANES, LANES), F32),
                   jax.ShapeDtypeStruct((b * s // tm, ROUTE_LANES, tm), jnp.int32),
                   jax.ShapeDtypeStruct((b, s, ROUTE_LANES), F32),
                   jax.ShapeDtypeStruct((SUBLANES, LANES), jnp.int32)],
        grid=(b, s // tm),
        in_specs=[lat(ATTN_WIDTH), allrows(HGRN_WIDTH), allrows(HGRN_WIDTH), allrows(HGRN_WIDTH),
                  lat(d), mod(), mod(), mod(),
                  const((1, ATTN_WIDTH)), const((1, HGRN_WIDTH)), const((1, d)),
                  const((ATTN_WIDTH + HGRN_WIDTH, d)), const((LANES, LANES)),
                  const((d, LANES)), const((d, LANES)), const((1, LANES))],
        out_specs=[lat(d),
                   pl.BlockSpec((tm // SUBLANES, d // LANES, SUBLANES, LANES),
                                lambda bi, i: (bi * (s // tm) + i, 0, 0, 0)),
                   pl.BlockSpec((1, ROUTE_LANES, tm), lambda bi, i: (bi * (s // tm) + i, 0, 0)),
                   lat(ROUTE_LANES), const((SUBLANES, LANES))],
        scratch_shapes=[pltpu.VMEM((SUBLANES, LANES), F32)],
        compiler_params=_params("arbitrary", "arbitrary"),
        name="out_proj",
    )(oa, o_f, o_b, gr, x, gate, shift, scale, ag, hg, g2, w_out, ones_bd, wr_hi, wr_lo, br)


def _slab_copies(stage, hbm, row0, sem, to_hbm):
    copies = []
    for c in range(stage.shape[0]):
        view = hbm.at[pl.ds(row0, stage.shape[1]), c, :]
        src, dst = (stage.at[c], view) if to_hbm else (view, stage.at[c])
        copies.append(pltpu.make_async_copy(src, dst, sem))
    return copies


def _stage_store(stage, x):
    for c in range(stage.shape[0]):
        stage[c] = x[:, c * LANES:(c + 1) * LANES]


def _stage_load(stage):
    return jnp.concatenate([stage[c] for c in range(stage.shape[0])], axis=1)


_DISPATCH_SLOTS = 3


def _dispatch_kernel(dest_ref, pad_ref, h2_hbm, xs_hbm, buf, zero_row, sem_in, sem_out):
    i = pl.program_id(0)
    n = pl.num_programs(0)
    groups = buf.shape[1]
    tm = groups * SUBLANES
    n_pad = pad_ref.shape[2]
    slot = i % _DISPATCH_SLOTS

    def fetch(t, sl):
        return pltpu.make_async_copy(h2_hbm.at[pl.ds(t * groups, groups)], buf.at[sl],
                                     sem_in.at[sl])

    def wait_rows(sl):
        for _ in range(TOP_K + n_pad // tm):
            pltpu.make_async_copy(h2_hbm.at[pl.ds(0, groups)], buf.at[sl], sem_out.at[sl]).wait()
        rest = (n_pad % tm) // SUBLANES
        if rest:
            pltpu.make_async_copy(h2_hbm.at[pl.ds(0, rest)], buf.at[sl, pl.ds(0, rest)],
                                  sem_out.at[sl]).wait()

    @pl.when(i == 0)
    def _():
        zero_row[...] = jnp.zeros_like(zero_row)
        fetch(0, 0).start()

    @pl.when(i >= 2)
    def _():
        wait_rows((i + 1) % _DISPATCH_SLOTS)

    @pl.when(i + 1 < n)
    def _():
        fetch(i + 1, (i + 1) % _DISPATCH_SLOTS).start()

    fetch(i, slot).wait()

    def body(g, carry):
        rows = [[dest_ref[0, 0, k * tm + g * SUBLANES + u] for k in range(TOP_K)]
                for u in range(SUBLANES)]
        for u in range(SUBLANES):
            for k in range(TOP_K):
                pltpu.make_async_copy(buf.at[slot, g, :, u, :], xs_hbm.at[rows[u][k]],
                                      sem_out.at[slot]).start()
        return carry
    lax.fori_loop(0, groups, body, 0)

    def pad_body(g, carry):
        rows = [pad_ref[0, 0, g * SUBLANES + u] for u in range(SUBLANES)]
        for u in range(SUBLANES):
            pltpu.make_async_copy(zero_row, xs_hbm.at[rows[u]], sem_out.at[slot]).start()
        return carry
    lax.fori_loop(0, n_pad // SUBLANES, pad_body, 0)

    @pl.when(i == n - 1)
    def _():
        wait_rows(slot)

    @pl.when((i == n - 1) & (i >= 1))
    def _():
        wait_rows((i - 1) % _DISPATCH_SLOTS)


def _dispatch(dest_tiles, pad_tiles, h2_tiles, n_slots):
    nt = dest_tiles.shape[0]
    n_groups, n_slabs, _, _ = h2_tiles.shape
    groups = n_groups // nt
    assert pad_tiles.shape[2] % SUBLANES == 0

    def idx_spec(arr):
        return pl.BlockSpec((1, 1, arr.shape[2]), lambda i: (i, 0, 0), memory_space=pltpu.SMEM)

    return pl.pallas_call(
        _dispatch_kernel,
        out_shape=jax.ShapeDtypeStruct((n_slots, n_slabs, LANES), F32),
        grid=(nt,),
        in_specs=[idx_spec(dest_tiles), idx_spec(pad_tiles), pl.BlockSpec(memory_space=pl.ANY)],
        out_specs=pl.BlockSpec(memory_space=pl.ANY),
        scratch_shapes=[pltpu.VMEM((_DISPATCH_SLOTS, groups, n_slabs, SUBLANES, LANES), F32),
                        pltpu.VMEM((n_slabs, LANES), F32),
                        pltpu.SemaphoreType.DMA((_DISPATCH_SLOTS,)),
                        pltpu.SemaphoreType.DMA((_DISPATCH_SLOTS,))],
        compiler_params=_params("arbitrary"),
        name="dispatch",
    )(dest_tiles, pad_tiles, h2_tiles)


_EXPERT_IN_SLOTS = 3


def _experts_kernel(meta_ref, xs_hbm, wg_ref, wu_ref, wd_ref, ys_hbm,
                    xstage, ystage, wg_bf, wu_bf, wd_bf, sem_in, sem_out):
    j = pl.program_id(0)
    last = pl.num_programs(0) - 1
    n_used = meta_ref[0]
    slot = j % 2

    def fetch(blk, sl):
        for cp in _slab_copies(xstage.at[sl], xs_hbm, blk * MOE_BLOCK, sem_in.at[sl], False):
            cp.start()

    def wait_out(sl):
        for cp in _slab_copies(ystage.at[sl], ys_hbm, 0, sem_out.at[sl], True):
            cp.wait()

    in_slot = j % _EXPERT_IN_SLOTS

    @pl.when((j == 0) & (n_used > 0))
    def _():
        fetch(0, 0)

    @pl.when((j == 0) & (n_used > 1))
    def _():
        fetch(1, 1)

    @pl.when(j + 2 < n_used)
    def _():
        fetch(j + 2, (j + 2) % _EXPERT_IN_SLOTS)

    new_expert = (j == 0) | (meta_ref[1 + j] != meta_ref[jnp.maximum(j, 1)])

    @pl.when((j < n_used) & new_expert)
    def _():
        wg_bf[...] = wg_ref[0].astype(BF16)
        wu_bf[...] = wu_ref[0].astype(BF16)
        wd_bf[...] = wd_ref[0].astype(BF16)

    @pl.when(j >= 2)
    def _():
        wait_out(slot)

    @pl.when(j < n_used)
    def _():
        for cp in _slab_copies(xstage.at[in_slot], xs_hbm, 0, sem_in.at[in_slot], False):
            cp.wait()
        xb = _stage_load(xstage.at[in_slot]).astype(BF16)
        a = _silu(_dot(xb, wg_bf[...])) * _dot(xb, wu_bf[...])
        _stage_store(ystage.at[slot], _dot(a.astype(BF16), wd_bf[...]))

    @pl.when(j >= n_used)
    def _():
        ystage[slot] = jnp.zeros(ystage.shape[1:], F32)

    for cp in _slab_copies(ystage.at[slot], ys_hbm, j * MOE_BLOCK, sem_out.at[slot], True):
        cp.start()

    @pl.when(j == last)
    def _():
        wait_out(slot)

    @pl.when((j == last) & (j >= 1))
    def _():
        wait_out(1 - slot)


def _experts(meta, xs, wg, wu, wd):
    n_slots, n_slabs, _ = xs.shape
    n_blk = n_slots // MOE_BLOCK
    d = wg.shape[1]
    ff = wg.shape[2]

    def wspec(shape):
        return pl.BlockSpec((1,) + shape, lambda j, meta: (meta[1 + j], 0, 0))

    in_stage = pltpu.VMEM((_EXPERT_IN_SLOTS, n_slabs, MOE_BLOCK, LANES), F32)
    stage = pltpu.VMEM((2, n_slabs, MOE_BLOCK, LANES), F32)
    return pl.pallas_call(
        _experts_kernel,
        out_shape=jax.ShapeDtypeStruct(xs.shape, F32),
        grid_spec=pltpu.PrefetchScalarGridSpec(
            num_scalar_prefetch=1,
            grid=(n_blk,),
            in_specs=[pl.BlockSpec(memory_space=pl.ANY),
                      wspec((d, ff)), wspec((d, ff)), wspec((ff, d))],
            out_specs=pl.BlockSpec(memory_space=pl.ANY),
            scratch_shapes=[in_stage, stage, pltpu.VMEM((d, ff), BF16),
                            pltpu.VMEM((d, ff), BF16), pltpu.VMEM((ff, d), BF16),
                            pltpu.SemaphoreType.DMA((_EXPERT_IN_SLOTS,)),
                            pltpu.SemaphoreType.DMA((2,))],
        ),
        compiler_params=_params("arbitrary"),
        name="experts",
    )(meta, xs, wg, wu, wd)


def _combine_kernel(dest_cur_ref, dest_next_ref, ys_hbm, x1_ref, gate_ref, rw_ref, o_ref,
                    buf, sem):
    i = pl.program_id(0)
    slot = i % 2
    tm = x1_ref.shape[0]
    groups = tm // SUBLANES

    def start(idx_ref, sl):
        for k in range(TOP_K):
            def body(g, carry, k=k):
                rows = [idx_ref[0, 0, k * tm + g * SUBLANES + u] for u in range(SUBLANES)]
                for u in range(SUBLANES):
                    pltpu.make_async_copy(ys_hbm.at[rows[u]], buf.at[sl, k, g, :, u, :],
                                          sem.at[sl, k]).start()
                return carry
            lax.fori_loop(0, groups, body, 0)

    def wait(sl, k):
        for u in range(SUBLANES):
            pltpu.make_async_copy(ys_hbm.at[pl.ds(0, groups)], buf.at[sl, k, :, :, u, :],
                                  sem.at[sl, k]).wait()

    @pl.when(i == 0)
    def _():
        start(dest_cur_ref, 0)

    @pl.when(i + 1 < pl.num_programs(0))
    def _():
        start(dest_next_ref, 1 - slot)

    rw = rw_ref[...]
    moe = None
    for k in range(TOP_K):
        wait(slot, k)
        rows = jnp.concatenate([buf[slot, k, :, c].reshape(tm, LANES)
                                for c in range(buf.shape[3])], axis=1)
        term = rw[:, k:k + 1] * rows
        moe = term if moe is None else moe + term
    o_ref[...] = x1_ref[...] + gate_ref[0] * moe


def _combine(dest_tiles, ys, x1, gate, rw):
    n_tok, d = x1.shape
    tm = TOKEN_TILE
    nt = n_tok // tm
    per_batch = nt // gate.shape[0]
    idx_blk = (1, 1, TOP_K * tm)
    return pl.pallas_call(
        _combine_kernel,
        out_shape=jax.ShapeDtypeStruct((n_tok, d), F32),
        grid=(nt,),
        in_specs=[
            pl.BlockSpec(idx_blk, lambda i: (i, 0, 0), memory_space=pltpu.SMEM),
            pl.BlockSpec(idx_blk, lambda i: (jnp.minimum(i + 1, nt - 1), 0, 0),
                         memory_space=pltpu.SMEM),
            pl.BlockSpec(memory_space=pl.ANY),
            pl.BlockSpec((tm, d), lambda i: (i, 0)),
            pl.BlockSpec((1, 1, d), lambda i: (i // per_batch, 0, 0)),
            pl.BlockSpec((tm, ROUTE_LANES), lambda i: (i, 0)),
        ],
        out_specs=pl.BlockSpec((tm, d), lambda i: (i, 0)),
        scratch_shapes=[pltpu.VMEM((2, TOP_K, tm // SUBLANES, d // LANES, SUBLANES, LANES), F32),
                        pltpu.SemaphoreType.DMA((2, TOP_K))],
        compiler_params=_params("arbitrary"),
        name="combine",
    )(dest_tiles, dest_tiles, ys, x1, gate, rw)


def _layer(x, ctx, c, c_ctx, w_ada, b_ada, norm1_g, norm2_g, w_in, q_norm_g, k_norm_g, attn_out_g,
           lb, hgrn_out_g, w_out, w_router_grp, b_router_grp, w_router_exp, b_router_exp,
           w_exp_gate, w_exp_up, w_exp_down):
    b, s, d = x.shape
    n_ctx = ctx.shape[1]
    assert n_ctx % TOKEN_TILE == 0 and s % TOKEN_TILE == 0 and s % GRID_W == 0
    assert n_ctx % HGRN_CHUNK == 0 and (n_ctx + s) % ATTN_KV_TILE == 0
    n_all = n_ctx + s

    cond = jnp.zeros((2 * SUBLANES, d), F32).at[:b].set(c).at[b].set(c_ctx)
    assert b + 1 <= cond.shape[0]
    mods = _adaln(cond, w_ada, b_ada)[:b + 1].reshape(b + 1, 1, 6, d)
    sh1, sc1, gt1, sh2, sc2, gt2 = (mods[:, :, m] for m in range(6))

    scale_q = ATTN_HEAD_DIM ** -0.5 * np.log2(np.e)
    qkg = jnp.concatenate([jnp.tile(q_norm_g, ATTN_HEADS) * scale_q,
                           jnp.tile(k_norm_g, ATTN_KV_HEADS)]).reshape(1, _QK_WIDTH)
    cos, sin = _rope_tables(n_ctx, s)
    qa, ka, vt, qr, ff, fb, ir, gr = _in_proj(
        ctx, x, sh1, sc1, norm1_g.reshape(1, d), w_in.astype(BF16), qkg, cos, sin)
    oa = _attention(qa, ka, vt)
    o_f, o_b = _hgrn(qr, ir, ff, fb, lb, n_ctx)

    w_router = jnp.zeros((d, LANES), F32)
    w_router = w_router.at[:, :N_EXPERTS].set(w_router_exp)
    w_router = w_router.at[:, _ROUTE_GROUP_LANE0:_ROUTE_GROUP_LANE0 + N_GROUPS].set(w_router_grp)
    b_router = jnp.zeros((1, LANES), F32)
    b_router = b_router.at[0, :N_EXPERTS].set(b_router_exp)
    b_router = b_router.at[0, _ROUTE_GROUP_LANE0:_ROUTE_GROUP_LANE0 + N_GROUPS].set(b_router_grp)
    wr_hi, wr_lo = _split_bf16(w_router)
    x1, h2, ri, rw, counts = _out_proj(
        oa, o_f, o_b, gr, x, gt1[:b], sh2[:b], sc2[:b], attn_out_g.reshape(1, -1),
        hgrn_out_g.reshape(1, -1), norm2_g.reshape(1, d), w_out.astype(BF16), wr_hi, wr_lo,
        b_router, n_ctx)

    n_tok = b * s
    counts = counts[0, :N_EXPERTS]
    padded = (counts + MOE_BLOCK - 1) // MOE_BLOCK * MOE_BLOCK
    pend = jnp.cumsum(padded)
    pstart = pend - padded
    n_blk = n_tok * TOP_K // MOE_BLOCK + N_EXPERTS
    dest = pstart[ri[:, :TOP_K]] + ri[:, TOP_K:2 * TOP_K]
    dest_tiles = dest.reshape(dest.shape[0], 1, -1)
    blk_start = jnp.arange(n_blk, dtype=jnp.int32) * MOE_BLOCK
    blk_e = jnp.minimum(jnp.sum(blk_start[:, None] >= pend[None, :], axis=1), N_EXPERTS - 1)
    n_used = pend[-1] // MOE_BLOCK
    meta = jnp.concatenate([n_used[None], blk_e]).astype(jnp.int32)

    n_slots = n_blk * MOE_BLOCK
    n_pad = n_slots - n_tok * TOP_K
    gap_start = jnp.concatenate([pstart + counts, pend[-1:]])
    gap_size = jnp.concatenate([padded - counts, n_slots - pend[-1:]])
    gap_end = jnp.cumsum(gap_size)
    pad_i = jnp.arange(n_pad, dtype=jnp.int32)
    gap = jnp.sum(pad_i[:, None] >= gap_end[None, :], axis=1)
    pad_slots = gap_start[gap] + pad_i - (gap_end - gap_size)[gap]
    pad_tiles = pad_slots.astype(jnp.int32).reshape(dest_tiles.shape[0], 1, -1)

    xs = _dispatch(dest_tiles, pad_tiles, h2, n_slots)
    ys = _experts(meta, xs, w_exp_gate, w_exp_up, w_exp_down)
    out = _combine(dest_tiles, ys, x1.reshape(n_tok, d), gt2[:b], rw.reshape(n_tok, ROUTE_LANES))
    return out.reshape(b, s, d)


def kernel(x, c, ctx, c_ctx, w_ada, b_ada, norm1_g, norm2_g, w_in, q_norm_g, k_norm_g, attn_out_g,
           hgrn_lb, hgrn_out_g, w_out, w_router_grp, b_router_grp, w_router_exp, b_router_exp,
           w_exp_gate, w_exp_up, w_exp_down):
    depth = w_in.shape[0]
    assert depth == 1, "context stream update between layers is not implemented"
    lb_all = jnp.cumsum(jax.nn.softmax(hgrn_lb.astype(F32), axis=1), axis=1)
    layer = 0
    return _layer(x, ctx, c, c_ctx, w_ada[layer], b_ada[layer], norm1_g[layer], norm2_g[layer],
                  w_in[layer], q_norm_g[layer], k_norm_g[layer], attn_out_g[layer],
                  lb_all[:, layer], hgrn_out_g[layer], w_out[layer], w_router_grp[layer],
                  b_router_grp[layer], w_router_exp[layer], b_router_exp[layer],
                  w_exp_gate[layer], w_exp_up[layer], w_exp_down[layer])
```

```python
import functools

import numpy as np
import jax
import jax.numpy as jnp
from jax import lax
from jax.experimental import pallas as pl
from jax.experimental.pallas import tpu as pltpu

F32 = jnp.float32
BF16 = jnp.bfloat16

GRID_W = 64
EPS = 1e-6
ATTN_HEADS = 8
ATTN_KV_HEADS = 2
ATTN_HEAD_DIM = 64
ATTN_GROUP = ATTN_HEADS // ATTN_KV_HEADS
ATTN_WIDTH = ATTN_HEADS * ATTN_HEAD_DIM
KV_WIDTH = ATTN_KV_HEADS * ATTN_HEAD_DIM
ROPE_THETA = 10000.0
HGRN_HEADS = 4
HGRN_HEAD_DIM = 128
HGRN_WIDTH = HGRN_HEADS * HGRN_HEAD_DIM
N_GROUPS = 4
EXPERTS_PER_GROUP = 8
N_EXPERTS = N_GROUPS * EXPERTS_PER_GROUP
TOP_K = 2
EXPERT_FF = 512

LANES = 128
SUBLANES = 8
MXU_DIM = 256
VMEM_LIMIT_BYTES = 48 * 1024 * 1024

TOKEN_TILE = 256
OUT_PROJ_SPLIT = 2
ATTN_Q_TILE = 1024
ATTN_KV_TILE = 256
ATTN_KV_UNROLL = 17
ATTN_BOUND_SLACK = 1.02
ATTN_BOUND_MAX = 60.0
HGRN_CHUNK = 64
HGRN_STEP_CHUNKS = 4
MOE_BLOCK = 256
GATHER_UNROLL = 8
ROUTE_LANES = 8
NEG_BIG = -1e30

_QA0 = 0
_KA0 = _QA0 + ATTN_WIDTH
_VA0 = _KA0 + KV_WIDTH
_QR0 = _VA0 + KV_WIDTH
_FF0 = _QR0 + HGRN_WIDTH
_FB0 = _FF0 + HGRN_WIDTH
_IR0 = _FB0 + HGRN_WIDTH
_GR0 = _IR0 + HGRN_WIDTH
_QK_WIDTH = ATTN_WIDTH + KV_WIDTH


def _dot(a, b):
    return jnp.dot(a, b, preferred_element_type=F32)


def _dot_nt(a, b):
    return lax.dot_general(a, b, (((1,), (1,)), ((), ())), preferred_element_type=F32)


def _dot_tn(a, b):
    return lax.dot_general(a, b, (((0,), (0,)), ((), ())), preferred_element_type=F32)


def _split_bf16(x):
    hi = x.astype(BF16)
    lo = (x - hi.astype(F32)).astype(BF16)
    return hi, lo


def _sigmoid(x):
    return 1.0 / (1.0 + jnp.exp(-x))


def _silu(x):
    return x * _sigmoid(x)


def _params(*sem):
    return pltpu.CompilerParams(dimension_semantics=sem, vmem_limit_bytes=VMEM_LIMIT_BYTES)


def _head_sum_matrix():
    idx = np.arange(LANES) // ATTN_HEAD_DIM
    return jnp.asarray(idx[:, None] == idx[None, :], dtype=BF16)


def _head_rms_scale(x, ones_bd):
    ssq = _dot((x * x).astype(BF16), ones_bd)
    return lax.rsqrt(ssq * (1.0 / ATTN_HEAD_DIM) + EPS)


def _adaln_kernel(cond_ref, w_ref, b_ref, o_ref):
    s = _silu(cond_ref[...])
    s_hi, s_lo = _split_bf16(s)
    w_hi, w_lo = _split_bf16(w_ref[...])
    o_ref[...] = _dot(s_hi, w_hi) + _dot(s_lo, w_hi) + _dot(s_hi, w_lo) + b_ref[...]


def _adaln(cond, w_ada, b_ada):
    rows, d = cond.shape
    n = w_ada.shape[1]
    tn = n // 6
    return pl.pallas_call(
        _adaln_kernel,
        out_shape=jax.ShapeDtypeStruct((rows, n), F32),
        grid=(n // tn,),
        in_specs=[pl.BlockSpec((rows, d), lambda j: (0, 0)),
                  pl.BlockSpec((d, tn), lambda j: (0, j)),
                  pl.BlockSpec((1, tn), lambda j: (0, j))],
        out_specs=pl.BlockSpec((rows, tn), lambda j: (0, j)),
        compiler_params=_params("arbitrary"),
        name="adaln",
    )(cond, w_ada, b_ada.reshape(1, n))


def _rope_tables(n_ctx, n_lat):
    half = ATTN_HEAD_DIM // 2
    freqs = ROPE_THETA ** (-np.arange(0, half, 2, dtype=np.float64) / half)
    tok = np.arange(n_lat)
    pos = np.stack([tok // GRID_W, tok % GRID_W], axis=1).astype(np.float64)
    lane = np.arange(ATTN_HEAD_DIM)
    axis = lane // half
    fi = (lane % half) // 2
    ang = pos[:, axis] * freqs[fi][None, :]
    sign = np.where(lane % 2 == 1, 1.0, -1.0)
    cos = np.concatenate([np.ones((n_ctx, ATTN_HEAD_DIM)), np.cos(ang)], axis=0)
    sin = np.concatenate([np.zeros((n_ctx, ATTN_HEAD_DIM)), np.sin(ang) * sign], axis=0)
    reps = LANES // ATTN_HEAD_DIM
    return (jnp.asarray(np.tile(cos, (1, reps)), F32), jnp.asarray(np.tile(sin, (1, reps)), F32))


def _in_proj_kernel(n_ctx_tiles, ctx_ref, x_ref, shift_ref, scale_ref, g1_ref, w_ref, qkg_ref,
                    ones_ref, cos_ref, sin_ref,
                    qa_ref, ka_ref, vt_ref, qr_ref, ff_ref, fb_ref, ir_ref, gr_ref):
    i = pl.program_id(1)
    xt = jnp.where(i < n_ctx_tiles, ctx_ref[0], x_ref[0])
    ms = jnp.mean(xt * xt, axis=-1, keepdims=True)
    h = xt * lax.rsqrt(ms + EPS) * g1_ref[...]
    h = h * (1.0 + scale_ref[0]) + shift_ref[0]
    p = _dot(h.astype(BF16), w_ref[...])

    ones_bd = ones_ref[...]
    cos = cos_ref[...]
    sin = sin_ref[...]
    even = lax.broadcasted_iota(jnp.int32, cos.shape, 1) % 2 == 0
    slabs = []
    for c0 in range(0, _QK_WIDTH, LANES):
        t = p[:, _QA0 + c0:_QA0 + c0 + LANES]
        t = t * _head_rms_scale(t, ones_bd) * qkg_ref[:, c0:c0 + LANES]
        partner = jnp.where(even, pltpu.roll(t, LANES - 1, 1), pltpu.roll(t, 1, 1))
        slabs.append((t * cos + partner * sin).astype(BF16))
    qa_ref[0] = jnp.concatenate(slabs[:ATTN_WIDTH // LANES], axis=1)
    k_all = jnp.concatenate(slabs[ATTN_WIDTH // LANES:], axis=1)
    vt_all = p[:, _VA0:_VA0 + KV_WIDTH].T
    ones = jnp.ones((vt_ref.shape[2] - ATTN_HEAD_DIM, vt_all.shape[1]), F32)
    for hd in range(ATTN_KV_HEADS):
        cols = slice(hd * ATTN_HEAD_DIM, (hd + 1) * ATTN_HEAD_DIM)
        ka_ref[0, hd] = k_all[:, cols]
        vt_ref[0, hd] = jnp.concatenate([vt_all[cols], ones], axis=0).astype(BF16)

    qr_ref[0] = (_silu(p[:, _QR0:_QR0 + HGRN_WIDTH]) * (HGRN_HEAD_DIM ** -0.5)).astype(BF16)
    ff_ref[0] = p[:, _FF0:_FF0 + HGRN_WIDTH]
    fb_ref[0] = p[:, _FB0:_FB0 + HGRN_WIDTH]
    ir_ref[0] = p[:, _IR0:_IR0 + HGRN_WIDTH].astype(BF16)
    gr_ref[0] = _silu(p[:, _GR0:_GR0 + HGRN_WIDTH]).astype(BF16)


def _in_proj(ctx, x, shift, scale, g1, w_in, qkg, cos, sin):
    b, n_ctx, d = ctx.shape
    s = x.shape[1]
    tm = TOKEN_TILE
    nct = n_ctx // tm
    n_all = n_ctx + s
    nt = n_all // tm
    pw = w_in.shape[1]
    ones_bd = _head_sum_matrix()

    def tok_spec(w):
        return pl.BlockSpec((1, tm, w), lambda bi, i: (bi, i, 0))

    mod_spec = pl.BlockSpec((1, 1, d), lambda bi, i: (jnp.where(i < nct, b, bi), 0, 0))
    outs = [(HGRN_WIDTH, BF16), (HGRN_WIDTH, F32), (HGRN_WIDTH, F32), (HGRN_WIDTH, BF16),
            (HGRN_WIDTH, BF16)]
    vt_rows = ATTN_HEAD_DIM + 2 * SUBLANES
    lat_spec = pl.BlockSpec((1, tm, ATTN_WIDTH), lambda bi, i: (bi, jnp.maximum(i - nct, 0), 0))
    k_spec = pl.BlockSpec((1, ATTN_KV_HEADS, tm, ATTN_HEAD_DIM), lambda bi, i: (bi, 0, i, 0))
    vt_spec = pl.BlockSpec((1, ATTN_KV_HEADS, vt_rows, tm), lambda bi, i: (bi, 0, 0, i))
    return pl.pallas_call(
        functools.partial(_in_proj_kernel, nct),
        out_shape=[jax.ShapeDtypeStruct((b, s, ATTN_WIDTH), BF16),
                   jax.ShapeDtypeStruct((b, ATTN_KV_HEADS, n_all, ATTN_HEAD_DIM), BF16),
                   jax.ShapeDtypeStruct((b, ATTN_KV_HEADS, vt_rows, n_all), BF16)]
        + [jax.ShapeDtypeStruct((b, n_all, w), dt) for w, dt in outs],
        grid=(b, nt),
        in_specs=[
            pl.BlockSpec((1, tm, d), lambda bi, i: (bi, jnp.minimum(i, nct - 1), 0)),
            pl.BlockSpec((1, tm, d), lambda bi, i: (bi, jnp.maximum(i - nct, 0), 0)),
            mod_spec, mod_spec,
            pl.BlockSpec((1, d), lambda bi, i: (0, 0)),
            pl.BlockSpec((d, pw), lambda bi, i: (0, 0)),
            pl.BlockSpec((1, _QK_WIDTH), lambda bi, i: (0, 0)),
            pl.BlockSpec((LANES, LANES), lambda bi, i: (0, 0)),
            pl.BlockSpec((tm, LANES), lambda bi, i: (i, 0)),
            pl.BlockSpec((tm, LANES), lambda bi, i: (i, 0)),
        ],
        out_specs=[lat_spec, k_spec, vt_spec] + [tok_spec(w) for w, _ in outs],
        compiler_params=_params("arbitrary", "arbitrary"),
        name="in_proj",
    )(ctx, x, shift, scale, g1, w_in, qkg, ones_bd, cos, sin)


def _attention_kernel(n_kv_tiles, q_ref, k_ref, vt_ref, o_ref, qs_ref, s_ref, ksq_ref):
    tq = q_ref.shape[1]
    cols = ATTN_GROUP * tq
    for h in range(ATTN_GROUP):
        qs_ref[h * tq:(h + 1) * tq, :] = q_ref[0, :, h * ATTN_HEAD_DIM:(h + 1) * ATTN_HEAD_DIM]

    @pl.when(pl.program_id(2) == 0)
    def _():
        k = k_ref[0, 0].astype(F32)
        ksq = jnp.max(jnp.sum(k * k, axis=1, keepdims=True), axis=0, keepdims=True)
        ksq_ref[...] = jnp.broadcast_to(ksq, ksq_ref.shape)

    def tile(j):
        return pl.ds(pl.multiple_of(j * ATTN_KV_TILE, ATTN_KV_TILE), ATTN_KV_TILE)

    def scores(j):
        return _dot_nt(k_ref[0, 0, tile(j), :], qs_ref[...])

    def weighted(j, p):
        return _dot(vt_ref[0, 0, :, tile(j)], p)

    def finish(acc):
        o = acc[:ATTN_HEAD_DIM] / acc[ATTN_HEAD_DIM:ATTN_HEAD_DIM + 1]
        o_ref[0] = jnp.concatenate(
            [o[:, h * tq:(h + 1) * tq].T for h in range(ATTN_GROUP)], axis=1).astype(o_ref.dtype)

    qf = qs_ref[...].astype(F32)
    qsq = _dot_nt(jnp.ones((SUBLANES, ATTN_HEAD_DIM), BF16), (qf * qf).astype(BF16))[0:1]
    bound = jnp.sqrt(qsq * ksq_ref[0:1, 0:1]) * ATTN_BOUND_SLACK
    acc0 = jnp.zeros((vt_ref.shape[2], cols), F32)
    safe = jnp.max(bound) <= ATTN_BOUND_MAX

    @pl.when(safe)
    def _():
        def absorb(j, acc):
            return acc + weighted(j, jnp.exp2(scores(j) - bound).astype(BF16))

        def group(i, acc):
            for u in range(ATTN_KV_UNROLL):
                acc = absorb(ATTN_KV_UNROLL * i + u, acc)
            return acc

        n_groups = n_kv_tiles // ATTN_KV_UNROLL
        acc = lax.fori_loop(0, n_groups, group, acc0)
        for j in range(n_groups * ATTN_KV_UNROLL, n_kv_tiles):
            acc = absorb(j, acc)
        finish(acc)

    @pl.when(jnp.logical_not(safe))
    def _():
        def score(j, slot):
            s_ref[slot] = scores(j)

        def absorb(j, slot, carry):
            m, acc = carry
            s = s_ref[slot]
            m_new = jnp.maximum(m, jnp.max(s, axis=0, keepdims=True))
            p = jnp.exp2(s - m_new).astype(BF16)
            return m_new, jnp.exp2(m - m_new) * acc + weighted(j, p)

        def pair(i, carry):
            score(2 * i + 1, 1)
            carry = absorb(2 * i, 0, carry)
            score(2 * i + 2, 0)
            return absorb(2 * i + 1, 1, carry)

        n_pairs = (n_kv_tiles - 1) // 2
        score(0, 0)
        carry = lax.fori_loop(0, n_pairs, pair, (jnp.full((1, cols), NEG_BIG, F32), acc0))
        if n_kv_tiles % 2 == 1:
            carry = absorb(n_kv_tiles - 1, 0, carry)
        else:
            score(n_kv_tiles - 1, 1)
            carry = absorb(n_kv_tiles - 2, 0, carry)
            carry = absorb(n_kv_tiles - 1, 1, carry)
        finish(carry[1])


def _attention(qa, k_heads, vt_heads):
    b, s, _ = qa.shape
    n_all = k_heads.shape[2]
    tq = ATTN_Q_TILE
    gw = ATTN_GROUP * ATTN_HEAD_DIM
    vt_rows = vt_heads.shape[2]
    return pl.pallas_call(
        functools.partial(_attention_kernel, n_all // ATTN_KV_TILE),
        out_shape=jax.ShapeDtypeStruct((b, s, ATTN_WIDTH), BF16),
        grid=(b, ATTN_KV_HEADS, s // tq),
        in_specs=[pl.BlockSpec((1, tq, gw), lambda bi, kv, i: (bi, i, kv)),
                  pl.BlockSpec((1, 1, n_all, ATTN_HEAD_DIM), lambda bi, kv, i: (bi, kv, 0, 0)),
                  pl.BlockSpec((1, 1, vt_rows, n_all), lambda bi, kv, i: (bi, kv, 0, 0))],
        out_specs=pl.BlockSpec((1, tq, gw), lambda bi, kv, i: (bi, i, kv)),
        scratch_shapes=[pltpu.VMEM((ATTN_GROUP * tq, ATTN_HEAD_DIM), BF16),
                        pltpu.VMEM((2, ATTN_KV_TILE, ATTN_GROUP * tq), F32),
                        pltpu.VMEM((SUBLANES, LANES), F32)],
        compiler_params=_params("arbitrary", "arbitrary", "arbitrary"),
        name="attention",
    )(qa, k_heads, vt_heads)


def _hgrn_masks(reverse):
    c = HGRN_CHUNK
    t = lax.broadcasted_iota(jnp.int32, (c, c), 0)
    u = lax.broadcasted_iota(jnp.int32, (c, c), 1)
    tri = (u >= t) if reverse else (u <= t)
    levels = []
    size = c // 2
    while size >= 1:
        same_parent = (t // (2 * size)) == (u // (2 * size))
        levels.append((size, same_parent if 2 * size < c else None))
        size //= 2
    levels.append((0, t == u))
    return tri.astype(BF16), levels


def _hgrn_chunk(q, fr, v, lb, state, reverse, tri, levels):
    c = HGRN_CHUNK
    coarse_levels = [lv for lv in levels if lv[0] >= SUBLANES]
    fine_levels = [lv for lv in levels if lv[0] < SUBLANES]
    f = lb + (1.0 - lb) * _sigmoid(fr)
    k = 1.0 - f
    g_hi, g_lo = _split_bf16(jnp.log2(f))
    bcum = _dot(tri, g_hi) + _dot(tri, g_lo)
    qf = q.astype(F32)
    yield None

    st = state()
    end = 0 if reverse else c - 1
    b_end = bcum[end:end + 1, :]
    inter = _dot_nt((qf * jnp.exp2(bcum)).astype(BF16), st.astype(BF16))
    ke = (k * jnp.exp2(b_end - bcum)).astype(BF16)
    st_add = _dot_tn(v, ke)

    products = []
    for size, mask in coarse_levels:
        q_rows, k_rows = [], []
        zeros = jnp.zeros((size, qf.shape[1]), BF16)
        for p0 in range(0, c, 2 * size):
            early = slice(p0, p0 + size)
            late = slice(p0 + size, p0 + 2 * size)
            r = p0 + size if reverse else p0 + size - 1
            ref = bcum[r:r + 1, :]
            q_sl, k_sl = (early, late) if reverse else (late, early)
            q_blk = (qf[q_sl] * jnp.exp2(bcum[q_sl] - ref)).astype(BF16)
            k_blk = (k[k_sl] * jnp.exp2(ref - bcum[k_sl])).astype(BF16)
            q_rows += [q_blk, zeros] if reverse else [zeros, q_blk]
            k_rows += [zeros, k_blk] if reverse else [k_blk, zeros]
        products.append((mask, _dot_nt(jnp.concatenate(q_rows, axis=0),
                                       jnp.concatenate(k_rows, axis=0))))
    yield None

    row = lax.broadcasted_iota(jnp.int32, (c, 1), 0)
    sub = lax.broadcasted_iota(jnp.int32, (SUBLANES, 1), 0)
    fine_products = []
    for size, mask in fine_levels:
        if size == 0:
            q_l, k_l = q, k.astype(BF16)
        else:
            late = (row // size) % 2 == 1
            q_side = jnp.logical_not(late) if reverse else late
            if size == 1:
                fac_q, fac_k = f, None
            else:
                groups = []
                for r0 in range(0, c, SUBLANES):
                    ref = None
                    for p0 in range(0, SUBLANES, 2 * size):
                        r = r0 + p0 + (size if reverse else size - 1)
                        piece = jnp.broadcast_to(bcum[r:r + 1, :], (SUBLANES, bcum.shape[1]))
                        ref = piece if ref is None else jnp.where(sub >= p0, piece, ref)
                    groups.append(ref)
                ref = jnp.concatenate(groups, axis=0)
                fac_q = fac_k = jnp.exp2(jnp.where(q_side, bcum - ref, ref - bcum))
            q_l = jnp.where(q_side, qf * fac_q, 0.0).astype(BF16)
            k_l = jnp.where(q_side, 0.0, k if fac_k is None else k * fac_k).astype(BF16)
        fine_products.append((mask, _dot_nt(q_l, k_l)))
    a = None
    for mask, a_l in products:
        if mask is not None:
            a_l = jnp.where(mask, a_l, 0.0)
        a = a_l if a is None else a + a_l
    yield None

    for mask, a_l in fine_products:
        a = a + jnp.where(mask, a_l, 0.0)
    intra = _dot(a.astype(BF16), v)
    st_new = st * jnp.exp2(b_end) + st_add
    yield None

    yield inter + intra, st_new


_HGRN_STAGES = 5


def _hgrn_kernel(qf_ref, if_ref, ff_ref, qb_ref, ib_ref, fb_ref, lb_ref, of_ref, ob_ref,
                 sf_ref, sb_ref):
    @pl.when(pl.program_id(1) == 0)
    def _():
        sf_ref[...] = jnp.zeros_like(sf_ref)
        sb_ref[...] = jnp.zeros_like(sb_ref)

    dirs = ((False, qf_ref, if_ref, ff_ref, of_ref, sf_ref),
            (True, qb_ref, ib_ref, fb_ref, ob_ref, sb_ref))
    masks = [_hgrn_masks(reverse) for reverse, *_ in dirs]
    n_sub = qf_ref.shape[1] // HGRN_CHUNK
    latest = {}
    chunks = []
    for t in range(n_sub):
        for h in range(HGRN_HEADS):
            sl = slice(h * HGRN_HEAD_DIM, (h + 1) * HGRN_HEAD_DIM)
            for d, (reverse, q_ref, i_ref, f_ref, o_ref, s_ref) in enumerate(dirs):
                r0 = (n_sub - 1 - t if reverse else t) * HGRN_CHUNK
                rows = slice(r0, r0 + HGRN_CHUNK)

                def state(h=h, d=d, s_ref=s_ref):
                    return latest[h, d] if (h, d) in latest else s_ref[h]

                gen = _hgrn_chunk(q_ref[0, rows, sl], f_ref[0, rows, sl], i_ref[0, rows, sl],
                                  lb_ref[d:d + 1, sl], state, reverse, *masks[d])
                chunks.append((gen, o_ref, s_ref, h, d, rows, sl, t == n_sub - 1))

    assert len(dirs) * HGRN_HEADS >= _HGRN_STAGES
    for step in range(len(chunks) + _HGRN_STAGES - 1):
        for stage in range(_HGRN_STAGES):
            idx = step - stage
            if 0 <= idx < len(chunks):
                gen, o_ref, s_ref, h, d, rows, sl, is_last = chunks[idx]
                result = next(gen)
                if stage == _HGRN_STAGES - 1:
                    o, st = result
                    o_ref[0, rows, sl] = o
                    latest[h, d] = st
                    if is_last:
                        s_ref[h] = st


def _hgrn(qr, ir, ff, fb, lb, n_ctx):
    b, n_all, w = qr.shape
    c = HGRN_CHUNK * HGRN_STEP_CHUNKS
    assert n_ctx % c == 0 and n_all % c == 0
    nc = n_all // c
    ncc = n_ctx // c

    def fwd(bi, j):
        return (bi, j, 0)

    def bwd(bi, j):
        return (bi, jnp.where(j < ncc, ncc - 1 - j, nc - 1 - (j - ncc)), 0)

    blk = (1, c, w)
    state = pltpu.VMEM((HGRN_HEADS, HGRN_HEAD_DIM, HGRN_HEAD_DIM), F32)
    return pl.pallas_call(
        _hgrn_kernel,
        out_shape=[jax.ShapeDtypeStruct((b, n_all, w), F32)] * 2,
        grid=(b, nc),
        in_specs=[pl.BlockSpec(blk, fwd), pl.BlockSpec(blk, fwd), pl.BlockSpec(blk, fwd),
                  pl.BlockSpec(blk, bwd), pl.BlockSpec(blk, bwd), pl.BlockSpec(blk, bwd),
                  pl.BlockSpec((2, w), lambda bi, j: (0, 0))],
        out_specs=[pl.BlockSpec(blk, fwd), pl.BlockSpec(blk, bwd)],
        scratch_shapes=[state, state],
        compiler_params=_params("arbitrary", "arbitrary"),
        name="hgrn",
    )(qr, ir, ff, qr, ir, fb, lb)


_ROUTE_GROUP_LANE0 = N_EXPERTS


def _lane_min_index(cond, lane):
    return jnp.min(jnp.where(cond, lane, LANES), axis=-1, keepdims=True)


def _out_proj_kernel(oa_ref, of_ref, ob_ref, gr_ref, x_ref, gate_ref, shift_ref, scale_ref,
                     ag_ref, hg_ref, g2_ref, w_ref, ones_ref, wr_hi_ref, wr_lo_ref, br_ref,
                     x1_ref, h2_ref, ri_ref, rw_ref, cnt_ref, carry_ref):
    first = (pl.program_id(0) == 0) & (pl.program_id(1) == 0)

    @pl.when(first)
    def _():
        carry_ref[...] = jnp.zeros_like(carry_ref)

    tm = x_ref.shape[1]
    hm = tm // OUT_PROJ_SPLIT
    ones_bd = ones_ref[...]
    counts = [carry_ref[0:1, :]]

    def rows_chain(r0):
        rs = slice(r0, r0 + hm)
        ts = [oa_ref[0, rs, c0:c0 + LANES].astype(F32) for c0 in range(0, ATTN_WIDTH, LANES)]
        scales = [_head_rms_scale(t, ones_bd) for t in ts]
        orr = of_ref[0, rs] + ob_ref[0, rs]
        yield

        slabs = [(t * sc * ag_ref[:, c * LANES:(c + 1) * LANES]).astype(BF16)
                 for c, (t, sc) in enumerate(zip(ts, scales))]
        parts = []
        for h in range(HGRN_HEADS):
            oh = orr[:, h * HGRN_HEAD_DIM:(h + 1) * HGRN_HEAD_DIM]
            parts.append(oh * lax.rsqrt(jnp.mean(oh * oh, axis=-1, keepdims=True) + EPS))
        orn = jnp.concatenate(parts, axis=1) * hg_ref[...] * gr_ref[0, rs].astype(F32)
        mix = _dot(jnp.concatenate(slabs + [orn.astype(BF16)], axis=1), w_ref[...])
        yield

        x1 = x_ref[0, rs] + gate_ref[0] * mix
        x1_ref[0, rs] = x1
        h2 = x1 * lax.rsqrt(jnp.mean(x1 * x1, axis=-1, keepdims=True) + EPS) * g2_ref[...]
        h2 = h2 * (1.0 + scale_ref[0]) + shift_ref[0]
        gs = slice(r0 // SUBLANES, (r0 + hm) // SUBLANES)
        for c in range(h2_ref.shape[1]):
            h2_ref[gs, c] = h2[:, c * LANES:(c + 1) * LANES].reshape(hm // SUBLANES, SUBLANES,
                                                                     LANES)
        h_hi, h_lo = _split_bf16(h2)
        logits = (_dot(h_hi, wr_hi_ref[...]) + _dot(h_lo, wr_hi_ref[...])
                  + _dot(h_hi, wr_lo_ref[...]) + br_ref[...])
        yield

        lane = lax.broadcasted_iota(jnp.int32, logits.shape, 1)
        is_grp = (lane >= _ROUTE_GROUP_LANE0) & (lane < _ROUTE_GROUP_LANE0 + N_GROUPS)
        lg = jnp.where(is_grp, logits, NEG_BIG)
        mg = jnp.max(lg, axis=-1, keepdims=True)
        g_sel = _lane_min_index(lg == mg, lane) - _ROUTE_GROUP_LANE0
        pg_top = 1.0 / jnp.sum(jnp.exp(lg - mg), axis=-1, keepdims=True)
        in_grp = (lane < N_EXPERTS) & ((lane // EXPERTS_PER_GROUP) == g_sel)
        le = jnp.where(in_grp, logits, NEG_BIG)
        m1 = jnp.max(le, axis=-1, keepdims=True)
        e1 = _lane_min_index(le == m1, lane)
        le2 = jnp.where(lane == e1, NEG_BIG, le)
        m2 = jnp.max(le2, axis=-1, keepdims=True)
        e2 = _lane_min_index(le2 == m2, lane)
        r2 = jnp.exp(m2 - m1)
        w1 = pg_top / (1.0 + r2)
        w2 = pg_top * r2 / (1.0 + r2)
        onehot = ((lane == e1) | (lane == e2)).astype(BF16)
        rt = lax.broadcasted_iota(jnp.int32, (hm, hm), 0)
        ru = lax.broadcasted_iota(jnp.int32, (hm, hm), 1)
        before_local = _dot((ru < rt).astype(BF16), onehot)
        yield

        before = before_local + counts[-1]
        rank1 = jnp.sum(jnp.where(lane == e1, before, 0.0), axis=-1, keepdims=True)
        rank2 = jnp.sum(jnp.where(lane == e2, before, 0.0), axis=-1, keepdims=True)
        counts.append(counts[-1] + jnp.sum(onehot.astype(F32), axis=0, keepdims=True))
        rec = jnp.where(lane == 0, e1.astype(F32), jnp.where(lane == 1, e2.astype(F32), jnp.where(
            lane == 2, rank1, jnp.where(lane == 3, rank2, 0.0))))
        ri_ref[0, :, rs] = rec.T[:ROUTE_LANES].astype(jnp.int32)
        rl = lax.broadcasted_iota(jnp.int32, (hm, ROUTE_LANES), 1)
        rw_ref[0, rs] = jnp.where(rl == 0, w1, jnp.where(rl == 1, w2, 0.0))
        yield

    n_stages = 5
    chains = [rows_chain(r0) for r0 in range(0, tm, hm)]
    for step in range(len(chains) + n_stages - 1):
        for stage in range(n_stages):
            idx = step - stage
            if 0 <= idx < len(chains):
                next(chains[idx])
    carry_ref[...] = jnp.broadcast_to(counts[-1], carry_ref.shape)
    cnt_ref[...] = jnp.broadcast_to(counts[-1], cnt_ref.shape).astype(jnp.int32)


def _out_proj(oa, o_f, o_b, gr, x, gate, shift, scale, ag, hg, g2, w_out, wr_hi, wr_lo, br, n_ctx):
    b, s, d = x.shape
    tm = TOKEN_TILE
    off = n_ctx // tm
    ones_bd = _head_sum_matrix()

    def lat(w):
        return pl.BlockSpec((1, tm, w), lambda bi, i: (bi, i, 0))

    def allrows(w):
        return pl.BlockSpec((1, tm, w), lambda bi, i: (bi, i + off, 0))

    def mod():
        return pl.BlockSpec((1, 1, d), lambda bi, i: (bi, 0, 0))

    def const(shape):
        return pl.BlockSpec(shape, lambda bi, i: (0,) * len(shape))

    return pl.pallas_call(
        _out_proj_kernel,
        out_shape=[jax.ShapeDtypeStruct((b, s, d), F32),
                   jax.ShapeDtypeStruct((b * s // SUBLANES, d // LANES, SUBLANES, LANES), F32),
                   jax.ShapeDtypeStruct((b * s // tm, ROUTE_LANES, tm), jnp.int32),
                   jax.ShapeDtypeStruct((b, s, ROUTE_LANES), F32),
                   jax.ShapeDtypeStruct((SUBLANES, LANES), jnp.int32)],
        grid=(b, s // tm),
        in_specs=[lat(ATTN_WIDTH), allrows(HGRN_WIDTH), allrows(HGRN_WIDTH), allrows(HGRN_WIDTH),
                  lat(d), mod(), mod(), mod(),
                  const((1, ATTN_WIDTH)), const((1, HGRN_WIDTH)), const((1, d)),
                  const((ATTN_WIDTH + HGRN_WIDTH, d)), const((LANES, LANES)),
                  const((d, LANES)), const((d, LANES)), const((1, LANES))],
        out_specs=[lat(d),
                   pl.BlockSpec((tm // SUBLANES, d // LANES, SUBLANES, LANES),
                                lambda bi, i: (bi * (s // tm) + i, 0, 0, 0)),
                   pl.BlockSpec((1, ROUTE_LANES, tm), lambda bi, i: (bi * (s // tm) + i, 0, 0)),
                   lat(ROUTE_LANES), const((SUBLANES, LANES))],
        scratch_shapes=[pltpu.VMEM((SUBLANES, LANES), F32)],
        compiler_params=_params("arbitrary", "arbitrary"),
        name="out_proj",
    )(oa, o_f, o_b, gr, x, gate, shift, scale, ag, hg, g2, w_out, ones_bd, wr_hi, wr_lo, br)


def _slab_copies(stage, hbm, row0, sem, to_hbm):
    copies = []
    for c in range(stage.shape[0]):
        view = hbm.at[pl.ds(row0, stage.shape[1]), c, :]
        src, dst = (stage.at[c], view) if to_hbm else (view, stage.at[c])
        copies.append(pltpu.make_async_copy(src, dst, sem))
    return copies


def _stage_store(stage, x):
    for c in range(stage.shape[0]):
        stage[c] = x[:, c * LANES:(c + 1) * LANES]


def _stage_load(stage):
    return jnp.concatenate([stage[c] for c in range(stage.shape[0])], axis=1)


_DISPATCH_SLOTS = 3


def _dispatch_kernel(dest_ref, pad_ref, h2_hbm, xs_hbm, buf, zero_row, sem_in, sem_out):
    i = pl.program_id(0)
    n = pl.num_programs(0)
    groups = buf.shape[1]
    tm = groups * SUBLANES
    n_pad = pad_ref.shape[2]
    slot = i % _DISPATCH_SLOTS

    def fetch(t, sl):
        return pltpu.make_async_copy(h2_hbm.at[pl.ds(t * groups, groups)], buf.at[sl],
                                     sem_in.at[sl])

    def wait_rows(sl):
        for _ in range(TOP_K + n_pad // tm):
            pltpu.make_async_copy(h2_hbm.at[pl.ds(0, groups)], buf.at[sl], sem_out.at[sl]).wait()
        rest = (n_pad % tm) // SUBLANES
        if rest:
            pltpu.make_async_copy(h2_hbm.at[pl.ds(0, rest)], buf.at[sl, pl.ds(0, rest)],
                                  sem_out.at[sl]).wait()

    @pl.when(i == 0)
    def _():
        zero_row[...] = jnp.zeros_like(zero_row)
        fetch(0, 0).start()

    @pl.when(i >= 2)
    def _():
        wait_rows((i + 1) % _DISPATCH_SLOTS)

    @pl.when(i + 1 < n)
    def _():
        fetch(i + 1, (i + 1) % _DISPATCH_SLOTS).start()

    fetch(i, slot).wait()

    def body(g, carry):
        rows = [[dest_ref[0, 0, k * tm + g * SUBLANES + u] for k in range(TOP_K)]
                for u in range(SUBLANES)]
        for u in range(SUBLANES):
            for k in range(TOP_K):
                pltpu.make_async_copy(buf.at[slot, g, :, u, :], xs_hbm.at[rows[u][k]],
                                      sem_out.at[slot]).start()
        return carry
    lax.fori_loop(0, groups, body, 0)

    def pad_body(g, carry):
        rows = [pad_ref[0, 0, g * SUBLANES + u] for u in range(SUBLANES)]
        for u in range(SUBLANES):
            pltpu.make_async_copy(zero_row, xs_hbm.at[rows[u]], sem_out.at[slot]).start()
        return carry
    lax.fori_loop(0, n_pad // SUBLANES, pad_body, 0)

    @pl.when(i == n - 1)
    def _():
        wait_rows(slot)

    @pl.when((i == n - 1) & (i >= 1))
    def _():
        wait_rows((i - 1) % _DISPATCH_SLOTS)


def _dispatch(dest_tiles, pad_tiles, h2_tiles, n_slots):
    nt = dest_tiles.shape[0]
    n_groups, n_slabs, _, _ = h2_tiles.shape
    groups = n_groups // nt
    assert pad_tiles.shape[2] % SUBLANES == 0

    def idx_spec(arr):
        return pl.BlockSpec((1, 1, arr.shape[2]), lambda i: (i, 0, 0), memory_space=pltpu.SMEM)

    return pl.pallas_call(
        _dispatch_kernel,
        out_shape=jax.ShapeDtypeStruct((n_slots, n_slabs, LANES), F32),
        grid=(nt,),
        in_specs=[idx_spec(dest_tiles), idx_spec(pad_tiles), pl.BlockSpec(memory_space=pl.ANY)],
        out_specs=pl.BlockSpec(memory_space=pl.ANY),
        scratch_shapes=[pltpu.VMEM((_DISPATCH_SLOTS, groups, n_slabs, SUBLANES, LANES), F32),
                        pltpu.VMEM((n_slabs, LANES), F32),
                        pltpu.SemaphoreType.DMA((_DISPATCH_SLOTS,)),
                        pltpu.SemaphoreType.DMA((_DISPATCH_SLOTS,))],
        compiler_params=_params("arbitrary"),
        name="dispatch",
    )(dest_tiles, pad_tiles, h2_tiles)


_EXPERT_IN_SLOTS = 3


def _experts_kernel(meta_ref, xs_hbm, wg_ref, wu_ref, wd_ref, ys_hbm,
                    xstage, ystage, wg_bf, wu_bf, wd_bf, sem_in, sem_out):
    j = pl.program_id(0)
    last = pl.num_programs(0) - 1
    n_used = meta_ref[0]
    slot = j % 2

    def fetch(blk, sl):
        for cp in _slab_copies(xstage.at[sl], xs_hbm, blk * MOE_BLOCK, sem_in.at[sl], False):
            cp.start()

    def wait_out(sl):
        for cp in _slab_copies(ystage.at[sl], ys_hbm, 0, sem_out.at[sl], True):
            cp.wait()

    in_slot = j % _EXPERT_IN_SLOTS

    @pl.when((j == 0) & (n_used > 0))
    def _():
        fetch(0, 0)

    @pl.when((j == 0) & (n_used > 1))
    def _():
        fetch(1, 1)

    @pl.when(j + 2 < n_used)
    def _():
        fetch(j + 2, (j + 2) % _EXPERT_IN_SLOTS)

    new_expert = (j == 0) | (meta_ref[1 + j] != meta_ref[jnp.maximum(j, 1)])

    @pl.when((j < n_used) & new_expert)
    def _():
        wg_bf[...] = wg_ref[0].astype(BF16)
        wu_bf[...] = wu_ref[0].astype(BF16)
        wd_bf[...] = wd_ref[0].astype(BF16)

    @pl.when(j >= 2)
    def _():
        wait_out(slot)

    @pl.when(j < n_used)
    def _():
        for cp in _slab_copies(xstage.at[in_slot], xs_hbm, 0, sem_in.at[in_slot], False):
            cp.wait()
        xb = _stage_load(xstage.at[in_slot]).astype(BF16)
        a = _silu(_dot(xb, wg_bf[...])) * _dot(xb, wu_bf[...])
        _stage_store(ystage.at[slot], _dot(a.astype(BF16), wd_bf[...]))

    @pl.when(j >= n_used)
    def _():
        ystage[slot] = jnp.zeros(ystage.shape[1:], F32)

    for cp in _slab_copies(ystage.at[slot], ys_hbm, j * MOE_BLOCK, sem_out.at[slot], True):
        cp.start()

    @pl.when(j == last)
    def _():
        wait_out(slot)

    @pl.when((j == last) & (j >= 1))
    def _():
        wait_out(1 - slot)


def _experts(meta, xs, wg, wu, wd):
    n_slots, n_slabs, _ = xs.shape
    n_blk = n_slots // MOE_BLOCK
    d = wg.shape[1]
    ff = wg.shape[2]

    def wspec(shape):
        return pl.BlockSpec((1,) + shape, lambda j, meta: (meta[1 + j], 0, 0))

    in_stage = pltpu.VMEM((_EXPERT_IN_SLOTS, n_slabs, MOE_BLOCK, LANES), F32)
    stage = pltpu.VMEM((2, n_slabs, MOE_BLOCK, LANES), F32)
    return pl.pallas_call(
        _experts_kernel,
        out_shape=jax.ShapeDtypeStruct(xs.shape, F32),
        grid_spec=pltpu.PrefetchScalarGridSpec(
            num_scalar_prefetch=1,
            grid=(n_blk,),
            in_specs=[pl.BlockSpec(memory_space=pl.ANY),
                      wspec((d, ff)), wspec((d, ff)), wspec((ff, d))],
            out_specs=pl.BlockSpec(memory_space=pl.ANY),
            scratch_shapes=[in_stage, stage, pltpu.VMEM((d, ff), BF16),
                            pltpu.VMEM((d, ff), BF16), pltpu.VMEM((ff, d), BF16),
                            pltpu.SemaphoreType.DMA((_EXPERT_IN_SLOTS,)),
                            pltpu.SemaphoreType.DMA((2,))],
        ),
        compiler_params=_params("arbitrary"),
        name="experts",
    )(meta, xs, wg, wu, wd)


def _combine_kernel(dest_cur_ref, dest_next_ref, ys_hbm, x1_ref, gate_ref, rw_ref, o_ref,
                    buf, sem):
    i = pl.program_id(0)
    slot = i % 2
    tm = x1_ref.shape[0]
    groups = tm // SUBLANES

    def start(idx_ref, sl):
        for k in range(TOP_K):
            def body(g, carry, k=k):
                rows = [idx_ref[0, 0, k * tm + g * SUBLANES + u] for u in range(SUBLANES)]
                for u in range(SUBLANES):
                    pltpu.make_async_copy(ys_hbm.at[rows[u]], buf.at[sl, k, g, :, u, :],
                                          sem.at[sl, k]).start()
                return carry
            lax.fori_loop(0, groups, body, 0)

    def wait(sl, k):
        for u in range(SUBLANES):
            pltpu.make_async_copy(ys_hbm.at[pl.ds(0, groups)], buf.at[sl, k, :, :, u, :],
                                  sem.at[sl, k]).wait()

    @pl.when(i == 0)
    def _():
        start(dest_cur_ref, 0)

    @pl.when(i + 1 < pl.num_programs(0))
    def _():
        start(dest_next_ref, 1 - slot)

    rw = rw_ref[...]
    moe = None
    for k in range(TOP_K):
        wait(slot, k)
        rows = jnp.concatenate([buf[slot, k, :, c].reshape(tm, LANES)
                                for c in range(buf.shape[3])], axis=1)
        term = rw[:, k:k + 1] * rows
        moe = term if moe is None else moe + term
    o_ref[...] = x1_ref[...] + gate_ref[0] * moe


def _combine(dest_tiles, ys, x1, gate, rw):
    n_tok, d = x1.shape
    tm = TOKEN_TILE
    nt = n_tok // tm
    per_batch = nt // gate.shape[0]
    idx_blk = (1, 1, TOP_K * tm)
    return pl.pallas_call(
        _combine_kernel,
        out_shape=jax.ShapeDtypeStruct((n_tok, d), F32),
        grid=(nt,),
        in_specs=[
            pl.BlockSpec(idx_blk, lambda i: (i, 0, 0), memory_space=pltpu.SMEM),
            pl.BlockSpec(idx_blk, lambda i: (jnp.minimum(i + 1, nt - 1), 0, 0),
                         memory_space=pltpu.SMEM),
            pl.BlockSpec(memory_space=pl.ANY),
            pl.BlockSpec((tm, d), lambda i: (i, 0)),
            pl.BlockSpec((1, 1, d), lambda i: (i // per_batch, 0, 0)),
            pl.BlockSpec((tm, ROUTE_LANES), lambda i: (i, 0)),
        ],
        out_specs=pl.BlockSpec((tm, d), lambda i: (i, 0)),
        scratch_shapes=[pltpu.VMEM((2, TOP_K, tm // SUBLANES, d // LANES, SUBLANES, LANES), F32),
                        pltpu.SemaphoreType.DMA((2, TOP_K))],
        compiler_params=_params("arbitrary"),
        name="combine",
    )(dest_tiles, dest_tiles, ys, x1, gate, rw)


def _layer(x, ctx, c, c_ctx, w_ada, b_ada, norm1_g, norm2_g, w_in, q_norm_g, k_norm_g, attn_out_g,
           lb, hgrn_out_g, w_out, w_router_grp, b_router_grp, w_router_exp, b_router_exp,
           w_exp_gate, w_exp_up, w_exp_down):
    b, s, d = x.shape
    n_ctx = ctx.shape[1]
    assert n_ctx % TOKEN_TILE == 0 and s % TOKEN_TILE == 0 and s % GRID_W == 0
    assert n_ctx % HGRN_CHUNK == 0 and (n_ctx + s) % ATTN_KV_TILE == 0
    n_all = n_ctx + s

    cond = jnp.zeros((2 * SUBLANES, d), F32).at[:b].set(c).at[b].set(c_ctx)
    assert b + 1 <= cond.shape[0]
    mods = _adaln(cond, w_ada, b_ada)[:b + 1].reshape(b + 1, 1, 6, d)
    sh1, sc1, gt1, sh2, sc2, gt2 = (mods[:, :, m] for m in range(6))

    scale_q = ATTN_HEAD_DIM ** -0.5 * np.log2(np.e)
    qkg = jnp.concatenate([jnp.tile(q_norm_g, ATTN_HEADS) * scale_q,
                           jnp.tile(k_norm_g, ATTN_KV_HEADS)]).reshape(1, _QK_WIDTH)
    cos, sin = _rope_tables(n_ctx, s)
    qa, ka, vt, qr, ff, fb, ir, gr = _in_proj(
        ctx, x, sh1, sc1, norm1_g.reshape(1, d), w_in.astype(BF16), qkg, cos, sin)
    oa = _attention(qa, ka, vt)
    o_f, o_b = _hgrn(qr, ir, ff, fb, lb, n_ctx)

    w_router = jnp.zeros((d, LANES), F32)
    w_router = w_router.at[:, :N_EXPERTS].set(w_router_exp)
    w_router = w_router.at[:, _ROUTE_GROUP_LANE0:_ROUTE_GROUP_LANE0 + N_GROUPS].set(w_router_grp)
    b_router = jnp.zeros((1, LANES), F32)
    b_router = b_router.at[0, :N_EXPERTS].set(b_router_exp)
    b_router = b_router.at[0, _ROUTE_GROUP_LANE0:_ROUTE_GROUP_LANE0 + N_GROUPS].set(b_router_grp)
    wr_hi, wr_lo = _split_bf16(w_router)
    x1, h2, ri, rw, counts = _out_proj(
        oa, o_f, o_b, gr, x, gt1[:b], sh2[:b], sc2[:b], attn_out_g.reshape(1, -1),
        hgrn_out_g.reshape(1, -1), norm2_g.reshape(1, d), w_out.astype(BF16), wr_hi, wr_lo,
        b_router, n_ctx)

    n_tok = b * s
    counts = counts[0, :N_EXPERTS]
    padded = (counts + MOE_BLOCK - 1) // MOE_BLOCK * MOE_BLOCK
    pend = jnp.cumsum(padded)
    pstart = pend - padded
    n_blk = n_tok * TOP_K // MOE_BLOCK + N_EXPERTS
    def lookup(table, idx):
        hit = idx[..., None] == jnp.arange(table.shape[0], dtype=jnp.int32)
        return jnp.sum(jnp.where(hit, table.astype(jnp.int32), 0), axis=-1)

    dest = lookup(pstart, ri[:, :TOP_K]) + ri[:, TOP_K:2 * TOP_K]
    dest_tiles = dest.reshape(dest.shape[0], 1, -1)
    blk_start = jnp.arange(n_blk, dtype=jnp.int32) * MOE_BLOCK
    blk_e = jnp.minimum(jnp.sum(blk_start[:, None] >= pend[None, :], axis=1), N_EXPERTS - 1)
    n_used = pend[-1] // MOE_BLOCK
    meta = jnp.concatenate([n_used[None], blk_e]).astype(jnp.int32)

    n_slots = n_blk * MOE_BLOCK
    n_pad = n_slots - n_tok * TOP_K
    gap_start = jnp.concatenate([pstart + counts, pend[-1:]])
    gap_size = jnp.concatenate([padded - counts, n_slots - pend[-1:]])
    gap_end = jnp.cumsum(gap_size)
    pad_i = jnp.arange(n_pad, dtype=jnp.int32)
    gap = jnp.sum(pad_i[:, None] >= gap_end[None, :], axis=1)
    pad_slots = lookup(gap_start - (gap_end - gap_size), gap) + pad_i
    pad_tiles = pad_slots.astype(jnp.int32).reshape(dest_tiles.shape[0], 1, -1)

    xs = _dispatch(dest_tiles, pad_tiles, h2, n_slots)
    ys = _experts(meta, xs, w_exp_gate, w_exp_up, w_exp_down)
    out = _combine(dest_tiles, ys, x1.reshape(n_tok, d), gt2[:b], rw.reshape(n_tok, ROUTE_LANES))
    return out.reshape(b, s, d)


def kernel(x, c, ctx, c_ctx, w_ada, b_ada, norm1_g, norm2_g, w_in, q_norm_g, k_norm_g, attn_out_g,
           hgrn_lb, hgrn_out_g, w_out, w_router_grp, b_router_grp, w_router_exp, b_router_exp,
           w_exp_gate, w_exp_up, w_exp_down):
    depth = w_in.shape[0]
    assert depth == 1, "context stream update between layers is not implemented"
    lb_all = jnp.cumsum(jax.nn.softmax(hgrn_lb.astype(F32), axis=1), axis=1)
    layer = 0
    return _layer(x, ctx, c, c_ctx, w_ada[layer], b_ada[layer], norm1_g[layer], norm2_g[layer],
                  w_in[layer], q_norm_g[layer], k_norm_g[layer], attn_out_g[layer],
                  lb_all[:, layer], hgrn_out_g[layer], w_out[layer], w_router_grp[layer],
                  b_router_grp[layer], w_router_exp[layer], b_router_exp[layer],
                  w_exp_gate[layer], w_exp_up[layer], w_exp_down[layer])
```

```python
import functools

import numpy as np
import jax
import jax.numpy as jnp
from jax import lax
from jax.experimental import pallas as pl
from jax.experimental.pallas import tpu as pltpu

F32 = jnp.float32
BF16 = jnp.bfloat16

GRID_W = 64
EPS = 1e-6
ATTN_HEADS = 8
ATTN_KV_HEADS = 2
ATTN_HEAD_DIM = 64
ATTN_GROUP = ATTN_HEADS // ATTN_KV_HEADS
ATTN_WIDTH = ATTN_HEADS * ATTN_HEAD_DIM
KV_WIDTH = ATTN_KV_HEADS * ATTN_HEAD_DIM
ROPE_THETA = 10000.0
HGRN_HEADS = 4
HGRN_HEAD_DIM = 128
HGRN_WIDTH = HGRN_HEADS * HGRN_HEAD_DIM
N_GROUPS = 4
EXPERTS_PER_GROUP = 8
N_EXPERTS = N_GROUPS * EXPERTS_PER_GROUP
TOP_K = 2
EXPERT_FF = 512

LANES = 128
SUBLANES = 8
MXU_DIM = 256
VMEM_LIMIT_BYTES = 48 * 1024 * 1024

TOKEN_TILE = 256
OUT_PROJ_SPLIT = 2
ATTN_Q_TILE = 1024
ATTN_KV_TILE = 256
ATTN_KV_UNROLL = 17
ATTN_BOUND_SLACK = 1.02
ATTN_BOUND_MAX = 60.0
HGRN_CHUNK = 64
HGRN_STEP_CHUNKS = 4
MOE_BLOCK = 256
GATHER_UNROLL = 8
ROUTE_LANES = 8
NEG_BIG = -1e30

_QA0 = 0
_KA0 = _QA0 + ATTN_WIDTH
_VA0 = _KA0 + KV_WIDTH
_QR0 = _VA0 + KV_WIDTH
_FF0 = _QR0 + HGRN_WIDTH
_FB0 = _FF0 + HGRN_WIDTH
_IR0 = _FB0 + HGRN_WIDTH
_GR0 = _IR0 + HGRN_WIDTH
_QK_WIDTH = ATTN_WIDTH + KV_WIDTH


def _dot(a, b):
    return jnp.dot(a, b, preferred_element_type=F32)


def _dot_nt(a, b):
    return lax.dot_general(a, b, (((1,), (1,)), ((), ())), preferred_element_type=F32)


def _dot_tn(a, b):
    return lax.dot_general(a, b, (((0,), (0,)), ((), ())), preferred_element_type=F32)


def _split_bf16(x):
    hi = x.astype(BF16)
    lo = (x - hi.astype(F32)).astype(BF16)
    return hi, lo


def _sigmoid(x):
    return 1.0 / (1.0 + jnp.exp(-x))


def _silu(x):
    return x * _sigmoid(x)


def _params(*sem):
    return pltpu.CompilerParams(dimension_semantics=sem, vmem_limit_bytes=VMEM_LIMIT_BYTES)


def _head_sum_matrix():
    idx = np.arange(LANES) // ATTN_HEAD_DIM
    return jnp.asarray(idx[:, None] == idx[None, :], dtype=BF16)


def _head_rms_scale(x, ones_bd):
    ssq = _dot((x * x).astype(BF16), ones_bd)
    return lax.rsqrt(ssq * (1.0 / ATTN_HEAD_DIM) + EPS)


def _adaln_kernel(cond_ref, w_ref, b_ref, o_ref):
    s = _silu(cond_ref[...])
    s_hi, s_lo = _split_bf16(s)
    w_hi, w_lo = _split_bf16(w_ref[...])
    o_ref[...] = _dot(s_hi, w_hi) + _dot(s_lo, w_hi) + _dot(s_hi, w_lo) + b_ref[...]


def _adaln(cond, w_ada, b_ada):
    rows, d = cond.shape
    n = w_ada.shape[1]
    tn = n // 6
    return pl.pallas_call(
        _adaln_kernel,
        out_shape=jax.ShapeDtypeStruct((rows, n), F32),
        grid=(n // tn,),
        in_specs=[pl.BlockSpec((rows, d), lambda j: (0, 0)),
                  pl.BlockSpec((d, tn), lambda j: (0, j)),
                  pl.BlockSpec((1, tn), lambda j: (0, j))],
        out_specs=pl.BlockSpec((rows, tn), lambda j: (0, j)),
        compiler_params=_params("arbitrary"),
        name="adaln",
    )(cond, w_ada, b_ada.reshape(1, n))


def _rope_tables(n_ctx, n_lat):
    half = ATTN_HEAD_DIM // 2
    freqs = ROPE_THETA ** (-np.arange(0, half, 2, dtype=np.float64) / half)
    tok = np.arange(n_lat)
    pos = np.stack([tok // GRID_W, tok % GRID_W], axis=1).astype(np.float64)
    lane = np.arange(ATTN_HEAD_DIM)
    axis = lane // half
    fi = (lane % half) // 2
    ang = pos[:, axis] * freqs[fi][None, :]
    sign = np.where(lane % 2 == 1, 1.0, -1.0)
    cos = np.concatenate([np.ones((n_ctx, ATTN_HEAD_DIM)), np.cos(ang)], axis=0)
    sin = np.concatenate([np.zeros((n_ctx, ATTN_HEAD_DIM)), np.sin(ang) * sign], axis=0)
    reps = LANES // ATTN_HEAD_DIM
    return (jnp.asarray(np.tile(cos, (1, reps)), F32), jnp.asarray(np.tile(sin, (1, reps)), F32))


def _in_proj_kernel(n_ctx_tiles, ctx_ref, x_ref, shift_ref, scale_ref, g1_ref, w_ref, qkg_ref,
                    ones_ref, cos_ref, sin_ref,
                    qa_ref, ka_ref, vt_ref, qr_ref, ff_ref, fb_ref, ir_ref, gr_ref):
    i = pl.program_id(1)
    xt = jnp.where(i < n_ctx_tiles, ctx_ref[0], x_ref[0])
    ms = jnp.mean(xt * xt, axis=-1, keepdims=True)
    h = xt * lax.rsqrt(ms + EPS) * g1_ref[...]
    h = h * (1.0 + scale_ref[0]) + shift_ref[0]
    p = _dot(h.astype(BF16), w_ref[...])

    ones_bd = ones_ref[...]
    cos = cos_ref[...]
    sin = sin_ref[...]
    even = lax.broadcasted_iota(jnp.int32, cos.shape, 1) % 2 == 0
    slabs = []
    for c0 in range(0, _QK_WIDTH, LANES):
        t = p[:, _QA0 + c0:_QA0 + c0 + LANES]
        t = t * _head_rms_scale(t, ones_bd) * qkg_ref[:, c0:c0 + LANES]
        partner = jnp.where(even, pltpu.roll(t, LANES - 1, 1), pltpu.roll(t, 1, 1))
        slabs.append((t * cos + partner * sin).astype(BF16))
    qa_ref[0] = jnp.concatenate(slabs[:ATTN_WIDTH // LANES], axis=1)
    k_all = jnp.concatenate(slabs[ATTN_WIDTH // LANES:], axis=1)
    vt_all = p[:, _VA0:_VA0 + KV_WIDTH].T
    ones = jnp.ones((vt_ref.shape[2] - ATTN_HEAD_DIM, vt_all.shape[1]), F32)
    for hd in range(ATTN_KV_HEADS):
        cols = slice(hd * ATTN_HEAD_DIM, (hd + 1) * ATTN_HEAD_DIM)
        ka_ref[0, hd] = k_all[:, cols]
        vt_ref[0, hd] = jnp.concatenate([vt_all[cols], ones], axis=0).astype(BF16)

    qr_ref[0] = (_silu(p[:, _QR0:_QR0 + HGRN_WIDTH]) * (HGRN_HEAD_DIM ** -0.5)).astype(BF16)
    ff_ref[0] = p[:, _FF0:_FF0 + HGRN_WIDTH]
    fb_ref[0] = p[:, _FB0:_FB0 + HGRN_WIDTH]
    ir_ref[0] = p[:, _IR0:_IR0 + HGRN_WIDTH].astype(BF16)
    gr_ref[0] = _silu(p[:, _GR0:_GR0 + HGRN_WIDTH]).astype(BF16)


def _in_proj(ctx, x, shift, scale, g1, w_in, qkg, cos, sin):
    b, n_ctx, d = ctx.shape
    s = x.shape[1]
    tm = TOKEN_TILE
    nct = n_ctx // tm
    n_all = n_ctx + s
    nt = n_all // tm
    pw = w_in.shape[1]
    ones_bd = _head_sum_matrix()

    def tok_spec(w):
        return pl.BlockSpec((1, tm, w), lambda bi, i: (bi, i, 0))

    mod_spec = pl.BlockSpec((1, 1, d), lambda bi, i: (jnp.where(i < nct, b, bi), 0, 0))
    outs = [(HGRN_WIDTH, BF16), (HGRN_WIDTH, F32), (HGRN_WIDTH, F32), (HGRN_WIDTH, BF16),
            (HGRN_WIDTH, BF16)]
    vt_rows = ATTN_HEAD_DIM + 2 * SUBLANES
    lat_spec = pl.BlockSpec((1, tm, ATTN_WIDTH), lambda bi, i: (bi, jnp.maximum(i - nct, 0), 0))
    k_spec = pl.BlockSpec((1, ATTN_KV_HEADS, tm, ATTN_HEAD_DIM), lambda bi, i: (bi, 0, i, 0))
    vt_spec = pl.BlockSpec((1, ATTN_KV_HEADS, vt_rows, tm), lambda bi, i: (bi, 0, 0, i))
    return pl.pallas_call(
        functools.partial(_in_proj_kernel, nct),
        out_shape=[jax.ShapeDtypeStruct((b, s, ATTN_WIDTH), BF16),
                   jax.ShapeDtypeStruct((b, ATTN_KV_HEADS, n_all, ATTN_HEAD_DIM), BF16),
                   jax.ShapeDtypeStruct((b, ATTN_KV_HEADS, vt_rows, n_all), BF16)]
        + [jax.ShapeDtypeStruct((b, n_all, w), dt) for w, dt in outs],
        grid=(b, nt),
        in_specs=[
            pl.BlockSpec((1, tm, d), lambda bi, i: (bi, jnp.minimum(i, nct - 1), 0)),
            pl.BlockSpec((1, tm, d), lambda bi, i: (bi, jnp.maximum(i - nct, 0), 0)),
            mod_spec, mod_spec,
            pl.BlockSpec((1, d), lambda bi, i: (0, 0)),
            pl.BlockSpec((d, pw), lambda bi, i: (0, 0)),
            pl.BlockSpec((1, _QK_WIDTH), lambda bi, i: (0, 0)),
            pl.BlockSpec((LANES, LANES), lambda bi, i: (0, 0)),
            pl.BlockSpec((tm, LANES), lambda bi, i: (i, 0)),
            pl.BlockSpec((tm, LANES), lambda bi, i: (i, 0)),
        ],
        out_specs=[lat_spec, k_spec, vt_spec] + [tok_spec(w) for w, _ in outs],
        compiler_params=_params("arbitrary", "arbitrary"),
        name="in_proj",
    )(ctx, x, shift, scale, g1, w_in, qkg, ones_bd, cos, sin)


def _attention_kernel(n_kv_tiles, q_ref, k_ref, vt_ref, o_ref, qs_ref, s_ref, ksq_ref):
    tq = q_ref.shape[1]
    cols = ATTN_GROUP * tq
    for h in range(ATTN_GROUP):
        qs_ref[h * tq:(h + 1) * tq, :] = q_ref[0, :, h * ATTN_HEAD_DIM:(h + 1) * ATTN_HEAD_DIM]

    @pl.when(pl.program_id(2) == 0)
    def _():
        k = k_ref[0, 0].astype(F32)
        ksq = jnp.max(jnp.sum(k * k, axis=1, keepdims=True), axis=0, keepdims=True)
        ksq_ref[...] = jnp.broadcast_to(ksq, ksq_ref.shape)

    def tile(j):
        return pl.ds(pl.multiple_of(j * ATTN_KV_TILE, ATTN_KV_TILE), ATTN_KV_TILE)

    def scores(j):
        return _dot_nt(k_ref[0, 0, tile(j), :], qs_ref[...])

    def weighted(j, p):
        return _dot(vt_ref[0, 0, :, tile(j)], p)

    def finish(acc):
        o = acc[:ATTN_HEAD_DIM] / acc[ATTN_HEAD_DIM:ATTN_HEAD_DIM + 1]
        o_ref[0] = jnp.concatenate(
            [o[:, h * tq:(h + 1) * tq].T for h in range(ATTN_GROUP)], axis=1).astype(o_ref.dtype)

    qf = qs_ref[...].astype(F32)
    qsq = _dot_nt(jnp.ones((SUBLANES, ATTN_HEAD_DIM), BF16), (qf * qf).astype(BF16))[0:1]
    bound = jnp.sqrt(qsq * ksq_ref[0:1, 0:1]) * ATTN_BOUND_SLACK
    acc0 = jnp.zeros((vt_ref.shape[2], cols), F32)
    safe = jnp.max(bound) <= ATTN_BOUND_MAX

    @pl.when(safe)
    def _():
        def absorb(j, acc):
            return acc + weighted(j, jnp.exp2(scores(j) - bound).astype(BF16))

        def group(i, acc):
            for u in range(ATTN_KV_UNROLL):
                acc = absorb(ATTN_KV_UNROLL * i + u, acc)
            return acc

        n_groups = n_kv_tiles // ATTN_KV_UNROLL
        acc = lax.fori_loop(0, n_groups, group, acc0)
        for j in range(n_groups * ATTN_KV_UNROLL, n_kv_tiles):
            acc = absorb(j, acc)
        finish(acc)

    @pl.when(jnp.logical_not(safe))
    def _():
        def score(j, slot):
            s_ref[slot] = scores(j)

        def absorb(j, slot, carry):
            m, acc = carry
            s = s_ref[slot]
            m_new = jnp.maximum(m, jnp.max(s, axis=0, keepdims=True))
            p = jnp.exp2(s - m_new).astype(BF16)
            return m_new, jnp.exp2(m - m_new) * acc + weighted(j, p)

        def pair(i, carry):
            score(2 * i + 1, 1)
            carry = absorb(2 * i, 0, carry)
            score(2 * i + 2, 0)
            return absorb(2 * i + 1, 1, carry)

        n_pairs = (n_kv_tiles - 1) // 2
        score(0, 0)
        carry = lax.fori_loop(0, n_pairs, pair, (jnp.full((1, cols), NEG_BIG, F32), acc0))
        if n_kv_tiles % 2 == 1:
            carry = absorb(n_kv_tiles - 1, 0, carry)
        else:
            score(n_kv_tiles - 1, 1)
            carry = absorb(n_kv_tiles - 2, 0, carry)
            carry = absorb(n_kv_tiles - 1, 1, carry)
        finish(carry[1])


def _attention(qa, k_heads, vt_heads):
    b, s, _ = qa.shape
    n_all = k_heads.shape[2]
    tq = ATTN_Q_TILE
    gw = ATTN_GROUP * ATTN_HEAD_DIM
    vt_rows = vt_heads.shape[2]
    return pl.pallas_call(
        functools.partial(_attention_kernel, n_all // ATTN_KV_TILE),
        out_shape=jax.ShapeDtypeStruct((b, s, ATTN_WIDTH), BF16),
        grid=(b, ATTN_KV_HEADS, s // tq),
        in_specs=[pl.BlockSpec((1, tq, gw), lambda bi, kv, i: (bi, i, kv)),
                  pl.BlockSpec((1, 1, n_all, ATTN_HEAD_DIM), lambda bi, kv, i: (bi, kv, 0, 0)),
                  pl.BlockSpec((1, 1, vt_rows, n_all), lambda bi, kv, i: (bi, kv, 0, 0))],
        out_specs=pl.BlockSpec((1, tq, gw), lambda bi, kv, i: (bi, i, kv)),
        scratch_shapes=[pltpu.VMEM((ATTN_GROUP * tq, ATTN_HEAD_DIM), BF16),
                        pltpu.VMEM((2, ATTN_KV_TILE, ATTN_GROUP * tq), F32),
                        pltpu.VMEM((SUBLANES, LANES), F32)],
        compiler_params=_params("arbitrary", "arbitrary", "arbitrary"),
        name="attention",
    )(qa, k_heads, vt_heads)


def _hgrn_masks(reverse):
    c = HGRN_CHUNK
    t = lax.broadcasted_iota(jnp.int32, (c, c), 0)
    u = lax.broadcasted_iota(jnp.int32, (c, c), 1)
    tri = (u >= t) if reverse else (u <= t)
    levels = []
    size = c // 2
    while size >= 1:
        same_parent = (t // (2 * size)) == (u // (2 * size))
        levels.append((size, same_parent if 2 * size < c else None))
        size //= 2
    levels.append((0, t == u))
    return tri.astype(BF16), levels


def _hgrn_chunk(q, fr, v, lb, state, reverse, tri, levels):
    c = HGRN_CHUNK
    coarse_levels = [lv for lv in levels if lv[0] >= SUBLANES]
    fine_levels = [lv for lv in levels if lv[0] < SUBLANES]
    f = lb + (1.0 - lb) * _sigmoid(fr)
    k = 1.0 - f
    g_hi, g_lo = _split_bf16(jnp.log2(f))
    bcum = _dot(tri, g_hi) + _dot(tri, g_lo)
    qf = q.astype(F32)
    yield None

    st = state()
    end = 0 if reverse else c - 1
    b_end = bcum[end:end + 1, :]
    inter = _dot_nt((qf * jnp.exp2(bcum)).astype(BF16), st.astype(BF16))
    ke = (k * jnp.exp2(b_end - bcum)).astype(BF16)
    st_add = _dot_tn(v, ke)

    products = []
    for size, mask in coarse_levels:
        q_rows, k_rows = [], []
        zeros = jnp.zeros((size, qf.shape[1]), BF16)
        for p0 in range(0, c, 2 * size):
            early = slice(p0, p0 + size)
            late = slice(p0 + size, p0 + 2 * size)
            r = p0 + size if reverse else p0 + size - 1
            ref = bcum[r:r + 1, :]
            q_sl, k_sl = (early, late) if reverse else (late, early)
            q_blk = (qf[q_sl] * jnp.exp2(bcum[q_sl] - ref)).astype(BF16)
            k_blk = (k[k_sl] * jnp.exp2(ref - bcum[k_sl])).astype(BF16)
            q_rows += [q_blk, zeros] if reverse else [zeros, q_blk]
            k_rows += [zeros, k_blk] if reverse else [k_blk, zeros]
        products.append((mask, _dot_nt(jnp.concatenate(q_rows, axis=0),
                                       jnp.concatenate(k_rows, axis=0))))
    yield None

    row = lax.broadcasted_iota(jnp.int32, (c, 1), 0)
    sub = lax.broadcasted_iota(jnp.int32, (SUBLANES, 1), 0)
    fine_products = []
    for size, mask in fine_levels:
        if size == 0:
            q_l, k_l = q, k.astype(BF16)
        else:
            late = (row // size) % 2 == 1
            q_side = jnp.logical_not(late) if reverse else late
            if size == 1:
                fac_q, fac_k = f, None
            else:
                groups = []
                for r0 in range(0, c, SUBLANES):
                    ref = None
                    for p0 in range(0, SUBLANES, 2 * size):
                        r = r0 + p0 + (size if reverse else size - 1)
                        piece = jnp.broadcast_to(bcum[r:r + 1, :], (SUBLANES, bcum.shape[1]))
                        ref = piece if ref is None else jnp.where(sub >= p0, piece, ref)
                    groups.append(ref)
                ref = jnp.concatenate(groups, axis=0)
                fac_q = fac_k = jnp.exp2(jnp.where(q_side, bcum - ref, ref - bcum))
            q_l = jnp.where(q_side, qf * fac_q, 0.0).astype(BF16)
            k_l = jnp.where(q_side, 0.0, k if fac_k is None else k * fac_k).astype(BF16)
        fine_products.append((mask, _dot_nt(q_l, k_l)))
    a = None
    for mask, a_l in products:
        if mask is not None:
            a_l = jnp.where(mask, a_l, 0.0)
        a = a_l if a is None else a + a_l
    yield None

    for mask, a_l in fine_products:
        a = a + jnp.where(mask, a_l, 0.0)
    intra = _dot(a.astype(BF16), v)
    st_new = st * jnp.exp2(b_end) + st_add
    yield None

    yield inter + intra, st_new


_HGRN_STAGES = 5


def _hgrn_kernel(qf_ref, if_ref, ff_ref, qb_ref, ib_ref, fb_ref, lb_ref, of_ref, ob_ref,
                 sf_ref, sb_ref):
    @pl.when(pl.program_id(1) == 0)
    def _():
        sf_ref[...] = jnp.zeros_like(sf_ref)
        sb_ref[...] = jnp.zeros_like(sb_ref)

    dirs = ((False, qf_ref, if_ref, ff_ref, of_ref, sf_ref),
            (True, qb_ref, ib_ref, fb_ref, ob_ref, sb_ref))
    masks = [_hgrn_masks(reverse) for reverse, *_ in dirs]
    n_sub = qf_ref.shape[1] // HGRN_CHUNK
    latest = {}
    chunks = []
    for t in range(n_sub):
        for h in range(HGRN_HEADS):
            sl = slice(h * HGRN_HEAD_DIM, (h + 1) * HGRN_HEAD_DIM)
            for d, (reverse, q_ref, i_ref, f_ref, o_ref, s_ref) in enumerate(dirs):
                r0 = (n_sub - 1 - t if reverse else t) * HGRN_CHUNK
                rows = slice(r0, r0 + HGRN_CHUNK)

                def state(h=h, d=d, s_ref=s_ref):
                    return latest[h, d] if (h, d) in latest else s_ref[h]

                gen = _hgrn_chunk(q_ref[0, rows, sl], f_ref[0, rows, sl], i_ref[0, rows, sl],
                                  lb_ref[d:d + 1, sl], state, reverse, *masks[d])
                chunks.append((gen, o_ref, s_ref, h, d, rows, sl, t == n_sub - 1))

    assert len(dirs) * HGRN_HEADS >= _HGRN_STAGES
    for step in range(len(chunks) + _HGRN_STAGES - 1):
        for stage in range(_HGRN_STAGES):
            idx = step - stage
            if 0 <= idx < len(chunks):
                gen, o_ref, s_ref, h, d, rows, sl, is_last = chunks[idx]
                result = next(gen)
                if stage == _HGRN_STAGES - 1:
                    o, st = result
                    o_ref[0, rows, sl] = o
                    latest[h, d] = st
                    if is_last:
                        s_ref[h] = st


def _hgrn(qr, ir, ff, fb, lb, n_ctx):
    b, n_all, w = qr.shape
    c = HGRN_CHUNK * HGRN_STEP_CHUNKS
    assert n_ctx % c == 0 and n_all % c == 0
    nc = n_all // c
    ncc = n_ctx // c

    def fwd(bi, j):
        return (bi, j, 0)

    def bwd(bi, j):
        return (bi, jnp.where(j < ncc, ncc - 1 - j, nc - 1 - (j - ncc)), 0)

    blk = (1, c, w)
    state = pltpu.VMEM((HGRN_HEADS, HGRN_HEAD_DIM, HGRN_HEAD_DIM), F32)
    return pl.pallas_call(
        _hgrn_kernel,
        out_shape=[jax.ShapeDtypeStruct((b, n_all, w), F32)] * 2,
        grid=(b, nc),
        in_specs=[pl.BlockSpec(blk, fwd), pl.BlockSpec(blk, fwd), pl.BlockSpec(blk, fwd),
                  pl.BlockSpec(blk, bwd), pl.BlockSpec(blk, bwd), pl.BlockSpec(blk, bwd),
                  pl.BlockSpec((2, w), lambda bi, j: (0, 0))],
        out_specs=[pl.BlockSpec(blk, fwd), pl.BlockSpec(blk, bwd)],
        scratch_shapes=[state, state],
        compiler_params=_params("arbitrary", "arbitrary"),
        name="hgrn",
    )(qr, ir, ff, qr, ir, fb, lb)


_ROUTE_GROUP_LANE0 = N_EXPERTS


def _lane_min_index(cond, lane):
    return jnp.min(jnp.where(cond, lane, LANES), axis=-1, keepdims=True)


def _out_proj_kernel(oa_ref, of_ref, ob_ref, gr_ref, x_ref, gate_ref, shift_ref, scale_ref,
                     ag_ref, hg_ref, g2_ref, w_ref, ones_ref, wr_hi_ref, wr_lo_ref, br_ref,
                     x1_ref, h2_ref, ri_ref, rw_ref, cnt_ref, carry_ref):
    first = (pl.program_id(0) == 0) & (pl.program_id(1) == 0)

    @pl.when(first)
    def _():
        carry_ref[...] = jnp.zeros_like(carry_ref)

    tm = x_ref.shape[1]
    hm = tm // OUT_PROJ_SPLIT
    ones_bd = ones_ref[...]
    counts = [carry_ref[0:1, :]]

    def rows_chain(r0):
        rs = slice(r0, r0 + hm)
        ts = [oa_ref[0, rs, c0:c0 + LANES].astype(F32) for c0 in range(0, ATTN_WIDTH, LANES)]
        scales = [_head_rms_scale(t, ones_bd) for t in ts]
        orr = of_ref[0, rs] + ob_ref[0, rs]
        yield

        slabs = [(t * sc * ag_ref[:, c * LANES:(c + 1) * LANES]).astype(BF16)
                 for c, (t, sc) in enumerate(zip(ts, scales))]
        parts = []
        for h in range(HGRN_HEADS):
            oh = orr[:, h * HGRN_HEAD_DIM:(h + 1) * HGRN_HEAD_DIM]
            parts.append(oh * lax.rsqrt(jnp.mean(oh * oh, axis=-1, keepdims=True) + EPS))
        orn = jnp.concatenate(parts, axis=1) * hg_ref[...] * gr_ref[0, rs].astype(F32)
        mix = _dot(jnp.concatenate(slabs + [orn.astype(BF16)], axis=1), w_ref[...])
        yield

        x1 = x_ref[0, rs] + gate_ref[0] * mix
        x1_ref[0, rs] = x1
        h2 = x1 * lax.rsqrt(jnp.mean(x1 * x1, axis=-1, keepdims=True) + EPS) * g2_ref[...]
        h2 = h2 * (1.0 + scale_ref[0]) + shift_ref[0]
        gs = slice(r0 // SUBLANES, (r0 + hm) // SUBLANES)
        for c in range(h2_ref.shape[1]):
            h2_ref[gs, c] = h2[:, c * LANES:(c + 1) * LANES].reshape(hm // SUBLANES, SUBLANES,
                                                                     LANES)
        h_hi, h_lo = _split_bf16(h2)
        logits = (_dot(h_hi, wr_hi_ref[...]) + _dot(h_lo, wr_hi_ref[...])
                  + _dot(h_hi, wr_lo_ref[...]) + br_ref[...])
        yield

        lane = lax.broadcasted_iota(jnp.int32, logits.shape, 1)
        is_grp = (lane >= _ROUTE_GROUP_LANE0) & (lane < _ROUTE_GROUP_LANE0 + N_GROUPS)
        lg = jnp.where(is_grp, logits, NEG_BIG)
        mg = jnp.max(lg, axis=-1, keepdims=True)
        g_sel = _lane_min_index(lg == mg, lane) - _ROUTE_GROUP_LANE0
        pg_top = 1.0 / jnp.sum(jnp.exp(lg - mg), axis=-1, keepdims=True)
        in_grp = (lane < N_EXPERTS) & ((lane // EXPERTS_PER_GROUP) == g_sel)
        le = jnp.where(in_grp, logits, NEG_BIG)
        m1 = jnp.max(le, axis=-1, keepdims=True)
        e1 = _lane_min_index(le == m1, lane)
        le2 = jnp.where(lane == e1, NEG_BIG, le)
        m2 = jnp.max(le2, axis=-1, keepdims=True)
        e2 = _lane_min_index(le2 == m2, lane)
        r2 = jnp.exp(m2 - m1)
        w1 = pg_top / (1.0 + r2)
        w2 = pg_top * r2 / (1.0 + r2)
        onehot = ((lane == e1) | (lane == e2)).astype(BF16)
        rt = lax.broadcasted_iota(jnp.int32, (hm, hm), 0)
        ru = lax.broadcasted_iota(jnp.int32, (hm, hm), 1)
        before_local = _dot((ru < rt).astype(BF16), onehot)
        yield

        before = before_local + counts[-1]
        rank1 = jnp.sum(jnp.where(lane == e1, before, 0.0), axis=-1, keepdims=True)
        rank2 = jnp.sum(jnp.where(lane == e2, before, 0.0), axis=-1, keepdims=True)
        counts.append(counts[-1] + jnp.sum(onehot.astype(F32), axis=0, keepdims=True))
        rec = jnp.where(lane == 0, e1.astype(F32), jnp.where(lane == 1, e2.astype(F32), jnp.where(
            lane == 2, rank1, jnp.where(lane == 3, rank2, 0.0))))
        ri_ref[0, :, rs] = rec.T[:ROUTE_LANES].astype(jnp.int32)
        rl = lax.broadcasted_iota(jnp.int32, (hm, ROUTE_LANES), 1)
        rw_ref[0, rs] = jnp.where(rl == 0, w1, jnp.where(rl == 1, w2, 0.0))
        yield

    n_stages = 5
    chains = [rows_chain(r0) for r0 in range(0, tm, hm)]
    for step in range(len(chains) + n_stages - 1):
        for stage in range(n_stages):
            idx = step - stage
            if 0 <= idx < len(chains):
                next(chains[idx])
    carry_ref[...] = jnp.broadcast_to(counts[-1], carry_ref.shape)
    cnt_ref[...] = jnp.broadcast_to(counts[-1], cnt_ref.shape).astype(jnp.int32)


def _out_proj(oa, o_f, o_b, gr, x, gate, shift, scale, ag, hg, g2, w_out, wr_hi, wr_lo, br, n_ctx):
    b, s, d = x.shape
    tm = TOKEN_TILE
    off = n_ctx // tm
    ones_bd = _head_sum_matrix()

    def lat(w):
        return pl.BlockSpec((1, tm, w), lambda bi, i: (bi, i, 0))

    def allrows(w):
        return pl.BlockSpec((1, tm, w), lambda bi, i: (bi, i + off, 0))

    def mod():
        return pl.BlockSpec((1, 1, d), lambda bi, i: (bi, 0, 0))

    def const(shape):
        return pl.BlockSpec(shape, lambda bi, i: (0,) * len(shape))

    return pl.pallas_call(
        _out_proj_kernel,
        out_shape=[jax.ShapeDtypeStruct((b, s, d), F32),
                   jax.ShapeDtypeStruct((b * s // SUBLANES, d // LANES, SUBLANES, LANES), F32),
                   jax.ShapeDtypeStruct((b * s // tm, ROUTE_LANES, tm), jnp.int32),
                   jax.ShapeDtypeStruct((b, s, ROUTE_LANES), F32),
                   jax.ShapeDtypeStruct((SUBLANES, LANES), jnp.int32)],
        grid=(b, s // tm),
        in_specs=[lat(ATTN_WIDTH), allrows(HGRN_WIDTH), allrows(HGRN_WIDTH), allrows(HGRN_WIDTH),
                  lat(d), mod(), mod(), mod(),
                  const((1, ATTN_WIDTH)), const((1, HGRN_WIDTH)), const((1, d)),
                  const((ATTN_WIDTH + HGRN_WIDTH, d)), const((LANES, LANES)),
                  const((d, LANES)), const((d, LANES)), const((1, LANES))],
        out_specs=[lat(d),
                   pl.BlockSpec((tm // SUBLANES, d // LANES, SUBLANES, LANES),
                                lambda bi, i: (bi * (s // tm) + i, 0, 0, 0)),
                   pl.BlockSpec((1, ROUTE_LANES, tm), lambda bi, i: (bi * (s // tm) + i, 0, 0)),
                   lat(ROUTE_LANES), const((SUBLANES, LANES))],
        scratch_shapes=[pltpu.VMEM((SUBLANES, LANES), F32)],
        compiler_params=_params("arbitrary", "arbitrary"),
        name="out_proj",
    )(oa, o_f, o_b, gr, x, gate, shift, scale, ag, hg, g2, w_out, ones_bd, wr_hi, wr_lo, br)


_DISPATCH_SLOTS = 3


def _dispatch_kernel(dest_ref, pad_ref, h2_hbm, xs_hbm, buf, zero_row, sem_in, sem_out):
    i = pl.program_id(0)
    n = pl.num_programs(0)
    groups = buf.shape[1]
    tm = groups * SUBLANES
    n_pad = pad_ref.shape[2]
    slot = i % _DISPATCH_SLOTS

    def fetch(t, sl):
        return pltpu.make_async_copy(h2_hbm.at[pl.ds(t * groups, groups)], buf.at[sl],
                                     sem_in.at[sl])

    def wait_rows(sl):
        for _ in range(TOP_K + n_pad // tm):
            pltpu.make_async_copy(h2_hbm.at[pl.ds(0, groups)], buf.at[sl], sem_out.at[sl]).wait()
        rest = (n_pad % tm) // SUBLANES
        if rest:
            pltpu.make_async_copy(h2_hbm.at[pl.ds(0, rest)], buf.at[sl, pl.ds(0, rest)],
                                  sem_out.at[sl]).wait()

    @pl.when(i == 0)
    def _():
        zero_row[...] = jnp.zeros_like(zero_row)
        fetch(0, 0).start()

    @pl.when(i >= 2)
    def _():
        wait_rows((i + 1) % _DISPATCH_SLOTS)

    @pl.when(i + 1 < n)
    def _():
        fetch(i + 1, (i + 1) % _DISPATCH_SLOTS).start()

    fetch(i, slot).wait()

    def body(g, carry):
        rows = [[dest_ref[0, 0, k * tm + g * SUBLANES + u] for k in range(TOP_K)]
                for u in range(SUBLANES)]
        for u in range(SUBLANES):
            for k in range(TOP_K):
                pltpu.make_async_copy(buf.at[slot, g, :, u, :], xs_hbm.at[rows[u][k]],
                                      sem_out.at[slot]).start()
        return carry
    lax.fori_loop(0, groups, body, 0)

    def pad_body(g, carry):
        rows = [pad_ref[0, 0, g * SUBLANES + u] for u in range(SUBLANES)]
        for u in range(SUBLANES):
            pltpu.make_async_copy(zero_row, xs_hbm.at[rows[u]], sem_out.at[slot]).start()
        return carry
    lax.fori_loop(0, n_pad // SUBLANES, pad_body, 0)

    @pl.when(i == n - 1)
    def _():
        wait_rows(slot)

    @pl.when((i == n - 1) & (i >= 1))
    def _():
        wait_rows((i - 1) % _DISPATCH_SLOTS)


def _dispatch(dest_tiles, pad_tiles, h2_tiles, n_slots):
    nt = dest_tiles.shape[0]
    n_groups, n_slabs, _, _ = h2_tiles.shape
    groups = n_groups // nt
    assert pad_tiles.shape[2] % SUBLANES == 0

    def idx_spec(arr):
        return pl.BlockSpec((1, 1, arr.shape[2]), lambda i: (i, 0, 0), memory_space=pltpu.SMEM)

    return pl.pallas_call(
        _dispatch_kernel,
        out_shape=jax.ShapeDtypeStruct((n_slots, n_slabs, LANES), F32),
        grid=(nt,),
        in_specs=[idx_spec(dest_tiles), idx_spec(pad_tiles), pl.BlockSpec(memory_space=pl.ANY)],
        out_specs=pl.BlockSpec(memory_space=pl.ANY),
        scratch_shapes=[pltpu.VMEM((_DISPATCH_SLOTS, groups, n_slabs, SUBLANES, LANES), F32),
                        pltpu.VMEM((n_slabs, LANES), F32),
                        pltpu.SemaphoreType.DMA((_DISPATCH_SLOTS,)),
                        pltpu.SemaphoreType.DMA((_DISPATCH_SLOTS,))],
        compiler_params=_params("arbitrary"),
        name="dispatch",
    )(dest_tiles, pad_tiles, h2_tiles)


def _experts_kernel(meta_ref, xs_ref, wg_ref, wu_ref, wd_ref, ys_ref, wg_bf, wu_bf, wd_bf):
    j = pl.program_id(0)
    n_used = meta_ref[0]
    n_slabs = xs_ref.shape[0] // MOE_BLOCK

    def slab(ref, c):
        return ref.at[pl.ds(c, MOE_BLOCK, stride=n_slabs), :]

    new_expert = (j == 0) | (meta_ref[1 + j] != meta_ref[jnp.maximum(j, 1)])

    @pl.when((j < n_used) & new_expert)
    def _():
        wg_bf[...] = wg_ref[0].astype(BF16)
        wu_bf[...] = wu_ref[0].astype(BF16)
        wd_bf[...] = wd_ref[0].astype(BF16)

    @pl.when(j < n_used)
    def _():
        xb = jnp.concatenate([slab(xs_ref, c)[...] for c in range(n_slabs)], axis=1).astype(BF16)
        a = _silu(_dot(xb, wg_bf[...])) * _dot(xb, wu_bf[...])
        y = _dot(a.astype(BF16), wd_bf[...])
        for c in range(n_slabs):
            slab(ys_ref, c)[...] = y[:, c * LANES:(c + 1) * LANES]

    @pl.when(j >= n_used)
    def _():
        ys_ref[...] = jnp.zeros_like(ys_ref)


def _experts(meta, xs, wg, wu, wd):
    n_slots, n_slabs, _ = xs.shape
    n_blk = n_slots // MOE_BLOCK
    d = wg.shape[1]
    ff = wg.shape[2]

    def wspec(shape):
        return pl.BlockSpec((1,) + shape, lambda j, meta: (meta[1 + j], 0, 0))

    blk = (MOE_BLOCK * n_slabs, LANES)
    ys = pl.pallas_call(
        _experts_kernel,
        out_shape=jax.ShapeDtypeStruct((n_slots * n_slabs, LANES), F32),
        grid_spec=pltpu.PrefetchScalarGridSpec(
            num_scalar_prefetch=1,
            grid=(n_blk,),
            in_specs=[pl.BlockSpec(blk, lambda j, meta: (
                          jnp.minimum(j, jnp.maximum(meta[0] - 1, 0)), 0)),
                      wspec((d, ff)), wspec((d, ff)), wspec((ff, d))],
            out_specs=pl.BlockSpec(blk, lambda j, meta: (j, 0)),
            scratch_shapes=[pltpu.VMEM((d, ff), BF16), pltpu.VMEM((d, ff), BF16),
                            pltpu.VMEM((ff, d), BF16)],
        ),
        compiler_params=_params("arbitrary"),
        name="experts",
    )(meta, xs.reshape(n_slots * n_slabs, LANES), wg, wu, wd)
    return ys.reshape(xs.shape)


def _combine_kernel(dest_cur_ref, dest_next_ref, ys_hbm, x1_ref, gate_ref, rw_ref, o_ref,
                    buf, sem):
    i = pl.program_id(0)
    slot = i % 2
    tm = x1_ref.shape[0]
    groups = tm // SUBLANES

    def start(idx_ref, sl):
        for k in range(TOP_K):
            def body(g, carry, k=k):
                rows = [idx_ref[0, 0, k * tm + g * SUBLANES + u] for u in range(SUBLANES)]
                for u in range(SUBLANES):
                    pltpu.make_async_copy(ys_hbm.at[rows[u]], buf.at[sl, k, g, :, u, :],
                                          sem.at[sl, k]).start()
                return carry
            lax.fori_loop(0, groups, body, 0)

    def wait(sl, k):
        for u in range(SUBLANES):
            pltpu.make_async_copy(ys_hbm.at[pl.ds(0, groups)], buf.at[sl, k, :, :, u, :],
                                  sem.at[sl, k]).wait()

    @pl.when(i == 0)
    def _():
        start(dest_cur_ref, 0)

    @pl.when(i + 1 < pl.num_programs(0))
    def _():
        start(dest_next_ref, 1 - slot)

    rw = rw_ref[...]
    moe = None
    for k in range(TOP_K):
        wait(slot, k)
        rows = jnp.concatenate([buf[slot, k, :, c].reshape(tm, LANES)
                                for c in range(buf.shape[3])], axis=1)
        term = rw[:, k:k + 1] * rows
        moe = term if moe is None else moe + term
    o_ref[...] = x1_ref[...] + gate_ref[0] * moe


def _combine(dest_tiles, ys, x1, gate, rw):
    n_tok, d = x1.shape
    tm = TOKEN_TILE
    nt = n_tok // tm
    per_batch = nt // gate.shape[0]
    idx_blk = (1, 1, TOP_K * tm)
    return pl.pallas_call(
        _combine_kernel,
        out_shape=jax.ShapeDtypeStruct((n_tok, d), F32),
        grid=(nt,),
        in_specs=[
            pl.BlockSpec(idx_blk, lambda i: (i, 0, 0), memory_space=pltpu.SMEM),
            pl.BlockSpec(idx_blk, lambda i: (jnp.minimum(i + 1, nt - 1), 0, 0),
                         memory_space=pltpu.SMEM),
            pl.BlockSpec(memory_space=pl.ANY),
            pl.BlockSpec((tm, d), lambda i: (i, 0)),
            pl.BlockSpec((1, 1, d), lambda i: (i // per_batch, 0, 0)),
            pl.BlockSpec((tm, ROUTE_LANES), lambda i: (i, 0)),
        ],
        out_specs=pl.BlockSpec((tm, d), lambda i: (i, 0)),
        scratch_shapes=[pltpu.VMEM((2, TOP_K, tm // SUBLANES, d // LANES, SUBLANES, LANES), F32),
                        pltpu.SemaphoreType.DMA((2, TOP_K))],
        compiler_params=_params("arbitrary"),
        name="combine",
    )(dest_tiles, dest_tiles, ys, x1, gate, rw)


def _layer(x, ctx, c, c_ctx, w_ada, b_ada, norm1_g, norm2_g, w_in, q_norm_g, k_norm_g, attn_out_g,
           lb, hgrn_out_g, w_out, w_router_grp, b_router_grp, w_router_exp, b_router_exp,
           w_exp_gate, w_exp_up, w_exp_down):
    b, s, d = x.shape
    n_ctx = ctx.shape[1]
    assert n_ctx % TOKEN_TILE == 0 and s % TOKEN_TILE == 0 and s % GRID_W == 0
    assert n_ctx % HGRN_CHUNK == 0 and (n_ctx + s) % ATTN_KV_TILE == 0
    n_all = n_ctx + s

    cond = jnp.zeros((2 * SUBLANES, d), F32).at[:b].set(c).at[b].set(c_ctx)
    assert b + 1 <= cond.shape[0]
    mods = _adaln(cond, w_ada, b_ada)[:b + 1].reshape(b + 1, 1, 6, d)
    sh1, sc1, gt1, sh2, sc2, gt2 = (mods[:, :, m] for m in range(6))

    scale_q = ATTN_HEAD_DIM ** -0.5 * np.log2(np.e)
    qkg = jnp.concatenate([jnp.tile(q_norm_g, ATTN_HEADS) * scale_q,
                           jnp.tile(k_norm_g, ATTN_KV_HEADS)]).reshape(1, _QK_WIDTH)
    cos, sin = _rope_tables(n_ctx, s)
    qa, ka, vt, qr, ff, fb, ir, gr = _in_proj(
        ctx, x, sh1, sc1, norm1_g.reshape(1, d), w_in.astype(BF16), qkg, cos, sin)
    oa = _attention(qa, ka, vt)
    o_f, o_b = _hgrn(qr, ir, ff, fb, lb, n_ctx)

    w_router = jnp.zeros((d, LANES), F32)
    w_router = w_router.at[:, :N_EXPERTS].set(w_router_exp)
    w_router = w_router.at[:, _ROUTE_GROUP_LANE0:_ROUTE_GROUP_LANE0 + N_GROUPS].set(w_router_grp)
    b_router = jnp.zeros((1, LANES), F32)
    b_router = b_router.at[0, :N_EXPERTS].set(b_router_exp)
    b_router = b_router.at[0, _ROUTE_GROUP_LANE0:_ROUTE_GROUP_LANE0 + N_GROUPS].set(b_router_grp)
    wr_hi, wr_lo = _split_bf16(w_router)
    x1, h2, ri, rw, counts = _out_proj(
        oa, o_f, o_b, gr, x, gt1[:b], sh2[:b], sc2[:b], attn_out_g.reshape(1, -1),
        hgrn_out_g.reshape(1, -1), norm2_g.reshape(1, d), w_out.astype(BF16), wr_hi, wr_lo,
        b_router, n_ctx)

    n_tok = b * s
    counts = counts[0, :N_EXPERTS]
    padded = (counts + MOE_BLOCK - 1) // MOE_BLOCK * MOE_BLOCK
    pend = jnp.cumsum(padded)
    pstart = pend - padded
    n_blk = n_tok * TOP_K // MOE_BLOCK + N_EXPERTS
    def lookup(table, idx):
        hit = idx[..., None] == jnp.arange(table.shape[0], dtype=jnp.int32)
        return jnp.sum(jnp.where(hit, table.astype(jnp.int32), 0), axis=-1)

    dest = lookup(pstart, ri[:, :TOP_K]) + ri[:, TOP_K:2 * TOP_K]
    dest_tiles = dest.reshape(dest.shape[0], 1, -1)
    blk_start = jnp.arange(n_blk, dtype=jnp.int32) * MOE_BLOCK
    blk_e = jnp.minimum(jnp.sum(blk_start[:, None] >= pend[None, :], axis=1), N_EXPERTS - 1)
    n_used = pend[-1] // MOE_BLOCK
    meta = jnp.concatenate([n_used[None], blk_e]).astype(jnp.int32)

    n_slots = n_blk * MOE_BLOCK
    n_pad = n_slots - n_tok * TOP_K
    gap_start = jnp.concatenate([pstart + counts, pend[-1:]])
    gap_size = jnp.concatenate([padded - counts, n_slots - pend[-1:]])
    gap_end = jnp.cumsum(gap_size)
    pad_i = jnp.arange(n_pad, dtype=jnp.int32)
    gap = jnp.sum(pad_i[:, None] >= gap_end[None, :], axis=1)
    pad_slots = lookup(gap_start - (gap_end - gap_size), gap) + pad_i
    pad_tiles = pad_slots.astype(jnp.int32).reshape(dest_tiles.shape[0], 1, -1)

    xs = _dispatch(dest_tiles, pad_tiles, h2, n_slots)
    ys = _experts(meta, xs, w_exp_gate, w_exp_up, w_exp_down)
    out = _combine(dest_tiles, ys, x1.reshape(n_tok, d), gt2[:b], rw.reshape(n_tok, ROUTE_LANES))
    return out.reshape(b, s, d)


def kernel(x, c, ctx, c_ctx, w_ada, b_ada, norm1_g, norm2_g, w_in, q_norm_g, k_norm_g, attn_out_g,
           hgrn_lb, hgrn_out_g, w_out, w_router_grp, b_router_grp, w_router_exp, b_router_exp,
           w_exp_gate, w_exp_up, w_exp_down):
    depth = w_in.shape[0]
    assert depth == 1, "context stream update between layers is not implemented"
    lb_all = jnp.cumsum(jax.nn.softmax(hgrn_lb.astype(F32), axis=1), axis=1)
    layer = 0
    return _layer(x, ctx, c, c_ctx, w_ada[layer], b_ada[layer], norm1_g[layer], norm2_g[layer],
                  w_in[layer], q_norm_g[layer], k_norm_g[layer], attn_out_g[layer],
                  lb_all[:, layer], hgrn_out_g[layer], w_out[layer], w_router_grp[layer],
                  b_router_grp[layer], w_router_exp[layer], b_router_exp[layer],
                  w_exp_gate[layer], w_exp_up[layer], w_exp_down[layer])
```

```python
import functools

import numpy as np
import jax
import jax.numpy as jnp
from jax import lax
from jax.experimental import pallas as pl
from jax.experimental.pallas import tpu as pltpu

F32 = jnp.float32
BF16 = jnp.bfloat16

GRID_W = 64
EPS = 1e-6
ATTN_HEADS = 8
ATTN_KV_HEADS = 2
ATTN_HEAD_DIM = 64
ATTN_GROUP = ATTN_HEADS // ATTN_KV_HEADS
ATTN_WIDTH = ATTN_HEADS * ATTN_HEAD_DIM
KV_WIDTH = ATTN_KV_HEADS * ATTN_HEAD_DIM
ROPE_THETA = 10000.0
HGRN_HEADS = 4
HGRN_HEAD_DIM = 128
HGRN_WIDTH = HGRN_HEADS * HGRN_HEAD_DIM
N_GROUPS = 4
EXPERTS_PER_GROUP = 8
N_EXPERTS = N_GROUPS * EXPERTS_PER_GROUP
TOP_K = 2
EXPERT_FF = 512

LANES = 128
SUBLANES = 8
MXU_DIM = 256
VMEM_LIMIT_BYTES = 48 * 1024 * 1024

TOKEN_TILE = 256
OUT_PROJ_SPLIT = 2
ATTN_Q_TILE = 2048
ATTN_KV_TILE = 256
ATTN_KV_UNROLL = 17
ATTN_BOUND_SLACK = 1.02
ATTN_BOUND_MAX = 60.0
HGRN_CHUNK = 64
HGRN_STEP_CHUNKS = 4
MOE_BLOCK = 256
GATHER_UNROLL = 8
ROUTE_LANES = 8
NEG_BIG = -1e30

_QA0 = 0
_KA0 = _QA0 + ATTN_WIDTH
_VA0 = _KA0 + KV_WIDTH
_QR0 = _VA0 + KV_WIDTH
_FF0 = _QR0 + HGRN_WIDTH
_FB0 = _FF0 + HGRN_WIDTH
_IR0 = _FB0 + HGRN_WIDTH
_GR0 = _IR0 + HGRN_WIDTH
_QK_WIDTH = ATTN_WIDTH + KV_WIDTH


def _dot(a, b):
    return jnp.dot(a, b, preferred_element_type=F32)


def _dot_nt(a, b):
    return lax.dot_general(a, b, (((1,), (1,)), ((), ())), preferred_element_type=F32)


def _dot_tn(a, b):
    return lax.dot_general(a, b, (((0,), (0,)), ((), ())), preferred_element_type=F32)


def _split_bf16(x):
    hi = x.astype(BF16)
    lo = (x - hi.astype(F32)).astype(BF16)
    return hi, lo


def _sigmoid(x):
    return 1.0 / (1.0 + jnp.exp(-x))


def _silu(x):
    return x * _sigmoid(x)


def _params(*sem):
    return pltpu.CompilerParams(dimension_semantics=sem, vmem_limit_bytes=VMEM_LIMIT_BYTES)


def _head_sum_matrix():
    idx = np.arange(LANES) // ATTN_HEAD_DIM
    return jnp.asarray(idx[:, None] == idx[None, :], dtype=BF16)


def _head_rms_scale(x, ones_bd):
    ssq = _dot((x * x).astype(BF16), ones_bd)
    return lax.rsqrt(ssq * (1.0 / ATTN_HEAD_DIM) + EPS)


def _adaln_kernel(cond_ref, w_ref, b_ref, o_ref):
    s = _silu(cond_ref[...])
    s_hi, s_lo = _split_bf16(s)
    w_hi, w_lo = _split_bf16(w_ref[...])
    o_ref[...] = _dot(s_hi, w_hi) + _dot(s_lo, w_hi) + _dot(s_hi, w_lo) + b_ref[...]


def _adaln(cond, w_ada, b_ada):
    rows, d = cond.shape
    n = w_ada.shape[1]
    tn = n // 6
    return pl.pallas_call(
        _adaln_kernel,
        out_shape=jax.ShapeDtypeStruct((rows, n), F32),
        grid=(n // tn,),
        in_specs=[pl.BlockSpec((rows, d), lambda j: (0, 0)),
                  pl.BlockSpec((d, tn), lambda j: (0, j)),
                  pl.BlockSpec((1, tn), lambda j: (0, j))],
        out_specs=pl.BlockSpec((rows, tn), lambda j: (0, j)),
        compiler_params=_params("arbitrary"),
        name="adaln",
    )(cond, w_ada, b_ada.reshape(1, n))


def _rope_tables(n_ctx, n_lat):
    half = ATTN_HEAD_DIM // 2
    freqs = ROPE_THETA ** (-np.arange(0, half, 2, dtype=np.float64) / half)
    tok = np.arange(n_lat)
    pos = np.stack([tok // GRID_W, tok % GRID_W], axis=1).astype(np.float64)
    lane = np.arange(ATTN_HEAD_DIM)
    axis = lane // half
    fi = (lane % half) // 2
    ang = pos[:, axis] * freqs[fi][None, :]
    sign = np.where(lane % 2 == 1, 1.0, -1.0)
    cos = np.concatenate([np.ones((n_ctx, ATTN_HEAD_DIM)), np.cos(ang)], axis=0)
    sin = np.concatenate([np.zeros((n_ctx, ATTN_HEAD_DIM)), np.sin(ang) * sign], axis=0)
    reps = LANES // ATTN_HEAD_DIM
    return (jnp.asarray(np.tile(cos, (1, reps)), F32), jnp.asarray(np.tile(sin, (1, reps)), F32))


def _in_proj_kernel(n_ctx_tiles, ctx_ref, x_ref, shift_ref, scale_ref, g1_ref, w_ref, qkg_ref,
                    ones_ref, cos_ref, sin_ref,
                    qa_ref, ka_ref, vt_ref, qr_ref, ff_ref, fb_ref, ir_ref, gr_ref):
    i = pl.program_id(1)
    xt = jnp.where(i < n_ctx_tiles, ctx_ref[0], x_ref[0])
    ms = jnp.mean(xt * xt, axis=-1, keepdims=True)
    h = xt * lax.rsqrt(ms + EPS) * g1_ref[...]
    h = h * (1.0 + scale_ref[0]) + shift_ref[0]
    p = _dot(h.astype(BF16), w_ref[...])

    ones_bd = ones_ref[...]
    cos = cos_ref[...]
    sin = sin_ref[...]
    even = lax.broadcasted_iota(jnp.int32, cos.shape, 1) % 2 == 0
    slabs = []
    for c0 in range(0, _QK_WIDTH, LANES):
        t = p[:, _QA0 + c0:_QA0 + c0 + LANES]
        t = t * _head_rms_scale(t, ones_bd) * qkg_ref[:, c0:c0 + LANES]
        partner = jnp.where(even, pltpu.roll(t, LANES - 1, 1), pltpu.roll(t, 1, 1))
        slabs.append((t * cos + partner * sin).astype(BF16))
    qa_ref[0] = jnp.concatenate(slabs[:ATTN_WIDTH // LANES], axis=1)
    k_all = jnp.concatenate(slabs[ATTN_WIDTH // LANES:], axis=1)
    vt_all = p[:, _VA0:_VA0 + KV_WIDTH].T
    ones = jnp.ones((vt_ref.shape[2] - ATTN_HEAD_DIM, vt_all.shape[1]), F32)
    for hd in range(ATTN_KV_HEADS):
        cols = slice(hd * ATTN_HEAD_DIM, (hd + 1) * ATTN_HEAD_DIM)
        ka_ref[0, hd] = k_all[:, cols]
        vt_ref[0, hd] = jnp.concatenate([vt_all[cols], ones], axis=0).astype(BF16)

    qr_ref[0] = (_silu(p[:, _QR0:_QR0 + HGRN_WIDTH]) * (HGRN_HEAD_DIM ** -0.5)).astype(BF16)
    ff_ref[0] = p[:, _FF0:_FF0 + HGRN_WIDTH]
    fb_ref[0] = p[:, _FB0:_FB0 + HGRN_WIDTH]
    ir_ref[0] = p[:, _IR0:_IR0 + HGRN_WIDTH].astype(BF16)
    gr_ref[0] = _silu(p[:, _GR0:_GR0 + HGRN_WIDTH]).astype(BF16)


def _in_proj(ctx, x, shift, scale, g1, w_in, qkg, cos, sin):
    b, n_ctx, d = ctx.shape
    s = x.shape[1]
    tm = TOKEN_TILE
    nct = n_ctx // tm
    n_all = n_ctx + s
    nt = n_all // tm
    pw = w_in.shape[1]
    ones_bd = _head_sum_matrix()

    def tok_spec(w):
        return pl.BlockSpec((1, tm, w), lambda bi, i: (bi, i, 0))

    mod_spec = pl.BlockSpec((1, 1, d), lambda bi, i: (jnp.where(i < nct, b, bi), 0, 0))
    outs = [(HGRN_WIDTH, BF16), (HGRN_WIDTH, F32), (HGRN_WIDTH, F32), (HGRN_WIDTH, BF16),
            (HGRN_WIDTH, BF16)]
    vt_rows = ATTN_HEAD_DIM + 2 * SUBLANES
    lat_spec = pl.BlockSpec((1, tm, ATTN_WIDTH), lambda bi, i: (bi, jnp.maximum(i - nct, 0), 0))
    k_spec = pl.BlockSpec((1, ATTN_KV_HEADS, tm, ATTN_HEAD_DIM), lambda bi, i: (bi, 0, i, 0))
    vt_spec = pl.BlockSpec((1, ATTN_KV_HEADS, vt_rows, tm), lambda bi, i: (bi, 0, 0, i))
    return pl.pallas_call(
        functools.partial(_in_proj_kernel, nct),
        out_shape=[jax.ShapeDtypeStruct((b, s, ATTN_WIDTH), BF16),
                   jax.ShapeDtypeStruct((b, ATTN_KV_HEADS, n_all, ATTN_HEAD_DIM), BF16),
                   jax.ShapeDtypeStruct((b, ATTN_KV_HEADS, vt_rows, n_all), BF16)]
        + [jax.ShapeDtypeStruct((b, n_all, w), dt) for w, dt in outs],
        grid=(b, nt),
        in_specs=[
            pl.BlockSpec((1, tm, d), lambda bi, i: (bi, jnp.minimum(i, nct - 1), 0)),
            pl.BlockSpec((1, tm, d), lambda bi, i: (bi, jnp.maximum(i - nct, 0), 0)),
            mod_spec, mod_spec,
            pl.BlockSpec((1, d), lambda bi, i: (0, 0)),
            pl.BlockSpec((d, pw), lambda bi, i: (0, 0)),
            pl.BlockSpec((1, _QK_WIDTH), lambda bi, i: (0, 0)),
            pl.BlockSpec((LANES, LANES), lambda bi, i: (0, 0)),
            pl.BlockSpec((tm, LANES), lambda bi, i: (i, 0)),
            pl.BlockSpec((tm, LANES), lambda bi, i: (i, 0)),
        ],
        out_specs=[lat_spec, k_spec, vt_spec] + [tok_spec(w) for w, _ in outs],
        compiler_params=_params("arbitrary", "arbitrary"),
        name="in_proj",
    )(ctx, x, shift, scale, g1, w_in, qkg, ones_bd, cos, sin)


def _attention_kernel(n_kv_tiles, q_ref, k_ref, vt_ref, o_ref, qs_ref, s_ref, ksq_ref):
    tq = q_ref.shape[1]
    cols = ATTN_GROUP * tq
    for h in range(ATTN_GROUP):
        qs_ref[h * tq:(h + 1) * tq, :] = q_ref[0, :, h * ATTN_HEAD_DIM:(h + 1) * ATTN_HEAD_DIM]

    @pl.when(pl.program_id(2) == 0)
    def _():
        k = k_ref[0, 0].astype(F32)
        ksq = jnp.max(jnp.sum(k * k, axis=1, keepdims=True), axis=0, keepdims=True)
        ksq_ref[...] = jnp.broadcast_to(ksq, ksq_ref.shape)

    def tile(j):
        return pl.ds(pl.multiple_of(j * ATTN_KV_TILE, ATTN_KV_TILE), ATTN_KV_TILE)

    def scores(j):
        return _dot_nt(k_ref[0, 0, tile(j), :], qs_ref[...])

    def weighted(j, p):
        return _dot(vt_ref[0, 0, :, tile(j)], p)

    def finish(acc):
        o = acc[:ATTN_HEAD_DIM] / acc[ATTN_HEAD_DIM:ATTN_HEAD_DIM + 1]
        o_ref[0] = jnp.concatenate(
            [o[:, h * tq:(h + 1) * tq].T for h in range(ATTN_GROUP)], axis=1).astype(o_ref.dtype)

    qf = qs_ref[...].astype(F32)
    qsq = _dot_nt(jnp.ones((SUBLANES, ATTN_HEAD_DIM), BF16), (qf * qf).astype(BF16))[0:1]
    bound = jnp.sqrt(qsq * ksq_ref[0:1, 0:1]) * ATTN_BOUND_SLACK
    acc0 = jnp.zeros((vt_ref.shape[2], cols), F32)
    safe = jnp.max(bound) <= ATTN_BOUND_MAX

    @pl.when(safe)
    def _():
        def absorb(j, acc):
            return acc + weighted(j, jnp.exp2(scores(j) - bound).astype(BF16))

        def group(i, acc):
            for u in range(ATTN_KV_UNROLL):
                acc = absorb(ATTN_KV_UNROLL * i + u, acc)
            return acc

        n_groups = n_kv_tiles // ATTN_KV_UNROLL
        acc = lax.fori_loop(0, n_groups, group, acc0)
        for j in range(n_groups * ATTN_KV_UNROLL, n_kv_tiles):
            acc = absorb(j, acc)
        finish(acc)

    @pl.when(jnp.logical_not(safe))
    def _():
        def score(j, slot):
            s_ref[slot] = scores(j)

        def absorb(j, slot, carry):
            m, acc = carry
            s = s_ref[slot]
            m_new = jnp.maximum(m, jnp.max(s, axis=0, keepdims=True))
            p = jnp.exp2(s - m_new).astype(BF16)
            return m_new, jnp.exp2(m - m_new) * acc + weighted(j, p)

        def pair(i, carry):
            score(2 * i + 1, 1)
            carry = absorb(2 * i, 0, carry)
            score(2 * i + 2, 0)
            return absorb(2 * i + 1, 1, carry)

        n_pairs = (n_kv_tiles - 1) // 2
        score(0, 0)
        carry = lax.fori_loop(0, n_pairs, pair, (jnp.full((1, cols), NEG_BIG, F32), acc0))
        if n_kv_tiles % 2 == 1:
            carry = absorb(n_kv_tiles - 1, 0, carry)
        else:
            score(n_kv_tiles - 1, 1)
            carry = absorb(n_kv_tiles - 2, 0, carry)
            carry = absorb(n_kv_tiles - 1, 1, carry)
        finish(carry[1])


def _attention(qa, k_heads, vt_heads):
    b, s, _ = qa.shape
    n_all = k_heads.shape[2]
    tq = ATTN_Q_TILE
    assert s % tq == 0 and n_all % ATTN_KV_TILE == 0
    gw = ATTN_GROUP * ATTN_HEAD_DIM
    vt_rows = vt_heads.shape[2]
    return pl.pallas_call(
        functools.partial(_attention_kernel, n_all // ATTN_KV_TILE),
        out_shape=jax.ShapeDtypeStruct((b, s, ATTN_WIDTH), BF16),
        grid=(b, ATTN_KV_HEADS, s // tq),
        in_specs=[pl.BlockSpec((1, tq, gw), lambda bi, kv, i: (bi, i, kv)),
                  pl.BlockSpec((1, 1, n_all, ATTN_HEAD_DIM), lambda bi, kv, i: (bi, kv, 0, 0)),
                  pl.BlockSpec((1, 1, vt_rows, n_all), lambda bi, kv, i: (bi, kv, 0, 0))],
        out_specs=pl.BlockSpec((1, tq, gw), lambda bi, kv, i: (bi, i, kv)),
        scratch_shapes=[pltpu.VMEM((ATTN_GROUP * tq, ATTN_HEAD_DIM), BF16),
                        pltpu.VMEM((2, ATTN_KV_TILE, ATTN_GROUP * tq), F32),
                        pltpu.VMEM((SUBLANES, LANES), F32)],
        compiler_params=_params("arbitrary", "arbitrary", "arbitrary"),
        name="attention",
    )(qa, k_heads, vt_heads)


def _hgrn_masks(reverse):
    c = HGRN_CHUNK
    t = lax.broadcasted_iota(jnp.int32, (c, c), 0)
    u = lax.broadcasted_iota(jnp.int32, (c, c), 1)
    tri = (u >= t) if reverse else (u <= t)
    levels = []
    size = c // 2
    while size >= 1:
        t_late = (t // size) % 2 == 1
        u_late = (u // size) % 2 == 1
        halves = (u_late & jnp.logical_not(t_late)) if reverse else (
            t_late & jnp.logical_not(u_late))
        region = ((t // (2 * size)) == (u // (2 * size))) & halves
        levels.append((size, region if 2 * size < c else None))
        size //= 2
    levels.append((0, t == u))
    return tri.astype(BF16), levels


def _hgrn_chunk(q, fr, v, lb, state, reverse, tri, levels):
    c = HGRN_CHUNK
    coarse_levels = [lv for lv in levels if lv[0] >= SUBLANES]
    fine_levels = [lv for lv in levels if lv[0] < SUBLANES]
    f = lb + (1.0 - lb) * _sigmoid(fr)
    k = 1.0 - f
    g_hi, g_lo = _split_bf16(jnp.log2(f))
    bcum = _dot(tri, g_hi) + _dot(tri, g_lo)
    qf = q.astype(F32)
    yield None

    st = state()
    end = 0 if reverse else c - 1
    b_end = bcum[end:end + 1, :]
    inter = _dot_nt((qf * jnp.exp2(bcum)).astype(BF16), st.astype(BF16))
    ke = (k * jnp.exp2(b_end - bcum)).astype(BF16)
    st_add = _dot_tn(v, ke)

    products = []
    for size, mask in coarse_levels:
        q_rows, k_rows = [], []
        zeros = jnp.zeros((size, qf.shape[1]), BF16)
        for p0 in range(0, c, 2 * size):
            early = slice(p0, p0 + size)
            late = slice(p0 + size, p0 + 2 * size)
            r = p0 + size if reverse else p0 + size - 1
            ref = bcum[r:r + 1, :]
            q_sl, k_sl = (early, late) if reverse else (late, early)
            q_blk = (qf[q_sl] * jnp.exp2(bcum[q_sl] - ref)).astype(BF16)
            k_blk = (k[k_sl] * jnp.exp2(ref - bcum[k_sl])).astype(BF16)
            q_rows += [q_blk, zeros] if reverse else [zeros, q_blk]
            k_rows += [zeros, k_blk] if reverse else [k_blk, zeros]
        products.append((mask, _dot_nt(jnp.concatenate(q_rows, axis=0),
                                       jnp.concatenate(k_rows, axis=0))))
    yield None

    row = lax.broadcasted_iota(jnp.int32, (c, 1), 0)
    sub = lax.broadcasted_iota(jnp.int32, (SUBLANES, 1), 0)
    fine_products = []
    for size, mask in fine_levels:
        if size == 0:
            q_l, k_l = q, k.astype(BF16)
        else:
            late = (row // size) % 2 == 1
            q_side = jnp.logical_not(late) if reverse else late
            if size == 1:
                fac_q, fac_k = f, None
            else:
                groups = []
                for r0 in range(0, c, SUBLANES):
                    ref = None
                    for p0 in range(0, SUBLANES, 2 * size):
                        r = r0 + p0 + (size if reverse else size - 1)
                        piece = jnp.broadcast_to(bcum[r:r + 1, :], (SUBLANES, bcum.shape[1]))
                        ref = piece if ref is None else jnp.where(sub >= p0, piece, ref)
                    groups.append(ref)
                ref = jnp.concatenate(groups, axis=0)
                fac_q = fac_k = jnp.exp2(jnp.where(q_side, bcum - ref, ref - bcum))
            q_l = jnp.where(q_side, qf * fac_q, 0.0).astype(BF16)
            k_l = jnp.where(q_side, 0.0, k if fac_k is None else k * fac_k).astype(BF16)
        fine_products.append((mask, _dot_nt(q_l, k_l)))
    a = None
    for mask, a_l in products:
        a = a_l if mask is None else jnp.where(mask, a_l, a)
    yield None

    for mask, a_l in fine_products:
        a = jnp.where(mask, a_l, a)
    intra = _dot(a.astype(BF16), v)
    st_new = st * jnp.exp2(b_end) + st_add
    yield None

    yield inter + intra, st_new


_HGRN_STAGES = 5


def _hgrn_kernel(qf_ref, if_ref, ff_ref, qb_ref, ib_ref, fb_ref, lb_ref, of_ref, ob_ref,
                 sf_ref, sb_ref):
    @pl.when(pl.program_id(1) == 0)
    def _():
        sf_ref[...] = jnp.zeros_like(sf_ref)
        sb_ref[...] = jnp.zeros_like(sb_ref)

    dirs = ((False, qf_ref, if_ref, ff_ref, of_ref, sf_ref),
            (True, qb_ref, ib_ref, fb_ref, ob_ref, sb_ref))
    masks = [_hgrn_masks(reverse) for reverse, *_ in dirs]
    n_sub = qf_ref.shape[1] // HGRN_CHUNK
    latest = {}
    chunks = []
    for t in range(n_sub):
        for h in range(HGRN_HEADS):
            sl = slice(h * HGRN_HEAD_DIM, (h + 1) * HGRN_HEAD_DIM)
            for d, (reverse, q_ref, i_ref, f_ref, o_ref, s_ref) in enumerate(dirs):
                r0 = (n_sub - 1 - t if reverse else t) * HGRN_CHUNK
                rows = slice(r0, r0 + HGRN_CHUNK)

                def state(h=h, d=d, s_ref=s_ref):
                    return latest[h, d] if (h, d) in latest else s_ref[h]

                gen = _hgrn_chunk(q_ref[0, rows, sl], f_ref[0, rows, sl], i_ref[0, rows, sl],
                                  lb_ref[d:d + 1, sl], state, reverse, *masks[d])
                chunks.append((gen, o_ref, s_ref, h, d, rows, sl, t == n_sub - 1))

    assert len(dirs) * HGRN_HEADS >= _HGRN_STAGES
    for step in range(len(chunks) + _HGRN_STAGES - 1):
        for stage in range(_HGRN_STAGES):
            idx = step - stage
            if 0 <= idx < len(chunks):
                gen, o_ref, s_ref, h, d, rows, sl, is_last = chunks[idx]
                result = next(gen)
                if stage == _HGRN_STAGES - 1:
                    o, st = result
                    o_ref[0, rows, sl] = o
                    latest[h, d] = st
                    if is_last:
                        s_ref[h] = st


def _hgrn(qr, ir, ff, fb, lb, n_ctx):
    b, n_all, w = qr.shape
    c = HGRN_CHUNK * HGRN_STEP_CHUNKS
    assert n_ctx % c == 0 and n_all % c == 0
    nc = n_all // c
    ncc = n_ctx // c

    def fwd(bi, j):
        return (bi, j, 0)

    def bwd(bi, j):
        return (bi, jnp.where(j < ncc, ncc - 1 - j, nc - 1 - (j - ncc)), 0)

    blk = (1, c, w)
    state = pltpu.VMEM((HGRN_HEADS, HGRN_HEAD_DIM, HGRN_HEAD_DIM), F32)
    return pl.pallas_call(
        _hgrn_kernel,
        out_shape=[jax.ShapeDtypeStruct((b, n_all, w), F32)] * 2,
        grid=(b, nc),
        in_specs=[pl.BlockSpec(blk, fwd), pl.BlockSpec(blk, fwd), pl.BlockSpec(blk, fwd),
                  pl.BlockSpec(blk, bwd), pl.BlockSpec(blk, bwd), pl.BlockSpec(blk, bwd),
                  pl.BlockSpec((2, w), lambda bi, j: (0, 0))],
        out_specs=[pl.BlockSpec(blk, fwd), pl.BlockSpec(blk, bwd)],
        scratch_shapes=[state, state],
        compiler_params=_params("arbitrary", "arbitrary"),
        name="hgrn",
    )(qr, ir, ff, qr, ir, fb, lb)


_ROUTE_GROUP_LANE0 = N_EXPERTS


def _lane_min_index(cond, lane):
    return jnp.min(jnp.where(cond, lane, LANES), axis=-1, keepdims=True)


def _out_proj_kernel(oa_ref, of_ref, ob_ref, gr_ref, x_ref, gate_ref, shift_ref, scale_ref,
                     ag_ref, hg_ref, g2_ref, w_ref, ones_ref, wr_hi_ref, wr_lo_ref, br_ref,
                     x1_ref, h2_ref, ri_ref, rw_ref, cnt_ref, carry_ref):
    first = (pl.program_id(0) == 0) & (pl.program_id(1) == 0)

    @pl.when(first)
    def _():
        carry_ref[...] = jnp.zeros_like(carry_ref)

    tm = x_ref.shape[1]
    hm = tm // OUT_PROJ_SPLIT
    ones_bd = ones_ref[...]
    counts = [carry_ref[0:1, :]]

    def rows_chain(r0):
        rs = slice(r0, r0 + hm)
        ts = [oa_ref[0, rs, c0:c0 + LANES].astype(F32) for c0 in range(0, ATTN_WIDTH, LANES)]
        scales = [_head_rms_scale(t, ones_bd) for t in ts]
        orr = of_ref[0, rs] + ob_ref[0, rs]
        yield

        slabs = [(t * sc * ag_ref[:, c * LANES:(c + 1) * LANES]).astype(BF16)
                 for c, (t, sc) in enumerate(zip(ts, scales))]
        parts = []
        for h in range(HGRN_HEADS):
            oh = orr[:, h * HGRN_HEAD_DIM:(h + 1) * HGRN_HEAD_DIM]
            parts.append(oh * lax.rsqrt(jnp.mean(oh * oh, axis=-1, keepdims=True) + EPS))
        orn = jnp.concatenate(parts, axis=1) * hg_ref[...] * gr_ref[0, rs].astype(F32)
        mix = _dot(jnp.concatenate(slabs + [orn.astype(BF16)], axis=1), w_ref[...])
        yield

        x1 = x_ref[0, rs] + gate_ref[0] * mix
        x1_ref[0, rs] = x1
        h2 = x1 * lax.rsqrt(jnp.mean(x1 * x1, axis=-1, keepdims=True) + EPS) * g2_ref[...]
        h2 = h2 * (1.0 + scale_ref[0]) + shift_ref[0]
        gs = slice(r0 // SUBLANES, (r0 + hm) // SUBLANES)
        for c in range(h2_ref.shape[1]):
            h2_ref[gs, c] = h2[:, c * LANES:(c + 1) * LANES].reshape(hm // SUBLANES, SUBLANES,
                                                                     LANES)
        h_hi, h_lo = _split_bf16(h2)
        logits = (_dot(h_hi, wr_hi_ref[...]) + _dot(h_lo, wr_hi_ref[...])
                  + _dot(h_hi, wr_lo_ref[...]) + br_ref[...])
        yield

        lane = lax.broadcasted_iota(jnp.int32, logits.shape, 1)
        is_grp = (lane >= _ROUTE_GROUP_LANE0) & (lane < _ROUTE_GROUP_LANE0 + N_GROUPS)
        lg = jnp.where(is_grp, logits, NEG_BIG)
        mg = jnp.max(lg, axis=-1, keepdims=True)
        g_sel = _lane_min_index(lg == mg, lane) - _ROUTE_GROUP_LANE0
        pg_top = 1.0 / jnp.sum(jnp.exp(lg - mg), axis=-1, keepdims=True)
        in_grp = (lane < N_EXPERTS) & ((lane // EXPERTS_PER_GROUP) == g_sel)
        le = jnp.where(in_grp, logits, NEG_BIG)
        m1 = jnp.max(le, axis=-1, keepdims=True)
        e1 = _lane_min_index(le == m1, lane)
        le2 = jnp.where(lane == e1, NEG_BIG, le)
        m2 = jnp.max(le2, axis=-1, keepdims=True)
        e2 = _lane_min_index(le2 == m2, lane)
        r2 = jnp.exp(m2 - m1)
        w1 = pg_top / (1.0 + r2)
        w2 = pg_top * r2 / (1.0 + r2)
        onehot = ((lane == e1) | (lane == e2)).astype(BF16)
        rt = lax.broadcasted_iota(jnp.int32, (hm, hm), 0)
        ru = lax.broadcasted_iota(jnp.int32, (hm, hm), 1)
        before_local = _dot((ru < rt).astype(BF16), onehot)
        yield

        before = before_local + counts[-1]
        rank1 = jnp.sum(jnp.where(lane == e1, before, 0.0), axis=-1, keepdims=True)
        rank2 = jnp.sum(jnp.where(lane == e2, before, 0.0), axis=-1, keepdims=True)
        counts.append(counts[-1] + jnp.sum(onehot.astype(F32), axis=0, keepdims=True))
        rec = jnp.where(lane == 0, e1.astype(F32), jnp.where(lane == 1, e2.astype(F32), jnp.where(
            lane == 2, rank1, jnp.where(lane == 3, rank2, 0.0))))
        ri_ref[0, :, rs] = rec.T[:ROUTE_LANES].astype(jnp.int32)
        rl = lax.broadcasted_iota(jnp.int32, (hm, ROUTE_LANES), 1)
        rw_ref[0, rs] = jnp.where(rl == 0, w1, jnp.where(rl == 1, w2, 0.0))
        yield

    n_stages = 5
    chains = [rows_chain(r0) for r0 in range(0, tm, hm)]
    for step in range(len(chains) + n_stages - 1):
        for stage in range(n_stages):
            idx = step - stage
            if 0 <= idx < len(chains):
                next(chains[idx])
    carry_ref[...] = jnp.broadcast_to(counts[-1], carry_ref.shape)
    cnt_ref[...] = jnp.broadcast_to(counts[-1], cnt_ref.shape).astype(jnp.int32)


def _out_proj(oa, o_f, o_b, gr, x, gate, shift, scale, ag, hg, g2, w_out, wr_hi, wr_lo, br, n_ctx):
    b, s, d = x.shape
    tm = TOKEN_TILE
    off = n_ctx // tm
    ones_bd = _head_sum_matrix()

    def lat(w):
        return pl.BlockSpec((1, tm, w), lambda bi, i: (bi, i, 0))

    def allrows(w):
        return pl.BlockSpec((1, tm, w), lambda bi, i: (bi, i + off, 0))

    def mod():
        return pl.BlockSpec((1, 1, d), lambda bi, i: (bi, 0, 0))

    def const(shape):
        return pl.BlockSpec(shape, lambda bi, i: (0,) * len(shape))

    return pl.pallas_call(
        _out_proj_kernel,
        out_shape=[jax.ShapeDtypeStruct((b, s, d), F32),
                   jax.ShapeDtypeStruct((b * s // SUBLANES, d // LANES, SUBLANES, LANES), F32),
                   jax.ShapeDtypeStruct((b * s // tm, ROUTE_LANES, tm), jnp.int32),
                   jax.ShapeDtypeStruct((b, s, ROUTE_LANES), F32),
                   jax.ShapeDtypeStruct((SUBLANES, LANES), jnp.int32)],
        grid=(b, s // tm),
        in_specs=[lat(ATTN_WIDTH), allrows(HGRN_WIDTH), allrows(HGRN_WIDTH), allrows(HGRN_WIDTH),
                  lat(d), mod(), mod(), mod(),
                  const((1, ATTN_WIDTH)), const((1, HGRN_WIDTH)), const((1, d)),
                  const((ATTN_WIDTH + HGRN_WIDTH, d)), const((LANES, LANES)),
                  const((d, LANES)), const((d, LANES)), const((1, LANES))],
        out_specs=[lat(d),
                   pl.BlockSpec((tm // SUBLANES, d // LANES, SUBLANES, LANES),
                                lambda bi, i: (bi * (s // tm) + i, 0, 0, 0)),
                   pl.BlockSpec((1, ROUTE_LANES, tm), lambda bi, i: (bi * (s // tm) + i, 0, 0)),
                   lat(ROUTE_LANES), const((SUBLANES, LANES))],
        scratch_shapes=[pltpu.VMEM((SUBLANES, LANES), F32)],
        compiler_params=_params("arbitrary", "arbitrary"),
        name="out_proj",
    )(oa, o_f, o_b, gr, x, gate, shift, scale, ag, hg, g2, w_out, ones_bd, wr_hi, wr_lo, br)


_DISPATCH_SLOTS = 3


def _dispatch_kernel(dest_ref, pad_ref, h2_hbm, xs_hbm, buf, zero_row, sem_in, sem_out):
    i = pl.program_id(0)
    n = pl.num_programs(0)
    groups = buf.shape[1]
    tm = groups * SUBLANES
    n_pad = pad_ref.shape[2]
    slot = i % _DISPATCH_SLOTS

    def fetch(t, sl):
        return pltpu.make_async_copy(h2_hbm.at[pl.ds(t * groups, groups)], buf.at[sl],
                                     sem_in.at[sl])

    def wait_rows(sl):
        for _ in range(TOP_K + n_pad // tm):
            pltpu.make_async_copy(h2_hbm.at[pl.ds(0, groups)], buf.at[sl], sem_out.at[sl]).wait()
        rest = (n_pad % tm) // SUBLANES
        if rest:
            pltpu.make_async_copy(h2_hbm.at[pl.ds(0, rest)], buf.at[sl, pl.ds(0, rest)],
                                  sem_out.at[sl]).wait()

    @pl.when(i == 0)
    def _():
        zero_row[...] = jnp.zeros_like(zero_row)
        fetch(0, 0).start()

    @pl.when(i >= 2)
    def _():
        wait_rows((i + 1) % _DISPATCH_SLOTS)

    @pl.when(i + 1 < n)
    def _():
        fetch(i + 1, (i + 1) % _DISPATCH_SLOTS).start()

    fetch(i, slot).wait()

    def body(g, carry):
        rows = [[dest_ref[0, 0, k * tm + g * SUBLANES + u] for k in range(TOP_K)]
                for u in range(SUBLANES)]
        for u in range(SUBLANES):
            for k in range(TOP_K):
                pltpu.make_async_copy(buf.at[slot, g, :, u, :], xs_hbm.at[rows[u][k]],
                                      sem_out.at[slot]).start()
        return carry
    lax.fori_loop(0, groups, body, 0)

    def pad_body(g, carry):
        rows = [pad_ref[0, 0, g * SUBLANES + u] for u in range(SUBLANES)]
        for u in range(SUBLANES):
            pltpu.make_async_copy(zero_row, xs_hbm.at[rows[u]], sem_out.at[slot]).start()
        return carry
    lax.fori_loop(0, n_pad // SUBLANES, pad_body, 0)

    @pl.when(i == n - 1)
    def _():
        wait_rows(slot)

    @pl.when((i == n - 1) & (i >= 1))
    def _():
        wait_rows((i - 1) % _DISPATCH_SLOTS)


def _dispatch(dest_tiles, pad_tiles, h2_tiles, n_slots):
    nt = dest_tiles.shape[0]
    n_groups, n_slabs, _, _ = h2_tiles.shape
    groups = n_groups // nt
    assert pad_tiles.shape[2] % SUBLANES == 0

    def idx_spec(arr):
        return pl.BlockSpec((1, 1, arr.shape[2]), lambda i: (i, 0, 0), memory_space=pltpu.SMEM)

    return pl.pallas_call(
        _dispatch_kernel,
        out_shape=jax.ShapeDtypeStruct((n_slots, n_slabs, LANES), F32),
        grid=(nt,),
        in_specs=[idx_spec(dest_tiles), idx_spec(pad_tiles), pl.BlockSpec(memory_space=pl.ANY)],
        out_specs=pl.BlockSpec(memory_space=pl.ANY),
        scratch_shapes=[pltpu.VMEM((_DISPATCH_SLOTS, groups, n_slabs, SUBLANES, LANES), F32),
                        pltpu.VMEM((n_slabs, LANES), F32),
                        pltpu.SemaphoreType.DMA((_DISPATCH_SLOTS,)),
                        pltpu.SemaphoreType.DMA((_DISPATCH_SLOTS,))],
        compiler_params=_params("arbitrary"),
        name="dispatch",
    )(dest_tiles, pad_tiles, h2_tiles)


def _experts_kernel(meta_ref, xs_ref, wg_ref, wu_ref, wd_ref, ys_ref, wg_bf, wu_bf, wd_bf):
    j = pl.program_id(0)
    n_used = meta_ref[0]
    n_slabs = xs_ref.shape[0] // MOE_BLOCK

    def slab(ref, c):
        return ref.at[pl.ds(c, MOE_BLOCK, stride=n_slabs), :]

    new_expert = (j == 0) | (meta_ref[1 + j] != meta_ref[jnp.maximum(j, 1)])

    @pl.when((j < n_used) & new_expert)
    def _():
        wg_bf[...] = wg_ref[0].astype(BF16)
        wu_bf[...] = wu_ref[0].astype(BF16)
        wd_bf[...] = wd_ref[0].astype(BF16)

    @pl.when(j < n_used)
    def _():
        xb = jnp.concatenate([slab(xs_ref, c)[...] for c in range(n_slabs)], axis=1).astype(BF16)
        a = _silu(_dot(xb, wg_bf[...])) * _dot(xb, wu_bf[...])
        y = _dot(a.astype(BF16), wd_bf[...])
        for c in range(n_slabs):
            slab(ys_ref, c)[...] = y[:, c * LANES:(c + 1) * LANES]

    @pl.when(j >= n_used)
    def _():
        ys_ref[...] = jnp.zeros_like(ys_ref)


def _experts(meta, xs, wg, wu, wd):
    n_slots, n_slabs, _ = xs.shape
    n_blk = n_slots // MOE_BLOCK
    d = wg.shape[1]
    ff = wg.shape[2]

    def wspec(shape):
        return pl.BlockSpec((1,) + shape, lambda j, meta: (meta[1 + j], 0, 0))

    blk = (MOE_BLOCK * n_slabs, LANES)
    ys = pl.pallas_call(
        _experts_kernel,
        out_shape=jax.ShapeDtypeStruct((n_slots * n_slabs, LANES), F32),
        grid_spec=pltpu.PrefetchScalarGridSpec(
            num_scalar_prefetch=1,
            grid=(n_blk,),
            in_specs=[pl.BlockSpec(blk, lambda j, meta: (
                          jnp.minimum(j, jnp.maximum(meta[0] - 1, 0)), 0)),
                      wspec((d, ff)), wspec((d, ff)), wspec((ff, d))],
            out_specs=pl.BlockSpec(blk, lambda j, meta: (j, 0)),
            scratch_shapes=[pltpu.VMEM((d, ff), BF16), pltpu.VMEM((d, ff), BF16),
                            pltpu.VMEM((ff, d), BF16)],
        ),
        compiler_params=_params("arbitrary"),
        name="experts",
    )(meta, xs.reshape(n_slots * n_slabs, LANES), wg, wu, wd)
    return ys.reshape(xs.shape)


def _combine_kernel(dest_cur_ref, dest_next_ref, ys_hbm, x1_ref, gate_ref, rw_ref, o_ref,
                    buf, sem):
    i = pl.program_id(0)
    slot = i % 2
    tm = x1_ref.shape[0]
    groups = tm // SUBLANES

    def start(idx_ref, sl):
        for k in range(TOP_K):
            def body(g, carry, k=k):
                rows = [idx_ref[0, 0, k * tm + g * SUBLANES + u] for u in range(SUBLANES)]
                for u in range(SUBLANES):
                    pltpu.make_async_copy(ys_hbm.at[rows[u]], buf.at[sl, k, g, :, u, :],
                                          sem.at[sl, k]).start()
                return carry
            lax.fori_loop(0, groups, body, 0)

    def wait(sl, k):
        for u in range(SUBLANES):
            pltpu.make_async_copy(ys_hbm.at[pl.ds(0, groups)], buf.at[sl, k, :, :, u, :],
                                  sem.at[sl, k]).wait()

    @pl.when(i == 0)
    def _():
        start(dest_cur_ref, 0)

    @pl.when(i + 1 < pl.num_programs(0))
    def _():
        start(dest_next_ref, 1 - slot)

    rw = rw_ref[...]
    moe = None
    for k in range(TOP_K):
        wait(slot, k)
        rows = jnp.concatenate([buf[slot, k, :, c].reshape(tm, LANES)
                                for c in range(buf.shape[3])], axis=1)
        term = rw[:, k:k + 1] * rows
        moe = term if moe is None else moe + term
    o_ref[...] = x1_ref[...] + gate_ref[0] * moe


def _combine(dest_tiles, ys, x1, gate, rw):
    n_tok, d = x1.shape
    tm = TOKEN_TILE
    nt = n_tok // tm
    per_batch = nt // gate.shape[0]
    idx_blk = (1, 1, TOP_K * tm)
    return pl.pallas_call(
        _combine_kernel,
        out_shape=jax.ShapeDtypeStruct((n_tok, d), F32),
        grid=(nt,),
        in_specs=[
            pl.BlockSpec(idx_blk, lambda i: (i, 0, 0), memory_space=pltpu.SMEM),
            pl.BlockSpec(idx_blk, lambda i: (jnp.minimum(i + 1, nt - 1), 0, 0),
                         memory_space=pltpu.SMEM),
            pl.BlockSpec(memory_space=pl.ANY),
            pl.BlockSpec((tm, d), lambda i: (i, 0)),
            pl.BlockSpec((1, 1, d), lambda i: (i // per_batch, 0, 0)),
            pl.BlockSpec((tm, ROUTE_LANES), lambda i: (i, 0)),
        ],
        out_specs=pl.BlockSpec((tm, d), lambda i: (i, 0)),
        scratch_shapes=[pltpu.VMEM((2, TOP_K, tm // SUBLANES, d // LANES, SUBLANES, LANES), F32),
                        pltpu.SemaphoreType.DMA((2, TOP_K))],
        compiler_params=_params("arbitrary"),
        name="combine",
    )(dest_tiles, dest_tiles, ys, x1, gate, rw)


def _layer(x, ctx, c, c_ctx, w_ada, b_ada, norm1_g, norm2_g, w_in, q_norm_g, k_norm_g, attn_out_g,
           lb, hgrn_out_g, w_out, w_router_grp, b_router_grp, w_router_exp, b_router_exp,
           w_exp_gate, w_exp_up, w_exp_down):
    b, s, d = x.shape
    n_ctx = ctx.shape[1]
    assert n_ctx % TOKEN_TILE == 0 and s % TOKEN_TILE == 0 and s % GRID_W == 0
    assert n_ctx % HGRN_CHUNK == 0 and (n_ctx + s) % ATTN_KV_TILE == 0
    n_all = n_ctx + s

    cond = jnp.zeros((2 * SUBLANES, d), F32).at[:b].set(c).at[b].set(c_ctx)
    assert b + 1 <= cond.shape[0]
    mods = _adaln(cond, w_ada, b_ada)[:b + 1].reshape(b + 1, 1, 6, d)
    sh1, sc1, gt1, sh2, sc2, gt2 = (mods[:, :, m] for m in range(6))

    scale_q = ATTN_HEAD_DIM ** -0.5 * np.log2(np.e)
    qkg = jnp.concatenate([jnp.tile(q_norm_g, ATTN_HEADS) * scale_q,
                           jnp.tile(k_norm_g, ATTN_KV_HEADS)]).reshape(1, _QK_WIDTH)
    cos, sin = _rope_tables(n_ctx, s)
    qa, ka, vt, qr, ff, fb, ir, gr = _in_proj(
        ctx, x, sh1, sc1, norm1_g.reshape(1, d), w_in.astype(BF16), qkg, cos, sin)
    oa = _attention(qa, ka, vt)
    o_f, o_b = _hgrn(qr, ir, ff, fb, lb, n_ctx)

    w_router = jnp.zeros((d, LANES), F32)
    w_router = w_router.at[:, :N_EXPERTS].set(w_router_exp)
    w_router = w_router.at[:, _ROUTE_GROUP_LANE0:_ROUTE_GROUP_LANE0 + N_GROUPS].set(w_router_grp)
    b_router = jnp.zeros((1, LANES), F32)
    b_router = b_router.at[0, :N_EXPERTS].set(b_router_exp)
    b_router = b_router.at[0, _ROUTE_GROUP_LANE0:_ROUTE_GROUP_LANE0 + N_GROUPS].set(b_router_grp)
    wr_hi, wr_lo = _split_bf16(w_router)
    x1, h2, ri, rw, counts = _out_proj(
        oa, o_f, o_b, gr, x, gt1[:b], sh2[:b], sc2[:b], attn_out_g.reshape(1, -1),
        hgrn_out_g.reshape(1, -1), norm2_g.reshape(1, d), w_out.astype(BF16), wr_hi, wr_lo,
        b_router, n_ctx)

    n_tok = b * s
    counts = counts[0, :N_EXPERTS]
    padded = (counts + MOE_BLOCK - 1) // MOE_BLOCK * MOE_BLOCK
    pend = jnp.cumsum(padded)
    pstart = pend - padded
    n_blk = n_tok * TOP_K // MOE_BLOCK + N_EXPERTS
    def lookup(table, idx):
        hit = idx[..., None] == jnp.arange(table.shape[0], dtype=jnp.int32)
        return jnp.sum(jnp.where(hit, table.astype(jnp.int32), 0), axis=-1)

    dest = lookup(pstart, ri[:, :TOP_K]) + ri[:, TOP_K:2 * TOP_K]
    dest_tiles = dest.reshape(dest.shape[0], 1, -1)
    blk_start = jnp.arange(n_blk, dtype=jnp.int32) * MOE_BLOCK
    blk_e = jnp.minimum(jnp.sum(blk_start[:, None] >= pend[None, :], axis=1), N_EXPERTS - 1)
    n_used = pend[-1] // MOE_BLOCK
    meta = jnp.concatenate([n_used[None], blk_e]).astype(jnp.int32)

    n_slots = n_blk * MOE_BLOCK
    n_pad = n_slots - n_tok * TOP_K
    gap_start = jnp.concatenate([pstart + counts, pend[-1:]])
    gap_size = jnp.concatenate([padded - counts, n_slots - pend[-1:]])
    gap_end = jnp.cumsum(gap_size)
    pad_i = jnp.arange(n_pad, dtype=jnp.int32)
    gap = jnp.sum(pad_i[:, None] >= gap_end[None, :], axis=1)
    pad_slots = lookup(gap_start - (gap_end - gap_size), gap) + pad_i
    pad_tiles = pad_slots.astype(jnp.int32).reshape(dest_tiles.shape[0], 1, -1)

    xs = _dispatch(dest_tiles, pad_tiles, h2, n_slots)
    ys = _experts(meta, xs, w_exp_gate, w_exp_up, w_exp_down)
    out = _combine(dest_tiles, ys, x1.reshape(n_tok, d), gt2[:b], rw.reshape(n_tok, ROUTE_LANES))
    return out.reshape(b, s, d)


def kernel(x, c, ctx, c_ctx, w_ada, b_ada, norm1_g, norm2_g, w_in, q_norm_g, k_norm_g, attn_out_g,
           hgrn_lb, hgrn_out_g, w_out, w_router_grp, b_router_grp, w_router_exp, b_router_exp,
           w_exp_gate, w_exp_up, w_exp_down):
    depth = w_in.shape[0]
    assert depth == 1, "context stream update between layers is not implemented"
    lb_all = jnp.cumsum(jax.nn.softmax(hgrn_lb.astype(F32), axis=1), axis=1)
    layer = 0
    return _layer(x, ctx, c, c_ctx, w_ada[layer], b_ada[layer], norm1_g[layer], norm2_g[layer],
                  w_in[layer], q_norm_g[layer], k_norm_g[layer], attn_out_g[layer],
                  lb_all[:, layer], hgrn_out_g[layer], w_out[layer], w_router_grp[layer],
                  b_router_grp[layer], w_router_exp[layer], b_router_exp[layer],
                  w_exp_gate[layer], w_exp_up[layer], w_exp_down[layer])
```

```python
import functools

import numpy as np
import jax
import jax.numpy as jnp
from jax import lax
from jax.experimental import pallas as pl
from jax.experimental.pallas import tpu as pltpu

F32 = jnp.float32
BF16 = jnp.bfloat16

GRID_W = 64
EPS = 1e-6
ATTN_HEADS = 8
ATTN_KV_HEADS = 2
ATTN_HEAD_DIM = 64
ATTN_GROUP = ATTN_HEADS // ATTN_KV_HEADS
ATTN_WIDTH = ATTN_HEADS * ATTN_HEAD_DIM
KV_WIDTH = ATTN_KV_HEADS * ATTN_HEAD_DIM
ROPE_THETA = 10000.0
HGRN_HEADS = 4
HGRN_HEAD_DIM = 128
HGRN_WIDTH = HGRN_HEADS * HGRN_HEAD_DIM
N_GROUPS = 4
EXPERTS_PER_GROUP = 8
N_EXPERTS = N_GROUPS * EXPERTS_PER_GROUP
TOP_K = 2

LANES = 128
SUBLANES = 8
VMEM_LIMIT_BYTES = 48 * 1024 * 1024

TOKEN_TILE = 256
OUT_PROJ_SPLIT = 2
ATTN_Q_TILE = 1024
ATTN_KV_TILE = 256
ATTN_KV_UNROLL = 17
ATTN_BOUND_SLACK = 1.02
ATTN_BOUND_MAX = 60.0
HGRN_CHUNK = 64
HGRN_STEP_CHUNKS = 4
MOE_BLOCK = 256
ROUTE_LANES = 8
NEG_BIG = -1e30

_QA0 = 0
_KA0 = _QA0 + ATTN_WIDTH
_VA0 = _KA0 + KV_WIDTH
_QR0 = _VA0 + KV_WIDTH
_FF0 = _QR0 + HGRN_WIDTH
_FB0 = _FF0 + HGRN_WIDTH
_IR0 = _FB0 + HGRN_WIDTH
_GR0 = _IR0 + HGRN_WIDTH
_QK_WIDTH = ATTN_WIDTH + KV_WIDTH


def _dot(a, b):
    return jnp.dot(a, b, preferred_element_type=F32)


def _dot_nt(a, b):
    return lax.dot_general(a, b, (((1,), (1,)), ((), ())), preferred_element_type=F32)


def _dot_tn(a, b):
    return lax.dot_general(a, b, (((0,), (0,)), ((), ())), preferred_element_type=F32)


def _split_bf16(x):
    hi = x.astype(BF16)
    lo = (x - hi.astype(F32)).astype(BF16)
    return hi, lo


def _sigmoid(x):
    return 1.0 / (1.0 + jnp.exp(-x))


def _silu(x):
    return x * _sigmoid(x)


def _params(*sem):
    return pltpu.CompilerParams(dimension_semantics=sem, vmem_limit_bytes=VMEM_LIMIT_BYTES)


def _head_sum_matrix():
    idx = np.arange(LANES) // ATTN_HEAD_DIM
    return jnp.asarray(idx[:, None] == idx[None, :], dtype=BF16)


def _head_rms_scale(x, ones_bd):
    ssq = _dot((x * x).astype(BF16), ones_bd)
    return lax.rsqrt(ssq * (1.0 / ATTN_HEAD_DIM) + EPS)


def _adaln_kernel(cond_ref, w_ref, b_ref, o_ref):
    s = _silu(cond_ref[...])
    s_hi, s_lo = _split_bf16(s)
    w_hi, w_lo = _split_bf16(w_ref[...])
    o_ref[...] = _dot(s_hi, w_hi) + _dot(s_lo, w_hi) + _dot(s_hi, w_lo) + b_ref[...]


def _adaln(cond, w_ada, b_ada):
    rows, d = cond.shape
    n = w_ada.shape[1]
    tn = n // 6
    return pl.pallas_call(
        _adaln_kernel,
        out_shape=jax.ShapeDtypeStruct((rows, n), F32),
        grid=(n // tn,),
        in_specs=[pl.BlockSpec((rows, d), lambda j: (0, 0)),
                  pl.BlockSpec((d, tn), lambda j: (0, j)),
                  pl.BlockSpec((1, tn), lambda j: (0, j))],
        out_specs=pl.BlockSpec((rows, tn), lambda j: (0, j)),
        compiler_params=_params("arbitrary"),
        name="adaln",
    )(cond, w_ada, b_ada.reshape(1, n))


def _rope_tables(n_ctx, n_lat):
    half = ATTN_HEAD_DIM // 2
    freqs = ROPE_THETA ** (-np.arange(0, half, 2, dtype=np.float64) / half)
    tok = np.arange(n_lat)
    pos = np.stack([tok // GRID_W, tok % GRID_W], axis=1).astype(np.float64)
    lane = np.arange(ATTN_HEAD_DIM)
    axis = lane // half
    fi = (lane % half) // 2
    ang = pos[:, axis] * freqs[fi][None, :]
    sign = np.where(lane % 2 == 1, 1.0, -1.0)
    cos = np.concatenate([np.ones((n_ctx, ATTN_HEAD_DIM)), np.cos(ang)], axis=0)
    sin = np.concatenate([np.zeros((n_ctx, ATTN_HEAD_DIM)), np.sin(ang) * sign], axis=0)
    reps = LANES // ATTN_HEAD_DIM
    return (jnp.asarray(np.tile(cos, (1, reps)), F32), jnp.asarray(np.tile(sin, (1, reps)), F32))


def _in_proj_kernel(n_ctx_tiles, ctx_ref, x_ref, shift_ref, scale_ref, g1_ref, w_ref, qkg_ref,
                    ones_ref, cos_ref, sin_ref,
                    qa_ref, ka_ref, vt_ref, qr_ref, ff_ref, fb_ref, ir_ref, gr_ref):
    i = pl.program_id(1)
    xt = jnp.where(i < n_ctx_tiles, ctx_ref[0], x_ref[0])
    ms = jnp.mean(xt * xt, axis=-1, keepdims=True)
    h = xt * lax.rsqrt(ms + EPS) * g1_ref[...]
    h = h * (1.0 + scale_ref[0]) + shift_ref[0]
    p = _dot(h.astype(BF16), w_ref[...])

    ones_bd = ones_ref[...]
    cos = cos_ref[...]
    sin = sin_ref[...]
    even = lax.broadcasted_iota(jnp.int32, cos.shape, 1) % 2 == 0
    slabs = []
    for c0 in range(0, _QK_WIDTH, LANES):
        t = p[:, _QA0 + c0:_QA0 + c0 + LANES]
        t = t * _head_rms_scale(t, ones_bd) * qkg_ref[:, c0:c0 + LANES]
        partner = jnp.where(even, pltpu.roll(t, LANES - 1, 1), pltpu.roll(t, 1, 1))
        slabs.append((t * cos + partner * sin).astype(BF16))
    qa_ref[0] = jnp.concatenate(slabs[:ATTN_WIDTH // LANES], axis=1)
    k_all = jnp.concatenate(slabs[ATTN_WIDTH // LANES:], axis=1)
    vt_all = p[:, _VA0:_VA0 + KV_WIDTH].T
    ones = jnp.ones((vt_ref.shape[2] - ATTN_HEAD_DIM, vt_all.shape[1]), F32)
    for hd in range(ATTN_KV_HEADS):
        cols = slice(hd * ATTN_HEAD_DIM, (hd + 1) * ATTN_HEAD_DIM)
        ka_ref[0, hd] = k_all[:, cols]
        vt_ref[0, hd] = jnp.concatenate([vt_all[cols], ones], axis=0).astype(BF16)

    qr_ref[0] = (_silu(p[:, _QR0:_QR0 + HGRN_WIDTH]) * (HGRN_HEAD_DIM ** -0.5)).astype(BF16)
    ff_ref[0] = p[:, _FF0:_FF0 + HGRN_WIDTH]
    fb_ref[0] = p[:, _FB0:_FB0 + HGRN_WIDTH]
    ir_ref[0] = p[:, _IR0:_IR0 + HGRN_WIDTH].astype(BF16)
    gr_ref[0] = _silu(p[:, _GR0:_GR0 + HGRN_WIDTH]).astype(BF16)


def _in_proj(ctx, x, shift, scale, g1, w_in, qkg, cos, sin):
    b, n_ctx, d = ctx.shape
    s = x.shape[1]
    tm = TOKEN_TILE
    nct = n_ctx // tm
    n_all = n_ctx + s
    nt = n_all // tm
    pw = w_in.shape[1]
    ones_bd = _head_sum_matrix()

    def tok_spec(w):
        return pl.BlockSpec((1, tm, w), lambda bi, i: (bi, i, 0))

    mod_spec = pl.BlockSpec((1, 1, d), lambda bi, i: (jnp.where(i < nct, b, bi), 0, 0))
    outs = [(HGRN_WIDTH, BF16), (HGRN_WIDTH, F32), (HGRN_WIDTH, F32), (HGRN_WIDTH, BF16),
            (HGRN_WIDTH, BF16)]
    vt_rows = ATTN_HEAD_DIM + 2 * SUBLANES
    lat_spec = pl.BlockSpec((1, tm, ATTN_WIDTH), lambda bi, i: (bi, jnp.maximum(i - nct, 0), 0))
    k_spec = pl.BlockSpec((1, ATTN_KV_HEADS, tm, ATTN_HEAD_DIM), lambda bi, i: (bi, 0, i, 0))
    vt_spec = pl.BlockSpec((1, ATTN_KV_HEADS, vt_rows, tm), lambda bi, i: (bi, 0, 0, i))
    return pl.pallas_call(
        functools.partial(_in_proj_kernel, nct),
        out_shape=[jax.ShapeDtypeStruct((b, s, ATTN_WIDTH), BF16),
                   jax.ShapeDtypeStruct((b, ATTN_KV_HEADS, n_all, ATTN_HEAD_DIM), BF16),
                   jax.ShapeDtypeStruct((b, ATTN_KV_HEADS, vt_rows, n_all), BF16)]
        + [jax.ShapeDtypeStruct((b, n_all, w), dt) for w, dt in outs],
        grid=(b, nt),
        in_specs=[
            pl.BlockSpec((1, tm, d), lambda bi, i: (bi, jnp.minimum(i, nct - 1), 0)),
            pl.BlockSpec((1, tm, d), lambda bi, i: (bi, jnp.maximum(i - nct, 0), 0)),
            mod_spec, mod_spec,
            pl.BlockSpec((1, d), lambda bi, i: (0, 0)),
            pl.BlockSpec((d, pw), lambda bi, i: (0, 0)),
            pl.BlockSpec((1, _QK_WIDTH), lambda bi, i: (0, 0)),
            pl.BlockSpec((LANES, LANES), lambda bi, i: (0, 0)),
            pl.BlockSpec((tm, LANES), lambda bi, i: (i, 0)),
            pl.BlockSpec((tm, LANES), lambda bi, i: (i, 0)),
        ],
        out_specs=[lat_spec, k_spec, vt_spec] + [tok_spec(w) for w, _ in outs],
        compiler_params=_params("arbitrary", "arbitrary"),
        name="in_proj",
    )(ctx, x, shift, scale, g1, w_in, qkg, ones_bd, cos, sin)


def _attention_kernel(n_kv_tiles, q_ref, k_ref, vt_ref, o_ref, qs_ref, s_ref, ksq_ref):
    tq = q_ref.shape[1]
    cols = ATTN_GROUP * tq
    for h in range(ATTN_GROUP):
        qs_ref[h * tq:(h + 1) * tq, :] = q_ref[0, :, h * ATTN_HEAD_DIM:(h + 1) * ATTN_HEAD_DIM]

    @pl.when(pl.program_id(2) == 0)
    def _():
        k = k_ref[0, 0].astype(F32)
        ksq = jnp.max(jnp.sum(k * k, axis=1, keepdims=True), axis=0, keepdims=True)
        ksq_ref[...] = jnp.broadcast_to(ksq, ksq_ref.shape)

    def tile(j):
        return pl.ds(pl.multiple_of(j * ATTN_KV_TILE, ATTN_KV_TILE), ATTN_KV_TILE)

    def scores(j):
        return _dot_nt(k_ref[0, 0, tile(j), :], qs_ref[...])

    def weighted(j, p):
        return _dot(vt_ref[0, 0, :, tile(j)], p)

    def finish(acc):
        o = acc[:ATTN_HEAD_DIM] / acc[ATTN_HEAD_DIM:ATTN_HEAD_DIM + 1]
        o_ref[0] = jnp.concatenate(
            [o[:, h * tq:(h + 1) * tq].T for h in range(ATTN_GROUP)], axis=1).astype(o_ref.dtype)

    qf = qs_ref[...].astype(F32)
    qsq = _dot_nt(jnp.ones((SUBLANES, ATTN_HEAD_DIM), BF16), (qf * qf).astype(BF16))[0:1]
    bound = jnp.sqrt(qsq * ksq_ref[0:1, 0:1]) * ATTN_BOUND_SLACK
    acc0 = jnp.zeros((vt_ref.shape[2], cols), F32)
    safe = jnp.max(bound) <= ATTN_BOUND_MAX

    @pl.when(safe)
    def _():
        def absorb(j, acc):
            return acc + weighted(j, jnp.exp2(scores(j) - bound).astype(BF16))

        def group(i, acc):
            for u in range(ATTN_KV_UNROLL):
                acc = absorb(ATTN_KV_UNROLL * i + u, acc)
            return acc

        n_groups = n_kv_tiles // ATTN_KV_UNROLL
        acc = lax.fori_loop(0, n_groups, group, acc0)
        for j in range(n_groups * ATTN_KV_UNROLL, n_kv_tiles):
            acc = absorb(j, acc)
        finish(acc)

    @pl.when(jnp.logical_not(safe))
    def _():
        def score(j, slot):
            s_ref[slot] = scores(j)

        def absorb(j, slot, carry):
            m, acc = carry
            s = s_ref[slot]
            m_new = jnp.maximum(m, jnp.max(s, axis=0, keepdims=True))
            p = jnp.exp2(s - m_new).astype(BF16)
            return m_new, jnp.exp2(m - m_new) * acc + weighted(j, p)

        def pair(i, carry):
            score(2 * i + 1, 1)
            carry = absorb(2 * i, 0, carry)
            score(2 * i + 2, 0)
            return absorb(2 * i + 1, 1, carry)

        n_pairs = (n_kv_tiles - 1) // 2
        score(0, 0)
        carry = lax.fori_loop(0, n_pairs, pair, (jnp.full((1, cols), NEG_BIG, F32), acc0))
        if n_kv_tiles % 2 == 1:
            carry = absorb(n_kv_tiles - 1, 0, carry)
        else:
            score(n_kv_tiles - 1, 1)
            carry = absorb(n_kv_tiles - 2, 0, carry)
            carry = absorb(n_kv_tiles - 1, 1, carry)
        finish(carry[1])


def _attention(qa, k_heads, vt_heads):
    b, s, _ = qa.shape
    n_all = k_heads.shape[2]
    tq = ATTN_Q_TILE
    assert s % tq == 0 and n_all % ATTN_KV_TILE == 0
    gw = ATTN_GROUP * ATTN_HEAD_DIM
    vt_rows = vt_heads.shape[2]
    return pl.pallas_call(
        functools.partial(_attention_kernel, n_all // ATTN_KV_TILE),
        out_shape=jax.ShapeDtypeStruct((b, s, ATTN_WIDTH), BF16),
        grid=(b, ATTN_KV_HEADS, s // tq),
        in_specs=[pl.BlockSpec((1, tq, gw), lambda bi, kv, i: (bi, i, kv)),
                  pl.BlockSpec((1, 1, n_all, ATTN_HEAD_DIM), lambda bi, kv, i: (bi, kv, 0, 0)),
                  pl.BlockSpec((1, 1, vt_rows, n_all), lambda bi, kv, i: (bi, kv, 0, 0))],
        out_specs=pl.BlockSpec((1, tq, gw), lambda bi, kv, i: (bi, i, kv)),
        scratch_shapes=[pltpu.VMEM((ATTN_GROUP * tq, ATTN_HEAD_DIM), BF16),
                        pltpu.VMEM((2, ATTN_KV_TILE, ATTN_GROUP * tq), F32),
                        pltpu.VMEM((SUBLANES, LANES), F32)],
        compiler_params=_params("arbitrary", "arbitrary", "arbitrary"),
        name="attention",
    )(qa, k_heads, vt_heads)


def _hgrn_masks(reverse):
    c = HGRN_CHUNK
    t = lax.broadcasted_iota(jnp.int32, (c, c), 0)
    u = lax.broadcasted_iota(jnp.int32, (c, c), 1)
    tri = (u >= t) if reverse else (u <= t)
    levels = []
    size = c // 2
    while size >= 1:
        same_parent = (t // (2 * size)) == (u // (2 * size))
        levels.append((size, same_parent if 2 * size < c else None))
        size //= 2
    levels.append((0, t == u))
    return tri.astype(BF16), levels


def _hgrn_chunk(q, fr, v, lb, state, reverse, tri, levels):
    c = HGRN_CHUNK
    coarse_levels = [lv for lv in levels if lv[0] >= SUBLANES]
    fine_levels = [lv for lv in levels if lv[0] < SUBLANES]
    f = lb + (1.0 - lb) * _sigmoid(fr)
    k = 1.0 - f
    g_hi, g_lo = _split_bf16(jnp.log2(f))
    bcum = _dot(tri, g_hi) + _dot(tri, g_lo)
    qf = q.astype(F32)
    yield None

    st = state()
    end = 0 if reverse else c - 1
    b_end = bcum[end:end + 1, :]
    inter = _dot_nt((qf * jnp.exp2(bcum)).astype(BF16), st.astype(BF16))
    ke = (k * jnp.exp2(b_end - bcum)).astype(BF16)
    st_add = _dot_tn(v, ke)

    products = []
    for size, mask in coarse_levels:
        q_rows, k_rows = [], []
        zeros = jnp.zeros((size, qf.shape[1]), BF16)
        for p0 in range(0, c, 2 * size):
            early = slice(p0, p0 + size)
            late = slice(p0 + size, p0 + 2 * size)
            r = p0 + size if reverse else p0 + size - 1
            ref = bcum[r:r + 1, :]
            q_sl, k_sl = (early, late) if reverse else (late, early)
            q_blk = (qf[q_sl] * jnp.exp2(bcum[q_sl] - ref)).astype(BF16)
            k_blk = (k[k_sl] * jnp.exp2(ref - bcum[k_sl])).astype(BF16)
            q_rows += [q_blk, zeros] if reverse else [zeros, q_blk]
            k_rows += [zeros, k_blk] if reverse else [k_blk, zeros]
        products.append((mask, _dot_nt(jnp.concatenate(q_rows, axis=0),
                                       jnp.concatenate(k_rows, axis=0))))
    yield None

    row = lax.broadcasted_iota(jnp.int32, (c, 1), 0)
    sub = lax.broadcasted_iota(jnp.int32, (SUBLANES, 1), 0)
    fine_products = []
    for size, mask in fine_levels:
        if size == 0:
            q_l, k_l = q, k.astype(BF16)
        else:
            late = (row // size) % 2 == 1
            q_side = jnp.logical_not(late) if reverse else late
            if size == 1:
                fac_q, fac_k = f, None
            else:
                groups = []
                for r0 in range(0, c, SUBLANES):
                    ref = None
                    for p0 in range(0, SUBLANES, 2 * size):
                        r = r0 + p0 + (size if reverse else size - 1)
                        piece = jnp.broadcast_to(bcum[r:r + 1, :], (SUBLANES, bcum.shape[1]))
                        ref = piece if ref is None else jnp.where(sub >= p0, piece, ref)
                    groups.append(ref)
                ref = jnp.concatenate(groups, axis=0)
                fac_q = fac_k = jnp.exp2(jnp.where(q_side, bcum - ref, ref - bcum))
            q_l = jnp.where(q_side, qf * fac_q, 0.0).astype(BF16)
            k_l = jnp.where(q_side, 0.0, k if fac_k is None else k * fac_k).astype(BF16)
        fine_products.append((mask, _dot_nt(q_l, k_l)))
    a = None
    for mask, a_l in products:
        if mask is not None:
            a_l = jnp.where(mask, a_l, 0.0)
        a = a_l if a is None else a + a_l
    yield None

    for mask, a_l in fine_products:
        a = a + jnp.where(mask, a_l, 0.0)
    intra = _dot(a.astype(BF16), v)
    st_new = st * jnp.exp2(b_end) + st_add
    yield None

    yield inter + intra, st_new


_HGRN_STAGES = 5


def _hgrn_kernel(qf_ref, if_ref, ff_ref, qb_ref, ib_ref, fb_ref, lb_ref, of_ref, ob_ref,
                 sf_ref, sb_ref):
    @pl.when(pl.program_id(1) == 0)
    def _():
        sf_ref[...] = jnp.zeros_like(sf_ref)
        sb_ref[...] = jnp.zeros_like(sb_ref)

    dirs = ((False, qf_ref, if_ref, ff_ref, of_ref, sf_ref),
            (True, qb_ref, ib_ref, fb_ref, ob_ref, sb_ref))
    masks = [_hgrn_masks(reverse) for reverse, *_ in dirs]
    n_sub = qf_ref.shape[1] // HGRN_CHUNK
    latest = {}
    chunks = []
    for t in range(n_sub):
        for h in range(HGRN_HEADS):
            sl = slice(h * HGRN_HEAD_DIM, (h + 1) * HGRN_HEAD_DIM)
            for d, (reverse, q_ref, i_ref, f_ref, o_ref, s_ref) in enumerate(dirs):
                r0 = (n_sub - 1 - t if reverse else t) * HGRN_CHUNK
                rows = slice(r0, r0 + HGRN_CHUNK)

                def state(h=h, d=d, s_ref=s_ref):
                    return latest[h, d] if (h, d) in latest else s_ref[h]

                gen = _hgrn_chunk(q_ref[0, rows, sl], f_ref[0, rows, sl], i_ref[0, rows, sl],
                                  lb_ref[d:d + 1, sl], state, reverse, *masks[d])
                chunks.append((gen, o_ref, s_ref, h, d, rows, sl, t == n_sub - 1))

    assert len(dirs) * HGRN_HEADS >= _HGRN_STAGES
    for step in range(len(chunks) + _HGRN_STAGES - 1):
        for stage in range(_HGRN_STAGES):
            idx = step - stage
            if 0 <= idx < len(chunks):
                gen, o_ref, s_ref, h, d, rows, sl, is_last = chunks[idx]
                result = next(gen)
                if stage == _HGRN_STAGES - 1:
                    o, st = result
                    o_ref[0, rows, sl] = o
                    latest[h, d] = st
                    if is_last:
                        s_ref[h] = st


def _hgrn(qr, ir, ff, fb, lb, n_ctx):
    b, n_all, w = qr.shape
    c = HGRN_CHUNK * HGRN_STEP_CHUNKS
    assert n_ctx % c == 0 and n_all % c == 0
    nc = n_all // c
    ncc = n_ctx // c

    def fwd(bi, j):
        return (bi, j, 0)

    def bwd(bi, j):
        return (bi, jnp.where(j < ncc, ncc - 1 - j, nc - 1 - (j - ncc)), 0)

    blk = (1, c, w)
    state = pltpu.VMEM((HGRN_HEADS, HGRN_HEAD_DIM, HGRN_HEAD_DIM), F32)
    return pl.pallas_call(
        _hgrn_kernel,
        out_shape=[jax.ShapeDtypeStruct((b, n_all, w), F32)] * 2,
        grid=(b, nc),
        in_specs=[pl.BlockSpec(blk, fwd), pl.BlockSpec(blk, fwd), pl.BlockSpec(blk, fwd),
                  pl.BlockSpec(blk, bwd), pl.BlockSpec(blk, bwd), pl.BlockSpec(blk, bwd),
                  pl.BlockSpec((2, w), lambda bi, j: (0, 0))],
        out_specs=[pl.BlockSpec(blk, fwd), pl.BlockSpec(blk, bwd)],
        scratch_shapes=[state, state],
        compiler_params=_params("arbitrary", "arbitrary"),
        name="hgrn",
    )(qr, ir, ff, qr, ir, fb, lb)


_ROUTE_GROUP_LANE0 = N_EXPERTS


def _lane_min_index(cond, lane):
    return jnp.min(jnp.where(cond, lane, LANES), axis=-1, keepdims=True)


def _out_proj_kernel(oa_ref, of_ref, ob_ref, gr_ref, x_ref, gate_ref, shift_ref, scale_ref,
                     ag_ref, hg_ref, g2_ref, w_ref, ones_ref, wr_hi_ref, wr_lo_ref, br_ref,
                     x1_ref, h2_ref, ri_ref, rw_ref, cnt_ref, carry_ref):
    first = (pl.program_id(0) == 0) & (pl.program_id(1) == 0)

    @pl.when(first)
    def _():
        carry_ref[...] = jnp.zeros_like(carry_ref)

    tm = x_ref.shape[1]
    hm = tm // OUT_PROJ_SPLIT
    ones_bd = ones_ref[...]
    counts = [carry_ref[0:1, :]]

    def rows_chain(r0):
        rs = slice(r0, r0 + hm)
        ts = [oa_ref[0, rs, c0:c0 + LANES].astype(F32) for c0 in range(0, ATTN_WIDTH, LANES)]
        scales = [_head_rms_scale(t, ones_bd) for t in ts]
        orr = of_ref[0, rs] + ob_ref[0, rs]
        yield

        slabs = [(t * sc * ag_ref[:, c * LANES:(c + 1) * LANES]).astype(BF16)
                 for c, (t, sc) in enumerate(zip(ts, scales))]
        parts = []
        for h in range(HGRN_HEADS):
            oh = orr[:, h * HGRN_HEAD_DIM:(h + 1) * HGRN_HEAD_DIM]
            parts.append(oh * lax.rsqrt(jnp.mean(oh * oh, axis=-1, keepdims=True) + EPS))
        orn = jnp.concatenate(parts, axis=1) * hg_ref[...] * gr_ref[0, rs].astype(F32)
        mix = _dot(jnp.concatenate(slabs + [orn.astype(BF16)], axis=1), w_ref[...])
        yield

        x1 = x_ref[0, rs] + gate_ref[0] * mix
        x1_ref[0, rs] = x1
        h2 = x1 * lax.rsqrt(jnp.mean(x1 * x1, axis=-1, keepdims=True) + EPS) * g2_ref[...]
        h2 = h2 * (1.0 + scale_ref[0]) + shift_ref[0]
        gs = slice(r0 // SUBLANES, (r0 + hm) // SUBLANES)
        for c in range(h2_ref.shape[1]):
            h2_ref[gs, c] = h2[:, c * LANES:(c + 1) * LANES].reshape(hm // SUBLANES, SUBLANES,
                                                                     LANES)
        h_hi, h_lo = _split_bf16(h2)
        logits = (_dot(h_hi, wr_hi_ref[...]) + _dot(h_lo, wr_hi_ref[...])
                  + _dot(h_hi, wr_lo_ref[...]) + br_ref[...])
        yield

        lane = lax.broadcasted_iota(jnp.int32, logits.shape, 1)
        is_grp = (lane >= _ROUTE_GROUP_LANE0) & (lane < _ROUTE_GROUP_LANE0 + N_GROUPS)
        lg = jnp.where(is_grp, logits, NEG_BIG)
        mg = jnp.max(lg, axis=-1, keepdims=True)
        g_sel = _lane_min_index(lg == mg, lane) - _ROUTE_GROUP_LANE0
        pg_top = 1.0 / jnp.sum(jnp.exp(lg - mg), axis=-1, keepdims=True)
        in_grp = (lane < N_EXPERTS) & ((lane // EXPERTS_PER_GROUP) == g_sel)
        le = jnp.where(in_grp, logits, NEG_BIG)
        m1 = jnp.max(le, axis=-1, keepdims=True)
        e1 = _lane_min_index(le == m1, lane)
        le2 = jnp.where(lane == e1, NEG_BIG, le)
        m2 = jnp.max(le2, axis=-1, keepdims=True)
        e2 = _lane_min_index(le2 == m2, lane)
        r2 = jnp.exp(m2 - m1)
        w1 = pg_top / (1.0 + r2)
        w2 = pg_top * r2 / (1.0 + r2)
        onehot = ((lane == e1) | (lane == e2)).astype(BF16)
        rt = lax.broadcasted_iota(jnp.int32, (hm, hm), 0)
        ru = lax.broadcasted_iota(jnp.int32, (hm, hm), 1)
        before_local = _dot((ru < rt).astype(BF16), onehot)
        yield

        before = before_local + counts[-1]
        rank1 = jnp.sum(jnp.where(lane == e1, before, 0.0), axis=-1, keepdims=True)
        rank2 = jnp.sum(jnp.where(lane == e2, before, 0.0), axis=-1, keepdims=True)
        counts.append(counts[-1] + jnp.sum(onehot.astype(F32), axis=0, keepdims=True))
        rec = jnp.where(lane == 0, e1.astype(F32), jnp.where(lane == 1, e2.astype(F32), jnp.where(
            lane == 2, rank1, jnp.where(lane == 3, rank2, 0.0))))
        ri_ref[0, :, rs] = rec.T[:ROUTE_LANES].astype(jnp.int32)
        rl = lax.broadcasted_iota(jnp.int32, (hm, ROUTE_LANES), 1)
        rw_ref[0, rs] = jnp.where(rl == 0, w1, jnp.where(rl == 1, w2, 0.0))
        yield

    n_stages = 5
    chains = [rows_chain(r0) for r0 in range(0, tm, hm)]
    for step in range(len(chains) + n_stages - 1):
        for stage in range(n_stages):
            idx = step - stage
            if 0 <= idx < len(chains):
                next(chains[idx])
    carry_ref[...] = jnp.broadcast_to(counts[-1], carry_ref.shape)
    cnt_ref[...] = jnp.broadcast_to(counts[-1], cnt_ref.shape).astype(jnp.int32)


def _out_proj(oa, o_f, o_b, gr, x, gate, shift, scale, ag, hg, g2, w_out, wr_hi, wr_lo, br, n_ctx):
    b, s, d = x.shape
    tm = TOKEN_TILE
    off = n_ctx // tm
    ones_bd = _head_sum_matrix()

    def lat(w):
        return pl.BlockSpec((1, tm, w), lambda bi, i: (bi, i, 0))

    def allrows(w):
        return pl.BlockSpec((1, tm, w), lambda bi, i: (bi, i + off, 0))

    def mod():
        return pl.BlockSpec((1, 1, d), lambda bi, i: (bi, 0, 0))

    def const(shape):
        return pl.BlockSpec(shape, lambda bi, i: (0,) * len(shape))

    return pl.pallas_call(
        _out_proj_kernel,
        out_shape=[jax.ShapeDtypeStruct((b, s, d), F32),
                   jax.ShapeDtypeStruct((b * s // SUBLANES, d // LANES, SUBLANES, LANES), F32),
                   jax.ShapeDtypeStruct((b * s // tm, ROUTE_LANES, tm), jnp.int32),
                   jax.ShapeDtypeStruct((b, s, ROUTE_LANES), F32),
                   jax.ShapeDtypeStruct((SUBLANES, LANES), jnp.int32)],
        grid=(b, s // tm),
        in_specs=[lat(ATTN_WIDTH), allrows(HGRN_WIDTH), allrows(HGRN_WIDTH), allrows(HGRN_WIDTH),
                  lat(d), mod(), mod(), mod(),
                  const((1, ATTN_WIDTH)), const((1, HGRN_WIDTH)), const((1, d)),
                  const((ATTN_WIDTH + HGRN_WIDTH, d)), const((LANES, LANES)),
                  const((d, LANES)), const((d, LANES)), const((1, LANES))],
        out_specs=[lat(d),
                   pl.BlockSpec((tm // SUBLANES, d // LANES, SUBLANES, LANES),
                                lambda bi, i: (bi * (s // tm) + i, 0, 0, 0)),
                   pl.BlockSpec((1, ROUTE_LANES, tm), lambda bi, i: (bi * (s // tm) + i, 0, 0)),
                   lat(ROUTE_LANES), const((SUBLANES, LANES))],
        scratch_shapes=[pltpu.VMEM((SUBLANES, LANES), F32)],
        compiler_params=_params("arbitrary", "arbitrary"),
        name="out_proj",
    )(oa, o_f, o_b, gr, x, gate, shift, scale, ag, hg, g2, w_out, ones_bd, wr_hi, wr_lo, br)


def _slab_copies(stage, hbm, row0, sem, to_hbm):
    copies = []
    for c in range(stage.shape[0]):
        view = hbm.at[pl.ds(row0, stage.shape[1]), c, :]
        src, dst = (stage.at[c], view) if to_hbm else (view, stage.at[c])
        copies.append(pltpu.make_async_copy(src, dst, sem))
    return copies


def _stage_store(stage, x):
    for c in range(stage.shape[0]):
        stage[c] = x[:, c * LANES:(c + 1) * LANES]


def _stage_load(stage):
    return jnp.concatenate([stage[c] for c in range(stage.shape[0])], axis=1)


_DISPATCH_SLOTS = 3


def _dispatch_kernel(dest_ref, pad_ref, h2_hbm, xs_hbm, buf, zero_row, sem_in, sem_out):
    i = pl.program_id(0)
    n = pl.num_programs(0)
    groups = buf.shape[1]
    tm = groups * SUBLANES
    n_pad = pad_ref.shape[2]
    slot = i % _DISPATCH_SLOTS

    def fetch(t, sl):
        return pltpu.make_async_copy(h2_hbm.at[pl.ds(t * groups, groups)], buf.at[sl],
                                     sem_in.at[sl])

    def wait_rows(sl):
        for _ in range(TOP_K + n_pad // tm):
            pltpu.make_async_copy(h2_hbm.at[pl.ds(0, groups)], buf.at[sl], sem_out.at[sl]).wait()
        rest = (n_pad % tm) // SUBLANES
        if rest:
            pltpu.make_async_copy(h2_hbm.at[pl.ds(0, rest)], buf.at[sl, pl.ds(0, rest)],
                                  sem_out.at[sl]).wait()

    @pl.when(i == 0)
    def _():
        zero_row[...] = jnp.zeros_like(zero_row)
        fetch(0, 0).start()

    @pl.when(i >= 2)
    def _():
        wait_rows((i + 1) % _DISPATCH_SLOTS)

    @pl.when(i + 1 < n)
    def _():
        fetch(i + 1, (i + 1) % _DISPATCH_SLOTS).start()

    fetch(i, slot).wait()

    def body(g, carry):
        rows = [[dest_ref[0, 0, k * tm + g * SUBLANES + u] for k in range(TOP_K)]
                for u in range(SUBLANES)]
        for u in range(SUBLANES):
            for k in range(TOP_K):
                pltpu.make_async_copy(buf.at[slot, g, :, u, :], xs_hbm.at[rows[u][k]],
                                      sem_out.at[slot]).start()
        return carry
    lax.fori_loop(0, groups, body, 0)

    def pad_body(g, carry):
        rows = [pad_ref[0, 0, g * SUBLANES + u] for u in range(SUBLANES)]
        for u in range(SUBLANES):
            pltpu.make_async_copy(zero_row, xs_hbm.at[rows[u]], sem_out.at[slot]).start()
        return carry
    lax.fori_loop(0, n_pad // SUBLANES, pad_body, 0)

    @pl.when(i == n - 1)
    def _():
        wait_rows(slot)

    @pl.when((i == n - 1) & (i >= 1))
    def _():
        wait_rows((i - 1) % _DISPATCH_SLOTS)


def _dispatch(dest_tiles, pad_tiles, h2_tiles, n_slots):
    nt = dest_tiles.shape[0]
    n_groups, n_slabs, _, _ = h2_tiles.shape
    groups = n_groups // nt
    assert pad_tiles.shape[2] % SUBLANES == 0

    def idx_spec(arr):
        return pl.BlockSpec((1, 1, arr.shape[2]), lambda i: (i, 0, 0), memory_space=pltpu.SMEM)

    return pl.pallas_call(
        _dispatch_kernel,
        out_shape=jax.ShapeDtypeStruct((n_slots, n_slabs, LANES), F32),
        grid=(nt,),
        in_specs=[idx_spec(dest_tiles), idx_spec(pad_tiles), pl.BlockSpec(memory_space=pl.ANY)],
        out_specs=pl.BlockSpec(memory_space=pl.ANY),
        scratch_shapes=[pltpu.VMEM((_DISPATCH_SLOTS, groups, n_slabs, SUBLANES, LANES), F32),
                        pltpu.VMEM((n_slabs, LANES), F32),
                        pltpu.SemaphoreType.DMA((_DISPATCH_SLOTS,)),
                        pltpu.SemaphoreType.DMA((_DISPATCH_SLOTS,))],
        compiler_params=_params("arbitrary"),
        name="dispatch",
    )(dest_tiles, pad_tiles, h2_tiles)


_EXPERT_IN_SLOTS = 3


def _experts_kernel(meta_ref, xs_hbm, wg_ref, wu_ref, wd_ref, ys_hbm,
                    xstage, ystage, wg_bf, wu_bf, wd_bf, sem_in, sem_out):
    j = pl.program_id(0)
    last = pl.num_programs(0) - 1
    n_used = meta_ref[0]
    slot = j % 2

    def fetch(blk, sl):
        for cp in _slab_copies(xstage.at[sl], xs_hbm, blk * MOE_BLOCK, sem_in.at[sl], False):
            cp.start()

    def wait_out(sl):
        for cp in _slab_copies(ystage.at[sl], ys_hbm, 0, sem_out.at[sl], True):
            cp.wait()

    in_slot = j % _EXPERT_IN_SLOTS

    @pl.when((j == 0) & (n_used > 0))
    def _():
        fetch(0, 0)

    @pl.when((j == 0) & (n_used > 1))
    def _():
        fetch(1, 1)

    @pl.when(j + 2 < n_used)
    def _():
        fetch(j + 2, (j + 2) % _EXPERT_IN_SLOTS)

    new_expert = (j == 0) | (meta_ref[1 + j] != meta_ref[jnp.maximum(j, 1)])

    @pl.when((j < n_used) & new_expert)
    def _():
        wg_bf[...] = wg_ref[0].astype(BF16)
        wu_bf[...] = wu_ref[0].astype(BF16)
        wd_bf[...] = wd_ref[0].astype(BF16)

    @pl.when(j >= 2)
    def _():
        wait_out(slot)

    @pl.when(j < n_used)
    def _():
        for cp in _slab_copies(xstage.at[in_slot], xs_hbm, 0, sem_in.at[in_slot], False):
            cp.wait()
        xb = _stage_load(xstage.at[in_slot]).astype(BF16)
        a = _silu(_dot(xb, wg_bf[...])) * _dot(xb, wu_bf[...])
        _stage_store(ystage.at[slot], _dot(a.astype(BF16), wd_bf[...]))

    @pl.when(j >= n_used)
    def _():
        ystage[slot] = jnp.zeros(ystage.shape[1:], F32)

    for cp in _slab_copies(ystage.at[slot], ys_hbm, j * MOE_BLOCK, sem_out.at[slot], True):
        cp.start()

    @pl.when(j == last)
    def _():
        wait_out(slot)

    @pl.when((j == last) & (j >= 1))
    def _():
        wait_out(1 - slot)


def _experts(meta, xs, wg, wu, wd):
    n_slots, n_slabs, _ = xs.shape
    n_blk = n_slots // MOE_BLOCK
    d = wg.shape[1]
    ff = wg.shape[2]

    def wspec(shape):
        return pl.BlockSpec((1,) + shape, lambda j, meta: (meta[1 + j], 0, 0))

    in_stage = pltpu.VMEM((_EXPERT_IN_SLOTS, n_slabs, MOE_BLOCK, LANES), F32)
    stage = pltpu.VMEM((2, n_slabs, MOE_BLOCK, LANES), F32)
    return pl.pallas_call(
        _experts_kernel,
        out_shape=jax.ShapeDtypeStruct(xs.shape, F32),
        grid_spec=pltpu.PrefetchScalarGridSpec(
            num_scalar_prefetch=1,
            grid=(n_blk,),
            in_specs=[pl.BlockSpec(memory_space=pl.ANY),
                      wspec((d, ff)), wspec((d, ff)), wspec((ff, d))],
            out_specs=pl.BlockSpec(memory_space=pl.ANY),
            scratch_shapes=[in_stage, stage, pltpu.VMEM((d, ff), BF16),
                            pltpu.VMEM((d, ff), BF16), pltpu.VMEM((ff, d), BF16),
                            pltpu.SemaphoreType.DMA((_EXPERT_IN_SLOTS,)),
                            pltpu.SemaphoreType.DMA((2,))],
        ),
        compiler_params=_params("arbitrary"),
        name="experts",
    )(meta, xs, wg, wu, wd)


def _combine_kernel(dest_cur_ref, dest_next_ref, ys_hbm, x1_ref, gate_ref, rw_ref, o_ref,
                    buf, sem):
    i = pl.program_id(0)
    slot = i % 2
    tm = x1_ref.shape[0]
    groups = tm // SUBLANES

    def start(idx_ref, sl):
        for k in range(TOP_K):
            def body(g, carry, k=k):
                rows = [idx_ref[0, 0, k * tm + g * SUBLANES + u] for u in range(SUBLANES)]
                for u in range(SUBLANES):
                    pltpu.make_async_copy(ys_hbm.at[rows[u]], buf.at[sl, k, g, :, u, :],
                                          sem.at[sl, k]).start()
                return carry
            lax.fori_loop(0, groups, body, 0)

    def wait(sl, k):
        for u in range(SUBLANES):
            pltpu.make_async_copy(ys_hbm.at[pl.ds(0, groups)], buf.at[sl, k, :, :, u, :],
                                  sem.at[sl, k]).wait()

    @pl.when(i == 0)
    def _():
        start(dest_cur_ref, 0)

    @pl.when(i + 1 < pl.num_programs(0))
    def _():
        start(dest_next_ref, 1 - slot)

    rw = rw_ref[...]
    moe = None
    for k in range(TOP_K):
        wait(slot, k)
        rows = jnp.concatenate([buf[slot, k, :, c].reshape(tm, LANES)
                                for c in range(buf.shape[3])], axis=1)
        term = rw[:, k:k + 1] * rows
        moe = term if moe is None else moe + term
    o_ref[...] = x1_ref[...] + gate_ref[0] * moe


def _combine(dest_tiles, ys, x1, gate, rw):
    n_tok, d = x1.shape
    tm = TOKEN_TILE
    nt = n_tok // tm
    per_batch = nt // gate.shape[0]
    idx_blk = (1, 1, TOP_K * tm)
    return pl.pallas_call(
        _combine_kernel,
        out_shape=jax.ShapeDtypeStruct((n_tok, d), F32),
        grid=(nt,),
        in_specs=[
            pl.BlockSpec(idx_blk, lambda i: (i, 0, 0), memory_space=pltpu.SMEM),
            pl.BlockSpec(idx_blk, lambda i: (jnp.minimum(i + 1, nt - 1), 0, 0),
                         memory_space=pltpu.SMEM),
            pl.BlockSpec(memory_space=pl.ANY),
            pl.BlockSpec((tm, d), lambda i: (i, 0)),
            pl.BlockSpec((1, 1, d), lambda i: (i // per_batch, 0, 0)),
            pl.BlockSpec((tm, ROUTE_LANES), lambda i: (i, 0)),
        ],
        out_specs=pl.BlockSpec((tm, d), lambda i: (i, 0)),
        scratch_shapes=[pltpu.VMEM((2, TOP_K, tm // SUBLANES, d // LANES, SUBLANES, LANES), F32),
                        pltpu.SemaphoreType.DMA((2, TOP_K))],
        compiler_params=_params("arbitrary"),
        name="combine",
    )(dest_tiles, dest_tiles, ys, x1, gate, rw)


def _layer(x, ctx, c, c_ctx, w_ada, b_ada, norm1_g, norm2_g, w_in, q_norm_g, k_norm_g, attn_out_g,
           lb, hgrn_out_g, w_out, w_router_grp, b_router_grp, w_router_exp, b_router_exp,
           w_exp_gate, w_exp_up, w_exp_down):
    b, s, d = x.shape
    n_ctx = ctx.shape[1]
    assert n_ctx % TOKEN_TILE == 0 and s % TOKEN_TILE == 0 and s % GRID_W == 0
    assert n_ctx % HGRN_CHUNK == 0 and (n_ctx + s) % ATTN_KV_TILE == 0
    n_all = n_ctx + s

    cond = jnp.zeros((2 * SUBLANES, d), F32).at[:b].set(c).at[b].set(c_ctx)
    assert b + 1 <= cond.shape[0]
    mods = _adaln(cond, w_ada, b_ada)[:b + 1].reshape(b + 1, 1, 6, d)
    sh1, sc1, gt1, sh2, sc2, gt2 = (mods[:, :, m] for m in range(6))

    scale_q = ATTN_HEAD_DIM ** -0.5 * np.log2(np.e)
    qkg = jnp.concatenate([jnp.tile(q_norm_g, ATTN_HEADS) * scale_q,
                           jnp.tile(k_norm_g, ATTN_KV_HEADS)]).reshape(1, _QK_WIDTH)
    cos, sin = _rope_tables(n_ctx, s)
    qa, ka, vt, qr, ff, fb, ir, gr = _in_proj(
        ctx, x, sh1, sc1, norm1_g.reshape(1, d), w_in.astype(BF16), qkg, cos, sin)
    oa = _attention(qa, ka, vt)
    o_f, o_b = _hgrn(qr, ir, ff, fb, lb, n_ctx)

    w_router = jnp.zeros((d, LANES), F32)
    w_router = w_router.at[:, :N_EXPERTS].set(w_router_exp)
    w_router = w_router.at[:, _ROUTE_GROUP_LANE0:_ROUTE_GROUP_LANE0 + N_GROUPS].set(w_router_grp)
    b_router = jnp.zeros((1, LANES), F32)
    b_router = b_router.at[0, :N_EXPERTS].set(b_router_exp)
    b_router = b_router.at[0, _ROUTE_GROUP_LANE0:_ROUTE_GROUP_LANE0 + N_GROUPS].set(b_router_grp)
    wr_hi, wr_lo = _split_bf16(w_router)
    x1, h2, ri, rw, counts = _out_proj(
        oa, o_f, o_b, gr, x, gt1[:b], sh2[:b], sc2[:b], attn_out_g.reshape(1, -1),
        hgrn_out_g.reshape(1, -1), norm2_g.reshape(1, d), w_out.astype(BF16), wr_hi, wr_lo,
        b_router, n_ctx)

    n_tok = b * s
    counts = counts[0, :N_EXPERTS]
    padded = (counts + MOE_BLOCK - 1) // MOE_BLOCK * MOE_BLOCK
    pend = jnp.cumsum(padded)
    pstart = pend - padded
    n_blk = n_tok * TOP_K // MOE_BLOCK + N_EXPERTS
    def lookup(table, idx):
        hit = idx[..., None] == jnp.arange(table.shape[0], dtype=jnp.int32)
        return jnp.sum(jnp.where(hit, table.astype(jnp.int32), 0), axis=-1)

    dest = lookup(pstart, ri[:, :TOP_K]) + ri[:, TOP_K:2 * TOP_K]
    dest_tiles = dest.reshape(dest.shape[0], 1, -1)
    blk_start = jnp.arange(n_blk, dtype=jnp.int32) * MOE_BLOCK
    blk_e = jnp.minimum(jnp.sum(blk_start[:, None] >= pend[None, :], axis=1), N_EXPERTS - 1)
    n_used = pend[-1] // MOE_BLOCK
    meta = jnp.concatenate([n_used[None], blk_e]).astype(jnp.int32)

    n_slots = n_blk * MOE_BLOCK
    n_pad = n_slots - n_tok * TOP_K
    gap_start = jnp.concatenate([pstart + counts, pend[-1:]])
    gap_size = jnp.concatenate([padded - counts, n_slots - pend[-1:]])
    gap_end = jnp.cumsum(gap_size)
    pad_i = jnp.arange(n_pad, dtype=jnp.int32)
    gap = jnp.sum(pad_i[:, None] >= gap_end[None, :], axis=1)
    pad_slots = lookup(gap_start - (gap_end - gap_size), gap) + pad_i
    pad_tiles = pad_slots.astype(jnp.int32).reshape(dest_tiles.shape[0], 1, -1)

    xs = _dispatch(dest_tiles, pad_tiles, h2, n_slots)
    ys = _experts(meta, xs, w_exp_gate, w_exp_up, w_exp_down)
    out = _combine(dest_tiles, ys, x1.reshape(n_tok, d), gt2[:b], rw.reshape(n_tok, ROUTE_LANES))
    return out.reshape(b, s, d)


def kernel(x, c, ctx, c_ctx, w_ada, b_ada, norm1_g, norm2_g, w_in, q_norm_g, k_norm_g, attn_out_g,
           hgrn_lb, hgrn_out_g, w_out, w_router_grp, b_router_grp, w_router_exp, b_router_exp,
           w_exp_gate, w_exp_up, w_exp_down):
    depth = w_in.shape[0]
    assert depth == 1, "context stream update between layers is not implemented"
    lb_all = jnp.cumsum(jax.nn.softmax(hgrn_lb.astype(F32), axis=1), axis=1)
    layer = 0
    return _layer(x, ctx, c, c_ctx, w_ada[layer], b_ada[layer], norm1_g[layer], norm2_g[layer],
                  w_in[layer], q_norm_g[layer], k_norm_g[layer], attn_out_g[layer],
                  lb_all[:, layer], hgrn_out_g[layer], w_out[layer], w_router_grp[layer],
                  b_router_grp[layer], w_router_exp[layer], b_router_exp[layer],
                  w_exp_gate[layer], w_exp_up[layer], w_exp_down[layer])
```

```python
import functools

import numpy as np
import jax
import jax.numpy as jnp
from jax import lax
from jax.experimental import pallas as pl
from jax.experimental.pallas import tpu as pltpu

F32 = jnp.float32
BF16 = jnp.bfloat16

GRID_W = 64
EPS = 1e-6
ATTN_HEADS = 8
ATTN_KV_HEADS = 2
ATTN_HEAD_DIM = 64
ATTN_GROUP = ATTN_HEADS // ATTN_KV_HEADS
ATTN_WIDTH = ATTN_HEADS * ATTN_HEAD_DIM
KV_WIDTH = ATTN_KV_HEADS * ATTN_HEAD_DIM
ROPE_THETA = 10000.0
HGRN_HEADS = 4
HGRN_HEAD_DIM = 128
HGRN_WIDTH = HGRN_HEADS * HGRN_HEAD_DIM
N_GROUPS = 4
EXPERTS_PER_GROUP = 8
N_EXPERTS = N_GROUPS * EXPERTS_PER_GROUP
TOP_K = 2

LANES = 128
SUBLANES = 8
DMA_PRIORITIES = 2
VMEM_LIMIT_BYTES = 48 * 1024 * 1024

TOKEN_TILE = 256
OUT_PROJ_SPLIT = 2
ATTN_Q_TILE = 1024
ATTN_KV_TILE = 256
ATTN_KV_UNROLL = 17
ATTN_BOUND_SLACK = 1.02
ATTN_BOUND_MAX = 60.0
HGRN_CHUNK = 64
HGRN_STEP_CHUNKS = 4
MOE_BLOCK = 256
ROUTE_LANES = 8
NEG_BIG = -1e30

_QA0 = 0
_KA0 = _QA0 + ATTN_WIDTH
_VA0 = _KA0 + KV_WIDTH
_QR0 = _VA0 + KV_WIDTH
_FF0 = _QR0 + HGRN_WIDTH
_FB0 = _FF0 + HGRN_WIDTH
_IR0 = _FB0 + HGRN_WIDTH
_GR0 = _IR0 + HGRN_WIDTH
_QK_WIDTH = ATTN_WIDTH + KV_WIDTH


def _dot(a, b):
    return jnp.dot(a, b, preferred_element_type=F32)


def _dot_nt(a, b):
    return lax.dot_general(a, b, (((1,), (1,)), ((), ())), preferred_element_type=F32)


def _dot_tn(a, b):
    return lax.dot_general(a, b, (((0,), (0,)), ((), ())), preferred_element_type=F32)


def _split_bf16(x):
    hi = x.astype(BF16)
    lo = (x - hi.astype(F32)).astype(BF16)
    return hi, lo


def _sigmoid(x):
    return 1.0 / (1.0 + jnp.exp(-x))


def _silu(x):
    return x * _sigmoid(x)


def _params(*sem):
    return pltpu.CompilerParams(dimension_semantics=sem, vmem_limit_bytes=VMEM_LIMIT_BYTES)


def _head_sum_matrix():
    idx = np.arange(LANES) // ATTN_HEAD_DIM
    return jnp.asarray(idx[:, None] == idx[None, :], dtype=BF16)


def _head_rms_scale(x, ones_bd):
    ssq = _dot((x * x).astype(BF16), ones_bd)
    return lax.rsqrt(ssq * (1.0 / ATTN_HEAD_DIM) + EPS)


def _adaln_kernel(cond_ref, w_ref, b_ref, o_ref):
    s = _silu(cond_ref[...])
    s_hi, s_lo = _split_bf16(s)
    w_hi, w_lo = _split_bf16(w_ref[...])
    o_ref[...] = _dot(s_hi, w_hi) + _dot(s_lo, w_hi) + _dot(s_hi, w_lo) + b_ref[...]


def _adaln(cond, w_ada, b_ada):
    rows, d = cond.shape
    n = w_ada.shape[1]
    tn = n // 6
    return pl.pallas_call(
        _adaln_kernel,
        out_shape=jax.ShapeDtypeStruct((rows, n), F32),
        grid=(n // tn,),
        in_specs=[pl.BlockSpec((rows, d), lambda j: (0, 0)),
                  pl.BlockSpec((d, tn), lambda j: (0, j)),
                  pl.BlockSpec((1, tn), lambda j: (0, j))],
        out_specs=pl.BlockSpec((rows, tn), lambda j: (0, j)),
        compiler_params=_params("arbitrary"),
        name="adaln",
    )(cond, w_ada, b_ada.reshape(1, n))


def _rope_tables(n_ctx, n_lat):
    half = ATTN_HEAD_DIM // 2
    freqs = ROPE_THETA ** (-np.arange(0, half, 2, dtype=np.float64) / half)
    tok = np.arange(n_lat)
    pos = np.stack([tok // GRID_W, tok % GRID_W], axis=1).astype(np.float64)
    lane = np.arange(ATTN_HEAD_DIM)
    axis = lane // half
    fi = (lane % half) // 2
    ang = pos[:, axis] * freqs[fi][None, :]
    sign = np.where(lane % 2 == 1, 1.0, -1.0)
    cos = np.concatenate([np.ones((n_ctx, ATTN_HEAD_DIM)), np.cos(ang)], axis=0)
    sin = np.concatenate([np.zeros((n_ctx, ATTN_HEAD_DIM)), np.sin(ang) * sign], axis=0)
    reps = LANES // ATTN_HEAD_DIM
    return (jnp.asarray(np.tile(cos, (1, reps)), F32), jnp.asarray(np.tile(sin, (1, reps)), F32))


def _in_proj_kernel(n_ctx_tiles, ctx_ref, x_ref, shift_ref, scale_ref, g1_ref, w_ref, qkg_ref,
                    ones_ref, cos_ref, sin_ref,
                    qa_ref, ka_ref, vt_ref, qr_ref, ff_ref, fb_ref, ir_ref, gr_ref):
    i = pl.program_id(1)
    xt = jnp.where(i < n_ctx_tiles, ctx_ref[0], x_ref[0])
    ms = jnp.mean(xt * xt, axis=-1, keepdims=True)
    h = xt * lax.rsqrt(ms + EPS) * g1_ref[...]
    h = h * (1.0 + scale_ref[0]) + shift_ref[0]
    p = _dot(h.astype(BF16), w_ref[...])

    ones_bd = ones_ref[...]
    cos = cos_ref[...]
    sin = sin_ref[...]
    even = lax.broadcasted_iota(jnp.int32, cos.shape, 1) % 2 == 0
    slabs = []
    for c0 in range(0, _QK_WIDTH, LANES):
        t = p[:, _QA0 + c0:_QA0 + c0 + LANES]
        t = t * _head_rms_scale(t, ones_bd) * qkg_ref[:, c0:c0 + LANES]
        partner = jnp.where(even, pltpu.roll(t, LANES - 1, 1), pltpu.roll(t, 1, 1))
        slabs.append((t * cos + partner * sin).astype(BF16))
    qa_ref[0] = jnp.concatenate(slabs[:ATTN_WIDTH // LANES], axis=1)
    k_all = jnp.concatenate(slabs[ATTN_WIDTH // LANES:], axis=1)
    vt_all = p[:, _VA0:_VA0 + KV_WIDTH].T
    ones = jnp.ones((vt_ref.shape[2] - ATTN_HEAD_DIM, vt_all.shape[1]), F32)
    for hd in range(ATTN_KV_HEADS):
        cols = slice(hd * ATTN_HEAD_DIM, (hd + 1) * ATTN_HEAD_DIM)
        ka_ref[0, hd] = k_all[:, cols]
        vt_ref[0, hd] = jnp.concatenate([vt_all[cols], ones], axis=0).astype(BF16)

    qr_ref[0] = (_silu(p[:, _QR0:_QR0 + HGRN_WIDTH]) * (HGRN_HEAD_DIM ** -0.5)).astype(BF16)
    ff_ref[0] = p[:, _FF0:_FF0 + HGRN_WIDTH]
    fb_ref[0] = p[:, _FB0:_FB0 + HGRN_WIDTH]
    ir_ref[0] = p[:, _IR0:_IR0 + HGRN_WIDTH].astype(BF16)
    gr_ref[0] = _silu(p[:, _GR0:_GR0 + HGRN_WIDTH]).astype(BF16)


def _in_proj(ctx, x, shift, scale, g1, w_in, qkg, cos, sin):
    b, n_ctx, d = ctx.shape
    s = x.shape[1]
    tm = TOKEN_TILE
    nct = n_ctx // tm
    n_all = n_ctx + s
    nt = n_all // tm
    pw = w_in.shape[1]
    ones_bd = _head_sum_matrix()

    def tok_spec(w):
        return pl.BlockSpec((1, tm, w), lambda bi, i: (bi, i, 0))

    mod_spec = pl.BlockSpec((1, 1, d), lambda bi, i: (jnp.where(i < nct, b, bi), 0, 0))
    outs = [(HGRN_WIDTH, BF16), (HGRN_WIDTH, F32), (HGRN_WIDTH, F32), (HGRN_WIDTH, BF16),
            (HGRN_WIDTH, BF16)]
    vt_rows = ATTN_HEAD_DIM + 2 * SUBLANES
    lat_spec = pl.BlockSpec((1, tm, ATTN_WIDTH), lambda bi, i: (bi, jnp.maximum(i - nct, 0), 0))
    k_spec = pl.BlockSpec((1, ATTN_KV_HEADS, tm, ATTN_HEAD_DIM), lambda bi, i: (bi, 0, i, 0))
    vt_spec = pl.BlockSpec((1, ATTN_KV_HEADS, vt_rows, tm), lambda bi, i: (bi, 0, 0, i))
    return pl.pallas_call(
        functools.partial(_in_proj_kernel, nct),
        out_shape=[jax.ShapeDtypeStruct((b, s, ATTN_WIDTH), BF16),
                   jax.ShapeDtypeStruct((b, ATTN_KV_HEADS, n_all, ATTN_HEAD_DIM), BF16),
                   jax.ShapeDtypeStruct((b, ATTN_KV_HEADS, vt_rows, n_all), BF16)]
        + [jax.ShapeDtypeStruct((b, n_all, w), dt) for w, dt in outs],
        grid=(b, nt),
        in_specs=[
            pl.BlockSpec((1, tm, d), lambda bi, i: (bi, jnp.minimum(i, nct - 1), 0)),
            pl.BlockSpec((1, tm, d), lambda bi, i: (bi, jnp.maximum(i - nct, 0), 0)),
            mod_spec, mod_spec,
            pl.BlockSpec((1, d), lambda bi, i: (0, 0)),
            pl.BlockSpec((d, pw), lambda bi, i: (0, 0)),
            pl.BlockSpec((1, _QK_WIDTH), lambda bi, i: (0, 0)),
            pl.BlockSpec((LANES, LANES), lambda bi, i: (0, 0)),
            pl.BlockSpec((tm, LANES), lambda bi, i: (i, 0)),
            pl.BlockSpec((tm, LANES), lambda bi, i: (i, 0)),
        ],
        out_specs=[lat_spec, k_spec, vt_spec] + [tok_spec(w) for w, _ in outs],
        compiler_params=_params("arbitrary", "arbitrary"),
        name="in_proj",
    )(ctx, x, shift, scale, g1, w_in, qkg, ones_bd, cos, sin)


def _attention_kernel(n_kv_tiles, q_ref, k_ref, vt_ref, o_ref, qs_ref, s_ref, ksq_ref):
    tq = q_ref.shape[1]
    cols = ATTN_GROUP * tq
    for h in range(ATTN_GROUP):
        qs_ref[h * tq:(h + 1) * tq, :] = q_ref[0, :, h * ATTN_HEAD_DIM:(h + 1) * ATTN_HEAD_DIM]

    @pl.when(pl.program_id(2) == 0)
    def _():
        k = k_ref[0, 0].astype(F32)
        ksq = jnp.max(jnp.sum(k * k, axis=1, keepdims=True), axis=0, keepdims=True)
        ksq_ref[...] = jnp.broadcast_to(ksq, ksq_ref.shape)

    def tile(j):
        return pl.ds(pl.multiple_of(j * ATTN_KV_TILE, ATTN_KV_TILE), ATTN_KV_TILE)

    def scores(j):
        return _dot_nt(k_ref[0, 0, tile(j), :], qs_ref[...])

    def weighted(j, p):
        return _dot(vt_ref[0, 0, :, tile(j)], p)

    def finish(acc):
        o = acc[:ATTN_HEAD_DIM] / acc[ATTN_HEAD_DIM:ATTN_HEAD_DIM + 1]
        o_ref[0] = jnp.concatenate(
            [o[:, h * tq:(h + 1) * tq].T for h in range(ATTN_GROUP)], axis=1).astype(o_ref.dtype)

    qf = qs_ref[...].astype(F32)
    qsq = _dot_nt(jnp.ones((SUBLANES, ATTN_HEAD_DIM), BF16), (qf * qf).astype(BF16))[0:1]
    bound = jnp.sqrt(qsq * ksq_ref[0:1, 0:1]) * ATTN_BOUND_SLACK
    acc0 = jnp.zeros((vt_ref.shape[2], cols), F32)
    safe = jnp.max(bound) <= ATTN_BOUND_MAX

    @pl.when(safe)
    def _():
        def absorb(j, acc):
            return acc + weighted(j, jnp.exp2(scores(j) - bound).astype(BF16))

        def group(i, acc):
            for u in range(ATTN_KV_UNROLL):
                acc = absorb(ATTN_KV_UNROLL * i + u, acc)
            return acc

        n_groups = n_kv_tiles // ATTN_KV_UNROLL
        acc = lax.fori_loop(0, n_groups, group, acc0)
        for j in range(n_groups * ATTN_KV_UNROLL, n_kv_tiles):
            acc = absorb(j, acc)
        finish(acc)

    @pl.when(jnp.logical_not(safe))
    def _():
        def score(j, slot):
            s_ref[slot] = scores(j)

        def absorb(j, slot, carry):
            m, acc = carry
            s = s_ref[slot]
            m_new = jnp.maximum(m, jnp.max(s, axis=0, keepdims=True))
            p = jnp.exp2(s - m_new).astype(BF16)
            return m_new, jnp.exp2(m - m_new) * acc + weighted(j, p)

        def pair(i, carry):
            score(2 * i + 1, 1)
            carry = absorb(2 * i, 0, carry)
            score(2 * i + 2, 0)
            return absorb(2 * i + 1, 1, carry)

        n_pairs = (n_kv_tiles - 1) // 2
        score(0, 0)
        carry = lax.fori_loop(0, n_pairs, pair, (jnp.full((1, cols), NEG_BIG, F32), acc0))
        if n_kv_tiles % 2 == 1:
            carry = absorb(n_kv_tiles - 1, 0, carry)
        else:
            score(n_kv_tiles - 1, 1)
            carry = absorb(n_kv_tiles - 2, 0, carry)
            carry = absorb(n_kv_tiles - 1, 1, carry)
        finish(carry[1])


def _attention(qa, k_heads, vt_heads):
    b, s, _ = qa.shape
    n_all = k_heads.shape[2]
    tq = ATTN_Q_TILE
    assert s % tq == 0 and n_all % ATTN_KV_TILE == 0
    gw = ATTN_GROUP * ATTN_HEAD_DIM
    vt_rows = vt_heads.shape[2]
    return pl.pallas_call(
        functools.partial(_attention_kernel, n_all // ATTN_KV_TILE),
        out_shape=jax.ShapeDtypeStruct((b, s, ATTN_WIDTH), BF16),
        grid=(b, ATTN_KV_HEADS, s // tq),
        in_specs=[pl.BlockSpec((1, tq, gw), lambda bi, kv, i: (bi, i, kv)),
                  pl.BlockSpec((1, 1, n_all, ATTN_HEAD_DIM), lambda bi, kv, i: (bi, kv, 0, 0)),
                  pl.BlockSpec((1, 1, vt_rows, n_all), lambda bi, kv, i: (bi, kv, 0, 0))],
        out_specs=pl.BlockSpec((1, tq, gw), lambda bi, kv, i: (bi, i, kv)),
        scratch_shapes=[pltpu.VMEM((ATTN_GROUP * tq, ATTN_HEAD_DIM), BF16),
                        pltpu.VMEM((2, ATTN_KV_TILE, ATTN_GROUP * tq), F32),
                        pltpu.VMEM((SUBLANES, LANES), F32)],
        compiler_params=_params("arbitrary", "arbitrary", "arbitrary"),
        name="attention",
    )(qa, k_heads, vt_heads)


def _hgrn_masks(reverse):
    c = HGRN_CHUNK
    t = lax.broadcasted_iota(jnp.int32, (c, c), 0)
    u = lax.broadcasted_iota(jnp.int32, (c, c), 1)
    tri = (u >= t) if reverse else (u <= t)
    levels = []
    size = c // 2
    while size >= 1:
        same_parent = (t // (2 * size)) == (u // (2 * size))
        levels.append((size, same_parent if 2 * size < c else None))
        size //= 2
    levels.append((0, t == u))
    return tri.astype(BF16), levels


def _hgrn_chunk(q, fr, v, lb, state, reverse, tri, levels):
    c = HGRN_CHUNK
    coarse_levels = [lv for lv in levels if lv[0] >= SUBLANES]
    fine_levels = [lv for lv in levels if lv[0] < SUBLANES]
    f = lb + (1.0 - lb) * _sigmoid(fr)
    k = 1.0 - f
    g_hi, g_lo = _split_bf16(jnp.log2(f))
    bcum = _dot(tri, g_hi) + _dot(tri, g_lo)
    qf = q.astype(F32)
    yield None

    st = state()
    end = 0 if reverse else c - 1
    b_end = bcum[end:end + 1, :]
    inter = _dot_nt((qf * jnp.exp2(bcum)).astype(BF16), st.astype(BF16))
    ke = (k * jnp.exp2(b_end - bcum)).astype(BF16)
    st_add = _dot_tn(v, ke)

    products = []
    for size, mask in coarse_levels:
        q_rows, k_rows = [], []
        zeros = jnp.zeros((size, qf.shape[1]), BF16)
        for p0 in range(0, c, 2 * size):
            early = slice(p0, p0 + size)
            late = slice(p0 + size, p0 + 2 * size)
            r = p0 + size if reverse else p0 + size - 1
            ref = bcum[r:r + 1, :]
            q_sl, k_sl = (early, late) if reverse else (late, early)
            q_blk = (qf[q_sl] * jnp.exp2(bcum[q_sl] - ref)).astype(BF16)
            k_blk = (k[k_sl] * jnp.exp2(ref - bcum[k_sl])).astype(BF16)
            q_rows += [q_blk, zeros] if reverse else [zeros, q_blk]
            k_rows += [zeros, k_blk] if reverse else [k_blk, zeros]
        products.append((mask, _dot_nt(jnp.concatenate(q_rows, axis=0),
                                       jnp.concatenate(k_rows, axis=0))))
    yield None

    row = lax.broadcasted_iota(jnp.int32, (c, 1), 0)
    sub = lax.broadcasted_iota(jnp.int32, (SUBLANES, 1), 0)
    fine_products = []
    for size, mask in fine_levels:
        if size == 0:
            q_l, k_l = q, k.astype(BF16)
        else:
            late = (row // size) % 2 == 1
            q_side = jnp.logical_not(late) if reverse else late
            if size == 1:
                fac_q, fac_k = f, None
            else:
                groups = []
                for r0 in range(0, c, SUBLANES):
                    ref = None
                    for p0 in range(0, SUBLANES, 2 * size):
                        r = r0 + p0 + (size if reverse else size - 1)
                        piece = jnp.broadcast_to(bcum[r:r + 1, :], (SUBLANES, bcum.shape[1]))
                        ref = piece if ref is None else jnp.where(sub >= p0, piece, ref)
                    groups.append(ref)
                ref = jnp.concatenate(groups, axis=0)
                fac_q = fac_k = jnp.exp2(jnp.where(q_side, bcum - ref, ref - bcum))
            q_l = jnp.where(q_side, qf * fac_q, 0.0).astype(BF16)
            k_l = jnp.where(q_side, 0.0, k if fac_k is None else k * fac_k).astype(BF16)
        fine_products.append((mask, _dot_nt(q_l, k_l)))
    a = None
    for mask, a_l in products:
        if mask is not None:
            a_l = jnp.where(mask, a_l, 0.0)
        a = a_l if a is None else a + a_l
    yield None

    for mask, a_l in fine_products:
        a = a + jnp.where(mask, a_l, 0.0)
    intra = _dot(a.astype(BF16), v)
    st_new = st * jnp.exp2(b_end) + st_add
    yield None

    yield inter + intra, st_new


_HGRN_STAGES = 5


def _hgrn_kernel(qf_ref, if_ref, ff_ref, qb_ref, ib_ref, fb_ref, lb_ref, of_ref, ob_ref,
                 sf_ref, sb_ref):
    @pl.when(pl.program_id(1) == 0)
    def _():
        sf_ref[...] = jnp.zeros_like(sf_ref)
        sb_ref[...] = jnp.zeros_like(sb_ref)

    dirs = ((False, qf_ref, if_ref, ff_ref, of_ref, sf_ref),
            (True, qb_ref, ib_ref, fb_ref, ob_ref, sb_ref))
    masks = [_hgrn_masks(reverse) for reverse, *_ in dirs]
    n_sub = qf_ref.shape[1] // HGRN_CHUNK
    latest = {}
    chunks = []
    for t in range(n_sub):
        for h in range(HGRN_HEADS):
            sl = slice(h * HGRN_HEAD_DIM, (h + 1) * HGRN_HEAD_DIM)
            for d, (reverse, q_ref, i_ref, f_ref, o_ref, s_ref) in enumerate(dirs):
                r0 = (n_sub - 1 - t if reverse else t) * HGRN_CHUNK
                rows = slice(r0, r0 + HGRN_CHUNK)

                def state(h=h, d=d, s_ref=s_ref):
                    return latest[h, d] if (h, d) in latest else s_ref[h]

                gen = _hgrn_chunk(q_ref[0, rows, sl], f_ref[0, rows, sl], i_ref[0, rows, sl],
                                  lb_ref[d:d + 1, sl], state, reverse, *masks[d])
                chunks.append((gen, o_ref, s_ref, h, d, rows, sl, t == n_sub - 1))

    assert len(dirs) * HGRN_HEADS >= _HGRN_STAGES
    for step in range(len(chunks) + _HGRN_STAGES - 1):
        for stage in range(_HGRN_STAGES):
            idx = step - stage
            if 0 <= idx < len(chunks):
                gen, o_ref, s_ref, h, d, rows, sl, is_last = chunks[idx]
                result = next(gen)
                if stage == _HGRN_STAGES - 1:
                    o, st = result
                    o_ref[0, rows, sl] = o
                    latest[h, d] = st
                    if is_last:
                        s_ref[h] = st


def _hgrn(qr, ir, ff, fb, lb, n_ctx):
    b, n_all, w = qr.shape
    c = HGRN_CHUNK * HGRN_STEP_CHUNKS
    assert n_ctx % c == 0 and n_all % c == 0
    nc = n_all // c
    ncc = n_ctx // c

    def fwd(bi, j):
        return (bi, j, 0)

    def bwd(bi, j):
        return (bi, jnp.where(j < ncc, ncc - 1 - j, nc - 1 - (j - ncc)), 0)

    blk = (1, c, w)
    state = pltpu.VMEM((HGRN_HEADS, HGRN_HEAD_DIM, HGRN_HEAD_DIM), F32)
    return pl.pallas_call(
        _hgrn_kernel,
        out_shape=[jax.ShapeDtypeStruct((b, n_all, w), F32)] * 2,
        grid=(b, nc),
        in_specs=[pl.BlockSpec(blk, fwd), pl.BlockSpec(blk, fwd), pl.BlockSpec(blk, fwd),
                  pl.BlockSpec(blk, bwd), pl.BlockSpec(blk, bwd), pl.BlockSpec(blk, bwd),
                  pl.BlockSpec((2, w), lambda bi, j: (0, 0))],
        out_specs=[pl.BlockSpec(blk, fwd), pl.BlockSpec(blk, bwd)],
        scratch_shapes=[state, state],
        compiler_params=_params("arbitrary", "arbitrary"),
        name="hgrn",
    )(qr, ir, ff, qr, ir, fb, lb)


_ROUTE_GROUP_LANE0 = N_EXPERTS


def _lane_min_index(cond, lane):
    return jnp.min(jnp.where(cond, lane, LANES), axis=-1, keepdims=True)


def _out_proj_kernel(oa_ref, of_ref, ob_ref, gr_ref, x_ref, gate_ref, shift_ref, scale_ref,
                     ag_ref, hg_ref, g2_ref, w_ref, ones_ref, wr_hi_ref, wr_lo_ref, br_ref,
                     x1_ref, h2_ref, ri_ref, rw_ref, cnt_ref, carry_ref):
    first = (pl.program_id(0) == 0) & (pl.program_id(1) == 0)

    @pl.when(first)
    def _():
        carry_ref[...] = jnp.zeros_like(carry_ref)

    tm = x_ref.shape[1]
    hm = tm // OUT_PROJ_SPLIT
    ones_bd = ones_ref[...]
    counts = [carry_ref[0:1, :]]

    def rows_chain(r0):
        rs = slice(r0, r0 + hm)
        ts = [oa_ref[0, rs, c0:c0 + LANES].astype(F32) for c0 in range(0, ATTN_WIDTH, LANES)]
        scales = [_head_rms_scale(t, ones_bd) for t in ts]
        orr = of_ref[0, rs] + ob_ref[0, rs]
        yield

        slabs = [(t * sc * ag_ref[:, c * LANES:(c + 1) * LANES]).astype(BF16)
                 for c, (t, sc) in enumerate(zip(ts, scales))]
        parts = []
        for h in range(HGRN_HEADS):
            oh = orr[:, h * HGRN_HEAD_DIM:(h + 1) * HGRN_HEAD_DIM]
            parts.append(oh * lax.rsqrt(jnp.mean(oh * oh, axis=-1, keepdims=True) + EPS))
        orn = jnp.concatenate(parts, axis=1) * hg_ref[...] * gr_ref[0, rs].astype(F32)
        mix = _dot(jnp.concatenate(slabs + [orn.astype(BF16)], axis=1), w_ref[...])
        yield

        x1 = x_ref[0, rs] + gate_ref[0] * mix
        x1_ref[0, rs] = x1
        h2 = x1 * lax.rsqrt(jnp.mean(x1 * x1, axis=-1, keepdims=True) + EPS) * g2_ref[...]
        h2 = h2 * (1.0 + scale_ref[0]) + shift_ref[0]
        gs = slice(r0 // SUBLANES, (r0 + hm) // SUBLANES)
        for c in range(h2_ref.shape[1]):
            h2_ref[gs, c] = h2[:, c * LANES:(c + 1) * LANES].reshape(hm // SUBLANES, SUBLANES,
                                                                     LANES)
        h_hi, h_lo = _split_bf16(h2)
        logits = (_dot(h_hi, wr_hi_ref[...]) + _dot(h_lo, wr_hi_ref[...])
                  + _dot(h_hi, wr_lo_ref[...]) + br_ref[...])
        yield

        lane = lax.broadcasted_iota(jnp.int32, logits.shape, 1)
        is_grp = (lane >= _ROUTE_GROUP_LANE0) & (lane < _ROUTE_GROUP_LANE0 + N_GROUPS)
        lg = jnp.where(is_grp, logits, NEG_BIG)
        mg = jnp.max(lg, axis=-1, keepdims=True)
        g_sel = _lane_min_index(lg == mg, lane) - _ROUTE_GROUP_LANE0
        pg_top = 1.0 / jnp.sum(jnp.exp(lg - mg), axis=-1, keepdims=True)
        in_grp = (lane < N_EXPERTS) & ((lane // EXPERTS_PER_GROUP) == g_sel)
        le = jnp.where(in_grp, logits, NEG_BIG)
        m1 = jnp.max(le, axis=-1, keepdims=True)
        e1 = _lane_min_index(le == m1, lane)
        le2 = jnp.where(lane == e1, NEG_BIG, le)
        m2 = jnp.max(le2, axis=-1, keepdims=True)
        e2 = _lane_min_index(le2 == m2, lane)
        r2 = jnp.exp(m2 - m1)
        w1 = pg_top / (1.0 + r2)
        w2 = pg_top * r2 / (1.0 + r2)
        onehot = ((lane == e1) | (lane == e2)).astype(BF16)
        rt = lax.broadcasted_iota(jnp.int32, (hm, hm), 0)
        ru = lax.broadcasted_iota(jnp.int32, (hm, hm), 1)
        before_local = _dot((ru < rt).astype(BF16), onehot)
        yield

        before = before_local + counts[-1]
        rank1 = jnp.sum(jnp.where(lane == e1, before, 0.0), axis=-1, keepdims=True)
        rank2 = jnp.sum(jnp.where(lane == e2, before, 0.0), axis=-1, keepdims=True)
        counts.append(counts[-1] + jnp.sum(onehot.astype(F32), axis=0, keepdims=True))
        rec = jnp.where(lane == 0, e1.astype(F32), jnp.where(lane == 1, e2.astype(F32), jnp.where(
            lane == 2, rank1, jnp.where(lane == 3, rank2, 0.0))))
        ri_ref[0, :, rs] = rec.T[:ROUTE_LANES].astype(jnp.int32)
        rl = lax.broadcasted_iota(jnp.int32, (hm, ROUTE_LANES), 1)
        rw_ref[0, rs] = jnp.where(rl == 0, w1, jnp.where(rl == 1, w2, 0.0))
        yield

    n_stages = 5
    chains = [rows_chain(r0) for r0 in range(0, tm, hm)]
    for step in range(len(chains) + n_stages - 1):
        for stage in range(n_stages):
            idx = step - stage
            if 0 <= idx < len(chains):
                next(chains[idx])
    carry_ref[...] = jnp.broadcast_to(counts[-1], carry_ref.shape)
    cnt_ref[...] = jnp.broadcast_to(counts[-1], cnt_ref.shape).astype(jnp.int32)


def _out_proj(oa, o_f, o_b, gr, x, gate, shift, scale, ag, hg, g2, w_out, wr_hi, wr_lo, br, n_ctx):
    b, s, d = x.shape
    tm = TOKEN_TILE
    off = n_ctx // tm
    ones_bd = _head_sum_matrix()

    def lat(w):
        return pl.BlockSpec((1, tm, w), lambda bi, i: (bi, i, 0))

    def allrows(w):
        return pl.BlockSpec((1, tm, w), lambda bi, i: (bi, i + off, 0))

    def mod():
        return pl.BlockSpec((1, 1, d), lambda bi, i: (bi, 0, 0))

    def const(shape):
        return pl.BlockSpec(shape, lambda bi, i: (0,) * len(shape))

    return pl.pallas_call(
        _out_proj_kernel,
        out_shape=[jax.ShapeDtypeStruct((b, s, d), F32),
                   jax.ShapeDtypeStruct((b * s // SUBLANES, d // LANES, SUBLANES, LANES), F32),
                   jax.ShapeDtypeStruct((b * s // tm, ROUTE_LANES, tm), jnp.int32),
                   jax.ShapeDtypeStruct((b, s, ROUTE_LANES), F32),
                   jax.ShapeDtypeStruct((SUBLANES, LANES), jnp.int32)],
        grid=(b, s // tm),
        in_specs=[lat(ATTN_WIDTH), allrows(HGRN_WIDTH), allrows(HGRN_WIDTH), allrows(HGRN_WIDTH),
                  lat(d), mod(), mod(), mod(),
                  const((1, ATTN_WIDTH)), const((1, HGRN_WIDTH)), const((1, d)),
                  const((ATTN_WIDTH + HGRN_WIDTH, d)), const((LANES, LANES)),
                  const((d, LANES)), const((d, LANES)), const((1, LANES))],
        out_specs=[lat(d),
                   pl.BlockSpec((tm // SUBLANES, d // LANES, SUBLANES, LANES),
                                lambda bi, i: (bi * (s // tm) + i, 0, 0, 0)),
                   pl.BlockSpec((1, ROUTE_LANES, tm), lambda bi, i: (bi * (s // tm) + i, 0, 0)),
                   lat(ROUTE_LANES), const((SUBLANES, LANES))],
        scratch_shapes=[pltpu.VMEM((SUBLANES, LANES), F32)],
        compiler_params=_params("arbitrary", "arbitrary"),
        name="out_proj",
    )(oa, o_f, o_b, gr, x, gate, shift, scale, ag, hg, g2, w_out, ones_bd, wr_hi, wr_lo, br)


def _slab_copies(stage, hbm, row0, sem, to_hbm):
    copies = []
    for c in range(stage.shape[0]):
        view = hbm.at[pl.ds(row0, stage.shape[1]), c, :]
        src, dst = (stage.at[c], view) if to_hbm else (view, stage.at[c])
        copies.append(pltpu.make_async_copy(src, dst, sem))
    return copies


def _stage_store(stage, x):
    for c in range(stage.shape[0]):
        stage[c] = x[:, c * LANES:(c + 1) * LANES]


def _stage_load(stage):
    return jnp.concatenate([stage[c] for c in range(stage.shape[0])], axis=1)


_DISPATCH_SLOTS = 3


def _dispatch_kernel(dest_ref, pad_ref, h2_hbm, xs_hbm, buf, zero_row, sem_in, sem_out):
    i = pl.program_id(0)
    n = pl.num_programs(0)
    groups = buf.shape[1]
    tm = groups * SUBLANES
    n_pad = pad_ref.shape[2]
    slot = i % _DISPATCH_SLOTS

    def fetch(t, sl):
        return pltpu.make_async_copy(h2_hbm.at[pl.ds(t * groups, groups)], buf.at[sl],
                                     sem_in.at[sl])

    def wait_rows(sl):
        for _ in range(TOP_K + n_pad // tm):
            pltpu.make_async_copy(h2_hbm.at[pl.ds(0, groups)], buf.at[sl], sem_out.at[sl]).wait()
        rest = (n_pad % tm) // SUBLANES
        if rest:
            pltpu.make_async_copy(h2_hbm.at[pl.ds(0, rest)], buf.at[sl, pl.ds(0, rest)],
                                  sem_out.at[sl]).wait()

    @pl.when(i == 0)
    def _():
        zero_row[...] = jnp.zeros_like(zero_row)
        fetch(0, 0).start()

    @pl.when(i >= 2)
    def _():
        wait_rows((i + 1) % _DISPATCH_SLOTS)

    @pl.when(i + 1 < n)
    def _():
        fetch(i + 1, (i + 1) % _DISPATCH_SLOTS).start()

    fetch(i, slot).wait()

    def body(g, carry):
        rows = [[dest_ref[0, 0, k * tm + g * SUBLANES + u] for k in range(TOP_K)]
                for u in range(SUBLANES)]
        for u in range(SUBLANES):
            for k in range(TOP_K):
                pltpu.make_async_copy(buf.at[slot, g, :, u, :], xs_hbm.at[rows[u][k]],
                                      sem_out.at[slot]).start(priority=k % DMA_PRIORITIES)
        return carry
    lax.fori_loop(0, groups, body, 0)

    def pad_body(g, carry):
        rows = [pad_ref[0, 0, g * SUBLANES + u] for u in range(SUBLANES)]
        for u in range(SUBLANES):
            pltpu.make_async_copy(zero_row, xs_hbm.at[rows[u]],
                                  sem_out.at[slot]).start(priority=u % DMA_PRIORITIES)
        return carry
    lax.fori_loop(0, n_pad // SUBLANES, pad_body, 0)

    @pl.when(i == n - 1)
    def _():
        wait_rows(slot)

    @pl.when((i == n - 1) & (i >= 1))
    def _():
        wait_rows((i - 1) % _DISPATCH_SLOTS)


def _dispatch(dest_tiles, pad_tiles, h2_tiles, n_slots):
    nt = dest_tiles.shape[0]
    n_groups, n_slabs, _, _ = h2_tiles.shape
    groups = n_groups // nt
    assert pad_tiles.shape[2] % SUBLANES == 0

    def idx_spec(arr):
        return pl.BlockSpec((1, 1, arr.shape[2]), lambda i: (i, 0, 0), memory_space=pltpu.SMEM)

    return pl.pallas_call(
        _dispatch_kernel,
        out_shape=jax.ShapeDtypeStruct((n_slots, n_slabs, LANES), F32),
        grid=(nt,),
        in_specs=[idx_spec(dest_tiles), idx_spec(pad_tiles), pl.BlockSpec(memory_space=pl.ANY)],
        out_specs=pl.BlockSpec(memory_space=pl.ANY),
        scratch_shapes=[pltpu.VMEM((_DISPATCH_SLOTS, groups, n_slabs, SUBLANES, LANES), F32),
                        pltpu.VMEM((n_slabs, LANES), F32),
                        pltpu.SemaphoreType.DMA((_DISPATCH_SLOTS,)),
                        pltpu.SemaphoreType.DMA((_DISPATCH_SLOTS,))],
        compiler_params=_params("arbitrary"),
        name="dispatch",
    )(dest_tiles, pad_tiles, h2_tiles)


_EXPERT_IN_SLOTS = 3


def _experts_kernel(meta_ref, xs_hbm, wg_ref, wu_ref, wd_ref, ys_hbm,
                    xstage, ystage, wg_bf, wu_bf, wd_bf, sem_in, sem_out):
    j = pl.program_id(0)
    last = pl.num_programs(0) - 1
    n_used = meta_ref[0]
    slot = j % 2

    def fetch(blk, sl):
        for cp in _slab_copies(xstage.at[sl], xs_hbm, blk * MOE_BLOCK, sem_in.at[sl], False):
            cp.start()

    def wait_out(sl):
        for cp in _slab_copies(ystage.at[sl], ys_hbm, 0, sem_out.at[sl], True):
            cp.wait()

    in_slot = j % _EXPERT_IN_SLOTS

    @pl.when((j == 0) & (n_used > 0))
    def _():
        fetch(0, 0)

    @pl.when((j == 0) & (n_used > 1))
    def _():
        fetch(1, 1)

    @pl.when(j + 2 < n_used)
    def _():
        fetch(j + 2, (j + 2) % _EXPERT_IN_SLOTS)

    new_expert = (j == 0) | (meta_ref[1 + j] != meta_ref[jnp.maximum(j, 1)])

    @pl.when((j < n_used) & new_expert)
    def _():
        wg_bf[...] = wg_ref[0].astype(BF16)
        wu_bf[...] = wu_ref[0].astype(BF16)
        wd_bf[...] = wd_ref[0].astype(BF16)

    @pl.when(j >= 2)
    def _():
        wait_out(slot)

    @pl.when(j < n_used)
    def _():
        for cp in _slab_copies(xstage.at[in_slot], xs_hbm, 0, sem_in.at[in_slot], False):
            cp.wait()
        xb = _stage_load(xstage.at[in_slot]).astype(BF16)
        a = _silu(_dot(xb, wg_bf[...])) * _dot(xb, wu_bf[...])
        _stage_store(ystage.at[slot], _dot(a.astype(BF16), wd_bf[...]))

    @pl.when(j >= n_used)
    def _():
        ystage[slot] = jnp.zeros(ystage.shape[1:], F32)

    for cp in _slab_copies(ystage.at[slot], ys_hbm, j * MOE_BLOCK, sem_out.at[slot], True):
        cp.start()

    @pl.when(j == last)
    def _():
        wait_out(slot)

    @pl.when((j == last) & (j >= 1))
    def _():
        wait_out(1 - slot)


def _experts(meta, xs, wg, wu, wd):
    n_slots, n_slabs, _ = xs.shape
    n_blk = n_slots // MOE_BLOCK
    d = wg.shape[1]
    ff = wg.shape[2]

    def wspec(shape):
        return pl.BlockSpec((1,) + shape, lambda j, meta: (meta[1 + j], 0, 0))

    in_stage = pltpu.VMEM((_EXPERT_IN_SLOTS, n_slabs, MOE_BLOCK, LANES), F32)
    stage = pltpu.VMEM((2, n_slabs, MOE_BLOCK, LANES), F32)
    return pl.pallas_call(
        _experts_kernel,
        out_shape=jax.ShapeDtypeStruct(xs.shape, F32),
        grid_spec=pltpu.PrefetchScalarGridSpec(
            num_scalar_prefetch=1,
            grid=(n_blk,),
            in_specs=[pl.BlockSpec(memory_space=pl.ANY),
                      wspec((d, ff)), wspec((d, ff)), wspec((ff, d))],
            out_specs=pl.BlockSpec(memory_space=pl.ANY),
            scratch_shapes=[in_stage, stage, pltpu.VMEM((d, ff), BF16),
                            pltpu.VMEM((d, ff), BF16), pltpu.VMEM((ff, d), BF16),
                            pltpu.SemaphoreType.DMA((_EXPERT_IN_SLOTS,)),
                            pltpu.SemaphoreType.DMA((2,))],
        ),
        compiler_params=_params("arbitrary"),
        name="experts",
    )(meta, xs, wg, wu, wd)


def _combine_kernel(dest_cur_ref, dest_next_ref, ys_hbm, x1_ref, gate_ref, rw_ref, o_ref,
                    buf, sem):
    i = pl.program_id(0)
    slot = i % 2
    tm = x1_ref.shape[0]
    groups = tm // SUBLANES

    def start(idx_ref, sl):
        for k in range(TOP_K):
            def body(g, carry, k=k):
                rows = [idx_ref[0, 0, k * tm + g * SUBLANES + u] for u in range(SUBLANES)]
                for u in range(SUBLANES):
                    pltpu.make_async_copy(ys_hbm.at[rows[u]], buf.at[sl, k, g, :, u, :],
                                          sem.at[sl, k]).start(priority=u % DMA_PRIORITIES)
                return carry
            lax.fori_loop(0, groups, body, 0)

    def wait(sl, k):
        for u in range(SUBLANES):
            pltpu.make_async_copy(ys_hbm.at[pl.ds(0, groups)], buf.at[sl, k, :, :, u, :],
                                  sem.at[sl, k]).wait()

    @pl.when(i == 0)
    def _():
        start(dest_cur_ref, 0)

    @pl.when(i + 1 < pl.num_programs(0))
    def _():
        start(dest_next_ref, 1 - slot)

    rw = rw_ref[...]
    moe = None
    for k in range(TOP_K):
        wait(slot, k)
        rows = jnp.concatenate([buf[slot, k, :, c].reshape(tm, LANES)
                                for c in range(buf.shape[3])], axis=1)
        term = rw[:, k:k + 1] * rows
        moe = term if moe is None else moe + term
    o_ref[...] = x1_ref[...] + gate_ref[0] * moe


def _combine(dest_tiles, ys, x1, gate, rw):
    n_tok, d = x1.shape
    tm = TOKEN_TILE
    nt = n_tok // tm
    per_batch = nt // gate.shape[0]
    idx_blk = (1, 1, TOP_K * tm)
    return pl.pallas_call(
        _combine_kernel,
        out_shape=jax.ShapeDtypeStruct((n_tok, d), F32),
        grid=(nt,),
        in_specs=[
            pl.BlockSpec(idx_blk, lambda i: (i, 0, 0), memory_space=pltpu.SMEM),
            pl.BlockSpec(idx_blk, lambda i: (jnp.minimum(i + 1, nt - 1), 0, 0),
                         memory_space=pltpu.SMEM),
            pl.BlockSpec(memory_space=pl.ANY),
            pl.BlockSpec((tm, d), lambda i: (i, 0)),
            pl.BlockSpec((1, 1, d), lambda i: (i // per_batch, 0, 0)),
            pl.BlockSpec((tm, ROUTE_LANES), lambda i: (i, 0)),
        ],
        out_specs=pl.BlockSpec((tm, d), lambda i: (i, 0)),
        scratch_shapes=[pltpu.VMEM((2, TOP_K, tm // SUBLANES, d // LANES, SUBLANES, LANES), F32),
                        pltpu.SemaphoreType.DMA((2, TOP_K))],
        compiler_params=_params("arbitrary"),
        name="combine",
    )(dest_tiles, dest_tiles, ys, x1, gate, rw)


def _layer(x, ctx, c, c_ctx, w_ada, b_ada, norm1_g, norm2_g, w_in, q_norm_g, k_norm_g, attn_out_g,
           lb, hgrn_out_g, w_out, w_router_grp, b_router_grp, w_router_exp, b_router_exp,
           w_exp_gate, w_exp_up, w_exp_down):
    b, s, d = x.shape
    n_ctx = ctx.shape[1]
    assert n_ctx % TOKEN_TILE == 0 and s % TOKEN_TILE == 0 and s % GRID_W == 0
    assert n_ctx % HGRN_CHUNK == 0 and (n_ctx + s) % ATTN_KV_TILE == 0
    n_all = n_ctx + s

    cond = jnp.zeros((2 * SUBLANES, d), F32).at[:b].set(c).at[b].set(c_ctx)
    assert b + 1 <= cond.shape[0]
    mods = _adaln(cond, w_ada, b_ada)[:b + 1].reshape(b + 1, 1, 6, d)
    sh1, sc1, gt1, sh2, sc2, gt2 = (mods[:, :, m] for m in range(6))

    scale_q = ATTN_HEAD_DIM ** -0.5 * np.log2(np.e)
    qkg = jnp.concatenate([jnp.tile(q_norm_g, ATTN_HEADS) * scale_q,
                           jnp.tile(k_norm_g, ATTN_KV_HEADS)]).reshape(1, _QK_WIDTH)
    cos, sin = _rope_tables(n_ctx, s)
    qa, ka, vt, qr, ff, fb, ir, gr = _in_proj(
        ctx, x, sh1, sc1, norm1_g.reshape(1, d), w_in.astype(BF16), qkg, cos, sin)
    oa = _attention(qa, ka, vt)
    o_f, o_b = _hgrn(qr, ir, ff, fb, lb, n_ctx)

    w_router = jnp.zeros((d, LANES), F32)
    w_router = w_router.at[:, :N_EXPERTS].set(w_router_exp)
    w_router = w_router.at[:, _ROUTE_GROUP_LANE0:_ROUTE_GROUP_LANE0 + N_GROUPS].set(w_router_grp)
    b_router = jnp.zeros((1, LANES), F32)
    b_router = b_router.at[0, :N_EXPERTS].set(b_router_exp)
    b_router = b_router.at[0, _ROUTE_GROUP_LANE0:_ROUTE_GROUP_LANE0 + N_GROUPS].set(b_router_grp)
    wr_hi, wr_lo = _split_bf16(w_router)
    x1, h2, ri, rw, counts = _out_proj(
        oa, o_f, o_b, gr, x, gt1[:b], sh2[:b], sc2[:b], attn_out_g.reshape(1, -1),
        hgrn_out_g.reshape(1, -1), norm2_g.reshape(1, d), w_out.astype(BF16), wr_hi, wr_lo,
        b_router, n_ctx)

    n_tok = b * s
    counts = counts[0, :N_EXPERTS]
    padded = (counts + MOE_BLOCK - 1) // MOE_BLOCK * MOE_BLOCK
    pend = jnp.cumsum(padded)
    pstart = pend - padded
    n_blk = n_tok * TOP_K // MOE_BLOCK + N_EXPERTS
    def lookup(table, idx):
        hit = idx[..., None] == jnp.arange(table.shape[0], dtype=jnp.int32)
        return jnp.sum(jnp.where(hit, table.astype(jnp.int32), 0), axis=-1)

    dest = lookup(pstart, ri[:, :TOP_K]) + ri[:, TOP_K:2 * TOP_K]
    dest_tiles = dest.reshape(dest.shape[0], 1, -1)
    blk_start = jnp.arange(n_blk, dtype=jnp.int32) * MOE_BLOCK
    blk_e = jnp.minimum(jnp.sum(blk_start[:, None] >= pend[None, :], axis=1), N_EXPERTS - 1)
    n_used = pend[-1] // MOE_BLOCK
    meta = jnp.concatenate([n_used[None], blk_e]).astype(jnp.int32)

    n_slots = n_blk * MOE_BLOCK
    n_pad = n_slots - n_tok * TOP_K
    gap_start = jnp.concatenate([pstart + counts, pend[-1:]])
    gap_size = jnp.concatenate([padded - counts, n_slots - pend[-1:]])
    gap_end = jnp.cumsum(gap_size)
    pad_i = jnp.arange(n_pad, dtype=jnp.int32)
    gap = jnp.sum(pad_i[:, None] >= gap_end[None, :], axis=1)
    pad_slots = lookup(gap_start - (gap_end - gap_size), gap) + pad_i
    pad_tiles = pad_slots.astype(jnp.int32).reshape(dest_tiles.shape[0], 1, -1)

    xs = _dispatch(dest_tiles, pad_tiles, h2, n_slots)
    ys = _experts(meta, xs, w_exp_gate, w_exp_up, w_exp_down)
    out = _combine(dest_tiles, ys, x1.reshape(n_tok, d), gt2[:b], rw.reshape(n_tok, ROUTE_LANES))
    return out.reshape(b, s, d)


def kernel(x, c, ctx, c_ctx, w_ada, b_ada, norm1_g, norm2_g, w_in, q_norm_g, k_norm_g, attn_out_g,
           hgrn_lb, hgrn_out_g, w_out, w_router_grp, b_router_grp, w_router_exp, b_router_exp,
           w_exp_gate, w_exp_up, w_exp_down):
    depth = w_in.shape[0]
    assert depth == 1, "context stream update between layers is not implemented"
    lb_all = jnp.cumsum(jax.nn.softmax(hgrn_lb.astype(F32), axis=1), axis=1)
    layer = 0
    return _layer(x, ctx, c, c_ctx, w_ada[layer], b_ada[layer], norm1_g[layer], norm2_g[layer],
                  w_in[layer], q_norm_g[layer], k_norm_g[layer], attn_out_g[layer],
                  lb_all[:, layer], hgrn_out_g[layer], w_out[layer], w_router_grp[layer],
                  b_router_grp[layer], w_router_exp[layer], b_router_exp[layer],
                  w_exp_gate[layer], w_exp_up[layer], w_exp_down[layer])
```

```python
import functools

import numpy as np
import jax
import jax.numpy as jnp
from jax import lax
from jax.experimental import pallas as pl
from jax.experimental.pallas import tpu as pltpu

F32 = jnp.float32
BF16 = jnp.bfloat16

GRID_W = 64
EPS = 1e-6
ATTN_HEADS = 8
ATTN_KV_HEADS = 2
ATTN_HEAD_DIM = 64
ATTN_GROUP = ATTN_HEADS // ATTN_KV_HEADS
ATTN_WIDTH = ATTN_HEADS * ATTN_HEAD_DIM
KV_WIDTH = ATTN_KV_HEADS * ATTN_HEAD_DIM
ROPE_THETA = 10000.0
HGRN_HEADS = 4
HGRN_HEAD_DIM = 128
HGRN_WIDTH = HGRN_HEADS * HGRN_HEAD_DIM
N_GROUPS = 4
EXPERTS_PER_GROUP = 8
N_EXPERTS = N_GROUPS * EXPERTS_PER_GROUP
TOP_K = 2

LANES = 128
SUBLANES = 8
DMA_PRIORITIES = 2
VMEM_LIMIT_BYTES = 48 * 1024 * 1024

TOKEN_TILE = 256
OUT_PROJ_SPLIT = 2
ATTN_Q_TILE = 1024
ATTN_KV_TILE = 256
ATTN_KV_UNROLL = 17
ATTN_BOUND_SLACK = 1.02
ATTN_BOUND_MAX = 60.0
HGRN_CHUNK = 64
HGRN_STEP_CHUNKS = 4
MOE_BLOCK = 256
ROUTE_LANES = 8
NEG_BIG = -1e30

_QA0 = 0
_KA0 = _QA0 + ATTN_WIDTH
_VA0 = _KA0 + KV_WIDTH
_QR0 = _VA0 + KV_WIDTH
_FF0 = _QR0 + HGRN_WIDTH
_FB0 = _FF0 + HGRN_WIDTH
_IR0 = _FB0 + HGRN_WIDTH
_GR0 = _IR0 + HGRN_WIDTH
_QK_WIDTH = ATTN_WIDTH + KV_WIDTH


def _dot(a, b):
    return jnp.dot(a, b, preferred_element_type=F32)


def _dot_nt(a, b):
    return lax.dot_general(a, b, (((1,), (1,)), ((), ())), preferred_element_type=F32)


def _dot_tn(a, b):
    return lax.dot_general(a, b, (((0,), (0,)), ((), ())), preferred_element_type=F32)


def _split_bf16(x):
    hi = x.astype(BF16)
    lo = (x - hi.astype(F32)).astype(BF16)
    return hi, lo


def _sigmoid(x):
    return 1.0 / (1.0 + jnp.exp(-x))


def _silu(x):
    return x * _sigmoid(x)


def _params(*sem):
    return pltpu.CompilerParams(dimension_semantics=sem, vmem_limit_bytes=VMEM_LIMIT_BYTES)


def _head_sum_matrix():
    idx = np.arange(LANES) // ATTN_HEAD_DIM
    return jnp.asarray(idx[:, None] == idx[None, :], dtype=BF16)


def _head_rms_scale(x, ones_bd):
    ssq = _dot((x * x).astype(BF16), ones_bd)
    return lax.rsqrt(ssq * (1.0 / ATTN_HEAD_DIM) + EPS)


def _adaln_kernel(cond_ref, w_ref, b_ref, o_ref):
    s = _silu(cond_ref[...])
    s_hi, s_lo = _split_bf16(s)
    w_hi, w_lo = _split_bf16(w_ref[...])
    o_ref[...] = _dot(s_hi, w_hi) + _dot(s_lo, w_hi) + _dot(s_hi, w_lo) + b_ref[...]


def _adaln(cond, w_ada, b_ada):
    rows, d = cond.shape
    n = w_ada.shape[1]
    tn = n // 6
    return pl.pallas_call(
        _adaln_kernel,
        out_shape=jax.ShapeDtypeStruct((rows, n), F32),
        grid=(n // tn,),
        in_specs=[pl.BlockSpec((rows, d), lambda j: (0, 0)),
                  pl.BlockSpec((d, tn), lambda j: (0, j)),
                  pl.BlockSpec((1, tn), lambda j: (0, j))],
        out_specs=pl.BlockSpec((rows, tn), lambda j: (0, j)),
        compiler_params=_params("arbitrary"),
        name="adaln",
    )(cond, w_ada, b_ada.reshape(1, n))


def _rope_tables(n_ctx, n_lat):
    half = ATTN_HEAD_DIM // 2
    freqs = ROPE_THETA ** (-np.arange(0, half, 2, dtype=np.float64) / half)
    tok = np.arange(n_lat)
    pos = np.stack([tok // GRID_W, tok % GRID_W], axis=1).astype(np.float64)
    lane = np.arange(ATTN_HEAD_DIM)
    axis = lane // half
    fi = (lane % half) // 2
    ang = pos[:, axis] * freqs[fi][None, :]
    sign = np.where(lane % 2 == 1, 1.0, -1.0)
    cos = np.concatenate([np.ones((n_ctx, ATTN_HEAD_DIM)), np.cos(ang)], axis=0)
    sin = np.concatenate([np.zeros((n_ctx, ATTN_HEAD_DIM)), np.sin(ang) * sign], axis=0)
    reps = LANES // ATTN_HEAD_DIM
    return (jnp.asarray(np.tile(cos, (1, reps)), F32), jnp.asarray(np.tile(sin, (1, reps)), F32))


def _in_proj_kernel(n_ctx_tiles, ctx_ref, x_ref, shift_ref, scale_ref, g1_ref, w_ref, qkg_ref,
                    ones_ref, cos_ref, sin_ref,
                    qa_ref, ka_ref, vt_ref, qr_ref, ff_ref, fb_ref, ir_ref, gr_ref):
    i = pl.program_id(1)
    xt = jnp.where(i < n_ctx_tiles, ctx_ref[0], x_ref[0])
    ms = jnp.mean(xt * xt, axis=-1, keepdims=True)
    h = xt * lax.rsqrt(ms + EPS) * g1_ref[...]
    h = h * (1.0 + scale_ref[0]) + shift_ref[0]
    p = _dot(h.astype(BF16), w_ref[...])

    ones_bd = ones_ref[...]
    cos = cos_ref[...]
    sin = sin_ref[...]
    even = lax.broadcasted_iota(jnp.int32, cos.shape, 1) % 2 == 0
    slabs = []
    for c0 in range(0, _QK_WIDTH, LANES):
        t = p[:, _QA0 + c0:_QA0 + c0 + LANES]
        t = t * _head_rms_scale(t, ones_bd) * qkg_ref[:, c0:c0 + LANES]
        partner = jnp.where(even, pltpu.roll(t, LANES - 1, 1), pltpu.roll(t, 1, 1))
        slabs.append((t * cos + partner * sin).astype(BF16))
    qa_ref[0] = jnp.concatenate(slabs[:ATTN_WIDTH // LANES], axis=1)
    k_all = jnp.concatenate(slabs[ATTN_WIDTH // LANES:], axis=1)
    vt_all = p[:, _VA0:_VA0 + KV_WIDTH].T
    ones = jnp.ones((vt_ref.shape[2] - ATTN_HEAD_DIM, vt_all.shape[1]), F32)
    for hd in range(ATTN_KV_HEADS):
        cols = slice(hd * ATTN_HEAD_DIM, (hd + 1) * ATTN_HEAD_DIM)
        ka_ref[0, hd] = k_all[:, cols]
        vt_ref[0, hd] = jnp.concatenate([vt_all[cols], ones], axis=0).astype(BF16)

    qr_ref[0] = (_silu(p[:, _QR0:_QR0 + HGRN_WIDTH]) * (HGRN_HEAD_DIM ** -0.5)).astype(BF16)
    ff_ref[0] = p[:, _FF0:_FF0 + HGRN_WIDTH]
    fb_ref[0] = p[:, _FB0:_FB0 + HGRN_WIDTH]
    ir_ref[0] = p[:, _IR0:_IR0 + HGRN_WIDTH].astype(BF16)
    gr_ref[0] = _silu(p[:, _GR0:_GR0 + HGRN_WIDTH]).astype(BF16)


def _in_proj(ctx, x, shift, scale, g1, w_in, qkg, cos, sin):
    b, n_ctx, d = ctx.shape
    s = x.shape[1]
    tm = TOKEN_TILE
    nct = n_ctx // tm
    n_all = n_ctx + s
    nt = n_all // tm
    pw = w_in.shape[1]
    ones_bd = _head_sum_matrix()

    def tok_spec(w):
        return pl.BlockSpec((1, tm, w), lambda bi, i: (bi, i, 0))

    mod_spec = pl.BlockSpec((1, 1, d), lambda bi, i: (jnp.where(i < nct, b, bi), 0, 0))
    outs = [(HGRN_WIDTH, BF16), (HGRN_WIDTH, F32), (HGRN_WIDTH, F32), (HGRN_WIDTH, BF16),
            (HGRN_WIDTH, BF16)]
    vt_rows = ATTN_HEAD_DIM + 2 * SUBLANES
    lat_spec = pl.BlockSpec((1, tm, ATTN_WIDTH), lambda bi, i: (bi, jnp.maximum(i - nct, 0), 0))
    k_spec = pl.BlockSpec((1, ATTN_KV_HEADS, tm, ATTN_HEAD_DIM), lambda bi, i: (bi, 0, i, 0))
    vt_spec = pl.BlockSpec((1, ATTN_KV_HEADS, vt_rows, tm), lambda bi, i: (bi, 0, 0, i))
    return pl.pallas_call(
        functools.partial(_in_proj_kernel, nct),
        out_shape=[jax.ShapeDtypeStruct((b, s, ATTN_WIDTH), BF16),
                   jax.ShapeDtypeStruct((b, ATTN_KV_HEADS, n_all, ATTN_HEAD_DIM), BF16),
                   jax.ShapeDtypeStruct((b, ATTN_KV_HEADS, vt_rows, n_all), BF16)]
        + [jax.ShapeDtypeStruct((b, n_all, w), dt) for w, dt in outs],
        grid=(b, nt),
        in_specs=[
            pl.BlockSpec((1, tm, d), lambda bi, i: (bi, jnp.minimum(i, nct - 1), 0)),
            pl.BlockSpec((1, tm, d), lambda bi, i: (bi, jnp.maximum(i - nct, 0), 0)),
            mod_spec, mod_spec,
            pl.BlockSpec((1, d), lambda bi, i: (0, 0)),
            pl.BlockSpec((d, pw), lambda bi, i: (0, 0)),
            pl.BlockSpec((1, _QK_WIDTH), lambda bi, i: (0, 0)),
            pl.BlockSpec((LANES, LANES), lambda bi, i: (0, 0)),
            pl.BlockSpec((tm, LANES), lambda bi, i: (i, 0)),
            pl.BlockSpec((tm, LANES), lambda bi, i: (i, 0)),
        ],
        out_specs=[lat_spec, k_spec, vt_spec] + [tok_spec(w) for w, _ in outs],
        compiler_params=_params("arbitrary", "arbitrary"),
        name="in_proj",
    )(ctx, x, shift, scale, g1, w_in, qkg, ones_bd, cos, sin)


def _attention_kernel(n_kv_tiles, q_ref, k_ref, vt_ref, o_ref, qs_ref, s_ref, ksq_ref):
    tq = q_ref.shape[1]
    cols = ATTN_GROUP * tq
    for h in range(ATTN_GROUP):
        qs_ref[h * tq:(h + 1) * tq, :] = q_ref[0, :, h * ATTN_HEAD_DIM:(h + 1) * ATTN_HEAD_DIM]

    @pl.when(pl.program_id(2) == 0)
    def _():
        k = k_ref[0, 0].astype(F32)
        ksq = jnp.max(jnp.sum(k * k, axis=1, keepdims=True), axis=0, keepdims=True)
        ksq_ref[...] = jnp.broadcast_to(ksq, ksq_ref.shape)

    def tile(j):
        return pl.ds(pl.multiple_of(j * ATTN_KV_TILE, ATTN_KV_TILE), ATTN_KV_TILE)

    def scores(j):
        return _dot_nt(k_ref[0, 0, tile(j), :], qs_ref[...])

    def weighted(j, p):
        return _dot(vt_ref[0, 0, :, tile(j)], p)

    def finish(acc):
        o = acc[:ATTN_HEAD_DIM] / acc[ATTN_HEAD_DIM:ATTN_HEAD_DIM + 1]
        o_ref[0] = jnp.concatenate(
            [o[:, h * tq:(h + 1) * tq].T for h in range(ATTN_GROUP)], axis=1).astype(o_ref.dtype)

    qf = qs_ref[...].astype(F32)
    qsq = _dot_nt(jnp.ones((SUBLANES, ATTN_HEAD_DIM), BF16), (qf * qf).astype(BF16))[0:1]
    bound = jnp.sqrt(qsq * ksq_ref[0:1, 0:1]) * ATTN_BOUND_SLACK
    acc0 = jnp.zeros((vt_ref.shape[2], cols), F32)
    safe = jnp.max(bound) <= ATTN_BOUND_MAX

    @pl.when(safe)
    def _():
        def absorb(j, acc):
            return acc + weighted(j, jnp.exp2(scores(j) - bound).astype(BF16))

        def group(i, acc):
            for u in range(ATTN_KV_UNROLL):
                acc = absorb(ATTN_KV_UNROLL * i + u, acc)
            return acc

        n_groups = n_kv_tiles // ATTN_KV_UNROLL
        acc = lax.fori_loop(0, n_groups, group, acc0)
        for j in range(n_groups * ATTN_KV_UNROLL, n_kv_tiles):
            acc = absorb(j, acc)
        finish(acc)

    @pl.when(jnp.logical_not(safe))
    def _():
        def score(j, slot):
            s_ref[slot] = scores(j)

        def absorb(j, slot, carry):
            m, acc = carry
            s = s_ref[slot]
            m_new = jnp.maximum(m, jnp.max(s, axis=0, keepdims=True))
            p = jnp.exp2(s - m_new).astype(BF16)
            return m_new, jnp.exp2(m - m_new) * acc + weighted(j, p)

        def pair(i, carry):
            score(2 * i + 1, 1)
            carry = absorb(2 * i, 0, carry)
            score(2 * i + 2, 0)
            return absorb(2 * i + 1, 1, carry)

        n_pairs = (n_kv_tiles - 1) // 2
        score(0, 0)
        carry = lax.fori_loop(0, n_pairs, pair, (jnp.full((1, cols), NEG_BIG, F32), acc0))
        if n_kv_tiles % 2 == 1:
            carry = absorb(n_kv_tiles - 1, 0, carry)
        else:
            score(n_kv_tiles - 1, 1)
            carry = absorb(n_kv_tiles - 2, 0, carry)
            carry = absorb(n_kv_tiles - 1, 1, carry)
        finish(carry[1])


def _attention(qa, k_heads, vt_heads):
    b, s, _ = qa.shape
    n_all = k_heads.shape[2]
    tq = ATTN_Q_TILE
    assert s % tq == 0 and n_all % ATTN_KV_TILE == 0
    gw = ATTN_GROUP * ATTN_HEAD_DIM
    vt_rows = vt_heads.shape[2]
    return pl.pallas_call(
        functools.partial(_attention_kernel, n_all // ATTN_KV_TILE),
        out_shape=jax.ShapeDtypeStruct((b, s, ATTN_WIDTH), BF16),
        grid=(b, ATTN_KV_HEADS, s // tq),
        in_specs=[pl.BlockSpec((1, tq, gw), lambda bi, kv, i: (bi, i, kv)),
                  pl.BlockSpec((1, 1, n_all, ATTN_HEAD_DIM), lambda bi, kv, i: (bi, kv, 0, 0)),
                  pl.BlockSpec((1, 1, vt_rows, n_all), lambda bi, kv, i: (bi, kv, 0, 0))],
        out_specs=pl.BlockSpec((1, tq, gw), lambda bi, kv, i: (bi, i, kv)),
        scratch_shapes=[pltpu.VMEM((ATTN_GROUP * tq, ATTN_HEAD_DIM), BF16),
                        pltpu.VMEM((2, ATTN_KV_TILE, ATTN_GROUP * tq), F32),
                        pltpu.VMEM((SUBLANES, LANES), F32)],
        compiler_params=_params("arbitrary", "arbitrary", "arbitrary"),
        name="attention",
    )(qa, k_heads, vt_heads)


def _hgrn_masks(reverse):
    c = HGRN_CHUNK
    t = lax.broadcasted_iota(jnp.int32, (c, c), 0)
    u = lax.broadcasted_iota(jnp.int32, (c, c), 1)
    tri = (u >= t) if reverse else (u <= t)
    levels = []
    size = c // 2
    while size >= 1:
        same_parent = (t // (2 * size)) == (u // (2 * size))
        levels.append((size, same_parent if 2 * size < c else None))
        size //= 2
    levels.append((0, t == u))
    return tri.astype(BF16), levels


def _hgrn_chunk(q, fr, v, lb, state, reverse, tri, levels):
    c = HGRN_CHUNK
    coarse_levels = [lv for lv in levels if lv[0] >= SUBLANES]
    fine_levels = [lv for lv in levels if lv[0] < SUBLANES]
    f = lb + (1.0 - lb) * _sigmoid(fr)
    k = 1.0 - f
    g_hi, g_lo = _split_bf16(jnp.log2(f))
    bcum = _dot(tri, g_hi) + _dot(tri, g_lo)
    qf = q.astype(F32)
    yield None

    st = state()
    end = 0 if reverse else c - 1
    b_end = bcum[end:end + 1, :]
    inter = _dot_nt((qf * jnp.exp2(bcum)).astype(BF16), st.astype(BF16))
    ke = (k * jnp.exp2(b_end - bcum)).astype(BF16)
    st_add = _dot_tn(v, ke)

    products = []
    for size, mask in coarse_levels:
        q_rows, k_rows = [], []
        zeros = jnp.zeros((size, qf.shape[1]), BF16)
        for p0 in range(0, c, 2 * size):
            early = slice(p0, p0 + size)
            late = slice(p0 + size, p0 + 2 * size)
            r = p0 + size if reverse else p0 + size - 1
            ref = bcum[r:r + 1, :]
            q_sl, k_sl = (early, late) if reverse else (late, early)
            q_blk = (qf[q_sl] * jnp.exp2(bcum[q_sl] - ref)).astype(BF16)
            k_blk = (k[k_sl] * jnp.exp2(ref - bcum[k_sl])).astype(BF16)
            q_rows += [q_blk, zeros] if reverse else [zeros, q_blk]
            k_rows += [zeros, k_blk] if reverse else [k_blk, zeros]
        products.append((mask, _dot_nt(jnp.concatenate(q_rows, axis=0),
                                       jnp.concatenate(k_rows, axis=0))))
    yield None

    row = lax.broadcasted_iota(jnp.int32, (c, 1), 0)
    sub = lax.broadcasted_iota(jnp.int32, (SUBLANES, 1), 0)
    fine_products = []
    for size, mask in fine_levels:
        if size == 0:
            q_l, k_l = q, k.astype(BF16)
        else:
            late = (row // size) % 2 == 1
            q_side = jnp.logical_not(late) if reverse else late
            if size == 1:
                fac_q, fac_k = f, None
            else:
                groups = []
                for r0 in range(0, c, SUBLANES):
                    ref = None
                    for p0 in range(0, SUBLANES, 2 * size):
                        r = r0 + p0 + (size if reverse else size - 1)
                        piece = jnp.broadcast_to(bcum[r:r + 1, :], (SUBLANES, bcum.shape[1]))
                        ref = piece if ref is None else jnp.where(sub >= p0, piece, ref)
                    groups.append(ref)
                ref = jnp.concatenate(groups, axis=0)
                fac_q = fac_k = jnp.exp2(jnp.where(q_side, bcum - ref, ref - bcum))
            q_l = jnp.where(q_side, qf * fac_q, 0.0).astype(BF16)
            k_l = jnp.where(q_side, 0.0, k if fac_k is None else k * fac_k).astype(BF16)
        fine_products.append((mask, _dot_nt(q_l, k_l)))
    a = None
    for mask, a_l in products:
        if mask is not None:
            a_l = jnp.where(mask, a_l, 0.0)
        a = a_l if a is None else a + a_l
    yield None

    for mask, a_l in fine_products:
        a = a + jnp.where(mask, a_l, 0.0)
    intra = _dot(a.astype(BF16), v)
    st_new = st * jnp.exp2(b_end) + st_add
    yield None

    yield inter + intra, st_new


_HGRN_STAGES = 5


def _hgrn_kernel(qf_ref, if_ref, ff_ref, qb_ref, ib_ref, fb_ref, lb_ref, of_ref, ob_ref,
                 sf_ref, sb_ref):
    @pl.when(pl.program_id(1) == 0)
    def _():
        sf_ref[...] = jnp.zeros_like(sf_ref)
        sb_ref[...] = jnp.zeros_like(sb_ref)

    dirs = ((False, qf_ref, if_ref, ff_ref, of_ref, sf_ref),
            (True, qb_ref, ib_ref, fb_ref, ob_ref, sb_ref))
    masks = [_hgrn_masks(reverse) for reverse, *_ in dirs]
    n_sub = qf_ref.shape[1] // HGRN_CHUNK
    latest = {}
    chunks = []
    for t in range(n_sub):
        for h in range(HGRN_HEADS):
            sl = slice(h * HGRN_HEAD_DIM, (h + 1) * HGRN_HEAD_DIM)
            for d, (reverse, q_ref, i_ref, f_ref, o_ref, s_ref) in enumerate(dirs):
                r0 = (n_sub - 1 - t if reverse else t) * HGRN_CHUNK
                rows = slice(r0, r0 + HGRN_CHUNK)

                def state(h=h, d=d, s_ref=s_ref):
                    return latest[h, d] if (h, d) in latest else s_ref[h]

                gen = _hgrn_chunk(q_ref[0, rows, sl], f_ref[0, rows, sl], i_ref[0, rows, sl],
                                  lb_ref[d:d + 1, sl], state, reverse, *masks[d])
                chunks.append((gen, o_ref, s_ref, h, d, rows, sl, t == n_sub - 1))

    assert len(dirs) * HGRN_HEADS >= _HGRN_STAGES
    for step in range(len(chunks) + _HGRN_STAGES - 1):
        for stage in range(_HGRN_STAGES):
            idx = step - stage
            if 0 <= idx < len(chunks):
                gen, o_ref, s_ref, h, d, rows, sl, is_last = chunks[idx]
                result = next(gen)
                if stage == _HGRN_STAGES - 1:
                    o, st = result
                    o_ref[0, rows, sl] = o
                    latest[h, d] = st
                    if is_last:
                        s_ref[h] = st


def _hgrn(qr, ir, ff, fb, lb, n_ctx):
    b, n_all, w = qr.shape
    c = HGRN_CHUNK * HGRN_STEP_CHUNKS
    assert n_ctx % c == 0 and n_all % c == 0
    nc = n_all // c
    ncc = n_ctx // c

    def fwd(bi, j):
        return (bi, j, 0)

    def bwd(bi, j):
        return (bi, jnp.where(j < ncc, ncc - 1 - j, nc - 1 - (j - ncc)), 0)

    blk = (1, c, w)
    state = pltpu.VMEM((HGRN_HEADS, HGRN_HEAD_DIM, HGRN_HEAD_DIM), F32)
    return pl.pallas_call(
        _hgrn_kernel,
        out_shape=[jax.ShapeDtypeStruct((b, n_all, w), F32)] * 2,
        grid=(b, nc),
        in_specs=[pl.BlockSpec(blk, fwd), pl.BlockSpec(blk, fwd), pl.BlockSpec(blk, fwd),
                  pl.BlockSpec(blk, bwd), pl.BlockSpec(blk, bwd), pl.BlockSpec(blk, bwd),
                  pl.BlockSpec((2, w), lambda bi, j: (0, 0))],
        out_specs=[pl.BlockSpec(blk, fwd), pl.BlockSpec(blk, bwd)],
        scratch_shapes=[state, state],
        compiler_params=_params("arbitrary", "arbitrary"),
        name="hgrn",
    )(qr, ir, ff, qr, ir, fb, lb)


_ROUTE_GROUP_LANE0 = N_EXPERTS


def _lane_min_index(cond, lane):
    return jnp.min(jnp.where(cond, lane, LANES), axis=-1, keepdims=True)


def _out_proj_kernel(oa_ref, of_ref, ob_ref, gr_ref, x_ref, gate_ref, shift_ref, scale_ref,
                     ag_ref, hg_ref, g2_ref, w_ref, ones_ref, wr_hi_ref, wr_lo_ref, br_ref,
                     x1_ref, h2_ref, ri_ref, rw_ref, cnt_ref, carry_ref):
    first = (pl.program_id(0) == 0) & (pl.program_id(1) == 0)

    @pl.when(first)
    def _():
        carry_ref[...] = jnp.zeros_like(carry_ref)

    tm = x_ref.shape[1]
    hm = tm // OUT_PROJ_SPLIT
    ones_bd = ones_ref[...]
    counts = [carry_ref[0:1, :]]

    def rows_chain(r0):
        rs = slice(r0, r0 + hm)
        ts = [oa_ref[0, rs, c0:c0 + LANES].astype(F32) for c0 in range(0, ATTN_WIDTH, LANES)]
        scales = [_head_rms_scale(t, ones_bd) for t in ts]
        orr = of_ref[0, rs] + ob_ref[0, rs]
        yield

        slabs = [(t * sc * ag_ref[:, c * LANES:(c + 1) * LANES]).astype(BF16)
                 for c, (t, sc) in enumerate(zip(ts, scales))]
        parts = []
        for h in range(HGRN_HEADS):
            oh = orr[:, h * HGRN_HEAD_DIM:(h + 1) * HGRN_HEAD_DIM]
            parts.append(oh * lax.rsqrt(jnp.mean(oh * oh, axis=-1, keepdims=True) + EPS))
        orn = jnp.concatenate(parts, axis=1) * hg_ref[...] * gr_ref[0, rs].astype(F32)
        mix = _dot(jnp.concatenate(slabs + [orn.astype(BF16)], axis=1), w_ref[...])
        yield

        x1 = x_ref[0, rs] + gate_ref[0] * mix
        x1_ref[0, rs] = x1
        h2 = x1 * lax.rsqrt(jnp.mean(x1 * x1, axis=-1, keepdims=True) + EPS) * g2_ref[...]
        h2 = h2 * (1.0 + scale_ref[0]) + shift_ref[0]
        gs = slice(r0 // SUBLANES, (r0 + hm) // SUBLANES)
        for c in range(h2_ref.shape[1]):
            h2_ref[gs, c] = h2[:, c * LANES:(c + 1) * LANES].reshape(hm // SUBLANES, SUBLANES,
                                                                     LANES)
        h_hi, h_lo = _split_bf16(h2)
        logits = (_dot(h_hi, wr_hi_ref[...]) + _dot(h_lo, wr_hi_ref[...])
                  + _dot(h_hi, wr_lo_ref[...]) + br_ref[...])
        yield

        lane = lax.broadcasted_iota(jnp.int32, logits.shape, 1)
        is_grp = (lane >= _ROUTE_GROUP_LANE0) & (lane < _ROUTE_GROUP_LANE0 + N_GROUPS)
        lg = jnp.where(is_grp, logits, NEG_BIG)
        mg = jnp.max(lg, axis=-1, keepdims=True)
        g_sel = _lane_min_index(lg == mg, lane) - _ROUTE_GROUP_LANE0
        pg_top = 1.0 / jnp.sum(jnp.exp(lg - mg), axis=-1, keepdims=True)
        in_grp = (lane < N_EXPERTS) & ((lane // EXPERTS_PER_GROUP) == g_sel)
        le = jnp.where(in_grp, logits, NEG_BIG)
        m1 = jnp.max(le, axis=-1, keepdims=True)
        e1 = _lane_min_index(le == m1, lane)
        le2 = jnp.where(lane == e1, NEG_BIG, le)
        m2 = jnp.max(le2, axis=-1, keepdims=True)
        e2 = _lane_min_index(le2 == m2, lane)
        r2 = jnp.exp(m2 - m1)
        w1 = pg_top / (1.0 + r2)
        w2 = pg_top * r2 / (1.0 + r2)
        onehot = ((lane == e1) | (lane == e2)).astype(BF16)
        rt = lax.broadcasted_iota(jnp.int32, (hm, hm), 0)
        ru = lax.broadcasted_iota(jnp.int32, (hm, hm), 1)
        before_local = _dot((ru < rt).astype(BF16), onehot)
        yield

        before = before_local + counts[-1]
        rank1 = jnp.sum(jnp.where(lane == e1, before, 0.0), axis=-1, keepdims=True)
        rank2 = jnp.sum(jnp.where(lane == e2, before, 0.0), axis=-1, keepdims=True)
        counts.append(counts[-1] + jnp.sum(onehot.astype(F32), axis=0, keepdims=True))
        rec = jnp.where(lane == 0, e1.astype(F32), jnp.where(lane == 1, e2.astype(F32), jnp.where(
            lane == 2, rank1, jnp.where(lane == 3, rank2, 0.0))))
        ri_ref[0, :, rs] = rec.T[:ROUTE_LANES].astype(jnp.int32)
        rl = lax.broadcasted_iota(jnp.int32, (hm, ROUTE_LANES), 1)
        rw_ref[0, rs] = jnp.where(rl == 0, w1, jnp.where(rl == 1, w2, 0.0))
        yield

    n_stages = 5
    chains = [rows_chain(r0) for r0 in range(0, tm, hm)]
    for step in range(len(chains) + n_stages - 1):
        for stage in range(n_stages):
            idx = step - stage
            if 0 <= idx < len(chains):
                next(chains[idx])
    carry_ref[...] = jnp.broadcast_to(counts[-1], carry_ref.shape)
    cnt_ref[...] = jnp.broadcast_to(counts[-1], cnt_ref.shape).astype(jnp.int32)


def _out_proj(oa, o_f, o_b, gr, x, gate, shift, scale, ag, hg, g2, w_out, wr_hi, wr_lo, br, n_ctx):
    b, s, d = x.shape
    tm = TOKEN_TILE
    off = n_ctx // tm
    ones_bd = _head_sum_matrix()

    def lat(w):
        return pl.BlockSpec((1, tm, w), lambda bi, i: (bi, i, 0))

    def allrows(w):
        return pl.BlockSpec((1, tm, w), lambda bi, i: (bi, i + off, 0))

    def mod():
        return pl.BlockSpec((1, 1, d), lambda bi, i: (bi, 0, 0))

    def const(shape):
        return pl.BlockSpec(shape, lambda bi, i: (0,) * len(shape))

    return pl.pallas_call(
        _out_proj_kernel,
        out_shape=[jax.ShapeDtypeStruct((b, s, d), F32),
                   jax.ShapeDtypeStruct((b * s // SUBLANES, d // LANES, SUBLANES, LANES), F32),
                   jax.ShapeDtypeStruct((b * s // tm, ROUTE_LANES, tm), jnp.int32),
                   jax.ShapeDtypeStruct((b, s, ROUTE_LANES), F32),
                   jax.ShapeDtypeStruct((SUBLANES, LANES), jnp.int32)],
        grid=(b, s // tm),
        in_specs=[lat(ATTN_WIDTH), allrows(HGRN_WIDTH), allrows(HGRN_WIDTH), allrows(HGRN_WIDTH),
                  lat(d), mod(), mod(), mod(),
                  const((1, ATTN_WIDTH)), const((1, HGRN_WIDTH)), const((1, d)),
                  const((ATTN_WIDTH + HGRN_WIDTH, d)), const((LANES, LANES)),
                  const((d, LANES)), const((d, LANES)), const((1, LANES))],
        out_specs=[lat(d),
                   pl.BlockSpec((tm // SUBLANES, d // LANES, SUBLANES, LANES),
                                lambda bi, i: (bi * (s // tm) + i, 0, 0, 0)),
                   pl.BlockSpec((1, ROUTE_LANES, tm), lambda bi, i: (bi * (s // tm) + i, 0, 0)),
                   lat(ROUTE_LANES), const((SUBLANES, LANES))],
        scratch_shapes=[pltpu.VMEM((SUBLANES, LANES), F32)],
        compiler_params=_params("arbitrary", "arbitrary"),
        name="out_proj",
    )(oa, o_f, o_b, gr, x, gate, shift, scale, ag, hg, g2, w_out, ones_bd, wr_hi, wr_lo, br)


def _slab_copies(stage, hbm, row0, sem, to_hbm):
    copies = []
    for c in range(stage.shape[0]):
        view = hbm.at[pl.ds(row0, stage.shape[1]), c, :]
        src, dst = (stage.at[c], view) if to_hbm else (view, stage.at[c])
        copies.append(pltpu.make_async_copy(src, dst, sem))
    return copies


def _stage_store(stage, x):
    for c in range(stage.shape[0]):
        stage[c] = x[:, c * LANES:(c + 1) * LANES]


def _stage_load(stage):
    return jnp.concatenate([stage[c] for c in range(stage.shape[0])], axis=1)


_DISPATCH_SLOTS = 3


def _dispatch_kernel(dest_ref, pad_ref, h2_hbm, xs_hbm, buf, zero_row, sem_in, sem_out):
    i = pl.program_id(0)
    n = pl.num_programs(0)
    groups = buf.shape[1]
    tm = groups * SUBLANES
    n_pad = pad_ref.shape[2]
    slot = i % _DISPATCH_SLOTS

    def fetch(t, sl):
        return pltpu.make_async_copy(h2_hbm.at[pl.ds(t * groups, groups)], buf.at[sl],
                                     sem_in.at[sl])

    def wait_rows(sl):
        for _ in range(TOP_K + n_pad // tm):
            pltpu.make_async_copy(h2_hbm.at[pl.ds(0, groups)], buf.at[sl], sem_out.at[sl]).wait()
        rest = (n_pad % tm) // SUBLANES
        if rest:
            pltpu.make_async_copy(h2_hbm.at[pl.ds(0, rest)], buf.at[sl, pl.ds(0, rest)],
                                  sem_out.at[sl]).wait()

    @pl.when(i == 0)
    def _():
        zero_row[...] = jnp.zeros_like(zero_row)
        fetch(0, 0).start()

    @pl.when(i >= 2)
    def _():
        wait_rows((i + 1) % _DISPATCH_SLOTS)

    @pl.when(i + 1 < n)
    def _():
        fetch(i + 1, (i + 1) % _DISPATCH_SLOTS).start()

    fetch(i, slot).wait()

    def body(g, carry):
        rows = [[dest_ref[0, 0, k * tm + g * SUBLANES + u] for k in range(TOP_K)]
                for u in range(SUBLANES)]
        for u in range(SUBLANES):
            for k in range(TOP_K):
                pltpu.make_async_copy(buf.at[slot, g, :, u, :], xs_hbm.at[rows[u][k]],
                                      sem_out.at[slot]).start(priority=k % DMA_PRIORITIES)
        return carry
    lax.fori_loop(0, groups, body, 0)

    def pad_body(g, carry):
        rows = [pad_ref[0, 0, g * SUBLANES + u] for u in range(SUBLANES)]
        for u in range(SUBLANES):
            pltpu.make_async_copy(zero_row, xs_hbm.at[rows[u]],
                                  sem_out.at[slot]).start(priority=u % DMA_PRIORITIES)
        return carry
    lax.fori_loop(0, n_pad // SUBLANES, pad_body, 0)

    @pl.when(i == n - 1)
    def _():
        wait_rows(slot)

    @pl.when((i == n - 1) & (i >= 1))
    def _():
        wait_rows((i - 1) % _DISPATCH_SLOTS)


def _dispatch(dest_tiles, pad_tiles, h2_tiles, n_slots):
    nt = dest_tiles.shape[0]
    n_groups, n_slabs, _, _ = h2_tiles.shape
    groups = n_groups // nt
    assert pad_tiles.shape[2] % SUBLANES == 0

    def idx_spec(arr):
        return pl.BlockSpec((1, 1, arr.shape[2]), lambda i: (i, 0, 0), memory_space=pltpu.SMEM)

    return pl.pallas_call(
        _dispatch_kernel,
        out_shape=jax.ShapeDtypeStruct((n_slots, n_slabs, LANES), F32),
        grid=(nt,),
        in_specs=[idx_spec(dest_tiles), idx_spec(pad_tiles), pl.BlockSpec(memory_space=pl.ANY)],
        out_specs=pl.BlockSpec(memory_space=pl.ANY),
        scratch_shapes=[pltpu.VMEM((_DISPATCH_SLOTS, groups, n_slabs, SUBLANES, LANES), F32),
                        pltpu.VMEM((n_slabs, LANES), F32),
                        pltpu.SemaphoreType.DMA((_DISPATCH_SLOTS,)),
                        pltpu.SemaphoreType.DMA((_DISPATCH_SLOTS,))],
        compiler_params=_params("arbitrary"),
        name="dispatch",
    )(dest_tiles, pad_tiles, h2_tiles)


_EXPERT_IN_SLOTS = 3


def _experts_kernel(meta_ref, xs_hbm, wg_ref, wu_ref, wd_ref, ys_hbm,
                    xstage, ystage, wg_bf, wu_bf, wd_bf, sem_in, sem_out):
    j = pl.program_id(0)
    last = pl.num_programs(0) - 1
    n_used = meta_ref[0]
    slot = j % 2

    def fetch(blk, sl):
        for cp in _slab_copies(xstage.at[sl], xs_hbm, blk * MOE_BLOCK, sem_in.at[sl], False):
            cp.start()

    def wait_out(sl):
        for cp in _slab_copies(ystage.at[sl], ys_hbm, 0, sem_out.at[sl], True):
            cp.wait()

    in_slot = j % _EXPERT_IN_SLOTS

    @pl.when((j == 0) & (n_used > 0))
    def _():
        fetch(0, 0)

    @pl.when((j == 0) & (n_used > 1))
    def _():
        fetch(1, 1)

    @pl.when(j + 2 < n_used)
    def _():
        fetch(j + 2, (j + 2) % _EXPERT_IN_SLOTS)

    new_expert = (j == 0) | (meta_ref[1 + j] != meta_ref[jnp.maximum(j, 1)])

    @pl.when((j < n_used) & new_expert)
    def _():
        wg_bf[...] = wg_ref[0].astype(BF16)
        wu_bf[...] = wu_ref[0].astype(BF16)
        wd_bf[...] = wd_ref[0].astype(BF16)

    @pl.when(j >= 2)
    def _():
        wait_out(slot)

    @pl.when(j < n_used)
    def _():
        for cp in _slab_copies(xstage.at[in_slot], xs_hbm, 0, sem_in.at[in_slot], False):
            cp.wait()
        xb = _stage_load(xstage.at[in_slot]).astype(BF16)
        a = _silu(_dot(xb, wg_bf[...])) * _dot(xb, wu_bf[...])
        _stage_store(ystage.at[slot], _dot(a.astype(BF16), wd_bf[...]))

    @pl.when(j >= n_used)
    def _():
        ystage[slot] = jnp.zeros(ystage.shape[1:], F32)

    for cp in _slab_copies(ystage.at[slot], ys_hbm, j * MOE_BLOCK, sem_out.at[slot], True):
        cp.start()

    @pl.when(j == last)
    def _():
        wait_out(slot)

    @pl.when((j == last) & (j >= 1))
    def _():
        wait_out(1 - slot)


def _experts(meta, xs, wg, wu, wd):
    n_slots, n_slabs, _ = xs.shape
    n_blk = n_slots // MOE_BLOCK
    d = wg.shape[1]
    ff = wg.shape[2]

    def wspec(shape):
        return pl.BlockSpec((1,) + shape, lambda j, meta: (meta[1 + j], 0, 0))

    in_stage = pltpu.VMEM((_EXPERT_IN_SLOTS, n_slabs, MOE_BLOCK, LANES), F32)
    stage = pltpu.VMEM((2, n_slabs, MOE_BLOCK, LANES), F32)
    return pl.pallas_call(
        _experts_kernel,
        out_shape=jax.ShapeDtypeStruct(xs.shape, F32),
        grid_spec=pltpu.PrefetchScalarGridSpec(
            num_scalar_prefetch=1,
            grid=(n_blk,),
            in_specs=[pl.BlockSpec(memory_space=pl.ANY),
                      wspec((d, ff)), wspec((d, ff)), wspec((ff, d))],
            out_specs=pl.BlockSpec(memory_space=pl.ANY),
            scratch_shapes=[in_stage, stage, pltpu.VMEM((d, ff), BF16),
                            pltpu.VMEM((d, ff), BF16), pltpu.VMEM((ff, d), BF16),
                            pltpu.SemaphoreType.DMA((_EXPERT_IN_SLOTS,)),
                            pltpu.SemaphoreType.DMA((2,))],
        ),
        compiler_params=_params("arbitrary"),
        name="experts",
    )(meta, xs, wg, wu, wd)


def _combine_kernel(dest_cur_ref, dest_next_ref, ys_hbm, x1_ref, gate_ref, rw_ref, o_ref,
                    buf, sem):
    i = pl.program_id(0)
    slot = i % 2
    tm = x1_ref.shape[0]
    groups = tm // SUBLANES

    def issue(idx_ref, sl, g):
        rows = [[idx_ref[0, 0, k * tm + g * SUBLANES + u] for u in range(SUBLANES)]
                for k in range(TOP_K)]
        for k in range(TOP_K):
            for u in range(SUBLANES):
                pltpu.make_async_copy(ys_hbm.at[rows[k][u]], buf.at[sl, k, g, :, u, :],
                                      sem.at[sl, k]).start(priority=u % DMA_PRIORITIES)

    def wait(sl, k):
        for u in range(SUBLANES):
            pltpu.make_async_copy(ys_hbm.at[pl.ds(0, groups)], buf.at[sl, k, :, :, u, :],
                                  sem.at[sl, k]).wait()

    @pl.when(i == 0)
    def _():
        def first(g, carry):
            issue(dest_cur_ref, 0, g)
            return carry
        lax.fori_loop(0, groups, first, 0)

    for k in range(TOP_K):
        wait(slot, k)

    def group(g, carry):
        issue(dest_next_ref, 1 - slot, g)
        r8 = pl.ds(pl.multiple_of(g * SUBLANES, SUBLANES), SUBLANES)
        rw = rw_ref[r8, :]
        moe = None
        for k in range(TOP_K):
            rows = jnp.concatenate([buf[slot, k, g, c] for c in range(buf.shape[3])], axis=1)
            term = rw[:, k:k + 1] * rows
            moe = term if moe is None else moe + term
        o_ref[r8, :] = x1_ref[r8, :] + gate_ref[0] * moe
        return carry
    lax.fori_loop(0, groups, group, 0)

    @pl.when(i == pl.num_programs(0) - 1)
    def _():
        for k in range(TOP_K):
            wait(1 - slot, k)


def _combine(dest_tiles, ys, x1, gate, rw):
    n_tok, d = x1.shape
    tm = TOKEN_TILE
    nt = n_tok // tm
    per_batch = nt // gate.shape[0]
    idx_blk = (1, 1, TOP_K * tm)
    return pl.pallas_call(
        _combine_kernel,
        out_shape=jax.ShapeDtypeStruct((n_tok, d), F32),
        grid=(nt,),
        in_specs=[
            pl.BlockSpec(idx_blk, lambda i: (i, 0, 0), memory_space=pltpu.SMEM),
            pl.BlockSpec(idx_blk, lambda i: (jnp.minimum(i + 1, nt - 1), 0, 0),
                         memory_space=pltpu.SMEM),
            pl.BlockSpec(memory_space=pl.ANY),
            pl.BlockSpec((tm, d), lambda i: (i, 0)),
            pl.BlockSpec((1, 1, d), lambda i: (i // per_batch, 0, 0)),
            pl.BlockSpec((tm, ROUTE_LANES), lambda i: (i, 0)),
        ],
        out_specs=pl.BlockSpec((tm, d), lambda i: (i, 0)),
        scratch_shapes=[pltpu.VMEM((2, TOP_K, tm // SUBLANES, d // LANES, SUBLANES, LANES), F32),
                        pltpu.SemaphoreType.DMA((2, TOP_K))],
        compiler_params=_params("arbitrary"),
        name="combine",
    )(dest_tiles, dest_tiles, ys, x1, gate, rw)


def _layer(x, ctx, c, c_ctx, w_ada, b_ada, norm1_g, norm2_g, w_in, q_norm_g, k_norm_g, attn_out_g,
           lb, hgrn_out_g, w_out, w_router_grp, b_router_grp, w_router_exp, b_router_exp,
           w_exp_gate, w_exp_up, w_exp_down):
    b, s, d = x.shape
    n_ctx = ctx.shape[1]
    assert n_ctx % TOKEN_TILE == 0 and s % TOKEN_TILE == 0 and s % GRID_W == 0
    assert n_ctx % HGRN_CHUNK == 0 and (n_ctx + s) % ATTN_KV_TILE == 0
    n_all = n_ctx + s

    cond = jnp.zeros((2 * SUBLANES, d), F32).at[:b].set(c).at[b].set(c_ctx)
    assert b + 1 <= cond.shape[0]
    mods = _adaln(cond, w_ada, b_ada)[:b + 1].reshape(b + 1, 1, 6, d)
    sh1, sc1, gt1, sh2, sc2, gt2 = (mods[:, :, m] for m in range(6))

    scale_q = ATTN_HEAD_DIM ** -0.5 * np.log2(np.e)
    qkg = jnp.concatenate([jnp.tile(q_norm_g, ATTN_HEADS) * scale_q,
                           jnp.tile(k_norm_g, ATTN_KV_HEADS)]).reshape(1, _QK_WIDTH)
    cos, sin = _rope_tables(n_ctx, s)
    qa, ka, vt, qr, ff, fb, ir, gr = _in_proj(
        ctx, x, sh1, sc1, norm1_g.reshape(1, d), w_in.astype(BF16), qkg, cos, sin)
    oa = _attention(qa, ka, vt)
    o_f, o_b = _hgrn(qr, ir, ff, fb, lb, n_ctx)

    w_router = jnp.zeros((d, LANES), F32)
    w_router = w_router.at[:, :N_EXPERTS].set(w_router_exp)
    w_router = w_router.at[:, _ROUTE_GROUP_LANE0:_ROUTE_GROUP_LANE0 + N_GROUPS].set(w_router_grp)
    b_router = jnp.zeros((1, LANES), F32)
    b_router = b_router.at[0, :N_EXPERTS].set(b_router_exp)
    b_router = b_router.at[0, _ROUTE_GROUP_LANE0:_ROUTE_GROUP_LANE0 + N_GROUPS].set(b_router_grp)
    wr_hi, wr_lo = _split_bf16(w_router)
    x1, h2, ri, rw, counts = _out_proj(
        oa, o_f, o_b, gr, x, gt1[:b], sh2[:b], sc2[:b], attn_out_g.reshape(1, -1),
        hgrn_out_g.reshape(1, -1), norm2_g.reshape(1, d), w_out.astype(BF16), wr_hi, wr_lo,
        b_router, n_ctx)

    n_tok = b * s
    counts = counts[0, :N_EXPERTS]
    padded = (counts + MOE_BLOCK - 1) // MOE_BLOCK * MOE_BLOCK
    pend = jnp.cumsum(padded)
    pstart = pend - padded
    n_blk = n_tok * TOP_K // MOE_BLOCK + N_EXPERTS
    def lookup(table, idx):
        hit = idx[..., None] == jnp.arange(table.shape[0], dtype=jnp.int32)
        return jnp.sum(jnp.where(hit, table.astype(jnp.int32), 0), axis=-1)

    dest = lookup(pstart, ri[:, :TOP_K]) + ri[:, TOP_K:2 * TOP_K]
    dest_tiles = dest.reshape(dest.shape[0], 1, -1)
    blk_start = jnp.arange(n_blk, dtype=jnp.int32) * MOE_BLOCK
    blk_e = jnp.minimum(jnp.sum(blk_start[:, None] >= pend[None, :], axis=1), N_EXPERTS - 1)
    n_used = pend[-1] // MOE_BLOCK
    meta = jnp.concatenate([n_used[None], blk_e]).astype(jnp.int32)

    n_slots = n_blk * MOE_BLOCK
    n_pad = n_slots - n_tok * TOP_K
    gap_start = jnp.concatenate([pstart + counts, pend[-1:]])
    gap_size = jnp.concatenate([padded - counts, n_slots - pend[-1:]])
    gap_end = jnp.cumsum(gap_size)
    pad_i = jnp.arange(n_pad, dtype=jnp.int32)
    gap = jnp.sum(pad_i[:, None] >= gap_end[None, :], axis=1)
    pad_slots = lookup(gap_start - (gap_end - gap_size), gap) + pad_i
    pad_tiles = pad_slots.astype(jnp.int32).reshape(dest_tiles.shape[0], 1, -1)

    xs = _dispatch(dest_tiles, pad_tiles, h2, n_slots)
    ys = _experts(meta, xs, w_exp_gate, w_exp_up, w_exp_down)
    out = _combine(dest_tiles, ys, x1.reshape(n_tok, d), gt2[:b], rw.reshape(n_tok, ROUTE_LANES))
    return out.reshape(b, s, d)


def kernel(x, c, ctx, c_ctx, w_ada, b_ada, norm1_g, norm2_g, w_in, q_norm_g, k_norm_g, attn_out_g,
           hgrn_lb, hgrn_out_g, w_out, w_router_grp, b_router_grp, w_router_exp, b_router_exp,
           w_exp_gate, w_exp_up, w_exp_down):
    depth = w_in.shape[0]
    assert depth == 1, "context stream update between layers is not implemented"
    lb_all = jnp.cumsum(jax.nn.softmax(hgrn_lb.astype(F32), axis=1), axis=1)
    layer = 0
    return _layer(x, ctx, c, c_ctx, w_ada[layer], b_ada[layer], norm1_g[layer], norm2_g[layer],
                  w_in[layer], q_norm_g[layer], k_norm_g[layer], attn_out_g[layer],
                  lb_all[:, layer], hgrn_out_g[layer], w_out[layer], w_router_grp[layer],
                  b_router_grp[layer], w_router_exp[layer], b_router_exp[layer],
                  w_exp_gate[layer], w_exp_up[layer], w_exp_down[layer])
```
